```python
import math
import jax
import jax.numpy as jnp
from jax import lax
import numpy as np

D_MODEL = 1024
BATCH = 16
SEQ = 256
DEPTH = 4
DEC_BATCH = 8
DEC_SEQ = 1024
PAST_LEN = 512

GRID_W = 64
N_MIXERS = 2
N_GDN = (DEPTH + 1) // 2
N_HYENA = DEPTH // 2
GDN_HEADS = 8
GDN_DK = 128
GDN_DV = 128
GDN_QK_W = GDN_HEADS * GDN_DK
GDN_V_W = GDN_HEADS * GDN_DV
GDN_CONV_W = 2 * GDN_QK_W + GDN_V_W
GDN_IN_W = GDN_CONV_W + GDN_V_W + 4 * GDN_HEADS
GDN_CONV_K = 5
CHUNK = 64
HY_ORDER = 2
HY_CONV_K = 3
HY_EMB = 33
HY_BANDS = (HY_EMB - 1) // 2
HY_FILTER_HID = 64
HY_TARGET = 1e-2
HY_SHORT_PCT = 0.3
HY_LONG_PCT = 1.5
HY_MAX_DECAY = math.log(HY_TARGET) / HY_SHORT_PCT
HY_MIN_DECAY = math.log(HY_TARGET) / HY_LONG_PCT
D_FF = (8 * D_MODEL + 3 * 256 - 1) // (3 * 256) * 256
POS_BASE = 10000.0
EPS = 1e-6

kernel_name = 'hybrid_gdn_hyena_diffusion_step'


def _rms_norm(x, g):
    xf = x.astype(jnp.float32)
    y = xf * lax.rsqrt(jnp.mean(xf * xf, axis=-1, keepdims=True) + EPS)
    return (y * g.astype(jnp.float32)).astype(x.dtype)


def _ada(cvec, w, b):
    m = jax.nn.silu(cvec) @ w + b
    return [t[:, None, :] for t in jnp.split(m, 6, axis=-1)]


def _modulate(h, shift, scale):
    return h * (1.0 + scale) + shift


def _dwconv(x, w):
    k, ch = w.shape
    return lax.conv_general_dilated(
        x, w[:, None, :].astype(x.dtype), window_strides=(1,),
        padding=[(k // 2, k // 2)], dimension_numbers=('NWC', 'WIO', 'NWC'),
        feature_group_count=ch)


def _l2norm(x):
    return x * lax.rsqrt(jnp.sum(x * x, axis=-1, keepdims=True) + EPS)


def _grid_pos_emb(n_tokens, dtype):
    rows = n_tokens // GRID_W
    r, col = jnp.meshgrid(jnp.arange(rows), jnp.arange(GRID_W), indexing='ij')
    quarter = D_MODEL // 4
    omega = 1.0 / (POS_BASE ** (jnp.arange(quarter, dtype=jnp.float32) / quarter))

    def emb1d(p):
        a = p.reshape(-1, 1).astype(jnp.float32) * omega[None, :]
        return jnp.concatenate([jnp.sin(a), jnp.cos(a)], axis=-1)

    return jnp.concatenate([emb1d(r), emb1d(col)], axis=-1).astype(dtype)


def _chunk_gated_delta(q, k, v, g, beta, s0):
    b_sz, seq, heads, _ = q.shape
    dv = v.shape[-1]
    n = seq // CHUNK

    def blocks(t):
        t = t.reshape((b_sz, n, CHUNK) + t.shape[2:])
        return jnp.moveaxis(jnp.moveaxis(t, 1, 0), 3, 2)

    q, k, v, g, beta = (blocks(t) for t in (q, k, v, g, beta))
    gc = jnp.cumsum(g, axis=-1)
    idx = jnp.arange(CHUNK)
    incl = idx[:, None] >= idx[None, :]
    strict = idx[:, None] > idx[None, :]
    diff = gc[..., :, None] - gc[..., None, :]
    decay = jnp.where(incl, jnp.exp(jnp.where(incl, diff, 0.0)), 0.0)
    kb = k * beta[..., None]
    a_kk = jnp.where(strict, jnp.einsum('nbhid,nbhjd->nbhij', kb, k) * decay, 0.0)
    m = a_kk + jnp.eye(CHUNK, dtype=a_kk.dtype)
    w = lax.linalg.triangular_solve(m, kb * jnp.exp(gc)[..., None], left_side=True,
                                    lower=True, unit_diagonal=True)
    u = lax.linalg.triangular_solve(m, v * beta[..., None], left_side=True,
                                    lower=True, unit_diagonal=True)
    a_qk = jnp.einsum('nbhid,nbhjd->nbhij', q, k) * decay
    q_dec = q * jnp.exp(gc)[..., None]
    k_tail = k * jnp.exp(gc[..., -1:] - gc)[..., None]
    g_last = jnp.exp(gc[..., -1])

    def step(s, xs):
        w_c, u_c, qd_c, aqk_c, kt_c, gl_c = xs
        v_new = u_c - jnp.einsum('bhcd,bhde->bhce', w_c, s)
        o_c = (jnp.einsum('bhcd,bhde->bhce', qd_c, s)
               + jnp.einsum('bhij,bhje->bhie', aqk_c, v_new))
        s = s * gl_c[..., None, None] + jnp.einsum('bhcd,bhce->bhde', kt_c, v_new)
        return s, o_c

    s_fin, o = lax.scan(step, s0, (w, u, q_dec, a_qk, k_tail, g_last))
    o = jnp.moveaxis(jnp.moveaxis(o, 2, 3), 0, 1).reshape(b_sz, seq, heads, dv)
    return o, s_fin


def _gdn_mixer(h, w_in, conv_w, a_log, dt_bias, onorm, w_out, s0):
    b_sz, seq, _ = h.shape
    f32 = jnp.float32
    proj = h @ w_in
    qkv = jax.nn.silu(_dwconv(proj[..., :GDN_CONV_W], conv_w)).astype(f32)
    q = _l2norm(qkv[..., :GDN_QK_W].reshape(b_sz, seq, GDN_HEADS, GDN_DK)) * GDN_DK ** -0.5
    k = _l2norm(qkv[..., GDN_QK_W:2 * GDN_QK_W].reshape(b_sz, seq, GDN_HEADS, GDN_DK))
    v = qkv[..., 2 * GDN_QK_W:].reshape(b_sz, seq, GDN_HEADS, GDN_DV)
    gate = proj[..., GDN_CONV_W:GDN_CONV_W + GDN_V_W].astype(f32).reshape(b_sz, seq, GDN_HEADS, GDN_DV)
    ab = proj[..., GDN_CONV_W + GDN_V_W:].astype(f32).reshape(b_sz, seq, 2, 2, GDN_HEADS)
    g = -jnp.exp(a_log.astype(f32)) * jax.nn.softplus(ab[:, :, 0] + dt_bias.astype(f32))
    beta = jax.nn.sigmoid(ab[:, :, 1])
    s0 = s0.astype(f32)
    o_f, s_f = _chunk_gated_delta(q, k, v, g[:, :, 0], beta[:, :, 0], s0[:, 0])

    def rev(t):
        return jnp.flip(t, axis=1)

    o_b, s_b = _chunk_gated_delta(rev(q), rev(k), rev(v), rev(g[:, :, 1]), rev(beta[:, :, 1]), s0[:, 1])
    o = o_f + rev(o_b)
    o = (o * lax.rsqrt(jnp.mean(o * o, axis=-1, keepdims=True) + EPS)
         * onorm.astype(f32) * jax.nn.silu(gate))
    out = o.reshape(b_sz, seq, GDN_V_W).astype(h.dtype) @ w_out
    return out, jnp.stack([s_f, s_b], axis=1)


def _hyena_filters(seq, w1, b1, w2, b2, w3, freq):
    f32 = jnp.float32
    t = jnp.linspace(0.0, 1.0, seq, dtype=f32)[:, None]
    wpos = (2.0 * math.pi / seq) * jnp.arange(seq, dtype=f32)[:, None]
    fb = jnp.linspace(1e-4, HY_BANDS - 1, HY_BANDS, dtype=f32)[None, :]
    z = jnp.concatenate([t, jnp.cos(fb * wpos), -jnp.sin(fb * wpos)], axis=-1)
    fr = freq.astype(f32)
    hid = jnp.sin(fr * (z @ w1.astype(f32) + b1.astype(f32)))
    hid = jnp.sin(fr * (hid @ w2.astype(f32) + b2.astype(f32)))
    hk = (hid @ w3.astype(f32)).reshape(seq, HY_ORDER, 2, D_MODEL)
    deltas = jnp.abs(jnp.linspace(HY_MIN_DECAY, HY_MAX_DECAY, D_MODEL, dtype=f32))
    window = jnp.exp(-t * deltas[None, :])
    return hk * window[:, None, None, :]


def _bidir_fftconv(z, hf, hb):
    seq = z.shape[1]
    h_full = jnp.concatenate([hf, jnp.zeros((1, hf.shape[1]), hf.dtype), hb[:0:-1]], axis=0)
    zf = jnp.fft.rfft(z, n=2 * seq, axis=1)
    kf = jnp.fft.rfft(h_full, axis=0)
    return jnp.fft.irfft(zf * kf[None], n=2 * seq, axis=1)[:, :seq]


def _hyena_mixer(h, w_in, b_in, conv_w, f_w1, f_b1, f_w2, f_b2, f_w3, freq, skip, w_out, b_out):
    seq = h.shape[1]
    u = _dwconv(h @ w_in + b_in, conv_w).astype(jnp.float32)
    parts = jnp.split(u, HY_ORDER + 1, axis=-1)
    filt = _hyena_filters(seq, f_w1, f_b1, f_w2, f_b2, f_w3, freq)
    z = parts[0]
    for n in range(HY_ORDER):
        z = parts[n + 1] * (_bidir_fftconv(z, filt[:, n, 0], filt[:, n, 1])
                            + skip[n].astype(jnp.float32) * z)
    return z.astype(h.dtype) @ w_out + b_out


def _swiglu(h, w_gu, w_down):
    gu = h @ w_gu
    return (jax.nn.silu(gu[..., :D_FF]) * gu[..., D_FF:]) @ w_down


def setup_inputs(seed: int = 0) -> dict:
    key = jax.random.key(seed)
    ks = iter(jax.random.split(key, 40))
    f32 = jnp.float32

    def nrm(shape, scale):
        return jax.random.normal(next(ks), shape, f32) * scale

    dt = jnp.exp(jax.random.uniform(next(ks), (N_GDN, 2, GDN_HEADS), f32,
                                    math.log(1e-3), math.log(1e-1)))
    a_val = jax.random.uniform(next(ks), (N_GDN, 2, GDN_HEADS), f32, 1.0, 16.0)
    hy_w = (HY_ORDER + 1) * D_MODEL
    return {
        'x_prompt': nrm((BATCH, SEQ, D_MODEL), 1.0),
        'x_sample': nrm((DEC_BATCH, DEC_SEQ, D_MODEL), 1.0),
        'state_delta': nrm((DEC_BATCH, N_GDN, 2, GDN_HEADS, GDN_DK, GDN_DV), 0.1),
        'c': nrm((DEC_BATCH, D_MODEL), 1.0),
        'c_ctx': nrm((D_MODEL,), 1.0),
        'ada_w': nrm((DEPTH, D_MODEL, 6 * D_MODEL), 0.5 * D_MODEL ** -0.5),
        'ada_b': nrm((DEPTH, 6 * D_MODEL), 0.02),
        'norm1_g': 1.0 + nrm((DEPTH, D_MODEL), 0.02),
        'norm2_g': 1.0 + nrm((DEPTH, D_MODEL), 0.02),
        'gdn_w_in': nrm((N_GDN, D_MODEL, GDN_IN_W), D_MODEL ** -0.5),
        'gdn_conv': nrm((N_GDN, GDN_CONV_K, GDN_CONV_W), GDN_CONV_K ** -0.5),
        'gdn_a_log': jnp.log(a_val),
        'gdn_dt_bias': dt + jnp.log(-jnp.expm1(-dt)),
        'gdn_onorm': 1.0 + nrm((N_GDN, GDN_DV), 0.02),
        'gdn_w_out': nrm((N_GDN, GDN_V_W, D_MODEL), GDN_V_W ** -0.5),
        'hy_w_in': nrm((N_HYENA, D_MODEL, hy_w), D_MODEL ** -0.5),
        'hy_b_in': nrm((N_HYENA, hy_w), 0.02),
        'hy_conv': nrm((N_HYENA, HY_CONV_K, hy_w), HY_CONV_K ** -0.5),
        'hy_f_w1': nrm((N_HYENA, HY_EMB, HY_FILTER_HID), HY_EMB ** -0.5),
        'hy_f_b1': nrm((N_HYENA, HY_FILTER_HID), 0.1),
        'hy_f_w2': nrm((N_HYENA, HY_FILTER_HID, HY_FILTER_HID), HY_FILTER_HID ** -0.5),
        'hy_f_b2': nrm((N_HYENA, HY_FILTER_HID), 0.1),
        'hy_f_w3': nrm((N_HYENA, HY_FILTER_HID, HY_ORDER * 2 * D_MODEL), 0.05 * HY_FILTER_HID ** -0.5),
        'hy_freq': 1.0 + nrm((N_HYENA, HY_FILTER_HID), 0.1),
        'hy_skip': nrm((N_HYENA, HY_ORDER, D_MODEL), 0.5),
        'hy_w_out': nrm((N_HYENA, D_MODEL, D_MODEL), D_MODEL ** -0.5),
        'hy_b_out': nrm((N_HYENA, D_MODEL), 0.02),
        'ffn_w_gu': nrm((DEPTH, D_MODEL, 2 * D_FF), D_MODEL ** -0.5),
        'ffn_w_down': nrm((DEPTH, D_FF, D_MODEL), D_FF ** -0.5),
        'final_g': 1.0 + nrm((D_MODEL,), 0.02),
    }


def reference(x_prompt, x_sample, state_delta, c, c_ctx, ada_w, ada_b, norm1_g, norm2_g,
              gdn_w_in, gdn_conv, gdn_a_log, gdn_dt_bias, gdn_onorm, gdn_w_out,
              hy_w_in, hy_b_in, hy_conv, hy_f_w1, hy_f_b1, hy_f_w2, hy_f_b2, hy_f_w3,
              hy_freq, hy_skip, hy_w_out, hy_b_out, ffn_w_gu, ffn_w_down, final_g):
    ctx = x_prompt
    lat = x_sample + _grid_pos_emb(x_sample.shape[1], x_sample.dtype)[None]
    n_ctx = x_prompt.shape[0]
    new_states = []
    for layer in range(DEPTH):
        mc = _ada(c_ctx[None, :], ada_w[layer], ada_b[layer])
        ml = _ada(c, ada_w[layer], ada_b[layer])
        hc = _modulate(_rms_norm(ctx, norm1_g[layer]), mc[0], mc[1])
        hl = _modulate(_rms_norm(lat, norm1_g[layer]), ml[0], ml[1])
        j = layer // N_MIXERS
        if layer % N_MIXERS == 0:
            gdn = (gdn_w_in[j], gdn_conv[j], gdn_a_log[j], gdn_dt_bias[j], gdn_onorm[j], gdn_w_out[j])
            s_zero = jnp.zeros((n_ctx, 2, GDN_HEADS, GDN_DK, GDN_DV), jnp.float32)
            oc, s_ctx = _gdn_mixer(hc, *gdn, s_zero)
            ol, _ = _gdn_mixer(hl, *gdn, state_delta[:, j])
            new_states.append(s_ctx)
        else:
            hy = (hy_w_in[j], hy_b_in[j], hy_conv[j], hy_f_w1[j], hy_f_b1[j], hy_f_w2[j],
                  hy_f_b2[j], hy_f_w3[j], hy_freq[j], hy_skip[j], hy_w_out[j], hy_b_out[j])
            oc = _hyena_mixer(hc, *hy)
            ol = _hyena_mixer(hl, *hy)
        ctx = ctx + mc[2] * oc
        lat = lat + ml[2] * ol
        ctx = ctx + mc[5] * _swiglu(_modulate(_rms_norm(ctx, norm2_g[layer]), mc[3], mc[4]),
                                    ffn_w_gu[layer], ffn_w_down[layer])
        lat = lat + ml[5] * _swiglu(_modulate(_rms_norm(lat, norm2_g[layer]), ml[3], ml[4]),
                                    ffn_w_gu[layer], ffn_w_down[layer])
    y_prompt = _rms_norm(ctx, final_g)
    y_sample = _rms_norm(lat, final_g)
    new_state_delta = jnp.stack(new_states, axis=1)
    return (y_prompt, y_sample, new_state_delta)
```

```python
import functools
import math

import jax
import jax.numpy as jnp
import numpy as np
from jax import lax
from jax.experimental import pallas as pl
from jax.experimental.pallas import tpu as pltpu

GRID_W = 64
CHUNK = 64
HY_ORDER = 2
HY_TARGET = 1e-2
HY_SHORT_PCT = 0.3
HY_LONG_PCT = 1.5
POS_BASE = 10000.0
EPS = 1e-6

V7X_LANES = 128
V7X_SUBLANES = 8
V7X_VMEM_LIMIT_BYTES = 48 * 1024 * 1024

BF16 = jnp.bfloat16
F32 = jnp.float32
HIGHEST = lax.Precision.HIGHEST


def _cparams(*sem):
    return pltpu.CompilerParams(dimension_semantics=sem, vmem_limit_bytes=V7X_VMEM_LIMIT_BYTES)


def _tile(n, target, align):
    if n <= target:
        return n
    best = None
    for t in range(align, target + 1, align):
        if n % t == 0:
            best = t
    assert best is not None, (n, target, align)
    return best


def _dot(a, b):
    return jnp.dot(a.astype(BF16), b.astype(BF16), preferred_element_type=F32)


def _dot_nt(a, b):
    return lax.dot_general(a.astype(BF16), b.astype(BF16), (((1,), (1,)), ((), ())),
                           preferred_element_type=F32)


def _dot_hi(a, b):
    return jnp.dot(a, b, preferred_element_type=F32, precision=HIGHEST)


def _silu(x):
    return x * jax.nn.sigmoid(x)


def _norm_mod(x, g, shift, scale):
    ms = jnp.mean(x * x, axis=-1, keepdims=True)
    return (x * lax.rsqrt(ms + EPS) * g) * (1.0 + scale) + shift


class _Rows:
    def __init__(self, tc, ll, t, tm):
        assert tc % tm == 0 and ll % tm == 0 and t % tm == 0
        self.n_ctx_tiles = tc // tm
        self.tiles_per_lat = ll // tm
        self.n_tiles = t // tm
        self.tm = tm

    def mod_index(self, i):
        lat = 1 + (i - self.n_ctx_tiles) // self.tiles_per_lat
        return jnp.where(i < self.n_ctx_tiles, 0, lat)


def _ada_kernel(c_ref, w_ref, b_ref, o_ref):
    o_ref[...] = _dot(_silu(c_ref[...]), w_ref[...]) + b_ref[...]


def _ada(cvec, ada_w, ada_b):
    depth, d, n = ada_w.shape
    bm = cvec.shape[0]
    tn = _tile(n, 1536, V7X_LANES)
    return pl.pallas_call(
        _ada_kernel,
        out_shape=jax.ShapeDtypeStruct((depth, bm, n), F32),
        grid=(depth, n // tn),
        in_specs=[pl.BlockSpec((bm, d), lambda l, j: (0, 0)),
                  pl.BlockSpec((None, d, tn), lambda l, j: (l, 0, j)),
                  pl.BlockSpec((None, 1, tn), lambda l, j: (l, 0, j))],
        out_specs=pl.BlockSpec((None, bm, tn), lambda l, j: (l, 0, j)),
        compiler_params=_cparams("parallel", "parallel"),
        name="ada",
    )(cvec, ada_w, ada_b.reshape(depth, 1, n))


def _embed_kernel(xp_ref, xs_ref, pos_ref, o_ref, *, n_ctx_tiles):
    i = pl.program_id(0)

    @pl.when(i < n_ctx_tiles)
    def _():
        o_ref[...] = xp_ref[...]

    @pl.when(i >= n_ctx_tiles)
    def _():
        o_ref[...] = xs_ref[...] + pos_ref[...]


def _embed(xp, xs, pos, rows):
    t, d = xp.shape[0] + xs.shape[0], xp.shape[1]
    tm, nct = rows.tm, rows.n_ctx_tiles
    npos = pos.shape[0] // tm
    return pl.pallas_call(
        functools.partial(_embed_kernel, n_ctx_tiles=nct),
        out_shape=jax.ShapeDtypeStruct((t, d), F32),
        grid=(rows.n_tiles,),
        in_specs=[pl.BlockSpec((tm, d), lambda i: (jnp.minimum(i, nct - 1), 0)),
                  pl.BlockSpec((tm, d), lambda i: (jnp.maximum(i - nct, 0), 0)),
                  pl.BlockSpec((tm, d), lambda i: (jnp.maximum(i - nct, 0) % npos, 0))],
        out_specs=pl.BlockSpec((tm, d), lambda i: (i, 0)),
        compiler_params=_cparams("parallel"),
        name="embed",
    )(xp, xs, pos)


def _in_kernel(x_ref, g_ref, mod_ref, w_ref, b_ref, o_ref, h_ref):
    @pl.when(pl.program_id(1) == 0)
    def _():
        m = mod_ref[...]
        h_ref[...] = _norm_mod(x_ref[...], g_ref[...], m[0:1, :], m[1:2, :]).astype(h_ref.dtype)

    o_ref[...] = jnp.dot(h_ref[...], w_ref[...], preferred_element_type=F32) + b_ref[...]


def _in_proj(x, g, mods, w, b, rows):
    t, d = x.shape
    n = w.shape[1]
    tm = rows.tm
    tn = _tile(n, 1536, V7X_LANES)
    return pl.pallas_call(
        _in_kernel,
        out_shape=jax.ShapeDtypeStruct((t, n), F32),
        grid=(rows.n_tiles, n // tn),
        in_specs=[pl.BlockSpec((tm, d), lambda i, j: (i, 0)),
                  pl.BlockSpec((1, d), lambda i, j: (0, 0)),
                  pl.BlockSpec((None, 6, d), lambda i, j: (rows.mod_index(i), 0, 0)),
                  pl.BlockSpec((d, tn), lambda i, j: (0, j)),
                  pl.BlockSpec((1, tn), lambda i, j: (0, j))],
        out_specs=pl.BlockSpec((tm, tn), lambda i, j: (i, j)),
        scratch_shapes=[pltpu.VMEM((tm, d), BF16)],
        compiler_params=_cparams("parallel", "arbitrary"),
        name="in_proj",
    )(x, g.reshape(1, d), mods, w, b.reshape(1, n))


def _out_kernel(a_ref, w_ref, b_ref, x_ref, mod_ref, o_ref):
    y = _dot(a_ref[...], w_ref[...]) + b_ref[...]
    o_ref[...] = x_ref[...] + mod_ref[2:3, :] * y


def _out_proj(a, w, b, x, mods, rows):
    t, k = a.shape
    d = w.shape[1]
    tm = rows.tm
    return pl.pallas_call(
        _out_kernel,
        out_shape=jax.ShapeDtypeStruct((t, d), F32),
        grid=(rows.n_tiles,),
        in_specs=[pl.BlockSpec((tm, k), lambda i: (i, 0)),
                  pl.BlockSpec((k, d), lambda i: (0, 0)),
                  pl.BlockSpec((1, d), lambda i: (0, 0)),
                  pl.BlockSpec((tm, d), lambda i: (i, 0)),
                  pl.BlockSpec((None, 6, d), lambda i: (rows.mod_index(i), 0, 0))],
        out_specs=pl.BlockSpec((tm, d), lambda i: (i, 0)),
        compiler_params=_cparams("parallel"),
        name="out_proj",
    )(a, w, b.reshape(1, d), x, mods)


def _ffn_kernel(x_ref, g_ref, mod_ref, wg_ref, wu_ref, wd_ref, o_ref, h_ref, acc_ref):
    j = pl.program_id(1)

    @pl.when(j == 0)
    def _():
        m = mod_ref[...]
        h_ref[...] = _norm_mod(x_ref[...], g_ref[...], m[3:4, :], m[4:5, :]).astype(h_ref.dtype)
        acc_ref[...] = jnp.zeros_like(acc_ref)

    h = h_ref[...]
    gate = jnp.dot(h, wg_ref[...], preferred_element_type=F32)
    up = jnp.dot(h, wu_ref[...], preferred_element_type=F32)
    acc_ref[...] += _dot(_silu(gate) * up, wd_ref[...])

    @pl.when(j == pl.num_programs(1) - 1)
    def _():
        o_ref[...] = x_ref[...] + mod_ref[5:6, :] * acc_ref[...]


def _ffn(x, g, mods, w_gu, w_down, rows):
    t, d = x.shape
    f = w_down.shape[0]
    tm = rows.tm
    tf = _tile(f, 256, V7X_LANES)
    nf = f // tf
    return pl.pallas_call(
        _ffn_kernel,
        out_shape=jax.ShapeDtypeStruct((t, d), F32),
        grid=(rows.n_tiles, nf),
        in_specs=[pl.BlockSpec((tm, d), lambda i, j: (i, 0)),
                  pl.BlockSpec((1, d), lambda i, j: (0, 0)),
                  pl.BlockSpec((None, 6, d), lambda i, j: (rows.mod_index(i), 0, 0)),
                  pl.BlockSpec((d, tf), lambda i, j: (0, j)),
                  pl.BlockSpec((d, tf), lambda i, j: (0, j + nf)),
                  pl.BlockSpec((tf, d), lambda i, j: (j, 0))],
        out_specs=pl.BlockSpec((tm, d), lambda i, j: (i, 0)),
        scratch_shapes=[pltpu.VMEM((tm, d), BF16), pltpu.VMEM((tm, d), F32)],
        compiler_params=_cparams("parallel", "arbitrary"),
        name="ffn",
    )(x, g.reshape(1, d), mods, w_gu, w_gu, w_down)


def _final_kernel(x_ref, g_ref, o_ref):
    x = x_ref[...]
    ms = jnp.mean(x * x, axis=-1, keepdims=True)
    o_ref[...] = x * lax.rsqrt(ms + EPS) * g_ref[...]


def _final_norm(x, g, row0, nrows, tm):
    d = x.shape[1]
    off = row0 // tm
    return pl.pallas_call(
        _final_kernel,
        out_shape=jax.ShapeDtypeStruct((nrows, d), F32),
        grid=(nrows // tm,),
        in_specs=[pl.BlockSpec((tm, d), lambda i: (i + off, 0)),
                  pl.BlockSpec((1, d), lambda i: (0, 0))],
        out_specs=pl.BlockSpec((tm, d), lambda i: (i, 0)),
        compiler_params=_cparams("parallel"),
        name="final_norm",
    )(x, g.reshape(1, d))


def _dwconv(x, w, k):
    n = x.shape[0]
    row = lax.broadcasted_iota(jnp.int32, x.shape, 0)
    half = k // 2
    acc = x * w[half:half + 1, :]
    for s in range(-half, half + 1):
        if s == 0:
            continue
        shifted = pltpu.roll(x, (-s) % n, axis=0)
        valid = jnp.logical_and(row + s >= 0, row + s < n)
        acc = acc + jnp.where(valid, shifted, 0.0) * w[s + half:s + half + 1, :]
    return acc


TRI_BASE = 8


def _unit_tri_inverse(a, ri, ci):
    c = a.shape[0]

    def same_block(s):
        sh = int(math.log2(s))
        return (ri >> sh) == (ci >> sh)

    p = jnp.where(same_block(TRI_BASE), -a, 0.0)
    e = p
    n_lvl = int(math.log2(TRI_BASE))
    for lvl in range(n_lvl):
        e = e + _dot_hi(p, e)
        if lvl < n_lvl - 1:
            p = _dot_hi(p, p)
    x = e + (ri == ci).astype(F32)
    s = TRI_BASE
    while s < c:
        off = jnp.where(jnp.logical_and(same_block(2 * s), jnp.logical_not(same_block(s))), a, 0.0)
        x = x - _dot_hi(x, _dot_hi(off, x))
        s *= 2
    return x


def _gdn_kernel(alog_ref, dtb_ref, q_ref, k_ref, v_ref, gt_ref, ab_ref, cq_ref, ck_ref, cv_ref,
                onorm_ref, s0_ref, o_ref, sfin_ref,
                qs, ks, vs, gb, bb, w_s, u_s, qd_s, kt_s, aqk_s, gl_s, o_s, st_s,
                *, n_heads, conv_k):
    h = pl.program_id(1)
    seq, dk = qs.shape
    dv = vs.shape[1]
    c = CHUNK
    n_chunks = seq // c

    q = _silu(_dwconv(q_ref[...], cq_ref[...], conv_k))
    k = _silu(_dwconv(k_ref[...], ck_ref[...], conv_k))
    vs[...] = _silu(_dwconv(v_ref[...], cv_ref[...], conv_k))
    qs[...] = q * lax.rsqrt(jnp.sum(q * q, axis=-1, keepdims=True) + EPS) * (dk ** -0.5)
    ks[...] = k * lax.rsqrt(jnp.sum(k * k, axis=-1, keepdims=True) + EPS)

    ab = ab_ref[...]
    lane = lax.broadcasted_iota(jnp.int32, ab.shape, 1)

    def column(idx):
        col = jnp.sum(jnp.where(lane == idx, ab, 0.0), axis=1, keepdims=True)
        return jnp.broadcast_to(col, (seq, V7X_LANES))

    for d in range(2):
        a_col = column(d * n_heads + h)
        b_col = column(2 * n_heads + d * n_heads + h)
        neg_a = -jnp.exp(jnp.full((1, V7X_LANES), alog_ref[d, h], F32))
        gb[d] = neg_a * jax.nn.softplus(a_col + dtb_ref[d, h])
        bb[d] = jax.nn.sigmoid(b_col)

    ri = lax.broadcasted_iota(jnp.int32, (c, c), 0)
    ci = lax.broadcasted_iota(jnp.int32, (c, c), 1)
    incl = (ri >= ci, ri <= ci)
    strict = (ri > ci, ri < ci)
    last = (c - 1, 0)

    def phase1(ic, carry):
        sl = pl.ds(pl.multiple_of(ic * c, c), c)
        qc, kc, vc = qs[sl, :], ks[sl, :], vs[sl, :]
        for d in range(2):
            beta = bb[d, sl, :]
            tri = incl[d].astype(F32)
            gc = _dot_hi(tri, gb[d, sl, :])
            gc_row = gc.T[:c, :]
            diff = gc[:, :c] - gc_row
            decay = jnp.where(incl[d], jnp.exp(jnp.where(incl[d], diff, 0.0)), 0.0)
            g_tot = gc[last[d]:last[d] + 1, :]
            eg = jnp.exp(gc)
            kb = kc * beta[:, :dk]
            kbg = kb * eg[:, :dk]
            vb = vc * beta[:, :dv]
            a_kk = jnp.where(strict[d], _dot_nt(kb, kc) * decay, 0.0)
            a_qk = _dot_nt(qc, kc) * decay
            t_inv = _unit_tri_inverse(a_kk, ri, ci)
            w_s[d, sl, :] = _dot_hi(t_inv, kbg)
            u_s[d, sl, :] = _dot_hi(t_inv, vb)
            qd_s[d, sl, :] = qc * eg[:, :dk]
            kt = kc * jnp.exp(g_tot - gc)[:, :dk]
            kt_s[d, ic] = kt.T
            aqk_s[d, sl, :] = a_qk
            gl_s[d, pl.ds(ic, 1), :] = jnp.exp(g_tot)
        return carry

    lax.fori_loop(0, n_chunks, phase1, 0)

    st_s[...] = s0_ref[...]

    def phase2(i, carry):
        for d in range(2):
            ic = i if d == 0 else n_chunks - 1 - i
            sl = pl.ds(pl.multiple_of(ic * c, c), c)
            s = st_s[d]
            v_new = u_s[d, sl, :] - _dot(w_s[d, sl, :], s)
            o_s[d, sl, :] = _dot(qd_s[d, sl, :], s) + _dot(aqk_s[d, sl, :], v_new)
            st_s[d] = s * gl_s[d, pl.ds(ic, 1), :][:, :dv] + _dot(kt_s[d, ic], v_new)
        return carry

    lax.fori_loop(0, n_chunks, phase2, 0)

    sfin_ref[...] = st_s[...]
    o = o_s[0] + o_s[1]
    o = o * lax.rsqrt(jnp.mean(o * o, axis=-1, keepdims=True) + EPS)
    o_ref[...] = o * onorm_ref[...] * _silu(gt_ref[...])


def _gdn_core(proj, conv_w, a_log, dt_bias, onorm, s0, *, row0, n_seq, seq, n_heads, dk, dv, s0_index):
    assert dk == dv and dk % V7X_LANES == 0 and seq % CHUNK == 0 and row0 % seq == 0
    conv_k = conv_w.shape[0]
    r0 = row0 // seq
    qk_blocks = n_heads * dk // dk
    n_chunks = seq // CHUNK
    smem = pl.BlockSpec(memory_space=pltpu.SMEM)
    if s0 is None:
        s0 = jnp.zeros((2, dk, dv), F32)
        s0_spec = pl.BlockSpec((2, dk, dv), lambda b, h: (0, 0, 0))
    else:
        s0_spec = s0_index
    col = lambda off: pl.BlockSpec((seq, dk), lambda b, h: (b + r0, off + h))
    cw = lambda off: pl.BlockSpec((conv_k, dk), lambda b, h: (0, off + h))
    f32 = lambda *s: pltpu.VMEM(s, F32)
    return pl.pallas_call(
        functools.partial(_gdn_kernel, n_heads=n_heads, conv_k=conv_k),
        out_shape=(jax.ShapeDtypeStruct((n_seq * seq, n_heads * dv), F32),
                   jax.ShapeDtypeStruct((n_seq, 2, n_heads, dk, dv), F32)),
        grid=(n_seq, n_heads),
        in_specs=[smem, smem,
                  col(0), col(qk_blocks), col(2 * qk_blocks), col(3 * qk_blocks),
                  pl.BlockSpec((seq, V7X_LANES), lambda b, h: (b + r0, 4 * qk_blocks)),
                  cw(0), cw(qk_blocks), cw(2 * qk_blocks),
                  pl.BlockSpec((1, dv), lambda b, h: (0, 0)),
                  s0_spec],
        out_specs=(pl.BlockSpec((seq, dv), lambda b, h: (b, h)),
                   pl.BlockSpec((None, 2, None, dk, dv), lambda b, h: (b, 0, h, 0, 0))),
        scratch_shapes=[f32(seq, dk), f32(seq, dk), f32(seq, dv),
                        f32(2, seq, V7X_LANES), f32(2, seq, V7X_LANES),
                        f32(2, seq, dk), f32(2, seq, dv), f32(2, seq, dk),
                        f32(2, n_chunks, dk, CHUNK), f32(2, seq, CHUNK),
                        f32(2, max(n_chunks, V7X_SUBLANES), V7X_LANES),
                        f32(2, seq, dv), f32(2, dk, dv)],
        compiler_params=_cparams("parallel", "parallel"),
        name="gdn_core",
    )(a_log, dt_bias, proj, proj, proj, proj, proj, conv_w, conv_w, conv_w, onorm.reshape(1, dv), s0)


def _odd_dft(seq):
    k = jnp.arange(seq, dtype=jnp.int32)[:, None]
    m = jnp.arange(seq, dtype=jnp.int32)[None, :]
    r = ((2 * k + 1) * m) % (4 * seq)
    ang = r.astype(F32) * (math.pi / (2 * seq))
    return jnp.cos(ang), jnp.sin(ang)


def _filter_kernel(feat_ref, w1_ref, b1_ref, w2_ref, b2_ref, fr_ref, w3f_ref, w3b_ref, dl_ref,
                   cos_ref, sin_ref, hre_ref, him_ref):
    feat = feat_ref[...]
    fr = fr_ref[...]
    hid = jnp.sin(fr * (_dot(feat, w1_ref[...]) + b1_ref[...]))
    hid = jnp.sin(fr * (_dot(hid, w2_ref[...]) + b2_ref[...]))
    window = jnp.exp(-feat[:, 0:1] * dl_ref[...])
    hf = _dot(hid, w3f_ref[...]) * window
    hb = _dot(hid, w3b_ref[...]) * window
    row = lax.broadcasted_iota(jnp.int32, hb.shape, 0)
    hb = jnp.where(row == 0, 0.0, hb)
    hre_ref[...] = _dot_hi(cos_ref[...], hf + hb)
    him_ref[...] = _dot_hi(sin_ref[...], hb - hf)


def _hyena_filters(seq, d, w1, b1, w2, b2, w3, freq, cos_m, sin_m):
    emb, hid = w1.shape
    bands = (emb - 1) // 2
    t = jnp.linspace(0.0, 1.0, seq, dtype=F32)[:, None]
    wpos = (2.0 * math.pi / seq) * jnp.arange(seq, dtype=F32)[:, None]
    fb = jnp.linspace(1e-4, bands - 1, bands, dtype=F32)[None, :]
    feat = jnp.concatenate([t, jnp.cos(fb * wpos), -jnp.sin(fb * wpos)], axis=-1)
    feat = jnp.pad(feat, ((0, 0), (0, V7X_LANES - emb)))
    w1p = jnp.pad(w1, ((0, V7X_LANES - emb), (0, 0)))
    max_decay = math.log(HY_TARGET) / HY_SHORT_PCT
    min_decay = math.log(HY_TARGET) / HY_LONG_PCT
    deltas = jnp.abs(jnp.linspace(min_decay, max_decay, d, dtype=F32))[None, :]
    tc = _tile(d, 256, V7X_LANES)
    nt = d // tc
    full = lambda r, c: pl.BlockSpec((r, c), lambda n, j: (0, 0))
    out_spec = pl.BlockSpec((None, seq, tc), lambda n, j: (n, 0, j))
    return pl.pallas_call(
        _filter_kernel,
        out_shape=(jax.ShapeDtypeStruct((HY_ORDER, seq, d), F32),) * 2,
        grid=(HY_ORDER, nt),
        in_specs=[full(seq, V7X_LANES), full(V7X_LANES, hid), full(1, hid), full(hid, hid), full(1, hid),
                  full(1, hid),
                  pl.BlockSpec((hid, tc), lambda n, j: (0, (2 * n) * nt + j)),
                  pl.BlockSpec((hid, tc), lambda n, j: (0, (2 * n + 1) * nt + j)),
                  pl.BlockSpec((1, tc), lambda n, j: (0, j)),
                  full(seq, seq), full(seq, seq)],
        out_specs=(out_spec, out_spec),
        compiler_params=_cparams("parallel", "parallel"),
        name="hyena_filter",
    )(feat, w1p, b1.reshape(1, hid), w2, b2.reshape(1, hid), freq.reshape(1, hid), w3, w3, deltas,
      cos_m, sin_m)


def _hyconv_kernel(z_ref, x_ref, hre_ref, him_ref, skip_ref, cz_ref, cx_ref, fwd_ref, inv_ref, o_ref,
                   *, conv_z, conv_k):
    seq = z_ref.shape[0]
    z = z_ref[...]
    if conv_z:
        z = _dwconv(z, cz_ref[...], conv_k)
    x = _dwconv(x_ref[...], cx_ref[...], conv_k)
    pq = jnp.dot(fwd_ref[...], z.astype(BF16), preferred_element_type=F32)
    p, q = pq[:seq], pq[seq:]
    hre, him = hre_ref[...], him_ref[...]
    y_spec = jnp.concatenate([p * hre + q * him, p * him - q * hre], axis=0).astype(BF16)
    y = jnp.dot(inv_ref[...], y_spec, preferred_element_type=F32)
    o_ref[...] = x * (y + skip_ref[...] * z)


def _hyconv(z, z_col0, conv_z, proj, x_col0, hre, him, skip, conv_w, fwd, inv, *, row0, n_seq, seq, d):
    conv_k = conv_w.shape[0]
    tc = _tile(d, 512, V7X_LANES)
    nt = d // tc
    r0 = row0 // seq
    zr0 = r0 if conv_z else 0
    zc, xc = z_col0 // tc, x_col0 // tc
    return pl.pallas_call(
        functools.partial(_hyconv_kernel, conv_z=conv_z, conv_k=conv_k),
        out_shape=jax.ShapeDtypeStruct((n_seq * seq, d), F32),
        grid=(nt, n_seq),
        in_specs=[pl.BlockSpec((seq, tc), lambda j, b: (b + zr0, zc + j)),
                  pl.BlockSpec((seq, tc), lambda j, b: (b + r0, xc + j)),
                  pl.BlockSpec((seq, tc), lambda j, b: (0, j)),
                  pl.BlockSpec((seq, tc), lambda j, b: (0, j)),
                  pl.BlockSpec((1, tc), lambda j, b: (0, j)),
                  pl.BlockSpec((conv_k, tc), lambda j, b: (0, zc + j)),
                  pl.BlockSpec((conv_k, tc), lambda j, b: (0, xc + j)),
                  pl.BlockSpec((2 * seq, seq), lambda j, b: (0, 0)),
                  pl.BlockSpec((seq, 2 * seq), lambda j, b: (0, 0))],
        out_specs=pl.BlockSpec((seq, tc), lambda j, b: (b, j)),
        compiler_params=_cparams("parallel", "parallel"),
        name="hyena_conv",
    )(z, proj, hre, him, skip.reshape(1, d), conv_w, conv_w, fwd, inv)


def _grid_pos_emb(n_tokens, d):
    rows = n_tokens // GRID_W
    r, col = jnp.meshgrid(jnp.arange(rows), jnp.arange(GRID_W), indexing='ij')
    quarter = d // 4
    omega = 1.0 / (POS_BASE ** (jnp.arange(quarter, dtype=F32) / quarter))

    def emb1d(p):
        a = p.reshape(-1, 1).astype(F32) * omega[None, :]
        return jnp.concatenate([jnp.sin(a), jnp.cos(a)], axis=-1)

    return jnp.concatenate([emb1d(r), emb1d(col)], axis=-1)


def kernel(x_prompt, x_sample, state_delta, c, c_ctx, ada_w, ada_b, norm1_g, norm2_g, gdn_w_in, gdn_conv, gdn_a_log, gdn_dt_bias, gdn_onorm, gdn_w_out, hy_w_in, hy_b_in, hy_conv, hy_f_w1, hy_f_b1, hy_f_w2, hy_f_b2, hy_f_w3, hy_freq, hy_skip, hy_w_out, hy_b_out, ffn_w_gu, ffn_w_down, final_g):
    bc, lc, d = x_prompt.shape
    bl, ll, _ = x_sample.shape
    depth = ada_w.shape[0]
    n_heads, dk, dv = state_delta.shape[3:]
    qk_w, v_w = n_heads * dk, n_heads * dv
    tc_rows, tl_rows = bc * lc, bl * ll
    t = tc_rows + tl_rows
    assert tc_rows % ll == 0 and ll % lc == 0
    rows = _Rows(tc_rows, ll, t, _tile(math.gcd(tc_rows, ll), 1024, V7X_SUBLANES))
    rows_small = _Rows(tc_rows, ll, t, _tile(math.gcd(tc_rows, ll), 256, V7X_SUBLANES))

    bm = 1 + bl
    bm_pad = -(-bm // V7X_SUBLANES) * V7X_SUBLANES
    cvec = jnp.concatenate([c_ctx[None, :], c, jnp.zeros((bm_pad - bm, d), F32)], axis=0)
    mods_all = _ada(cvec, ada_w, ada_b).reshape(depth, bm_pad, 6, d)

    x = _embed(x_prompt.reshape(tc_rows, d), x_sample.reshape(tl_rows, d), _grid_pos_emb(ll, d), rows_small)

    n_hy = hy_w_in.shape[0]
    dft = {}
    for seq in (lc, ll):
        cos_m, sin_m = _odd_dft(seq)
        fwd = jnp.concatenate([cos_m, sin_m], axis=0).astype(BF16)
        inv = (jnp.concatenate([cos_m.T, -sin_m.T], axis=1) / seq).astype(BF16)
        spectra = [_hyena_filters(seq, d, hy_f_w1[j], hy_f_b1[j], hy_f_w2[j], hy_f_b2[j], hy_f_w3[j],
                                  hy_freq[j], cos_m, sin_m) for j in range(n_hy)]
        dft[seq] = (fwd, inv, spectra)

    zero_b = jnp.zeros((d,), F32)
    new_states = []
    n_mixers = 2
    for layer in range(depth):
        mods = mods_all[layer]
        j = layer // n_mixers
        if layer % n_mixers == 0:
            ab_pad = V7X_LANES - 4 * n_heads
            w_in = jnp.pad(gdn_w_in[j], ((0, 0), (0, ab_pad))).astype(BF16)
            proj = _in_proj(x, norm1_g[layer], mods, w_in, jnp.zeros((w_in.shape[1],), F32), rows)
            gdn = dict(n_heads=n_heads, dk=dk, dv=dv)
            oc, s_ctx = _gdn_core(proj, gdn_conv[j], gdn_a_log[j], gdn_dt_bias[j], gdn_onorm[j], None,
                                  row0=0, n_seq=bc, seq=lc, s0_index=None, **gdn)
            s0_index = pl.BlockSpec((None, None, 2, None, dk, dv), lambda b, h, j=j: (b, j, 0, h, 0, 0))
            ol, _ = _gdn_core(proj, gdn_conv[j], gdn_a_log[j], gdn_dt_bias[j], gdn_onorm[j], state_delta,
                              row0=tc_rows, n_seq=bl, seq=ll, s0_index=s0_index, **gdn)
            new_states.append(s_ctx)
            mixed = jnp.concatenate([oc, ol], axis=0)
            x = _out_proj(mixed, gdn_w_out[j].astype(BF16), zero_b, x, mods, rows)
        else:
            proj = _in_proj(x, norm1_g[layer], mods, hy_w_in[j].astype(BF16), hy_b_in[j], rows)
            outs = []
            for row0, n_seq, seq in ((0, bc, lc), (tc_rows, bl, ll)):
                fwd, inv, spectra = dft[seq]
                hre, him = spectra[j]
                z = proj
                for n in range(HY_ORDER):
                    z = _hyconv(z, 0, n == 0, proj, (n + 1) * d, hre[n], him[n], hy_skip[j, n], hy_conv[j],
                                fwd, inv, row0=row0, n_seq=n_seq, seq=seq, d=d)
                outs.append(z)
            mixed = jnp.concatenate(outs, axis=0)
            x = _out_proj(mixed, hy_w_out[j].astype(BF16), hy_b_out[j], x, mods, rows)
        x = _ffn(x, norm2_g[layer], mods, ffn_w_gu[layer].astype(BF16), ffn_w_down[layer].astype(BF16), rows)

    tm_out = rows_small.tm
    y_prompt = _final_norm(x, final_g, 0, tc_rows, tm_out).reshape(bc, lc, d)
    y_sample = _final_norm(x, final_g, tc_rows, tl_rows, tm_out).reshape(bl, ll, d)
    new_state_delta = jnp.stack(new_states, axis=1)
    return (y_prompt, y_sample, new_state_delta)
```

```python
import functools
import math

import jax
import jax.numpy as jnp
import numpy as np
from jax import lax
from jax.experimental import pallas as pl
from jax.experimental.pallas import tpu as pltpu

GRID_W = 64
CHUNK = 64
HY_ORDER = 2
HY_TARGET = 1e-2
HY_SHORT_PCT = 0.3
HY_LONG_PCT = 1.5
POS_BASE = 10000.0
EPS = 1e-6

V7X_LANES = 128
V7X_SUBLANES = 8
V7X_VMEM_LIMIT_BYTES = 48 * 1024 * 1024

BF16 = jnp.bfloat16
F32 = jnp.float32
HIGHEST = lax.Precision.HIGHEST


def _cparams(*sem):
    return pltpu.CompilerParams(dimension_semantics=sem, vmem_limit_bytes=V7X_VMEM_LIMIT_BYTES)


def _tile(n, target, align):
    if n <= target:
        return n
    best = None
    for t in range(align, target + 1, align):
        if n % t == 0:
            best = t
    assert best is not None, (n, target, align)
    return best


def _dot(a, b):
    return jnp.dot(a.astype(BF16), b.astype(BF16), preferred_element_type=F32)


def _dot_nt(a, b):
    return lax.dot_general(a.astype(BF16), b.astype(BF16), (((1,), (1,)), ((), ())),
                           preferred_element_type=F32)


def _dot_hi(a, b):
    return jnp.dot(a, b, preferred_element_type=F32, precision=HIGHEST)


def _silu(x):
    return x * jax.nn.sigmoid(x)


def _norm_mod(x, g, shift, scale):
    ms = jnp.mean(x * x, axis=-1, keepdims=True)
    return (x * lax.rsqrt(ms + EPS) * g) * (1.0 + scale) + shift


class _Rows:
    def __init__(self, tc, ll, t, tm):
        assert tc % tm == 0 and ll % tm == 0 and t % tm == 0
        self.n_ctx_tiles = tc // tm
        self.tiles_per_lat = ll // tm
        self.n_tiles = t // tm
        self.tm = tm

    def mod_index(self, i):
        lat = 1 + (i - self.n_ctx_tiles) // self.tiles_per_lat
        return jnp.where(i < self.n_ctx_tiles, 0, lat)


def _ada_kernel(c_ref, w_ref, b_ref, o_ref):
    o_ref[...] = _dot(_silu(c_ref[...]), w_ref[...]) + b_ref[...]


def _ada(cvec, ada_w, ada_b):
    depth, d, n = ada_w.shape
    bm = cvec.shape[0]
    tn = _tile(n, 1536, V7X_LANES)
    return pl.pallas_call(
        _ada_kernel,
        out_shape=jax.ShapeDtypeStruct((depth, bm, n), F32),
        grid=(depth, n // tn),
        in_specs=[pl.BlockSpec((bm, d), lambda l, j: (0, 0)),
                  pl.BlockSpec((None, d, tn), lambda l, j: (l, 0, j)),
                  pl.BlockSpec((None, 1, tn), lambda l, j: (l, 0, j))],
        out_specs=pl.BlockSpec((None, bm, tn), lambda l, j: (l, 0, j)),
        compiler_params=_cparams("parallel", "parallel"),
        name="ada",
    )(cvec, ada_w, ada_b.reshape(depth, 1, n))


def _embed_kernel(xp_ref, xs_ref, pos_ref, o_ref, *, n_ctx_tiles):
    i = pl.program_id(0)

    @pl.when(i < n_ctx_tiles)
    def _():
        o_ref[...] = xp_ref[...]

    @pl.when(i >= n_ctx_tiles)
    def _():
        o_ref[...] = xs_ref[...] + pos_ref[...]


def _embed(xp, xs, pos, rows):
    t, d = xp.shape[0] + xs.shape[0], xp.shape[1]
    tm, nct = rows.tm, rows.n_ctx_tiles
    npos = pos.shape[0] // tm
    return pl.pallas_call(
        functools.partial(_embed_kernel, n_ctx_tiles=nct),
        out_shape=jax.ShapeDtypeStruct((t, d), F32),
        grid=(rows.n_tiles,),
        in_specs=[pl.BlockSpec((tm, d), lambda i: (jnp.minimum(i, nct - 1), 0)),
                  pl.BlockSpec((tm, d), lambda i: (jnp.maximum(i - nct, 0), 0)),
                  pl.BlockSpec((tm, d), lambda i: (jnp.maximum(i - nct, 0) % npos, 0))],
        out_specs=pl.BlockSpec((tm, d), lambda i: (i, 0)),
        compiler_params=_cparams("parallel"),
        name="embed",
    )(xp, xs, pos)


def _in_kernel(x_ref, g_ref, mod_ref, w_ref, b_ref, o_ref, h_ref):
    @pl.when(pl.program_id(1) == 0)
    def _():
        m = mod_ref[...]
        h_ref[...] = _norm_mod(x_ref[...], g_ref[...], m[0:1, :], m[1:2, :]).astype(h_ref.dtype)

    o_ref[...] = jnp.dot(h_ref[...], w_ref[...], preferred_element_type=F32) + b_ref[...]


def _in_proj(x, g, mods, w, b, rows):
    t, d = x.shape
    n = w.shape[1]
    tm = rows.tm
    tn = _tile(n, 1536, V7X_LANES)
    return pl.pallas_call(
        _in_kernel,
        out_shape=jax.ShapeDtypeStruct((t, n), F32),
        grid=(rows.n_tiles, n // tn),
        in_specs=[pl.BlockSpec((tm, d), lambda i, j: (i, 0)),
                  pl.BlockSpec((1, d), lambda i, j: (0, 0)),
                  pl.BlockSpec((None, 6, d), lambda i, j: (rows.mod_index(i), 0, 0)),
                  pl.BlockSpec((d, tn), lambda i, j: (0, j)),
                  pl.BlockSpec((1, tn), lambda i, j: (0, j))],
        out_specs=pl.BlockSpec((tm, tn), lambda i, j: (i, j)),
        scratch_shapes=[pltpu.VMEM((tm, d), BF16)],
        compiler_params=_cparams("parallel", "arbitrary"),
        name="in_proj",
    )(x, g.reshape(1, d), mods, w, b.reshape(1, n))


def _out_kernel(a_ref, w_ref, b_ref, x_ref, mod_ref, o_ref):
    y = _dot(a_ref[...], w_ref[...]) + b_ref[...]
    o_ref[...] = x_ref[...] + mod_ref[2:3, :] * y


def _out_proj(a, w, b, x, mods, rows):
    t, k = a.shape
    d = w.shape[1]
    tm = rows.tm
    return pl.pallas_call(
        _out_kernel,
        out_shape=jax.ShapeDtypeStruct((t, d), F32),
        grid=(rows.n_tiles,),
        in_specs=[pl.BlockSpec((tm, k), lambda i: (i, 0)),
                  pl.BlockSpec((k, d), lambda i: (0, 0)),
                  pl.BlockSpec((1, d), lambda i: (0, 0)),
                  pl.BlockSpec((tm, d), lambda i: (i, 0)),
                  pl.BlockSpec((None, 6, d), lambda i: (rows.mod_index(i), 0, 0))],
        out_specs=pl.BlockSpec((tm, d), lambda i: (i, 0)),
        compiler_params=_cparams("parallel"),
        name="out_proj",
    )(a, w, b.reshape(1, d), x, mods)


def _ffn_kernel(x_ref, g_ref, mod_ref, wg_ref, wu_ref, wd_ref, o_ref, h_ref, acc_ref):
    j = pl.program_id(1)

    @pl.when(j == 0)
    def _():
        m = mod_ref[...]
        h_ref[...] = _norm_mod(x_ref[...], g_ref[...], m[3:4, :], m[4:5, :]).astype(h_ref.dtype)
        acc_ref[...] = jnp.zeros_like(acc_ref)

    h = h_ref[...]
    gate = jnp.dot(h, wg_ref[...], preferred_element_type=F32)
    up = jnp.dot(h, wu_ref[...], preferred_element_type=F32)
    acc_ref[...] += _dot(_silu(gate) * up, wd_ref[...])

    @pl.when(j == pl.num_programs(1) - 1)
    def _():
        o_ref[...] = x_ref[...] + mod_ref[5:6, :] * acc_ref[...]


def _ffn(x, g, mods, w_gu, w_down, rows):
    t, d = x.shape
    f = w_down.shape[0]
    tm = rows.tm
    tf = _tile(f, 256, V7X_LANES)
    nf = f // tf
    return pl.pallas_call(
        _ffn_kernel,
        out_shape=jax.ShapeDtypeStruct((t, d), F32),
        grid=(rows.n_tiles, nf),
        in_specs=[pl.BlockSpec((tm, d), lambda i, j: (i, 0)),
                  pl.BlockSpec((1, d), lambda i, j: (0, 0)),
                  pl.BlockSpec((None, 6, d), lambda i, j: (rows.mod_index(i), 0, 0)),
                  pl.BlockSpec((d, tf), lambda i, j: (0, j)),
                  pl.BlockSpec((d, tf), lambda i, j: (0, j + nf)),
                  pl.BlockSpec((tf, d), lambda i, j: (j, 0))],
        out_specs=pl.BlockSpec((tm, d), lambda i, j: (i, 0)),
        scratch_shapes=[pltpu.VMEM((tm, d), BF16), pltpu.VMEM((tm, d), F32)],
        compiler_params=_cparams("parallel", "arbitrary"),
        name="ffn",
    )(x, g.reshape(1, d), mods, w_gu, w_gu, w_down)


def _final_kernel(x_ref, g_ref, o_ref):
    x = x_ref[...]
    ms = jnp.mean(x * x, axis=-1, keepdims=True)
    o_ref[...] = x * lax.rsqrt(ms + EPS) * g_ref[...]


def _final_norm(x, g, row0, nrows, tm):
    d = x.shape[1]
    off = row0 // tm
    return pl.pallas_call(
        _final_kernel,
        out_shape=jax.ShapeDtypeStruct((nrows, d), F32),
        grid=(nrows // tm,),
        in_specs=[pl.BlockSpec((tm, d), lambda i: (i + off, 0)),
                  pl.BlockSpec((1, d), lambda i: (0, 0))],
        out_specs=pl.BlockSpec((tm, d), lambda i: (i, 0)),
        compiler_params=_cparams("parallel"),
        name="final_norm",
    )(x, g.reshape(1, d))


def _dwconv(x, w, k):
    n = x.shape[0]
    row = lax.broadcasted_iota(jnp.int32, x.shape, 0)
    half = k // 2
    acc = x * w[half:half + 1, :]
    for s in range(-half, half + 1):
        if s == 0:
            continue
        shifted = pltpu.roll(x, (-s) % n, axis=0)
        valid = jnp.logical_and(row + s >= 0, row + s < n)
        acc = acc + jnp.where(valid, shifted, 0.0) * w[s + half:s + half + 1, :]
    return acc


TRI_BASE = 8
GDN_LOCKSTEP_CHUNKS = 4


def _unit_tri_inverses_minus_eye(mats, ri, ci):
    c = mats[0].shape[0]

    def same_block(s):
        sh = int(math.log2(s))
        return (ri >> sh) == (ci >> sh)

    ps = [jnp.where(same_block(TRI_BASE), -a, 0.0) for a in mats]
    es = ps
    n_lvl = int(math.log2(TRI_BASE))
    for lvl in range(n_lvl):
        es = [e + _dot(p, e) for p, e in zip(ps, es)]
        if lvl < n_lvl - 1:
            ps = [_dot(p, p) for p in ps]
    s = TRI_BASE
    while s < c:
        mask = jnp.logical_and(same_block(2 * s), jnp.logical_not(same_block(s)))
        offs = [jnp.where(mask, a, 0.0) for a in mats]
        ys = [off + _dot(off, e) for off, e in zip(offs, es)]
        es = [e - (y + _dot(e, y)) for e, y in zip(es, ys)]
        s *= 2
    return es


def _chunk_cumsum(x, pos, reverse):
    n = x.shape[0]
    s = 1
    while s < CHUNK:
        if reverse:
            x = x + jnp.where(pos + s < CHUNK, pltpu.roll(x, n - s, axis=0), 0.0)
        else:
            x = x + jnp.where(pos >= s, pltpu.roll(x, s, axis=0), 0.0)
        s *= 2
    return x


def _gdn_kernel(alog_ref, dtb_ref, q_ref, k_ref, v_ref, gt_ref, ab_ref, cq_ref, ck_ref, cv_ref,
                onorm_ref, s0_ref, o_ref, sfin_ref,
                gates_s, qs, ks, gc_s, kb_s, w_s, u_s, qd_s, ktl_s, kt_s, aqk_s, gl_s, o_s, st_s,
                *, n_heads, conv_k, lockstep):
    h = pl.program_id(1)
    seq, dk = qs.shape
    c = CHUNK
    n_chunks = seq // c

    @pl.when(h == 0)
    def _():
        ab = ab_ref[...]
        pos = jnp.bitwise_and(lax.broadcasted_iota(jnp.int32, ab.shape, 0), c - 1)
        g = -jnp.exp(alog_ref[...]) * jax.nn.softplus(ab + dtb_ref[...])
        g_fwd = _chunk_cumsum(g, pos, reverse=False)
        g_rev = _chunk_cumsum(g, pos, reverse=True)
        gates_s[0] = g_fwd
        gates_s[1] = g_rev - g
        gates_s[2] = g_rev
        gates_s[3] = g_fwd - g
        gates_s[4] = jax.nn.sigmoid(ab)

    q = _silu(_dwconv(q_ref[...], cq_ref[...], conv_k))
    k = _silu(_dwconv(k_ref[...], ck_ref[...], conv_k))
    v = _silu(_dwconv(v_ref[...], cv_ref[...], conv_k))
    q = q * lax.rsqrt(jnp.sum(q * q, axis=-1, keepdims=True) + EPS) * (dk ** -0.5)
    k = k * lax.rsqrt(jnp.sum(k * k, axis=-1, keepdims=True) + EPS)
    qs[...] = q
    ks[...] = k

    lane = lax.broadcasted_iota(jnp.int32, (seq, V7X_LANES), 1)

    def column(i, idx):
        col = jnp.sum(jnp.where(lane == idx, gates_s[i], 0.0), axis=1, keepdims=True)
        return jnp.broadcast_to(col, (seq, V7X_LANES))

    for d in range(2):
        g_cum = column(2 * d, d * n_heads + h)
        g_tail = column(2 * d + 1, d * n_heads + h)
        beta = column(4, 2 * n_heads + d * n_heads + h)
        e_cum = jnp.exp(g_cum)
        kb = k * beta
        gc_s[d] = g_cum
        kb_s[d] = kb
        w_s[d] = kb * e_cum
        u_s[d] = v * beta
        qd_s[d] = q * e_cum
        ktl_s[d] = k * jnp.exp(g_tail)
        gl_s[d] = jnp.exp(g_cum + g_tail)

    ri = lax.broadcasted_iota(jnp.int32, (c, c), 0)
    ci = lax.broadcasted_iota(jnp.int32, (c, c), 1)
    incl = (ri >= ci, ri <= ci)
    strict = (ri > ci, ri < ci)

    def phase1(it, carry):
        loaded = []
        for gi in range(lockstep):
            ic = it * lockstep + gi
            sl = pl.ds(pl.multiple_of(ic * c, c), c)
            qc, kc = qs[sl, :], ks[sl, :]
            for d in range(2):
                loaded.append((ic, sl, d, qc, kc, gc_s[d, sl, :], kb_s[d, sl, :], w_s[d, sl, :], u_s[d, sl, :],
                               ktl_s[d, sl, :]))
        decays, kqs = [], []
        for ic, sl, d, qc, kc, gc, kb, w0, u0, ktl in loaded:
            diff = gc[:, :c] - gc.T[:c, :]
            decays.append(jnp.where(incl[d], jnp.exp(jnp.where(incl[d], diff, 0.0)), 0.0))
            kqs.append(_dot_nt(jnp.concatenate([kb, qc], axis=0), kc))
        a_kks = [jnp.where(strict[item[2]], kq[:c] * decay, 0.0)
                 for item, kq, decay in zip(loaded, kqs, decays)]
        es = _unit_tri_inverses_minus_eye(a_kks, ri, ci)
        rhss = [jnp.concatenate([item[7], item[8]], axis=1) for item in loaded]
        wus = [rhs + _dot(e, rhs) for e, rhs in zip(es, rhss)]
        for item, kq, decay, wu in zip(loaded, kqs, decays, wus):
            ic, sl, d = item[:3]
            aqk_s[d, sl, :] = kq[c:] * decay
            w_s[d, sl, :] = wu[:, :dk]
            u_s[d, sl, :] = wu[:, dk:]
            kt_s[d, ic] = item[9].T
        return carry

    lax.fori_loop(0, n_chunks // lockstep, phase1, 0)

    st_s[...] = s0_ref[...]

    def phase2(i, carry):
        ics = (i, n_chunks - 1 - i)
        r0s = [pl.multiple_of(ic * c, c) for ic in ics]
        sls = [pl.ds(r0, c) for r0 in r0s]
        ss = [st_s[d] for d in range(2)]
        wqs = [_dot(jnp.concatenate([w_s[d, sls[d], :], qd_s[d, sls[d], :]], axis=0), ss[d]) for d in range(2)]
        v_news = [u_s[d, sls[d], :] - wqs[d][:c] for d in range(2)]
        intra = [_dot(aqk_s[d, sls[d], :], v_news[d]) for d in range(2)]
        upd = [_dot(kt_s[d, ics[d]], v_news[d]) for d in range(2)]
        for d in range(2):
            o_s[d, sls[d], :] = wqs[d][c:] + intra[d]
            st_s[d] = ss[d] * gl_s[d, pl.ds(r0s[d], 1), :] + upd[d]
        return carry

    lax.fori_loop(0, n_chunks, phase2, 0)

    sfin_ref[...] = st_s[...]
    o = o_s[0] + o_s[1]
    o = o * lax.rsqrt(jnp.mean(o * o, axis=-1, keepdims=True) + EPS)
    o_ref[...] = o * onorm_ref[...] * _silu(gt_ref[...])


def _gdn_core(proj, conv_w, a_log, dt_bias, onorm, s0, *, row0, n_seq, seq, n_heads, dk, dv, s0_index):
    assert dk == dv == V7X_LANES and seq % (2 * CHUNK) == 0 and row0 % seq == 0
    conv_k = conv_w.shape[0]
    r0 = row0 // seq
    qk_blocks = n_heads * dk // dk
    n_chunks = seq // CHUNK
    lockstep = _tile(n_chunks, GDN_LOCKSTEP_CHUNKS, 1)
    gate_pad = lambda p: jnp.pad(p.reshape(1, 2 * n_heads), ((0, 0), (0, V7X_LANES - 2 * n_heads)))
    lane_vec = pl.BlockSpec((1, V7X_LANES), lambda b, h: (0, 0))
    if s0 is None:
        s0 = jnp.zeros((2, dk, dv), F32)
        s0_spec = pl.BlockSpec((2, dk, dv), lambda b, h: (0, 0, 0))
    else:
        s0_spec = s0_index
    col = lambda off: pl.BlockSpec((seq, dk), lambda b, h: (b + r0, off + h))
    cw = lambda off: pl.BlockSpec((conv_k, dk), lambda b, h: (0, off + h))
    f32 = lambda *s: pltpu.VMEM(s, F32)
    return pl.pallas_call(
        functools.partial(_gdn_kernel, n_heads=n_heads, conv_k=conv_k, lockstep=lockstep),
        out_shape=(jax.ShapeDtypeStruct((n_seq * seq, n_heads * dv), F32),
                   jax.ShapeDtypeStruct((n_seq, 2, n_heads, dk, dv), F32)),
        grid=(n_seq, n_heads),
        in_specs=[lane_vec, lane_vec,
                  col(0), col(qk_blocks), col(2 * qk_blocks), col(3 * qk_blocks),
                  pl.BlockSpec((seq, V7X_LANES), lambda b, h: (b + r0, 4 * qk_blocks)),
                  cw(0), cw(qk_blocks), cw(2 * qk_blocks),
                  pl.BlockSpec((1, dv), lambda b, h: (0, 0)),
                  s0_spec],
        out_specs=(pl.BlockSpec((seq, dv), lambda b, h: (b, h)),
                   pl.BlockSpec((None, 2, None, dk, dv), lambda b, h: (b, 0, h, 0, 0))),
        scratch_shapes=[f32(5, seq, V7X_LANES),
                        f32(seq, dk), f32(seq, dk),
                        f32(2, seq, V7X_LANES), f32(2, seq, dk),
                        f32(2, seq, dk), f32(2, seq, dv), f32(2, seq, dk),
                        f32(2, seq, dk), f32(2, n_chunks, dk, CHUNK),
                        f32(2, seq, CHUNK), f32(2, seq, V7X_LANES),
                        f32(2, seq, dv), f32(2, dk, dv)],
        compiler_params=_cparams("parallel", "arbitrary"),
        name="gdn_core",
    )(gate_pad(a_log), gate_pad(dt_bias), proj, proj, proj, proj, proj, conv_w, conv_w, conv_w, onorm.reshape(1, dv), s0)


def _odd_dft(seq):
    k = jnp.arange(seq, dtype=jnp.int32)[:, None]
    m = jnp.arange(seq, dtype=jnp.int32)[None, :]
    r = ((2 * k + 1) * m) % (4 * seq)
    ang = r.astype(F32) * (math.pi / (2 * seq))
    return jnp.cos(ang), jnp.sin(ang)


def _filter_kernel(feat_ref, w1_ref, b1_ref, w2_ref, b2_ref, fr_ref, w3f_ref, w3b_ref, dl_ref,
                   cos_ref, sin_ref, hre_ref, him_ref):
    feat = feat_ref[...]
    fr = fr_ref[...]
    hid = jnp.sin(fr * (_dot(feat, w1_ref[...]) + b1_ref[...]))
    hid = jnp.sin(fr * (_dot(hid, w2_ref[...]) + b2_ref[...]))
    window = jnp.exp(-feat[:, 0:1] * dl_ref[...])
    hf = _dot(hid, w3f_ref[...]) * window
    hb = _dot(hid, w3b_ref[...]) * window
    row = lax.broadcasted_iota(jnp.int32, hb.shape, 0)
    hb = jnp.where(row == 0, 0.0, hb)
    hre_ref[...] = _dot_hi(cos_ref[...], hf + hb)
    him_ref[...] = _dot_hi(sin_ref[...], hb - hf)


def _hyena_filters(seq, d, w1, b1, w2, b2, w3, freq, cos_m, sin_m):
    emb, hid = w1.shape
    bands = (emb - 1) // 2
    t = jnp.linspace(0.0, 1.0, seq, dtype=F32)[:, None]
    wpos = (2.0 * math.pi / seq) * jnp.arange(seq, dtype=F32)[:, None]
    fb = jnp.linspace(1e-4, bands - 1, bands, dtype=F32)[None, :]
    feat = jnp.concatenate([t, jnp.cos(fb * wpos), -jnp.sin(fb * wpos)], axis=-1)
    feat = jnp.pad(feat, ((0, 0), (0, V7X_LANES - emb)))
    w1p = jnp.pad(w1, ((0, V7X_LANES - emb), (0, 0)))
    max_decay = math.log(HY_TARGET) / HY_SHORT_PCT
    min_decay = math.log(HY_TARGET) / HY_LONG_PCT
    deltas = jnp.abs(jnp.linspace(min_decay, max_decay, d, dtype=F32))[None, :]
    tc = _tile(d, 256, V7X_LANES)
    nt = d // tc
    full = lambda r, c: pl.BlockSpec((r, c), lambda n, j: (0, 0))
    out_spec = pl.BlockSpec((None, seq, tc), lambda n, j: (n, 0, j))
    return pl.pallas_call(
        _filter_kernel,
        out_shape=(jax.ShapeDtypeStruct((HY_ORDER, seq, d), F32),) * 2,
        grid=(HY_ORDER, nt),
        in_specs=[full(seq, V7X_LANES), full(V7X_LANES, hid), full(1, hid), full(hid, hid), full(1, hid),
                  full(1, hid),
                  pl.BlockSpec((hid, tc), lambda n, j: (0, (2 * n) * nt + j)),
                  pl.BlockSpec((hid, tc), lambda n, j: (0, (2 * n + 1) * nt + j)),
                  pl.BlockSpec((1, tc), lambda n, j: (0, j)),
                  full(seq, seq), full(seq, seq)],
        out_specs=(out_spec, out_spec),
        compiler_params=_cparams("parallel", "parallel"),
        name="hyena_filter",
    )(feat, w1p, b1.reshape(1, hid), w2, b2.reshape(1, hid), freq.reshape(1, hid), w3, w3, deltas,
      cos_m, sin_m)


def _hyconv_kernel(z_ref, x_ref, hre_ref, him_ref, skip_ref, cz_ref, cx_ref, fwd_ref, inv_ref, o_ref,
                   *, conv_z, conv_k):
    seq = z_ref.shape[0]
    z = z_ref[...]
    if conv_z:
        z = _dwconv(z, cz_ref[...], conv_k)
    x = _dwconv(x_ref[...], cx_ref[...], conv_k)
    pq = jnp.dot(fwd_ref[...], z.astype(BF16), preferred_element_type=F32)
    p, q = pq[:seq], pq[seq:]
    hre, him = hre_ref[...], him_ref[...]
    y_spec = jnp.concatenate([p * hre + q * him, p * him - q * hre], axis=0).astype(BF16)
    y = jnp.dot(inv_ref[...], y_spec, preferred_element_type=F32)
    o_ref[...] = x * (y + skip_ref[...] * z)


def _hyconv(z, z_col0, conv_z, proj, x_col0, hre, him, skip, conv_w, fwd, inv, *, row0, n_seq, seq, d):
    conv_k = conv_w.shape[0]
    tc = _tile(d, 512, V7X_LANES)
    nt = d // tc
    r0 = row0 // seq
    zr0 = r0 if conv_z else 0
    zc, xc = z_col0 // tc, x_col0 // tc
    return pl.pallas_call(
        functools.partial(_hyconv_kernel, conv_z=conv_z, conv_k=conv_k),
        out_shape=jax.ShapeDtypeStruct((n_seq * seq, d), F32),
        grid=(nt, n_seq),
        in_specs=[pl.BlockSpec((seq, tc), lambda j, b: (b + zr0, zc + j)),
                  pl.BlockSpec((seq, tc), lambda j, b: (b + r0, xc + j)),
                  pl.BlockSpec((seq, tc), lambda j, b: (0, j)),
                  pl.BlockSpec((seq, tc), lambda j, b: (0, j)),
                  pl.BlockSpec((1, tc), lambda j, b: (0, j)),
                  pl.BlockSpec((conv_k, tc), lambda j, b: (0, zc + j)),
                  pl.BlockSpec((conv_k, tc), lambda j, b: (0, xc + j)),
                  pl.BlockSpec((2 * seq, seq), lambda j, b: (0, 0)),
                  pl.BlockSpec((seq, 2 * seq), lambda j, b: (0, 0))],
        out_specs=pl.BlockSpec((seq, tc), lambda j, b: (b, j)),
        compiler_params=_cparams("parallel", "parallel"),
        name="hyena_conv",
    )(z, proj, hre, him, skip.reshape(1, d), conv_w, conv_w, fwd, inv)


def _grid_pos_emb(n_tokens, d):
    rows = n_tokens // GRID_W
    r, col = jnp.meshgrid(jnp.arange(rows), jnp.arange(GRID_W), indexing='ij')
    quarter = d // 4
    omega = 1.0 / (POS_BASE ** (jnp.arange(quarter, dtype=F32) / quarter))

    def emb1d(p):
        a = p.reshape(-1, 1).astype(F32) * omega[None, :]
        return jnp.concatenate([jnp.sin(a), jnp.cos(a)], axis=-1)

    return jnp.concatenate([emb1d(r), emb1d(col)], axis=-1)


def kernel(x_prompt, x_sample, state_delta, c, c_ctx, ada_w, ada_b, norm1_g, norm2_g, gdn_w_in, gdn_conv, gdn_a_log, gdn_dt_bias, gdn_onorm, gdn_w_out, hy_w_in, hy_b_in, hy_conv, hy_f_w1, hy_f_b1, hy_f_w2, hy_f_b2, hy_f_w3, hy_freq, hy_skip, hy_w_out, hy_b_out, ffn_w_gu, ffn_w_down, final_g):
    bc, lc, d = x_prompt.shape
    bl, ll, _ = x_sample.shape
    depth = ada_w.shape[0]
    n_heads, dk, dv = state_delta.shape[3:]
    qk_w, v_w = n_heads * dk, n_heads * dv
    tc_rows, tl_rows = bc * lc, bl * ll
    t = tc_rows + tl_rows
    assert tc_rows % ll == 0 and ll % lc == 0
    rows = _Rows(tc_rows, ll, t, _tile(math.gcd(tc_rows, ll), 1024, V7X_SUBLANES))
    rows_small = _Rows(tc_rows, ll, t, _tile(math.gcd(tc_rows, ll), 256, V7X_SUBLANES))

    bm = 1 + bl
    bm_pad = -(-bm // V7X_SUBLANES) * V7X_SUBLANES
    cvec = jnp.concatenate([c_ctx[None, :], c, jnp.zeros((bm_pad - bm, d), F32)], axis=0)
    mods_all = _ada(cvec, ada_w, ada_b).reshape(depth, bm_pad, 6, d)

    x = _embed(x_prompt.reshape(tc_rows, d), x_sample.reshape(tl_rows, d), _grid_pos_emb(ll, d), rows_small)

    n_hy = hy_w_in.shape[0]
    dft = {}
    for seq in (lc, ll):
        cos_m, sin_m = _odd_dft(seq)
        fwd = jnp.concatenate([cos_m, sin_m], axis=0).astype(BF16)
        inv = (jnp.concatenate([cos_m.T, -sin_m.T], axis=1) / seq).astype(BF16)
        spectra = [_hyena_filters(seq, d, hy_f_w1[j], hy_f_b1[j], hy_f_w2[j], hy_f_b2[j], hy_f_w3[j],
                                  hy_freq[j], cos_m, sin_m) for j in range(n_hy)]
        dft[seq] = (fwd, inv, spectra)

    zero_b = jnp.zeros((d,), F32)
    new_states = []
    n_mixers = 2
    for layer in range(depth):
        mods = mods_all[layer]
        j = layer // n_mixers
        if layer % n_mixers == 0:
            ab_pad = V7X_LANES - 4 * n_heads
            w_in = jnp.pad(gdn_w_in[j], ((0, 0), (0, ab_pad))).astype(BF16)
            proj = _in_proj(x, norm1_g[layer], mods, w_in, jnp.zeros((w_in.shape[1],), F32), rows)
            gdn = dict(n_heads=n_heads, dk=dk, dv=dv)
            oc, s_ctx = _gdn_core(proj, gdn_conv[j], gdn_a_log[j], gdn_dt_bias[j], gdn_onorm[j], None,
                                  row0=0, n_seq=bc, seq=lc, s0_index=None, **gdn)
            s0_index = pl.BlockSpec((None, None, 2, None, dk, dv), lambda b, h, j=j: (b, j, 0, h, 0, 0))
            ol, _ = _gdn_core(proj, gdn_conv[j], gdn_a_log[j], gdn_dt_bias[j], gdn_onorm[j], state_delta,
                              row0=tc_rows, n_seq=bl, seq=ll, s0_index=s0_index, **gdn)
            new_states.append(s_ctx)
            mixed = jnp.concatenate([oc, ol], axis=0)
            x = _out_proj(mixed, gdn_w_out[j].astype(BF16), zero_b, x, mods, rows)
        else:
            proj = _in_proj(x, norm1_g[layer], mods, hy_w_in[j].astype(BF16), hy_b_in[j], rows)
            outs = []
            for row0, n_seq, seq in ((0, bc, lc), (tc_rows, bl, ll)):
                fwd, inv, spectra = dft[seq]
                hre, him = spectra[j]
                z = proj
                for n in range(HY_ORDER):
                    z = _hyconv(z, 0, n == 0, proj, (n + 1) * d, hre[n], him[n], hy_skip[j, n], hy_conv[j],
                                fwd, inv, row0=row0, n_seq=n_seq, seq=seq, d=d)
                outs.append(z)
            mixed = jnp.concatenate(outs, axis=0)
            x = _out_proj(mixed, hy_w_out[j].astype(BF16), hy_b_out[j], x, mods, rows)
        x = _ffn(x, norm2_g[layer], mods, ffn_w_gu[layer].astype(BF16), ffn_w_down[layer].astype(BF16), rows)

    tm_out = rows_small.tm
    y_prompt = _final_norm(x, final_g, 0, tc_rows, tm_out).reshape(bc, lc, d)
    y_sample = _final_norm(x, final_g, tc_rows, tl_rows, tm_out).reshape(bl, ll, d)
    new_state_delta = jnp.stack(new_states, axis=1)
    return (y_prompt, y_sample, new_state_delta)
```

```python
import functools
import math

import jax
import jax.numpy as jnp
import numpy as np
from jax import lax
from jax.experimental import pallas as pl
from jax.experimental.pallas import tpu as pltpu

GRID_W = 64
CHUNK = 64
HY_ORDER = 2
HY_TARGET = 1e-2
HY_SHORT_PCT = 0.3
HY_LONG_PCT = 1.5
POS_BASE = 10000.0
EPS = 1e-6

V7X_LANES = 128
V7X_SUBLANES = 8
V7X_VMEM_LIMIT_BYTES = 48 * 1024 * 1024

BF16 = jnp.bfloat16
F32 = jnp.float32
HIGHEST = lax.Precision.HIGHEST


def _cparams(*sem):
    return pltpu.CompilerParams(dimension_semantics=sem, vmem_limit_bytes=V7X_VMEM_LIMIT_BYTES)


def _tile(n, target, align):
    if n <= target:
        return n
    best = None
    for t in range(align, target + 1, align):
        if n % t == 0:
            best = t
    assert best is not None, (n, target, align)
    return best


def _dot(a, b):
    return jnp.dot(a.astype(BF16), b.astype(BF16), preferred_element_type=F32)


def _dot_nt(a, b):
    return lax.dot_general(a.astype(BF16), b.astype(BF16), (((1,), (1,)), ((), ())),
                           preferred_element_type=F32)


def _dot_hi(a, b):
    return jnp.dot(a, b, preferred_element_type=F32, precision=HIGHEST)


def _silu(x):
    return x * jax.nn.sigmoid(x)


def _norm_mod(x, g, shift, scale):
    ms = jnp.mean(x * x, axis=-1, keepdims=True)
    return (x * lax.rsqrt(ms + EPS) * g) * (1.0 + scale) + shift


class _Rows:
    def __init__(self, tc, ll, t, tm):
        assert tc % tm == 0 and ll % tm == 0 and t % tm == 0
        self.n_ctx_tiles = tc // tm
        self.tiles_per_lat = ll // tm
        self.n_tiles = t // tm
        self.tm = tm

    def mod_index(self, i):
        lat = 1 + (i - self.n_ctx_tiles) // self.tiles_per_lat
        return jnp.where(i < self.n_ctx_tiles, 0, lat)


def _ada_kernel(c_ref, w_ref, b_ref, o_ref):
    o_ref[...] = _dot(_silu(c_ref[...]), w_ref[...]) + b_ref[...]


def _ada(cvec, ada_w, ada_b):
    depth, d, n = ada_w.shape
    bm = cvec.shape[0]
    tn = _tile(n, 1536, V7X_LANES)
    return pl.pallas_call(
        _ada_kernel,
        out_shape=jax.ShapeDtypeStruct((depth, bm, n), F32),
        grid=(depth, n // tn),
        in_specs=[pl.BlockSpec((bm, d), lambda l, j: (0, 0)),
                  pl.BlockSpec((None, d, tn), lambda l, j: (l, 0, j)),
                  pl.BlockSpec((None, 1, tn), lambda l, j: (l, 0, j))],
        out_specs=pl.BlockSpec((None, bm, tn), lambda l, j: (l, 0, j)),
        compiler_params=_cparams("parallel", "parallel"),
        name="ada",
    )(cvec, ada_w, ada_b.reshape(depth, 1, n))


def _embed_kernel(xp_ref, xs_ref, pos_ref, o_ref, *, n_ctx_tiles):
    i = pl.program_id(0)

    @pl.when(i < n_ctx_tiles)
    def _():
        o_ref[...] = xp_ref[...]

    @pl.when(i >= n_ctx_tiles)
    def _():
        o_ref[...] = xs_ref[...] + pos_ref[...]


def _embed(xp, xs, pos, rows):
    t, d = xp.shape[0] + xs.shape[0], xp.shape[1]
    tm, nct = rows.tm, rows.n_ctx_tiles
    npos = pos.shape[0] // tm
    return pl.pallas_call(
        functools.partial(_embed_kernel, n_ctx_tiles=nct),
        out_shape=jax.ShapeDtypeStruct((t, d), F32),
        grid=(rows.n_tiles,),
        in_specs=[pl.BlockSpec((tm, d), lambda i: (jnp.minimum(i, nct - 1), 0)),
                  pl.BlockSpec((tm, d), lambda i: (jnp.maximum(i - nct, 0), 0)),
                  pl.BlockSpec((tm, d), lambda i: (jnp.maximum(i - nct, 0) % npos, 0))],
        out_specs=pl.BlockSpec((tm, d), lambda i: (i, 0)),
        compiler_params=_cparams("parallel"),
        name="embed",
    )(xp, xs, pos)


_RESIDENT = pl.Buffered(1)


def _in_kernel(x_ref, g_ref, mod_ref, w_ref, b_ref, o_ref, *side_ref, chunk):
    m = mod_ref[...]
    h = _norm_mod(x_ref[...], g_ref[...], m[0:1, :], m[1:2, :]).astype(BF16)
    n = o_ref.shape[1]
    for c0 in range(0, n, chunk):
        cols = slice(c0, c0 + chunk)
        y = jnp.dot(h, w_ref[:, cols], preferred_element_type=F32) + b_ref[:, cols]
        o_ref[:, cols] = y.astype(o_ref.dtype)
    if side_ref:
        side_ref[0][...] = jnp.dot(h, w_ref[:, n:], preferred_element_type=F32) + b_ref[:, n:]


def _in_proj(x, g, mods, w, b, rows, n_side=0):
    t, d = x.shape
    n = w.shape[1] - n_side
    tm = rows.tm
    chunk = _tile(n, 512, V7X_LANES)
    out_shape = [jax.ShapeDtypeStruct((t, n), BF16)]
    out_specs = [pl.BlockSpec((tm, n), lambda i: (i, 0))]
    if n_side:
        out_shape.append(jax.ShapeDtypeStruct((t, n_side), F32))
        out_specs.append(pl.BlockSpec((tm, n_side), lambda i: (i, 0)))
    out = pl.pallas_call(
        functools.partial(_in_kernel, chunk=chunk),
        out_shape=out_shape,
        grid=(rows.n_tiles,),
        in_specs=[pl.BlockSpec((tm, d), lambda i: (i, 0)),
                  pl.BlockSpec((1, d), lambda i: (0, 0), pipeline_mode=_RESIDENT),
                  pl.BlockSpec((None, 6, d), lambda i: (rows.mod_index(i), 0, 0)),
                  pl.BlockSpec((d, n + n_side), lambda i: (0, 0), pipeline_mode=_RESIDENT),
                  pl.BlockSpec((1, n + n_side), lambda i: (0, 0), pipeline_mode=_RESIDENT)],
        out_specs=out_specs,
        compiler_params=_cparams("parallel"),
        name="in_proj",
    )(x, g.reshape(1, d), mods, w, b.reshape(1, n + n_side))
    return out if n_side else out[0]


def _out_kernel(a_ref, w_ref, b_ref, x_ref, mod_ref, o_ref):
    y = _dot(a_ref[...], w_ref[...]) + b_ref[...]
    o_ref[...] = x_ref[...] + mod_ref[2:3, :] * y


def _out_proj(a, w, b, x, mods, rows):
    t, k = a.shape
    d = w.shape[1]
    tm = rows.tm
    return pl.pallas_call(
        _out_kernel,
        out_shape=jax.ShapeDtypeStruct((t, d), F32),
        grid=(rows.n_tiles,),
        in_specs=[pl.BlockSpec((tm, k), lambda i: (i, 0)),
                  pl.BlockSpec((k, d), lambda i: (0, 0)),
                  pl.BlockSpec((1, d), lambda i: (0, 0)),
                  pl.BlockSpec((tm, d), lambda i: (i, 0)),
                  pl.BlockSpec((None, 6, d), lambda i: (rows.mod_index(i), 0, 0))],
        out_specs=pl.BlockSpec((tm, d), lambda i: (i, 0)),
        compiler_params=_cparams("parallel"),
        name="out_proj",
    )(a, w, b.reshape(1, d), x, mods)


def _ffn_kernel(x_ref, g_ref, mod_ref, wgu_ref, wd_ref, o_ref, *, chunk):
    x = x_ref[...]
    m = mod_ref[...]
    h = _norm_mod(x, g_ref[...], m[3:4, :], m[4:5, :]).astype(BF16)
    f = wd_ref.shape[0]
    n_chunks = f // chunk

    def gate_up(k):
        gate = jnp.dot(h, wgu_ref[:, k * chunk:(k + 1) * chunk], preferred_element_type=F32)
        up = jnp.dot(h, wgu_ref[:, f + k * chunk:f + (k + 1) * chunk], preferred_element_type=F32)
        return gate, up

    y = None
    pending = gate_up(0)
    for k in range(n_chunks):
        following = gate_up(k + 1) if k + 1 < n_chunks else None
        act = (_silu(pending[0]) * pending[1]).astype(BF16)
        part = jnp.dot(act, wd_ref[k * chunk:(k + 1) * chunk, :], preferred_element_type=F32)
        y = part if y is None else y + part
        pending = following
    o_ref[...] = x + m[5:6, :] * y


def _ffn(x, g, mods, w_gu, w_down, rows):
    t, d = x.shape
    f = w_down.shape[0]
    tm = rows.tm
    chunk = _tile(f, 256, V7X_LANES)
    return pl.pallas_call(
        functools.partial(_ffn_kernel, chunk=chunk),
        out_shape=jax.ShapeDtypeStruct((t, d), F32),
        grid=(rows.n_tiles,),
        in_specs=[pl.BlockSpec((tm, d), lambda i: (i, 0)),
                  pl.BlockSpec((1, d), lambda i: (0, 0), pipeline_mode=_RESIDENT),
                  pl.BlockSpec((None, 6, d), lambda i: (rows.mod_index(i), 0, 0)),
                  pl.BlockSpec((d, 2 * f), lambda i: (0, 0), pipeline_mode=_RESIDENT),
                  pl.BlockSpec((f, d), lambda i: (0, 0), pipeline_mode=_RESIDENT)],
        out_specs=pl.BlockSpec((tm, d), lambda i: (i, 0)),
        compiler_params=_cparams("parallel"),
        name="ffn",
    )(x, g.reshape(1, d), mods, w_gu, w_down)


def _final_kernel(x_ref, g_ref, o_ref):
    x = x_ref[...]
    ms = jnp.mean(x * x, axis=-1, keepdims=True)
    o_ref[...] = x * lax.rsqrt(ms + EPS) * g_ref[...]


def _final_norm(x, g, row0, nrows, tm):
    d = x.shape[1]
    off = row0 // tm
    return pl.pallas_call(
        _final_kernel,
        out_shape=jax.ShapeDtypeStruct((nrows, d), F32),
        grid=(nrows // tm,),
        in_specs=[pl.BlockSpec((tm, d), lambda i: (i + off, 0)),
                  pl.BlockSpec((1, d), lambda i: (0, 0))],
        out_specs=pl.BlockSpec((tm, d), lambda i: (i, 0)),
        compiler_params=_cparams("parallel"),
        name="final_norm",
    )(x, g.reshape(1, d))


def _dwconv(x, w, k):
    n = x.shape[0]
    row = lax.broadcasted_iota(jnp.int32, x.shape, 0)
    half = k // 2
    acc = x * w[half:half + 1, :]
    for s in range(-half, half + 1):
        if s == 0:
            continue
        shifted = pltpu.roll(x, (-s) % n, axis=0)
        valid = jnp.logical_and(row + s >= 0, row + s < n)
        acc = acc + jnp.where(valid, shifted, 0.0) * w[s + half:s + half + 1, :]
    return acc


TRI_BASE = 8
GDN_LOCKSTEP_CHUNKS = 4
GDN_HEADS_PER_STEP = 2


def _unit_tri_inverses_minus_eye(mats, ri, ci):
    c = mats[0].shape[0]

    def same_block(s):
        sh = int(math.log2(s))
        return (ri >> sh) == (ci >> sh)

    ps = [jnp.where(same_block(TRI_BASE), -a, 0.0) for a in mats]
    es = ps
    n_lvl = int(math.log2(TRI_BASE))
    for lvl in range(n_lvl):
        es = [e + _dot(p, e) for p, e in zip(ps, es)]
        if lvl < n_lvl - 1:
            ps = [_dot(p, p) for p in ps]
    s = TRI_BASE
    while s < c:
        mask = jnp.logical_and(same_block(2 * s), jnp.logical_not(same_block(s)))
        offs = [jnp.where(mask, a, 0.0) for a in mats]
        ys = [off + _dot(off, e) for off, e in zip(offs, es)]
        es = [e - (y + _dot(e, y)) for e, y in zip(es, ys)]
        s *= 2
    return es


def _chunk_cumsum(x, pos, reverse):
    n = x.shape[0]
    s = 1
    while s < CHUNK:
        if reverse:
            x = x + jnp.where(pos + s < CHUNK, pltpu.roll(x, n - s, axis=0), 0.0)
        else:
            x = x + jnp.where(pos >= s, pltpu.roll(x, s, axis=0), 0.0)
        s *= 2
    return x


def _gdn_kernel(*refs, n_heads, conv_k, lockstep, n_kept):
    (alog_ref, dtb_ref, q_ref, k_ref, v_ref, gt_ref, ab_ref, cq_ref, ck_ref, cv_ref,
     onorm_ref, s0_ref) = refs[:12]
    (o_ref, sfin_ref, gates_s, qs, ks, gc_s, kb_s, w_s, u_s, qd_s, ktl_s, kt_s, aqk_s, gl_s, o_s,
     st_s) = refs[12 + n_kept:]
    hb, seq, dk = qs.shape
    head0 = pl.program_id(1) * hb
    c = CHUNK
    n_chunks = seq // c
    chains = [(hh, d) for hh in range(hb) for d in range(2)]

    @pl.when(head0 == 0)
    def _():
        ab = ab_ref[...]
        pos = jnp.bitwise_and(lax.broadcasted_iota(jnp.int32, ab.shape, 0), c - 1)
        g = -jnp.exp(alog_ref[...]) * jax.nn.softplus(ab + dtb_ref[...])
        g_fwd = _chunk_cumsum(g, pos, reverse=False)
        g_rev = _chunk_cumsum(g, pos, reverse=True)
        gates_s[0] = g_fwd
        gates_s[1] = g_rev - g
        gates_s[2] = g_rev
        gates_s[3] = g_fwd - g
        gates_s[4] = jax.nn.sigmoid(ab)

    lane = lax.broadcasted_iota(jnp.int32, (seq, V7X_LANES), 1)

    def column(i, idx):
        col = jnp.sum(jnp.where(lane == idx, gates_s[i], 0.0), axis=1, keepdims=True)
        return jnp.broadcast_to(col, (seq, V7X_LANES))

    for hh in range(hb):
        cols = slice(hh * dk, (hh + 1) * dk)
        q = _silu(_dwconv(q_ref[:, cols].astype(F32), cq_ref[:, cols], conv_k))
        k = _silu(_dwconv(k_ref[:, cols].astype(F32), ck_ref[:, cols], conv_k))
        v = _silu(_dwconv(v_ref[:, cols].astype(F32), cv_ref[:, cols], conv_k))
        q = q * lax.rsqrt(jnp.sum(q * q, axis=-1, keepdims=True) + EPS) * (dk ** -0.5)
        k = k * lax.rsqrt(jnp.sum(k * k, axis=-1, keepdims=True) + EPS)
        qs[hh] = q
        ks[hh] = k
        for d in range(2):
            ch = 2 * hh + d
            head = head0 + hh
            g_cum = column(2 * d, d * n_heads + head)
            g_tail = column(2 * d + 1, d * n_heads + head)
            beta = column(4, 2 * n_heads + d * n_heads + head)
            e_cum = jnp.exp(g_cum)
            kb = k * beta
            gc_s[ch] = g_cum
            kb_s[ch] = kb
            w_s[ch] = kb * e_cum
            u_s[ch] = v * beta
            qd_s[ch] = q * e_cum
            ktl_s[ch] = k * jnp.exp(g_tail)
            gl_s[ch] = jnp.exp(g_cum + g_tail)

    ri = lax.broadcasted_iota(jnp.int32, (c, c), 0)
    ci = lax.broadcasted_iota(jnp.int32, (c, c), 1)
    incl = (ri >= ci, ri <= ci)
    strict = (ri > ci, ri < ci)

    def phase1(it, carry):
        loaded = []
        for gi in range(lockstep):
            ic = it * lockstep + gi
            sl = pl.ds(pl.multiple_of(ic * c, c), c)
            for hh, d in chains:
                ch = 2 * hh + d
                loaded.append((ic, sl, ch, d, qs[hh, sl, :], ks[hh, sl, :], gc_s[ch, sl, :], kb_s[ch, sl, :],
                               w_s[ch, sl, :], u_s[ch, sl, :], ktl_s[ch, sl, :]))
        decays, kqs = [], []
        for ic, sl, ch, d, qc, kc, gc, kb, w0, u0, ktl in loaded:
            diff = gc[:, :c] - gc.T[:c, :]
            decays.append(jnp.where(incl[d], jnp.exp(jnp.where(incl[d], diff, 0.0)), 0.0))
            kqs.append(_dot_nt(jnp.concatenate([kb, qc], axis=0), kc))
        a_kks = [jnp.where(strict[item[3]], kq[:c] * decay, 0.0)
                 for item, kq, decay in zip(loaded, kqs, decays)]
        es = _unit_tri_inverses_minus_eye(a_kks, ri, ci)
        rhss = [jnp.concatenate([item[8], item[9]], axis=1) for item in loaded]
        wus = [rhs + _dot(e, rhs) for e, rhs in zip(es, rhss)]
        for item, kq, decay, wu in zip(loaded, kqs, decays, wus):
            ic, sl, ch = item[:3]
            aqk_s[ch, sl, :] = kq[c:] * decay
            w_s[ch, sl, :] = wu[:, :dk]
            u_s[ch, sl, :] = wu[:, dk:]
            kt_s[ch, ic] = item[10].T
        return carry

    lax.fori_loop(0, n_chunks // lockstep, phase1, 0)

    for hh, d in chains:
        st_s[2 * hh + d] = s0_ref[d, hh]

    def phase2(i, carry):
        n = 2 * hb
        ics = [n_chunks - 1 - i if d else i for _, d in chains]
        r0s = [pl.multiple_of(ic * c, c) for ic in ics]
        sls = [pl.ds(r0, c) for r0 in r0s]
        ss = [st_s[ch] for ch in range(n)]
        wqs = [_dot(jnp.concatenate([w_s[ch, sls[ch], :], qd_s[ch, sls[ch], :]], axis=0), ss[ch]) for ch in range(n)]
        v_news = [u_s[ch, sls[ch], :] - wqs[ch][:c] for ch in range(n)]
        intra = [_dot(aqk_s[ch, sls[ch], :], v_news[ch]) for ch in range(n)]
        upd = [_dot(kt_s[ch, ics[ch]], v_news[ch]) for ch in range(n)]
        for ch in range(n):
            o_s[ch, sls[ch], :] = wqs[ch][c:] + intra[ch]
            st_s[ch] = ss[ch] * gl_s[ch, pl.ds(r0s[ch], 1), :] + upd[ch]
        return carry

    lax.fori_loop(0, n_chunks, phase2, 0)

    for hh, d in chains:
        sfin_ref[d, hh] = st_s[2 * hh + d]
    for hh in range(hb):
        cols = slice(hh * dk, (hh + 1) * dk)
        o = o_s[2 * hh] + o_s[2 * hh + 1]
        o = o * lax.rsqrt(jnp.mean(o * o, axis=-1, keepdims=True) + EPS)
        o_ref[:, cols] = (o * onorm_ref[...] * _silu(gt_ref[:, cols].astype(F32))).astype(o_ref.dtype)


def _gdn_core(proj, ab, conv_w, a_log, dt_bias, onorm, s0, dst, states_dst, *, t_total, row0, n_seq, seq,
              n_heads, dk, dv, s0_spec, state_slot):
    assert dk == dv == V7X_LANES and row0 % seq == 0
    conv_k = conv_w.shape[0]
    r0 = row0 // seq
    n_chunks = seq // CHUNK
    hb = _tile(n_heads, GDN_HEADS_PER_STEP, 1)
    lockstep = _tile(n_chunks, GDN_LOCKSTEP_CHUNKS, 1)
    gate_pad = lambda p: jnp.pad(p.reshape(1, 2 * n_heads), ((0, 0), (0, V7X_LANES - 2 * n_heads)))
    lane_vec = pl.BlockSpec((1, V7X_LANES), lambda b, h: (0, 0))
    if s0 is None:
        s0 = jnp.zeros((2, hb, dk, dv), F32)
        s0_in = pl.BlockSpec((2, hb, dk, dv), lambda b, h: (0, 0, 0, 0))
    else:
        s0_in = s0_spec(hb)
    nb = n_heads // hb
    col = lambda sec: pl.BlockSpec((seq, hb * dk), lambda b, h: (b + r0, sec * nb + h))
    cw = lambda sec: pl.BlockSpec((conv_k, hb * dk), lambda b, h: (0, sec * nb + h))
    f32 = lambda *s: pltpu.VMEM(s, F32)
    in_specs = [lane_vec, lane_vec, col(0), col(1), col(2), col(3),
                pl.BlockSpec((seq, V7X_LANES), lambda b, h: (b + r0, 0)),
                cw(0), cw(1), cw(2),
                pl.BlockSpec((1, dv), lambda b, h: (0, 0)),
                s0_in]
    args = [gate_pad(a_log), gate_pad(dt_bias), proj, proj, proj, proj, ab, conv_w, conv_w, conv_w,
            onorm.reshape(1, dv), s0]
    aliases = {}
    for out_idx, kept in enumerate((dst, states_dst)):
        if kept is not None:
            in_specs.append(pl.BlockSpec(memory_space=pl.ANY))
            args.append(kept)
            aliases[len(args) - 1] = out_idx
    nc = 2 * hb
    slot, n_slots = state_slot
    return pl.pallas_call(
        functools.partial(_gdn_kernel, n_heads=n_heads, conv_k=conv_k, lockstep=lockstep,
                          n_kept=len(aliases)),
        out_shape=(jax.ShapeDtypeStruct((t_total, n_heads * dv), BF16),
                   jax.ShapeDtypeStruct((n_seq, n_slots, 2, n_heads, dk, dv), F32)),
        grid=(n_seq, nb),
        in_specs=in_specs,
        out_specs=(pl.BlockSpec((seq, hb * dv), lambda b, h: (b + r0, h)),
                   pl.BlockSpec((None, None, 2, hb, dk, dv), lambda b, h: (b, slot, 0, h, 0, 0))),
        scratch_shapes=[f32(5, seq, V7X_LANES),
                        f32(hb, seq, dk), f32(hb, seq, dk),
                        f32(nc, seq, V7X_LANES), f32(nc, seq, dk),
                        f32(nc, seq, dk), f32(nc, seq, dv), f32(nc, seq, dk),
                        f32(nc, seq, dk), f32(nc, n_chunks, dk, CHUNK),
                        f32(nc, seq, CHUNK), f32(nc, seq, V7X_LANES),
                        f32(nc, seq, dv), f32(nc, dk, dv)],
        input_output_aliases=aliases,
        compiler_params=_cparams("parallel", "arbitrary"),
        name="gdn_core",
    )(*args)


def _odd_dft(seq):
    k = jnp.arange(seq, dtype=jnp.int32)[:, None]
    m = jnp.arange(seq, dtype=jnp.int32)[None, :]
    r = ((2 * k + 1) * m) % (4 * seq)
    ang = r.astype(F32) * (math.pi / (2 * seq))
    return jnp.cos(ang), jnp.sin(ang)


def _filter_kernel(feat_ref, w1_ref, b1_ref, w2_ref, b2_ref, fr_ref, w3f_ref, w3b_ref, dl_ref,
                   dft_hi_ref, dft_lo_ref, hre_ref, him_ref):
    seq = feat_ref.shape[0]
    feat = feat_ref[...]
    fr = fr_ref[...]
    hid = jnp.sin(fr * (_dot(feat, w1_ref[...]) + b1_ref[...]))
    hid = jnp.sin(fr * (_dot(hid, w2_ref[...]) + b2_ref[...]))
    window = jnp.exp(-feat[:, 0:1] * dl_ref[...])
    hf = _dot(hid, w3f_ref[...]) * window
    hb = _dot(hid, w3b_ref[...]) * window
    row = lax.broadcasted_iota(jnp.int32, hb.shape, 0)
    hb = jnp.where(row == 0, 0.0, hb)

    def dft(rows, val):
        v_hi = val.astype(BF16)
        v_lo = (val - v_hi.astype(F32)).astype(BF16)
        m_hi, m_lo = dft_hi_ref[rows, :], dft_lo_ref[rows, :]
        return (jnp.dot(m_hi, v_hi, preferred_element_type=F32)
                + (jnp.dot(m_hi, v_lo, preferred_element_type=F32)
                   + jnp.dot(m_lo, v_hi, preferred_element_type=F32)))

    hre_ref[...] = dft(slice(0, seq), hf + hb)
    him_ref[...] = dft(slice(seq, 2 * seq), hb - hf)


def _hyena_filters(seq, d, w1, b1, w2, b2, w3, freq, dft_hi, dft_lo):
    emb, hid = w1.shape
    bands = (emb - 1) // 2
    t = jnp.linspace(0.0, 1.0, seq, dtype=F32)[:, None]
    wpos = (2.0 * math.pi / seq) * jnp.arange(seq, dtype=F32)[:, None]
    fb = jnp.linspace(1e-4, bands - 1, bands, dtype=F32)[None, :]
    feat = jnp.concatenate([t, jnp.cos(fb * wpos), -jnp.sin(fb * wpos)], axis=-1)
    feat = jnp.pad(feat, ((0, 0), (0, V7X_LANES - emb)))
    w1p = jnp.pad(w1, ((0, V7X_LANES - emb), (0, 0)))
    max_decay = math.log(HY_TARGET) / HY_SHORT_PCT
    min_decay = math.log(HY_TARGET) / HY_LONG_PCT
    deltas = jnp.abs(jnp.linspace(min_decay, max_decay, d, dtype=F32))[None, :]
    tc = _tile(d, 256, V7X_LANES)
    nt = d // tc
    full = lambda r, c: pl.BlockSpec((r, c), lambda n, j: (0, 0))
    out_spec = pl.BlockSpec((None, seq, tc), lambda n, j: (n, 0, j))
    return pl.pallas_call(
        _filter_kernel,
        out_shape=(jax.ShapeDtypeStruct((HY_ORDER, seq, d), F32),) * 2,
        grid=(HY_ORDER, nt),
        in_specs=[full(seq, V7X_LANES), full(V7X_LANES, hid), full(1, hid), full(hid, hid), full(1, hid),
                  full(1, hid),
                  pl.BlockSpec((hid, tc), lambda n, j: (0, (2 * n) * nt + j)),
                  pl.BlockSpec((hid, tc), lambda n, j: (0, (2 * n + 1) * nt + j)),
                  pl.BlockSpec((1, tc), lambda n, j: (0, j)),
                  full(2 * seq, seq), full(2 * seq, seq)],
        out_specs=(out_spec, out_spec),
        compiler_params=_cparams("parallel", "parallel"),
        name="hyena_filter",
    )(feat, w1p, b1.reshape(1, hid), w2, b2.reshape(1, hid), freq.reshape(1, hid), w3, w3, deltas,
      dft_hi, dft_lo)


def _hyconv_kernel(*refs, conv_z, conv_k, has_dst):
    z_ref, x_ref, hre_ref, him_ref, skip_ref, cz_ref, cx_ref, fwd_ref, inv_ref = refs[:9]
    o_ref = refs[9 + has_dst]
    seq = z_ref.shape[0]
    z = z_ref[...].astype(F32)
    if conv_z:
        z = _dwconv(z, cz_ref[...], conv_k)
    x = _dwconv(x_ref[...].astype(F32), cx_ref[...], conv_k)
    pq = jnp.dot(fwd_ref[...], z.astype(BF16), preferred_element_type=F32)
    p, q = pq[:seq], pq[seq:]
    hre, him = hre_ref[...], him_ref[...]
    y_spec = jnp.concatenate([p * hre + q * him, p * him - q * hre], axis=0).astype(BF16)
    y = jnp.dot(inv_ref[...], y_spec, preferred_element_type=F32)
    o_ref[...] = (x * (y + skip_ref[...] * z)).astype(o_ref.dtype)


def _hyconv(z, z_col0, conv_z, proj, x_col0, hre, him, order, skip, conv_w, fwd, inv, dst,
            *, row0, n_seq, seq, d, out_rows, out_row0, out_dtype):
    conv_k = conv_w.shape[0]
    tc = _tile(d, 512, V7X_LANES)
    nt = d // tc
    r0 = row0 // seq
    zr0 = r0 if conv_z else 0
    zc, xc = z_col0 // tc, x_col0 // tc
    in_specs = [pl.BlockSpec((seq, tc), lambda j, b: (b + zr0, zc + j)),
                pl.BlockSpec((seq, tc), lambda j, b: (b + r0, xc + j)),
                pl.BlockSpec((None, seq, tc), lambda j, b: (order, 0, j)),
                pl.BlockSpec((None, seq, tc), lambda j, b: (order, 0, j)),
                pl.BlockSpec((1, tc), lambda j, b: (0, j)),
                pl.BlockSpec((conv_k, tc), lambda j, b: (0, zc + j)),
                pl.BlockSpec((conv_k, tc), lambda j, b: (0, xc + j)),
                pl.BlockSpec((2 * seq, seq), lambda j, b: (0, 0)),
                pl.BlockSpec((seq, 2 * seq), lambda j, b: (0, 0))]
    args = [z, proj, hre, him, skip.reshape(1, d), conv_w, conv_w, fwd, inv]
    aliases = {}
    if dst is not None:
        in_specs.append(pl.BlockSpec(memory_space=pl.ANY))
        args.append(dst)
        aliases = {len(args) - 1: 0}
    out_r0 = out_row0 // seq
    return pl.pallas_call(
        functools.partial(_hyconv_kernel, conv_z=conv_z, conv_k=conv_k, has_dst=dst is not None),
        out_shape=jax.ShapeDtypeStruct((out_rows, d), out_dtype),
        grid=(nt, n_seq),
        in_specs=in_specs,
        out_specs=pl.BlockSpec((seq, tc), lambda j, b: (b + out_r0, j)),
        input_output_aliases=aliases,
        compiler_params=_cparams("parallel", "parallel"),
        name="hyena_conv",
    )(*args)


def _grid_pos_emb(n_tokens, d):
    rows = n_tokens // GRID_W
    r, col = jnp.meshgrid(jnp.arange(rows), jnp.arange(GRID_W), indexing='ij')
    quarter = d // 4
    omega = 1.0 / (POS_BASE ** (jnp.arange(quarter, dtype=F32) / quarter))

    def emb1d(p):
        a = p.reshape(-1, 1).astype(F32) * omega[None, :]
        return jnp.concatenate([jnp.sin(a), jnp.cos(a)], axis=-1)

    return jnp.concatenate([emb1d(r), emb1d(col)], axis=-1)


def kernel(x_prompt, x_sample, state_delta, c, c_ctx, ada_w, ada_b, norm1_g, norm2_g, gdn_w_in, gdn_conv, gdn_a_log, gdn_dt_bias, gdn_onorm, gdn_w_out, hy_w_in, hy_b_in, hy_conv, hy_f_w1, hy_f_b1, hy_f_w2, hy_f_b2, hy_f_w3, hy_freq, hy_skip, hy_w_out, hy_b_out, ffn_w_gu, ffn_w_down, final_g):
    bc, lc, d = x_prompt.shape
    bl, ll, _ = x_sample.shape
    depth = ada_w.shape[0]
    n_heads, dk, dv = state_delta.shape[3:]
    qk_w, v_w = n_heads * dk, n_heads * dv
    tc_rows, tl_rows = bc * lc, bl * ll
    t = tc_rows + tl_rows
    assert tc_rows % ll == 0 and ll % lc == 0
    rows = _Rows(tc_rows, ll, t, _tile(math.gcd(tc_rows, ll), 512, V7X_SUBLANES))
    rows_small = _Rows(tc_rows, ll, t, _tile(math.gcd(tc_rows, ll), 256, V7X_SUBLANES))

    bm = 1 + bl
    bm_pad = -(-bm // V7X_SUBLANES) * V7X_SUBLANES
    cvec = jnp.concatenate([c_ctx[None, :], c, jnp.zeros((bm_pad - bm, d), F32)], axis=0)
    mods_all = _ada(cvec, ada_w, ada_b).reshape(depth, bm_pad, 6, d)

    x = _embed(x_prompt.reshape(tc_rows, d), x_sample.reshape(tl_rows, d), _grid_pos_emb(ll, d), rows_small)

    n_hy = hy_w_in.shape[0]
    dft = {}
    for seq in (lc, ll):
        cos_m, sin_m = _odd_dft(seq)
        fwd_f32 = jnp.concatenate([cos_m, sin_m], axis=0)
        fwd = fwd_f32.astype(BF16)
        fwd_lo = (fwd_f32 - fwd.astype(F32)).astype(BF16)
        inv = (jnp.concatenate([cos_m.T, -sin_m.T], axis=1) / seq).astype(BF16)
        spectra = [_hyena_filters(seq, d, hy_f_w1[j], hy_f_b1[j], hy_f_w2[j], hy_f_b2[j], hy_f_w3[j],
                                  hy_freq[j], fwd, fwd_lo) for j in range(n_hy)]
        dft[seq] = (fwd, inv, spectra)

    zero_b = jnp.zeros((d,), F32)
    new_state_delta = None
    n_mixers = 2
    n_gdn = gdn_w_in.shape[0]
    for layer in range(depth):
        mods = mods_all[layer]
        j = layer // n_mixers
        if layer % n_mixers == 0:
            ab_pad = V7X_LANES - 4 * n_heads
            w_in = jnp.pad(gdn_w_in[j], ((0, 0), (0, ab_pad))).astype(BF16)
            proj, ab = _in_proj(x, norm1_g[layer], mods, w_in, jnp.zeros((w_in.shape[1],), F32), rows,
                                n_side=V7X_LANES)
            gdn = dict(t_total=t, n_heads=n_heads, dk=dk, dv=dv)
            weights = (gdn_conv[j], gdn_a_log[j], gdn_dt_bias[j], gdn_onorm[j])
            mixed, new_state_delta = _gdn_core(proj, ab, *weights, None, None, new_state_delta, row0=0, n_seq=bc,
                                               seq=lc, s0_spec=None, state_slot=(j, n_gdn), **gdn)
            s0_spec = lambda hb, j=j: pl.BlockSpec((None, None, 2, hb, dk, dv), lambda b, h: (b, j, 0, h, 0, 0))
            mixed, _ = _gdn_core(proj, ab, *weights, state_delta, mixed, None, row0=tc_rows, n_seq=bl, seq=ll,
                                 s0_spec=s0_spec, state_slot=(0, 1), **gdn)
            x = _out_proj(mixed, gdn_w_out[j].astype(BF16), zero_b, x, mods, rows)
        else:
            proj = _in_proj(x, norm1_g[layer], mods, hy_w_in[j].astype(BF16), hy_b_in[j], rows)
            mixed = None
            for row0, n_seq, seq in ((0, bc, lc), (tc_rows, bl, ll)):
                fwd, inv, spectra = dft[seq]
                hre, him = spectra[j]
                z = proj
                for n in range(HY_ORDER):
                    last = n == HY_ORDER - 1
                    z = _hyconv(z, 0, n == 0, proj, (n + 1) * d, hre, him, n, hy_skip[j, n], hy_conv[j],
                                fwd, inv, mixed if last else None, row0=row0, n_seq=n_seq, seq=seq, d=d,
                                out_rows=t if last else n_seq * seq, out_row0=row0 if last else 0,
                                out_dtype=BF16 if last else F32)
                mixed = z
            x = _out_proj(mixed, hy_w_out[j].astype(BF16), hy_b_out[j], x, mods, rows)
        x = _ffn(x, norm2_g[layer], mods, ffn_w_gu[layer].astype(BF16), ffn_w_down[layer].astype(BF16), rows)

    tm_out = rows_small.tm
    y_prompt = _final_norm(x, final_g, 0, tc_rows, tm_out).reshape(bc, lc, d)
    y_sample = _final_norm(x, final_g, tc_rows, tl_rows, tm_out).reshape(bl, ll, d)
    return (y_prompt, y_sample, new_state_delta)
```

```python
import functools
import math

import jax
import jax.numpy as jnp
from jax import lax
from jax.experimental import pallas as pl
from jax.experimental.pallas import tpu as pltpu

GRID_W = 64
CHUNK = 64
HY_ORDER = 2
HY_TARGET = 1e-2
HY_SHORT_PCT = 0.3
HY_LONG_PCT = 1.5
POS_BASE = 10000.0
EPS = 1e-6

V7X_LANES = 128
V7X_SUBLANES = 8
V7X_VMEM_LIMIT_BYTES = 48 * 1024 * 1024

BF16 = jnp.bfloat16
F32 = jnp.float32
HIGHEST = lax.Precision.HIGHEST


def _cparams(*sem):
    return pltpu.CompilerParams(dimension_semantics=sem, vmem_limit_bytes=V7X_VMEM_LIMIT_BYTES)


def _tile(n, target, align):
    if n <= target:
        return n
    best = None
    for t in range(align, target + 1, align):
        if n % t == 0:
            best = t
    assert best is not None, (n, target, align)
    return best


def _dot(a, b):
    return jnp.dot(a.astype(BF16), b.astype(BF16), preferred_element_type=F32)


def _dot_nt(a, b):
    return lax.dot_general(a.astype(BF16), b.astype(BF16), (((1,), (1,)), ((), ())),
                           preferred_element_type=F32)


def _dot_hi(a, b):
    return jnp.dot(a, b, preferred_element_type=F32, precision=HIGHEST)


def _silu(x):
    return x * jax.nn.sigmoid(x)


def _norm_mod(x, g, shift, scale):
    ms = jnp.mean(x * x, axis=-1, keepdims=True)
    return (x * lax.rsqrt(ms + EPS) * g) * (1.0 + scale) + shift


class _Rows:
    def __init__(self, tc, ll, t, tm):
        assert tc % tm == 0 and ll % tm == 0 and t % tm == 0
        self.n_ctx_tiles = tc // tm
        self.tiles_per_lat = ll // tm
        self.n_tiles = t // tm
        self.tm = tm

    def mod_index(self, i):
        lat = 1 + (i - self.n_ctx_tiles) // self.tiles_per_lat
        return jnp.where(i < self.n_ctx_tiles, 0, lat)


def _ada_kernel(c_ref, w_ref, b_ref, o_ref):
    o_ref[...] = _dot(_silu(c_ref[...]), w_ref[...]) + b_ref[...]


def _ada(cvec, ada_w, ada_b):
    depth, d, n = ada_w.shape
    bm = cvec.shape[0]
    tn = _tile(n, 1536, V7X_LANES)
    return pl.pallas_call(
        _ada_kernel,
        out_shape=jax.ShapeDtypeStruct((depth, bm, n), F32),
        grid=(depth, n // tn),
        in_specs=[pl.BlockSpec((bm, d), lambda l, j: (0, 0)),
                  pl.BlockSpec((None, d, tn), lambda l, j: (l, 0, j)),
                  pl.BlockSpec((None, 1, tn), lambda l, j: (l, 0, j))],
        out_specs=pl.BlockSpec((None, bm, tn), lambda l, j: (l, 0, j)),
        compiler_params=_cparams("parallel", "parallel"),
        name="ada",
    )(cvec, ada_w, ada_b.reshape(depth, 1, n))


def _embed_kernel(xp_ref, xs_ref, pos_ref, o_ref, *, n_ctx_tiles):
    i = pl.program_id(0)

    @pl.when(i < n_ctx_tiles)
    def _():
        o_ref[...] = xp_ref[...]

    @pl.when(i >= n_ctx_tiles)
    def _():
        o_ref[...] = xs_ref[...] + pos_ref[...]


def _embed(xp, xs, pos, rows):
    t, d = xp.shape[0] + xs.shape[0], xp.shape[1]
    tm, nct = rows.tm, rows.n_ctx_tiles
    npos = pos.shape[0] // tm
    return pl.pallas_call(
        functools.partial(_embed_kernel, n_ctx_tiles=nct),
        out_shape=jax.ShapeDtypeStruct((t, d), F32),
        grid=(rows.n_tiles,),
        in_specs=[pl.BlockSpec((tm, d), lambda i: (jnp.minimum(i, nct - 1), 0)),
                  pl.BlockSpec((tm, d), lambda i: (jnp.maximum(i - nct, 0), 0)),
                  pl.BlockSpec((tm, d), lambda i: (jnp.maximum(i - nct, 0) % npos, 0))],
        out_specs=pl.BlockSpec((tm, d), lambda i: (i, 0)),
        compiler_params=_cparams("parallel"),
        name="embed",
    )(xp, xs, pos)


_RESIDENT = pl.Buffered(1)


def _in_kernel(x_ref, g_ref, mod_ref, w_ref, b_ref, o_ref, *side_ref, chunk):
    m = mod_ref[...]
    h = _norm_mod(x_ref[...], g_ref[...], m[0:1, :], m[1:2, :]).astype(BF16)
    n = o_ref.shape[1]
    for c0 in range(0, n, chunk):
        cols = slice(c0, c0 + chunk)
        y = jnp.dot(h, w_ref[:, cols], preferred_element_type=F32) + b_ref[:, cols]
        o_ref[:, cols] = y.astype(o_ref.dtype)
    if side_ref:
        side_ref[0][...] = jnp.dot(h, w_ref[:, n:], preferred_element_type=F32) + b_ref[:, n:]


def _in_proj(x, g, mods_all, layer, w_all, w_index, b, rows, n_side=0):
    t, d = x.shape
    n = w_all.shape[2] - n_side
    tm = rows.tm
    chunk = _tile(n, 512, V7X_LANES)
    out_shape = [jax.ShapeDtypeStruct((t, n), BF16)]
    out_specs = [pl.BlockSpec((tm, n), lambda i: (i, 0))]
    if n_side:
        out_shape.append(jax.ShapeDtypeStruct((t, n_side), F32))
        out_specs.append(pl.BlockSpec((tm, n_side), lambda i: (i, 0)))
    out = pl.pallas_call(
        functools.partial(_in_kernel, chunk=chunk),
        out_shape=out_shape,
        grid=(rows.n_tiles,),
        in_specs=[pl.BlockSpec((tm, d), lambda i: (i, 0)),
                  pl.BlockSpec((1, d), lambda i: (0, 0), pipeline_mode=_RESIDENT),
                  pl.BlockSpec((None, None, 6, d), lambda i: (layer, rows.mod_index(i), 0, 0)),
                  pl.BlockSpec((None, d, n + n_side), lambda i: (w_index, 0, 0), pipeline_mode=_RESIDENT),
                  pl.BlockSpec((1, n + n_side), lambda i: (0, 0), pipeline_mode=_RESIDENT)],
        out_specs=out_specs,
        compiler_params=_cparams("parallel"),
        name="in_proj",
    )(x, g.reshape(1, d), mods_all, w_all, b.reshape(1, n + n_side))
    return out if n_side else out[0]


def _out_kernel(a_ref, w_ref, b_ref, x_ref, mod_ref, o_ref):
    y = _dot(a_ref[...], w_ref[...]) + b_ref[...]
    o_ref[...] = x_ref[...] + mod_ref[2:3, :] * y


def _out_proj(a, w_all, w_index, b, x, mods_all, layer, rows):
    t, k = a.shape
    d = w_all.shape[2]
    tm = rows.tm
    return pl.pallas_call(
        _out_kernel,
        out_shape=jax.ShapeDtypeStruct((t, d), F32),
        grid=(rows.n_tiles,),
        in_specs=[pl.BlockSpec((tm, k), lambda i: (i, 0)),
                  pl.BlockSpec((None, k, d), lambda i: (w_index, 0, 0), pipeline_mode=_RESIDENT),
                  pl.BlockSpec((1, d), lambda i: (0, 0), pipeline_mode=_RESIDENT),
                  pl.BlockSpec((tm, d), lambda i: (i, 0)),
                  pl.BlockSpec((None, None, 6, d), lambda i: (layer, rows.mod_index(i), 0, 0))],
        out_specs=pl.BlockSpec((tm, d), lambda i: (i, 0)),
        compiler_params=_cparams("parallel"),
        name="out_proj",
    )(a, w_all, b.reshape(1, d), x, mods_all)


def _ffn_kernel(x_ref, g_ref, mod_ref, wgu_ref, wd_ref, *rest, chunk):
    o_ref = rest[-1]
    x = x_ref[...]
    m = mod_ref[...]
    h = _norm_mod(x, g_ref[...], m[3:4, :], m[4:5, :]).astype(BF16)
    f = wd_ref.shape[0]
    n_chunks = f // chunk

    def gate_up(k):
        gate = jnp.dot(h, wgu_ref[:, k * chunk:(k + 1) * chunk], preferred_element_type=F32)
        up = jnp.dot(h, wgu_ref[:, f + k * chunk:f + (k + 1) * chunk], preferred_element_type=F32)
        return gate, up

    y = None
    pending = gate_up(0)
    for k in range(n_chunks):
        following = gate_up(k + 1) if k + 1 < n_chunks else None
        act = (_silu(pending[0]) * pending[1]).astype(BF16)
        part = jnp.dot(act, wd_ref[k * chunk:(k + 1) * chunk, :], preferred_element_type=F32)
        y = part if y is None else y + part
        pending = following
    out = x + m[5:6, :] * y
    if len(rest) == 2:
        out = out * lax.rsqrt(jnp.mean(out * out, axis=-1, keepdims=True) + EPS) * rest[0][...]
    o_ref[...] = out


def _ffn(x, g, mods_all, w_gu_all, w_down_all, layer, rows, final_g=None, tile0=0, n_tiles=None):
    d = x.shape[1]
    f = w_down_all.shape[1]
    tm = rows.tm
    n_tiles = rows.n_tiles if n_tiles is None else n_tiles
    chunk = _tile(f, 256, V7X_LANES)
    resident = lambda *s: pl.BlockSpec(s, lambda i: (0,) * len(s), pipeline_mode=_RESIDENT)
    in_specs = [pl.BlockSpec((tm, d), lambda i: (i + tile0, 0)),
                resident(1, d),
                pl.BlockSpec((None, None, 6, d), lambda i: (layer, rows.mod_index(i + tile0), 0, 0)),
                pl.BlockSpec((None, d, 2 * f), lambda i: (layer, 0, 0), pipeline_mode=_RESIDENT),
                pl.BlockSpec((None, f, d), lambda i: (layer, 0, 0), pipeline_mode=_RESIDENT)]
    args = [x, g.reshape(1, d), mods_all, w_gu_all, w_down_all]
    if final_g is not None:
        in_specs.append(resident(1, d))
        args.append(final_g.reshape(1, d))
    return pl.pallas_call(
        functools.partial(_ffn_kernel, chunk=chunk),
        out_shape=jax.ShapeDtypeStruct((n_tiles * tm, d), F32),
        grid=(n_tiles,),
        in_specs=in_specs,
        out_specs=pl.BlockSpec((tm, d), lambda i: (i, 0)),
        compiler_params=_cparams("parallel"),
        name="ffn",
    )(*args)


def _dwconv(x, w, k):
    n = x.shape[0]
    row = lax.broadcasted_iota(jnp.int32, x.shape, 0)
    half = k // 2
    acc = x * w[half:half + 1, :]
    for s in range(-half, half + 1):
        if s == 0:
            continue
        shifted = pltpu.roll(x, (-s) % n, axis=0)
        valid = jnp.logical_and(row + s >= 0, row + s < n)
        acc = acc + jnp.where(valid, shifted, 0.0) * w[s + half:s + half + 1, :]
    return acc


TRI_BASE = 8
GDN_PHASE1_CHAINS = 32
GDN_HEADS_PER_STEP = 4
GDN_ROWS_PER_STEP = 2048


def _unit_tri_inverses_minus_eye(mats, ri, ci):
    c = mats[0].shape[0]

    def same_block(s):
        sh = int(math.log2(s))
        return (ri >> sh) == (ci >> sh)

    ps = [jnp.where(same_block(TRI_BASE), -a, 0.0) for a in mats]
    es = ps
    n_lvl = int(math.log2(TRI_BASE))
    for lvl in range(n_lvl):
        es = [e + _dot(p, e) for p, e in zip(ps, es)]
        if lvl < n_lvl - 1:
            ps = [_dot(p, p) for p in ps]
    s = TRI_BASE
    while s < c:
        mask = jnp.logical_and(same_block(2 * s), jnp.logical_not(same_block(s)))
        offs = [jnp.where(mask, a, 0.0) for a in mats]
        ys = [off + _dot(off, e) for off, e in zip(offs, es)]
        es = [e - (y + _dot(e, y)) for e, y in zip(es, ys)]
        s *= 2
    return es


def _chunk_cumsum(x, pos, reverse):
    n = x.shape[0]
    s = 1
    while s < CHUNK:
        if reverse:
            x = x + jnp.where(pos + s < CHUNK, pltpu.roll(x, n - s, axis=0), 0.0)
        else:
            x = x + jnp.where(pos >= s, pltpu.roll(x, s, axis=0), 0.0)
        s *= 2
    return x


def _gdn_kernel(*refs, n_heads, conv_k, lockstep, n_kept):
    (alog_ref, dtb_ref, q_ref, k_ref, v_ref, gt_ref, ab_ref, cq_ref, ck_ref, cv_ref,
     onorm_ref, s0_ref) = refs[:12]
    (o_ref, sfin_ref, gates_s, qs, ks, gc_s, kb_s, w_s, u_s, qd_s, ktl_s, kt_s, aqk_s, gl_s, o_s,
     st_s) = refs[12 + n_kept:]
    hb, seq, dk = qs.shape
    head0 = pl.program_id(1) * hb
    c = CHUNK
    n_chunks = seq // c
    chains = [(hh, d) for hh in range(hb) for d in range(2)]

    @pl.when(head0 == 0)
    def _():
        ab = ab_ref[...]
        pos = jnp.bitwise_and(lax.broadcasted_iota(jnp.int32, ab.shape, 0), c - 1)
        g = -jnp.exp(alog_ref[...]) * jax.nn.softplus(ab + dtb_ref[...])
        g_fwd = _chunk_cumsum(g, pos, reverse=False)
        g_rev = _chunk_cumsum(g, pos, reverse=True)
        gates_s[0] = g_fwd
        gates_s[1] = g_rev - g
        gates_s[2] = g_rev
        gates_s[3] = g_fwd - g
        gates_s[4] = jax.nn.sigmoid(ab)

    lane = lax.broadcasted_iota(jnp.int32, (seq, V7X_LANES), 1)

    def column(i, idx):
        col = jnp.sum(jnp.where(lane == idx, gates_s[i], 0.0), axis=1, keepdims=True)
        return jnp.broadcast_to(col, (seq, V7X_LANES))

    for hh in range(hb):
        cols = slice(hh * dk, (hh + 1) * dk)
        q = _silu(_dwconv(q_ref[:, cols].astype(F32), cq_ref[:, cols], conv_k))
        k = _silu(_dwconv(k_ref[:, cols].astype(F32), ck_ref[:, cols], conv_k))
        v = _silu(_dwconv(v_ref[:, cols].astype(F32), cv_ref[:, cols], conv_k))
        q = q * lax.rsqrt(jnp.sum(q * q, axis=-1, keepdims=True) + EPS) * (dk ** -0.5)
        k = k * lax.rsqrt(jnp.sum(k * k, axis=-1, keepdims=True) + EPS)
        qs[hh] = q
        ks[hh] = k
        for d in range(2):
            ch = 2 * hh + d
            head = head0 + hh
            g_cum = column(2 * d, d * n_heads + head)
            g_tail = column(2 * d + 1, d * n_heads + head)
            beta = column(4, 2 * n_heads + d * n_heads + head)
            e_cum = jnp.exp(g_cum)
            kb = k * beta
            gc_s[ch] = g_cum
            kb_s[ch] = kb
            w_s[ch] = kb * e_cum
            u_s[ch] = v * beta
            qd_s[ch] = q * e_cum
            ktl_s[ch] = k * jnp.exp(g_tail)
            gl_s[ch] = jnp.exp(g_cum + g_tail)

    ri = lax.broadcasted_iota(jnp.int32, (c, c), 0)
    ci = lax.broadcasted_iota(jnp.int32, (c, c), 1)
    incl = (ri >= ci, ri <= ci)
    strict = (ri > ci, ri < ci)

    def phase1(it, carry):
        loaded = []
        for gi in range(lockstep):
            ic = it * lockstep + gi
            sl = pl.ds(pl.multiple_of(ic * c, c), c)
            for hh, d in chains:
                ch = 2 * hh + d
                loaded.append((ic, sl, ch, d, qs[hh, sl, :], ks[hh, sl, :], gc_s[ch, sl, :], kb_s[ch, sl, :],
                               w_s[ch, sl, :], u_s[ch, sl, :], ktl_s[ch, sl, :]))
        decays, kqs = [], []
        for ic, sl, ch, d, qc, kc, gc, kb, w0, u0, ktl in loaded:
            diff = gc[:, :c] - gc.T[:c, :]
            decays.append(jnp.where(incl[d], jnp.exp(jnp.where(incl[d], diff, 0.0)), 0.0))
            kqs.append(_dot_nt(jnp.concatenate([kb, qc], axis=0), kc))
        a_kks = [jnp.where(strict[item[3]], kq[:c] * decay, 0.0)
                 for item, kq, decay in zip(loaded, kqs, decays)]
        es = _unit_tri_inverses_minus_eye(a_kks, ri, ci)
        rhss = [jnp.concatenate([item[8], item[9]], axis=1) for item in loaded]
        wus = [rhs + _dot(e, rhs) for e, rhs in zip(es, rhss)]
        for item, kq, decay, wu in zip(loaded, kqs, decays, wus):
            ic, sl, ch = item[:3]
            aqk_s[ch, sl, :] = kq[c:] * decay
            w_s[ch, sl, :] = wu[:, :dk]
            u_s[ch, sl, :] = wu[:, dk:]
            kt_s[ch, ic] = item[10].T
        return carry

    lax.fori_loop(0, n_chunks // lockstep, phase1, 0)

    for hh, d in chains:
        st_s[2 * hh + d] = s0_ref[d, hh]

    def phase2(i, carry):
        n = 2 * hb
        ics = [n_chunks - 1 - i if d else i for _, d in chains]
        r0s = [pl.multiple_of(ic * c, c) for ic in ics]
        sls = [pl.ds(r0, c) for r0 in r0s]
        ss = [st_s[ch] for ch in range(n)]
        wqs = [_dot(jnp.concatenate([w_s[ch, sls[ch], :], qd_s[ch, sls[ch], :]], axis=0), ss[ch]) for ch in range(n)]
        v_news = [u_s[ch, sls[ch], :] - wqs[ch][:c] for ch in range(n)]
        intra = [_dot(aqk_s[ch, sls[ch], :], v_news[ch]) for ch in range(n)]
        upd = [_dot(kt_s[ch, ics[ch]], v_news[ch]) for ch in range(n)]
        for ch in range(n):
            o_s[ch, sls[ch], :] = wqs[ch][c:] + intra[ch]
            st_s[ch] = ss[ch] * gl_s[ch, pl.ds(r0s[ch], 1), :] + upd[ch]
        return carry

    lax.fori_loop(0, n_chunks, phase2, 0)

    for hh, d in chains:
        sfin_ref[d, hh] = st_s[2 * hh + d]
    for hh in range(hb):
        cols = slice(hh * dk, (hh + 1) * dk)
        o = o_s[2 * hh] + o_s[2 * hh + 1]
        o = o * lax.rsqrt(jnp.mean(o * o, axis=-1, keepdims=True) + EPS)
        o_ref[:, cols] = (o * onorm_ref[...] * _silu(gt_ref[:, cols].astype(F32))).astype(o_ref.dtype)


def _gdn_core(proj, ab, conv_w, a_log, dt_bias, onorm, s0, dst, states_dst, *, t_total, row0, n_seq, seq,
              n_heads, dk, dv, s0_spec, state_slot):
    assert dk == dv == V7X_LANES and row0 % seq == 0
    conv_k = conv_w.shape[0]
    r0 = row0 // seq
    n_chunks = seq // CHUNK
    hb = _tile(n_heads, min(GDN_HEADS_PER_STEP, max(1, GDN_ROWS_PER_STEP // seq)), 1)
    lockstep = _tile(n_chunks, max(1, GDN_PHASE1_CHAINS // (2 * hb)), 1)
    gate_pad = lambda p: jnp.pad(p.reshape(1, 2 * n_heads), ((0, 0), (0, V7X_LANES - 2 * n_heads)))
    lane_vec = pl.BlockSpec((1, V7X_LANES), lambda b, h: (0, 0))
    if s0 is None:
        s0 = jnp.zeros((2, hb, dk, dv), F32)
        s0_in = pl.BlockSpec((2, hb, dk, dv), lambda b, h: (0, 0, 0, 0))
    else:
        s0_in = s0_spec(hb)
    nb = n_heads // hb
    col = lambda sec: pl.BlockSpec((seq, hb * dk), lambda b, h: (b + r0, sec * nb + h))
    cw = lambda sec: pl.BlockSpec((conv_k, hb * dk), lambda b, h: (0, sec * nb + h))
    f32 = lambda *s: pltpu.VMEM(s, F32)
    in_specs = [lane_vec, lane_vec, col(0), col(1), col(2), col(3),
                pl.BlockSpec((seq, V7X_LANES), lambda b, h: (b + r0, 0)),
                cw(0), cw(1), cw(2),
                pl.BlockSpec((1, dv), lambda b, h: (0, 0)),
                s0_in]
    args = [gate_pad(a_log), gate_pad(dt_bias), proj, proj, proj, proj, ab, conv_w, conv_w, conv_w,
            onorm.reshape(1, dv), s0]
    aliases = {}
    for out_idx, kept in enumerate((dst, states_dst)):
        if kept is not None:
            in_specs.append(pl.BlockSpec(memory_space=pl.ANY))
            args.append(kept)
            aliases[len(args) - 1] = out_idx
    nc = 2 * hb
    slot, n_slots = state_slot
    return pl.pallas_call(
        functools.partial(_gdn_kernel, n_heads=n_heads, conv_k=conv_k, lockstep=lockstep,
                          n_kept=len(aliases)),
        out_shape=(jax.ShapeDtypeStruct((t_total, n_heads * dv), BF16),
                   jax.ShapeDtypeStruct((n_seq, n_slots, 2, n_heads, dk, dv), F32)),
        grid=(n_seq, nb),
        in_specs=in_specs,
        out_specs=(pl.BlockSpec((seq, hb * dv), lambda b, h: (b + r0, h)),
                   pl.BlockSpec((None, None, 2, hb, dk, dv), lambda b, h: (b, slot, 0, h, 0, 0))),
        scratch_shapes=[f32(5, seq, V7X_LANES),
                        f32(hb, seq, dk), f32(hb, seq, dk),
                        f32(nc, seq, V7X_LANES), f32(nc, seq, dk),
                        f32(nc, seq, dk), f32(nc, seq, dv), f32(nc, seq, dk),
                        f32(nc, seq, dk), f32(nc, n_chunks, dk, CHUNK),
                        f32(nc, seq, CHUNK), f32(nc, seq, V7X_LANES),
                        f32(nc, seq, dv), f32(nc, dk, dv)],
        input_output_aliases=aliases,
        compiler_params=_cparams("parallel", "arbitrary"),
        name="gdn_core",
    )(*args)


def _odd_dft(seq):
    k = jnp.arange(seq, dtype=jnp.int32)[:, None]
    m = jnp.arange(seq, dtype=jnp.int32)[None, :]
    r = ((2 * k + 1) * m) % (4 * seq)
    ang = r.astype(F32) * (math.pi / (2 * seq))
    return jnp.cos(ang), jnp.sin(ang)


def _filter_kernel(feat_ref, w1_ref, b1_ref, w2_ref, b2_ref, fr_ref, w3f_ref, w3b_ref, dl_ref,
                   dft_hi_ref, dft_lo_ref, hre_ref, him_ref):
    seq = feat_ref.shape[0]
    feat = feat_ref[...]
    fr = fr_ref[...]
    hid = jnp.sin(fr * (_dot(feat, w1_ref[...]) + b1_ref[...]))
    hid = jnp.sin(fr * (_dot(hid, w2_ref[...]) + b2_ref[...]))
    window = jnp.exp(-feat[:, 0:1] * dl_ref[...])
    hf = _dot(hid, w3f_ref[...]) * window
    hb = _dot(hid, w3b_ref[...]) * window
    row = lax.broadcasted_iota(jnp.int32, hb.shape, 0)
    hb = jnp.where(row == 0, 0.0, hb)

    def dft(rows, val):
        v_hi = val.astype(BF16)
        v_lo = (val - v_hi.astype(F32)).astype(BF16)
        m_hi, m_lo = dft_hi_ref[rows, :], dft_lo_ref[rows, :]
        return (jnp.dot(m_hi, v_hi, preferred_element_type=F32)
                + (jnp.dot(m_hi, v_lo, preferred_element_type=F32)
                   + jnp.dot(m_lo, v_hi, preferred_element_type=F32)))

    hre_ref[...] = dft(slice(0, seq), hf + hb)
    him_ref[...] = dft(slice(seq, 2 * seq), hb - hf)


def _hyena_filters(seq, d, w1, b1, w2, b2, w3, freq, dft_hi, dft_lo):
    emb, hid = w1.shape
    bands = (emb - 1) // 2
    t = jnp.linspace(0.0, 1.0, seq, dtype=F32)[:, None]
    wpos = (2.0 * math.pi / seq) * jnp.arange(seq, dtype=F32)[:, None]
    fb = jnp.linspace(1e-4, bands - 1, bands, dtype=F32)[None, :]
    feat = jnp.concatenate([t, jnp.cos(fb * wpos), -jnp.sin(fb * wpos)], axis=-1)
    feat = jnp.pad(feat, ((0, 0), (0, V7X_LANES - emb)))
    w1p = jnp.pad(w1, ((0, V7X_LANES - emb), (0, 0)))
    max_decay = math.log(HY_TARGET) / HY_SHORT_PCT
    min_decay = math.log(HY_TARGET) / HY_LONG_PCT
    deltas = jnp.abs(jnp.linspace(min_decay, max_decay, d, dtype=F32))[None, :]
    tc = _tile(d, 256, V7X_LANES)
    nt = d // tc
    full = lambda r, c: pl.BlockSpec((r, c), lambda n, j: (0, 0))
    out_spec = pl.BlockSpec((None, seq, tc), lambda n, j: (n, 0, j))
    return pl.pallas_call(
        _filter_kernel,
        out_shape=(jax.ShapeDtypeStruct((HY_ORDER, seq, d), F32),) * 2,
        grid=(HY_ORDER, nt),
        in_specs=[full(seq, V7X_LANES), full(V7X_LANES, hid), full(1, hid), full(hid, hid), full(1, hid),
                  full(1, hid),
                  pl.BlockSpec((hid, tc), lambda n, j: (0, (2 * n) * nt + j)),
                  pl.BlockSpec((hid, tc), lambda n, j: (0, (2 * n + 1) * nt + j)),
                  pl.BlockSpec((1, tc), lambda n, j: (0, j)),
                  full(2 * seq, seq), full(2 * seq, seq)],
        out_specs=(out_spec, out_spec),
        compiler_params=_cparams("parallel", "parallel"),
        name="hyena_filter",
    )(feat, w1p, b1.reshape(1, hid), w2, b2.reshape(1, hid), freq.reshape(1, hid), w3, w3, deltas,
      dft_hi, dft_lo)


HYENA_ROWS_PER_STEP = 2048


def _hyconv_kernel(*refs, conv_z, conv_k, has_dst):
    z_ref, x_ref, hre_ref, him_ref, skip_ref, cz_ref, cx_ref, fwd_ref, inv_ref = refs[:9]
    o_ref = refs[9 + has_dst]
    seq = hre_ref.shape[0]
    n_sub = z_ref.shape[0] // seq
    zs = []
    for s in range(n_sub):
        z = z_ref[s * seq:(s + 1) * seq, :].astype(F32)
        zs.append(_dwconv(z, cz_ref[...], conv_k) if conv_z else z)
    pqs = [jnp.dot(fwd_ref[...], z.astype(BF16), preferred_element_type=F32) for z in zs]
    hre, him = hre_ref[...], him_ref[...]
    for s in range(n_sub):
        p, q = pqs[s][:seq], pqs[s][seq:]
        y_spec = jnp.concatenate([p * hre + q * him, p * him - q * hre], axis=0).astype(BF16)
        y = jnp.dot(inv_ref[...], y_spec, preferred_element_type=F32)
        x = _dwconv(x_ref[s * seq:(s + 1) * seq, :].astype(F32), cx_ref[...], conv_k)
        o_ref[s * seq:(s + 1) * seq, :] = (x * (y + skip_ref[...] * zs[s])).astype(o_ref.dtype)


def _hyconv(z, z_col0, conv_z, proj, x_col0, hre, him, order, skip, conv_w, fwd, inv, dst,
            *, row0, n_seq, seq, d, out_rows, out_row0, out_dtype):
    conv_k = conv_w.shape[0]
    n_sub = max(s for s in range(1, max(1, HYENA_ROWS_PER_STEP // seq) + 1)
                if n_seq % s == 0 and row0 % (s * seq) == 0 and out_row0 % (s * seq) == 0)
    blk = n_sub * seq
    tc = _tile(d, 512 if blk <= 1024 else 256, V7X_LANES)
    nt = d // tc
    r0 = row0 // blk
    zr0 = r0 if conv_z else 0
    zc, xc = z_col0 // tc, x_col0 // tc
    in_specs = [pl.BlockSpec((blk, tc), lambda j, b: (b + zr0, zc + j)),
                pl.BlockSpec((blk, tc), lambda j, b: (b + r0, xc + j)),
                pl.BlockSpec((None, seq, tc), lambda j, b: (order, 0, j)),
                pl.BlockSpec((None, seq, tc), lambda j, b: (order, 0, j)),
                pl.BlockSpec((1, tc), lambda j, b: (0, j)),
                pl.BlockSpec((conv_k, tc), lambda j, b: (0, zc + j)),
                pl.BlockSpec((conv_k, tc), lambda j, b: (0, xc + j)),
                pl.BlockSpec((2 * seq, seq), lambda j, b: (0, 0), pipeline_mode=_RESIDENT),
                pl.BlockSpec((seq, 2 * seq), lambda j, b: (0, 0), pipeline_mode=_RESIDENT)]
    args = [z, proj, hre, him, skip.reshape(1, d), conv_w, conv_w, fwd, inv]
    aliases = {}
    if dst is not None:
        in_specs.append(pl.BlockSpec(memory_space=pl.ANY))
        args.append(dst)
        aliases = {len(args) - 1: 0}
    out_r0 = out_row0 // blk
    return pl.pallas_call(
        functools.partial(_hyconv_kernel, conv_z=conv_z, conv_k=conv_k, has_dst=dst is not None),
        out_shape=jax.ShapeDtypeStruct((out_rows, d), out_dtype),
        grid=(nt, n_seq // n_sub),
        in_specs=in_specs,
        out_specs=pl.BlockSpec((blk, tc), lambda j, b: (b + out_r0, j)),
        input_output_aliases=aliases,
        compiler_params=_cparams("parallel", "parallel"),
        name="hyena_conv",
    )(*args)


def _grid_pos_emb(n_tokens, d):
    rows = n_tokens // GRID_W
    r, col = jnp.meshgrid(jnp.arange(rows), jnp.arange(GRID_W), indexing='ij')
    quarter = d // 4
    omega = 1.0 / (POS_BASE ** (jnp.arange(quarter, dtype=F32) / quarter))

    def emb1d(p):
        a = p.reshape(-1, 1).astype(F32) * omega[None, :]
        return jnp.concatenate([jnp.sin(a), jnp.cos(a)], axis=-1)

    return jnp.concatenate([emb1d(r), emb1d(col)], axis=-1)


def kernel(x_prompt, x_sample, state_delta, c, c_ctx, ada_w, ada_b, norm1_g, norm2_g, gdn_w_in, gdn_conv, gdn_a_log, gdn_dt_bias, gdn_onorm, gdn_w_out, hy_w_in, hy_b_in, hy_conv, hy_f_w1, hy_f_b1, hy_f_w2, hy_f_b2, hy_f_w3, hy_freq, hy_skip, hy_w_out, hy_b_out, ffn_w_gu, ffn_w_down, final_g):
    bc, lc, d = x_prompt.shape
    bl, ll, _ = x_sample.shape
    depth = ada_w.shape[0]
    n_heads, dk, dv = state_delta.shape[3:]
    tc_rows, tl_rows = bc * lc, bl * ll
    t = tc_rows + tl_rows
    assert tc_rows % ll == 0 and ll % lc == 0
    rows = _Rows(tc_rows, ll, t, _tile(math.gcd(tc_rows, ll), 512, V7X_SUBLANES))
    rows_small = _Rows(tc_rows, ll, t, _tile(math.gcd(tc_rows, ll), 256, V7X_SUBLANES))

    bm = 1 + bl
    bm_pad = -(-bm // V7X_SUBLANES) * V7X_SUBLANES
    cvec = jnp.concatenate([c_ctx[None, :], c, jnp.zeros((bm_pad - bm, d), F32)], axis=0)
    mods_all = _ada(cvec, ada_w, ada_b).reshape(depth, bm_pad, 6, d)

    x = _embed(x_prompt.reshape(tc_rows, d), x_sample.reshape(tl_rows, d), _grid_pos_emb(ll, d), rows_small)

    n_hy = hy_w_in.shape[0]
    dft = {}
    for seq in (lc, ll):
        cos_m, sin_m = _odd_dft(seq)
        fwd_f32 = jnp.concatenate([cos_m, sin_m], axis=0)
        fwd = fwd_f32.astype(BF16)
        fwd_lo = (fwd_f32 - fwd.astype(F32)).astype(BF16)
        inv = (jnp.concatenate([cos_m.T, -sin_m.T], axis=1) / seq).astype(BF16)
        spectra = [_hyena_filters(seq, d, hy_f_w1[j], hy_f_b1[j], hy_f_w2[j], hy_f_b2[j], hy_f_w3[j],
                                  hy_freq[j], fwd, fwd_lo) for j in range(n_hy)]
        dft[seq] = (fwd, inv, spectra)

    zero_b = jnp.zeros((d,), F32)
    new_state_delta = None
    n_mixers = 2
    n_gdn = gdn_w_in.shape[0]
    gdn_w_in_b = jnp.pad(gdn_w_in, ((0, 0), (0, 0), (0, V7X_LANES - 4 * n_heads))).astype(BF16)
    gdn_w_out_b, hy_w_in_b, hy_w_out_b = (w.astype(BF16) for w in (gdn_w_out, hy_w_in, hy_w_out))
    ffn_w_gu_b, ffn_w_down_b = ffn_w_gu.astype(BF16), ffn_w_down.astype(BF16)
    for layer in range(depth):
        j = layer // n_mixers
        if layer % n_mixers == 0:
            proj, ab = _in_proj(x, norm1_g[layer], mods_all, layer, gdn_w_in_b, j,
                                jnp.zeros((gdn_w_in_b.shape[2],), F32), rows, n_side=V7X_LANES)
            gdn = dict(t_total=t, n_heads=n_heads, dk=dk, dv=dv)
            weights = (gdn_conv[j], gdn_a_log[j], gdn_dt_bias[j], gdn_onorm[j])
            mixed, new_state_delta = _gdn_core(proj, ab, *weights, None, None, new_state_delta, row0=0, n_seq=bc,
                                               seq=lc, s0_spec=None, state_slot=(j, n_gdn), **gdn)
            s0_spec = lambda hb, j=j: pl.BlockSpec((None, None, 2, hb, dk, dv), lambda b, h: (b, j, 0, h, 0, 0))
            mixed, _ = _gdn_core(proj, ab, *weights, state_delta, mixed, None, row0=tc_rows, n_seq=bl, seq=ll,
                                 s0_spec=s0_spec, state_slot=(0, 1), **gdn)
            x = _out_proj(mixed, gdn_w_out_b, j, zero_b, x, mods_all, layer, rows)
        else:
            proj = _in_proj(x, norm1_g[layer], mods_all, layer, hy_w_in_b, j, hy_b_in[j], rows)
            mixed = None
            for row0, n_seq, seq in ((0, bc, lc), (tc_rows, bl, ll)):
                fwd, inv, spectra = dft[seq]
                hre, him = spectra[j]
                z = proj
                for n in range(HY_ORDER):
                    last = n == HY_ORDER - 1
                    z = _hyconv(z, 0, n == 0, proj, (n + 1) * d, hre, him, n, hy_skip[j, n], hy_conv[j],
                                fwd, inv, mixed if last else None, row0=row0, n_seq=n_seq, seq=seq, d=d,
                                out_rows=t if last else n_seq * seq, out_row0=row0 if last else 0,
                                out_dtype=BF16 if last else F32)
                mixed = z
            x = _out_proj(mixed, hy_w_out_b, j, hy_b_out[j], x, mods_all, layer, rows)
        ffn = functools.partial(_ffn, x, norm2_g[layer], mods_all, ffn_w_gu_b, ffn_w_down_b, layer, rows)
        if layer < depth - 1:
            x = ffn()
        else:
            y_prompt = ffn(final_g=final_g, tile0=0, n_tiles=rows.n_ctx_tiles)
            y_sample = ffn(final_g=final_g, tile0=rows.n_ctx_tiles, n_tiles=rows.n_tiles - rows.n_ctx_tiles)
    return (y_prompt.reshape(bc, lc, d), y_sample.reshape(bl, ll, d), new_state_delta)
```

```python
import functools
import math

import jax
import jax.numpy as jnp
from jax import lax
from jax.experimental import pallas as pl
from jax.experimental.pallas import tpu as pltpu

GRID_W = 64
CHUNK = 64
HY_ORDER = 2
HY_TARGET = 1e-2
HY_SHORT_PCT = 0.3
HY_LONG_PCT = 1.5
POS_BASE = 10000.0
EPS = 1e-6

V7X_LANES = 128
V7X_SUBLANES = 8
V7X_VMEM_LIMIT_BYTES = 48 * 1024 * 1024

BF16 = jnp.bfloat16
F32 = jnp.float32
HIGHEST = lax.Precision.HIGHEST


def _cparams(*sem):
    return pltpu.CompilerParams(dimension_semantics=sem, vmem_limit_bytes=V7X_VMEM_LIMIT_BYTES)


def _tile(n, target, align):
    if n <= target:
        return n
    best = None
    for t in range(align, target + 1, align):
        if n % t == 0:
            best = t
    assert best is not None, (n, target, align)
    return best


def _dot(a, b):
    return jnp.dot(a.astype(BF16), b.astype(BF16), preferred_element_type=F32)


def _dot_nt(a, b):
    return lax.dot_general(a.astype(BF16), b.astype(BF16), (((1,), (1,)), ((), ())),
                           preferred_element_type=F32)


def _dot_hi(a, b):
    return jnp.dot(a, b, preferred_element_type=F32, precision=HIGHEST)


def _silu(x):
    return x * jax.nn.sigmoid(x)


def _norm_mod(x, g, shift, scale):
    ms = jnp.mean(x * x, axis=-1, keepdims=True)
    return (x * lax.rsqrt(ms + EPS) * g) * (1.0 + scale) + shift


class _Rows:
    def __init__(self, tc, ll, t, tm):
        assert tc % tm == 0 and ll % tm == 0 and t % tm == 0
        self.n_ctx_tiles = tc // tm
        self.tiles_per_lat = ll // tm
        self.n_tiles = t // tm
        self.tm = tm

    def mod_index(self, i):
        lat = 1 + (i - self.n_ctx_tiles) // self.tiles_per_lat
        return jnp.where(i < self.n_ctx_tiles, 0, lat)


def _ada_kernel(c_ref, w_ref, b_ref, o_ref):
    o_ref[...] = _dot(_silu(c_ref[...]), w_ref[...]) + b_ref[...]


def _ada(cvec, ada_w, ada_b):
    depth, d, n = ada_w.shape
    bm = cvec.shape[0]
    tn = _tile(n, 1536, V7X_LANES)
    return pl.pallas_call(
        _ada_kernel,
        out_shape=jax.ShapeDtypeStruct((depth, bm, n), F32),
        grid=(depth, n // tn),
        in_specs=[pl.BlockSpec((bm, d), lambda l, j: (0, 0)),
                  pl.BlockSpec((None, d, tn), lambda l, j: (l, 0, j)),
                  pl.BlockSpec((None, 1, tn), lambda l, j: (l, 0, j))],
        out_specs=pl.BlockSpec((None, bm, tn), lambda l, j: (l, 0, j)),
        compiler_params=_cparams("parallel", "parallel"),
        name="ada",
    )(cvec, ada_w, ada_b.reshape(depth, 1, n))


def _embed_kernel(xp_ref, xs_ref, pos_ref, o_ref, *, n_ctx_tiles):
    i = pl.program_id(0)

    @pl.when(i < n_ctx_tiles)
    def _():
        o_ref[...] = xp_ref[...]

    @pl.when(i >= n_ctx_tiles)
    def _():
        o_ref[...] = xs_ref[...] + pos_ref[...]


def _embed(xp, xs, pos, rows):
    t, d = xp.shape[0] + xs.shape[0], xp.shape[1]
    tm, nct = rows.tm, rows.n_ctx_tiles
    npos = pos.shape[0] // tm
    return pl.pallas_call(
        functools.partial(_embed_kernel, n_ctx_tiles=nct),
        out_shape=jax.ShapeDtypeStruct((t, d), F32),
        grid=(rows.n_tiles,),
        in_specs=[pl.BlockSpec((tm, d), lambda i: (jnp.minimum(i, nct - 1), 0)),
                  pl.BlockSpec((tm, d), lambda i: (jnp.maximum(i - nct, 0), 0)),
                  pl.BlockSpec((tm, d), lambda i: (jnp.maximum(i - nct, 0) % npos, 0))],
        out_specs=pl.BlockSpec((tm, d), lambda i: (i, 0)),
        compiler_params=_cparams("parallel"),
        name="embed",
    )(xp, xs, pos)


_RESIDENT = pl.Buffered(1)


def _in_kernel(x_ref, g_ref, mod_ref, w_ref, b_ref, o_ref, *side_ref, chunk):
    m = mod_ref[...]
    h = _norm_mod(x_ref[...], g_ref[...], m[0:1, :], m[1:2, :]).astype(BF16)
    n = o_ref.shape[1]
    for c0 in range(0, n, chunk):
        cols = slice(c0, c0 + chunk)
        y = jnp.dot(h, w_ref[:, cols], preferred_element_type=F32) + b_ref[:, cols]
        o_ref[:, cols] = y.astype(o_ref.dtype)
    if side_ref:
        side_ref[0][...] = jnp.dot(h, w_ref[:, n:], preferred_element_type=F32) + b_ref[:, n:]


def _in_proj(x, g, mods_all, layer, w_all, w_index, b, rows, n_side=0):
    t, d = x.shape
    n = w_all.shape[2] - n_side
    tm = rows.tm
    chunk = _tile(n, 512, V7X_LANES)
    out_shape = [jax.ShapeDtypeStruct((t, n), BF16)]
    out_specs = [pl.BlockSpec((tm, n), lambda i: (i, 0))]
    if n_side:
        out_shape.append(jax.ShapeDtypeStruct((t, n_side), F32))
        out_specs.append(pl.BlockSpec((tm, n_side), lambda i: (i, 0)))
    out = pl.pallas_call(
        functools.partial(_in_kernel, chunk=chunk),
        out_shape=out_shape,
        grid=(rows.n_tiles,),
        in_specs=[pl.BlockSpec((tm, d), lambda i: (i, 0)),
                  pl.BlockSpec((1, d), lambda i: (0, 0), pipeline_mode=_RESIDENT),
                  pl.BlockSpec((None, None, 6, d), lambda i: (layer, rows.mod_index(i), 0, 0)),
                  pl.BlockSpec((None, d, n + n_side), lambda i: (w_index, 0, 0), pipeline_mode=_RESIDENT),
                  pl.BlockSpec((1, n + n_side), lambda i: (0, 0), pipeline_mode=_RESIDENT)],
        out_specs=out_specs,
        compiler_params=_cparams("parallel"),
        name="in_proj",
    )(x, g.reshape(1, d), mods_all, w_all, b.reshape(1, n + n_side))
    return out if n_side else out[0]


def _ffn_kernel(x_ref, g_ref, mod_ref, wgu_ref, wd_ref, *rest, chunk, has_mixer, has_final):
    o_ref = rest[-1]
    x = x_ref[...]
    m = mod_ref[...]
    if has_mixer:
        a_ref, wo_ref, bo_ref = rest[:3]
        x = x + m[2:3, :] * (_dot(a_ref[...], wo_ref[...]) + bo_ref[...])
    h = _norm_mod(x, g_ref[...], m[3:4, :], m[4:5, :]).astype(BF16)
    f = wd_ref.shape[0]
    n_chunks = f // chunk

    def gate_up(k):
        gate = jnp.dot(h, wgu_ref[:, k * chunk:(k + 1) * chunk], preferred_element_type=F32)
        up = jnp.dot(h, wgu_ref[:, f + k * chunk:f + (k + 1) * chunk], preferred_element_type=F32)
        return gate, up

    y = None
    pending = gate_up(0)
    for k in range(n_chunks):
        following = gate_up(k + 1) if k + 1 < n_chunks else None
        act = (_silu(pending[0]) * pending[1]).astype(BF16)
        part = jnp.dot(act, wd_ref[k * chunk:(k + 1) * chunk, :], preferred_element_type=F32)
        y = part if y is None else y + part
        pending = following
    out = x + m[5:6, :] * y
    if has_final:
        out = out * lax.rsqrt(jnp.mean(out * out, axis=-1, keepdims=True) + EPS) * rest[-2][...]
    o_ref[...] = out


def _ffn(x, g, mods_all, w_gu_all, w_down_all, layer, rows, mixer=None, final_g=None, tile0=0, n_tiles=None):
    d = x.shape[1]
    f = w_down_all.shape[1]
    tm = rows.tm
    n_tiles = rows.n_tiles if n_tiles is None else n_tiles
    chunk = _tile(f, 256, V7X_LANES)
    resident = lambda *s: pl.BlockSpec(s, lambda i: (0,) * len(s), pipeline_mode=_RESIDENT)
    in_specs = [pl.BlockSpec((tm, d), lambda i: (i + tile0, 0)),
                resident(1, d),
                pl.BlockSpec((None, None, 6, d), lambda i: (layer, rows.mod_index(i + tile0), 0, 0)),
                pl.BlockSpec((None, d, 2 * f), lambda i: (layer, 0, 0), pipeline_mode=_RESIDENT),
                pl.BlockSpec((None, f, d), lambda i: (layer, 0, 0), pipeline_mode=_RESIDENT)]
    args = [x, g.reshape(1, d), mods_all, w_gu_all, w_down_all]
    if mixer is not None:
        a, wo_all, wo_index, bo = mixer
        k = a.shape[1]
        in_specs += [pl.BlockSpec((tm, k), lambda i: (i + tile0, 0)),
                     pl.BlockSpec((None, k, d), lambda i: (wo_index, 0, 0), pipeline_mode=_RESIDENT),
                     resident(1, d)]
        args += [a, wo_all, bo.reshape(1, d)]
    if final_g is not None:
        in_specs.append(resident(1, d))
        args.append(final_g.reshape(1, d))
    return pl.pallas_call(
        functools.partial(_ffn_kernel, chunk=chunk, has_mixer=mixer is not None, has_final=final_g is not None),
        out_shape=jax.ShapeDtypeStruct((n_tiles * tm, d), F32),
        grid=(n_tiles,),
        in_specs=in_specs,
        out_specs=pl.BlockSpec((tm, d), lambda i: (i, 0)),
        compiler_params=_cparams("parallel"),
        name="ffn",
    )(*args)


def _dwconv(x, w, k):
    n = x.shape[0]
    row = lax.broadcasted_iota(jnp.int32, x.shape, 0)
    half = k // 2
    acc = x * w[half:half + 1, :]
    for s in range(-half, half + 1):
        if s == 0:
            continue
        shifted = pltpu.roll(x, (-s) % n, axis=0)
        valid = jnp.logical_and(row + s >= 0, row + s < n)
        acc = acc + jnp.where(valid, shifted, 0.0) * w[s + half:s + half + 1, :]
    return acc


TRI_BASE = 8
GDN_PHASE1_CHAINS = 32
GDN_HEADS_PER_STEP = 4
GDN_ROWS_PER_STEP = 2048


def _unit_tri_inverses_minus_eye(mats, ri, ci):
    c = mats[0].shape[0]

    def same_block(s):
        sh = int(math.log2(s))
        return (ri >> sh) == (ci >> sh)

    ps = [jnp.where(same_block(TRI_BASE), -a, 0.0) for a in mats]
    es = ps
    n_lvl = int(math.log2(TRI_BASE))
    for lvl in range(n_lvl):
        es = [e + _dot(p, e) for p, e in zip(ps, es)]
        if lvl < n_lvl - 1:
            ps = [_dot(p, p) for p in ps]
    s = TRI_BASE
    while s < c:
        mask = jnp.logical_and(same_block(2 * s), jnp.logical_not(same_block(s)))
        offs = [jnp.where(mask, a, 0.0) for a in mats]
        ys = [off + _dot(off, e) for off, e in zip(offs, es)]
        es = [e - (y + _dot(e, y)) for e, y in zip(es, ys)]
        s *= 2
    return es


def _chunk_cumsum(x, pos, reverse):
    n = x.shape[0]
    s = 1
    while s < CHUNK:
        if reverse:
            x = x + jnp.where(pos + s < CHUNK, pltpu.roll(x, n - s, axis=0), 0.0)
        else:
            x = x + jnp.where(pos >= s, pltpu.roll(x, s, axis=0), 0.0)
        s *= 2
    return x


def _gdn_kernel(*refs, n_heads, conv_k, lockstep, n_kept):
    (alog_ref, dtb_ref, q_ref, k_ref, v_ref, gt_ref, ab_ref, cq_ref, ck_ref, cv_ref,
     onorm_ref, s0_ref) = refs[:12]
    (o_ref, sfin_ref, gates_s, qs, ks, gc_s, kb_s, w_s, u_s, qd_s, ktl_s, ak_s, gl_s, o_s,
     st_s) = refs[12 + n_kept:]
    hb, seq, dk = qs.shape
    head0 = pl.program_id(1) * hb
    c = CHUNK
    n_chunks = seq // c
    chains = [(hh, d) for hh in range(hb) for d in range(2)]

    @pl.when(head0 == 0)
    def _():
        ab = ab_ref[...]
        pos = jnp.bitwise_and(lax.broadcasted_iota(jnp.int32, ab.shape, 0), c - 1)
        g = -jnp.exp(alog_ref[...]) * jax.nn.softplus(ab + dtb_ref[...])
        g_fwd = _chunk_cumsum(g, pos, reverse=False)
        g_rev = _chunk_cumsum(g, pos, reverse=True)
        gates_s[0] = g_fwd
        gates_s[1] = g_rev - g
        gates_s[2] = g_rev
        gates_s[3] = g_fwd - g
        gates_s[4] = jax.nn.sigmoid(ab)

    lane = lax.broadcasted_iota(jnp.int32, (seq, V7X_LANES), 1)

    def column(i, idx):
        col = jnp.sum(jnp.where(lane == idx, gates_s[i], 0.0), axis=1, keepdims=True)
        return jnp.broadcast_to(col, (seq, V7X_LANES))

    for hh in range(hb):
        cols = slice(hh * dk, (hh + 1) * dk)
        q = _silu(_dwconv(q_ref[:, cols].astype(F32), cq_ref[:, cols], conv_k))
        k = _silu(_dwconv(k_ref[:, cols].astype(F32), ck_ref[:, cols], conv_k))
        v = _silu(_dwconv(v_ref[:, cols].astype(F32), cv_ref[:, cols], conv_k))
        q = q * lax.rsqrt(jnp.sum(q * q, axis=-1, keepdims=True) + EPS) * (dk ** -0.5)
        k = k * lax.rsqrt(jnp.sum(k * k, axis=-1, keepdims=True) + EPS)
        qs[hh] = q
        ks[hh] = k
        for d in range(2):
            ch = 2 * hh + d
            head = head0 + hh
            g_cum = column(2 * d, d * n_heads + head)
            g_tail = column(2 * d + 1, d * n_heads + head)
            beta = column(4, 2 * n_heads + d * n_heads + head)
            e_cum = jnp.exp(g_cum)
            kb = k * beta
            gc_s[ch] = g_cum
            kb_s[ch] = kb
            w_s[ch] = kb * e_cum
            u_s[ch] = v * beta
            qd_s[ch] = q * e_cum
            ktl_s[ch] = k * jnp.exp(g_tail)
            gl_s[ch] = jnp.exp(g_cum + g_tail)

    ri = lax.broadcasted_iota(jnp.int32, (c, c), 0)
    ci = lax.broadcasted_iota(jnp.int32, (c, c), 1)
    incl = (ri >= ci, ri <= ci)
    strict = (ri > ci, ri < ci)

    def phase1(it, carry):
        loaded = []
        for gi in range(lockstep):
            ic = it * lockstep + gi
            sl = pl.ds(pl.multiple_of(ic * c, c), c)
            for hh, d in chains:
                ch = 2 * hh + d
                loaded.append((ic, sl, ch, d, qs[hh, sl, :], ks[hh, sl, :], gc_s[ch, sl, :], kb_s[ch, sl, :],
                               w_s[ch, sl, :], u_s[ch, sl, :], ktl_s[ch, sl, :]))
        decays, kqs = [], []
        for ic, sl, ch, d, qc, kc, gc, kb, w0, u0, ktl in loaded:
            diff = gc[:, :c] - gc.T[:c, :]
            decays.append(jnp.where(incl[d], jnp.exp(jnp.where(incl[d], diff, 0.0)), 0.0))
            kqs.append(_dot_nt(jnp.concatenate([kb, qc], axis=0), kc))
        a_kks = [jnp.where(strict[item[3]], kq[:c] * decay, 0.0)
                 for item, kq, decay in zip(loaded, kqs, decays)]
        es = _unit_tri_inverses_minus_eye(a_kks, ri, ci)
        rhss = [jnp.concatenate([item[8], item[9]], axis=1) for item in loaded]
        wus = [rhs + _dot(e, rhs) for e, rhs in zip(es, rhss)]
        for item, kq, decay, wu in zip(loaded, kqs, decays, wus):
            ic, sl, ch = item[:3]
            w_s[ch, sl, :] = wu[:, :dk]
            u_s[ch, sl, :] = wu[:, dk:]
            ak_s[ch, ic] = jnp.concatenate([kq[c:] * decay, item[10].T], axis=0)
        return carry

    lax.fori_loop(0, n_chunks // lockstep, phase1, 0)

    for hh, d in chains:
        st_s[2 * hh + d] = s0_ref[d, hh]

    def phase2(i, carry):
        n = 2 * hb
        ics = [n_chunks - 1 - i if d else i for _, d in chains]
        r0s = [pl.multiple_of(ic * c, c) for ic in ics]
        sls = [pl.ds(r0, c) for r0 in r0s]
        ss = [st_s[ch] for ch in range(n)]
        wqs = [_dot(jnp.concatenate([w_s[ch, sls[ch], :], qd_s[ch, sls[ch], :]], axis=0), ss[ch]) for ch in range(n)]
        v_news = [u_s[ch, sls[ch], :] - wqs[ch][:c] for ch in range(n)]
        outs = [_dot(ak_s[ch, ics[ch]], v_news[ch]) for ch in range(n)]
        for ch in range(n):
            o_s[ch, sls[ch], :] = wqs[ch][c:] + outs[ch][:c]
            st_s[ch] = ss[ch] * gl_s[ch, pl.ds(r0s[ch], 1), :] + outs[ch][c:]
        return carry

    lax.fori_loop(0, n_chunks, phase2, 0, unroll=2)

    for hh, d in chains:
        sfin_ref[d, hh] = st_s[2 * hh + d]
    for hh in range(hb):
        cols = slice(hh * dk, (hh + 1) * dk)
        o = o_s[2 * hh] + o_s[2 * hh + 1]
        o = o * lax.rsqrt(jnp.mean(o * o, axis=-1, keepdims=True) + EPS)
        o_ref[:, cols] = (o * onorm_ref[...] * _silu(gt_ref[:, cols].astype(F32))).astype(o_ref.dtype)


def _gdn_core(proj, ab, conv_w, a_log, dt_bias, onorm, s0, dst, states_dst, *, t_total, row0, n_seq, seq,
              n_heads, dk, dv, s0_spec, state_slot):
    assert dk == dv == V7X_LANES and row0 % seq == 0
    conv_k = conv_w.shape[0]
    r0 = row0 // seq
    n_chunks = seq // CHUNK
    hb = _tile(n_heads, min(GDN_HEADS_PER_STEP, max(1, GDN_ROWS_PER_STEP // seq)), 1)
    lockstep = _tile(n_chunks, max(1, GDN_PHASE1_CHAINS // (2 * hb)), 1)
    gate_pad = lambda p: jnp.pad(p.reshape(1, 2 * n_heads), ((0, 0), (0, V7X_LANES - 2 * n_heads)))
    lane_vec = pl.BlockSpec((1, V7X_LANES), lambda b, h: (0, 0))
    if s0 is None:
        s0 = jnp.zeros((2, hb, dk, dv), F32)
        s0_in = pl.BlockSpec((2, hb, dk, dv), lambda b, h: (0, 0, 0, 0))
    else:
        s0_in = s0_spec(hb)
    nb = n_heads // hb
    col = lambda sec: pl.BlockSpec((seq, hb * dk), lambda b, h: (b + r0, sec * nb + h))
    cw = lambda sec: pl.BlockSpec((conv_k, hb * dk), lambda b, h: (0, sec * nb + h))
    f32 = lambda *s: pltpu.VMEM(s, F32)
    in_specs = [lane_vec, lane_vec, col(0), col(1), col(2), col(3),
                pl.BlockSpec((seq, V7X_LANES), lambda b, h: (b + r0, 0)),
                cw(0), cw(1), cw(2),
                pl.BlockSpec((1, dv), lambda b, h: (0, 0)),
                s0_in]
    args = [gate_pad(a_log), gate_pad(dt_bias), proj, proj, proj, proj, ab, conv_w, conv_w, conv_w,
            onorm.reshape(1, dv), s0]
    aliases = {}
    for out_idx, kept in enumerate((dst, states_dst)):
        if kept is not None:
            in_specs.append(pl.BlockSpec(memory_space=pl.ANY))
            args.append(kept)
            aliases[len(args) - 1] = out_idx
    nc = 2 * hb
    slot, n_slots = state_slot
    return pl.pallas_call(
        functools.partial(_gdn_kernel, n_heads=n_heads, conv_k=conv_k, lockstep=lockstep,
                          n_kept=len(aliases)),
        out_shape=(jax.ShapeDtypeStruct((t_total, n_heads * dv), BF16),
                   jax.ShapeDtypeStruct((n_seq, n_slots, 2, n_heads, dk, dv), F32)),
        grid=(n_seq, nb),
        in_specs=in_specs,
        out_specs=(pl.BlockSpec((seq, hb * dv), lambda b, h: (b + r0, h)),
                   pl.BlockSpec((None, None, 2, hb, dk, dv), lambda b, h: (b, slot, 0, h, 0, 0))),
        scratch_shapes=[f32(5, seq, V7X_LANES),
                        f32(hb, seq, dk), f32(hb, seq, dk),
                        f32(nc, seq, V7X_LANES), f32(nc, seq, dk),
                        f32(nc, seq, dk), f32(nc, seq, dv), f32(nc, seq, dk),
                        f32(nc, seq, dk),
                        f32(nc, n_chunks, CHUNK + dk, CHUNK),
                        f32(nc, seq, V7X_LANES),
                        f32(nc, seq, dv), f32(nc, dk, dv)],
        input_output_aliases=aliases,
        compiler_params=_cparams("parallel", "arbitrary"),
        name="gdn_core",
    )(*args)


def _odd_dft(seq):
    k = jnp.arange(seq, dtype=jnp.int32)[:, None]
    m = jnp.arange(seq, dtype=jnp.int32)[None, :]
    r = ((2 * k + 1) * m) % (4 * seq)
    ang = r.astype(F32) * (math.pi / (2 * seq))
    return jnp.cos(ang), jnp.sin(ang)


def _filter_kernel(feat_ref, w1_ref, b1_ref, w2_ref, b2_ref, fr_ref, w3f_ref, w3b_ref, dl_ref,
                   dft_hi_ref, dft_lo_ref, hre_ref, him_ref):
    seq = feat_ref.shape[0]
    feat = feat_ref[...]
    fr = fr_ref[...]
    hid = jnp.sin(fr * (_dot(feat, w1_ref[...]) + b1_ref[...]))
    hid = jnp.sin(fr * (_dot(hid, w2_ref[...]) + b2_ref[...]))
    window = jnp.exp(-feat[:, 0:1] * dl_ref[...])
    hf = _dot(hid, w3f_ref[...]) * window
    hb = _dot(hid, w3b_ref[...]) * window
    row = lax.broadcasted_iota(jnp.int32, hb.shape, 0)
    hb = jnp.where(row == 0, 0.0, hb)

    def dft(rows, val):
        v_hi = val.astype(BF16)
        v_lo = (val - v_hi.astype(F32)).astype(BF16)
        m_hi, m_lo = dft_hi_ref[rows, :], dft_lo_ref[rows, :]
        return (jnp.dot(m_hi, v_hi, preferred_element_type=F32)
                + (jnp.dot(m_hi, v_lo, preferred_element_type=F32)
                   + jnp.dot(m_lo, v_hi, preferred_element_type=F32)))

    hre_ref[...] = dft(slice(0, seq), hf + hb)
    him_ref[...] = dft(slice(seq, 2 * seq), hb - hf)


def _hyena_filters(seq, d, w1, b1, w2, b2, w3, freq, dft_hi, dft_lo):
    emb, hid = w1.shape
    bands = (emb - 1) // 2
    t = jnp.linspace(0.0, 1.0, seq, dtype=F32)[:, None]
    wpos = (2.0 * math.pi / seq) * jnp.arange(seq, dtype=F32)[:, None]
    fb = jnp.linspace(1e-4, bands - 1, bands, dtype=F32)[None, :]
    feat = jnp.concatenate([t, jnp.cos(fb * wpos), -jnp.sin(fb * wpos)], axis=-1)
    feat = jnp.pad(feat, ((0, 0), (0, V7X_LANES - emb)))
    w1p = jnp.pad(w1, ((0, V7X_LANES - emb), (0, 0)))
    max_decay = math.log(HY_TARGET) / HY_SHORT_PCT
    min_decay = math.log(HY_TARGET) / HY_LONG_PCT
    deltas = jnp.abs(jnp.linspace(min_decay, max_decay, d, dtype=F32))[None, :]
    tc = _tile(d, 256, V7X_LANES)
    nt = d // tc
    full = lambda r, c: pl.BlockSpec((r, c), lambda n, j: (0, 0))
    out_spec = pl.BlockSpec((None, seq, tc), lambda n, j: (n, 0, j))
    return pl.pallas_call(
        _filter_kernel,
        out_shape=(jax.ShapeDtypeStruct((HY_ORDER, seq, d), F32),) * 2,
        grid=(HY_ORDER, nt),
        in_specs=[full(seq, V7X_LANES), full(V7X_LANES, hid), full(1, hid), full(hid, hid), full(1, hid),
                  full(1, hid),
                  pl.BlockSpec((hid, tc), lambda n, j: (0, (2 * n) * nt + j)),
                  pl.BlockSpec((hid, tc), lambda n, j: (0, (2 * n + 1) * nt + j)),
                  pl.BlockSpec((1, tc), lambda n, j: (0, j)),
                  full(2 * seq, seq), full(2 * seq, seq)],
        out_specs=(out_spec, out_spec),
        compiler_params=_cparams("parallel", "parallel"),
        name="hyena_filter",
    )(feat, w1p, b1.reshape(1, hid), w2, b2.reshape(1, hid), freq.reshape(1, hid), w3, w3, deltas,
      dft_hi, dft_lo)


HYENA_ROWS_PER_STEP = 2048


def _hyconv_kernel(*refs, conv_z, conv_k, has_dst):
    z_ref, x_ref, hre_ref, him_ref, skip_ref, cz_ref, cx_ref, fwd_ref, inv_ref = refs[:9]
    o_ref = refs[9 + has_dst]
    seq = hre_ref.shape[0]
    n_sub = z_ref.shape[0] // seq
    zs = []
    for s in range(n_sub):
        z = z_ref[s * seq:(s + 1) * seq, :].astype(F32)
        zs.append(_dwconv(z, cz_ref[...], conv_k) if conv_z else z)
    pqs = [jnp.dot(fwd_ref[...], z.astype(BF16), preferred_element_type=F32) for z in zs]
    hre, him = hre_ref[...], him_ref[...]
    for s in range(n_sub):
        p, q = pqs[s][:seq], pqs[s][seq:]
        y_spec = jnp.concatenate([p * hre + q * him, p * him - q * hre], axis=0).astype(BF16)
        y = jnp.dot(inv_ref[...], y_spec, preferred_element_type=F32)
        x = _dwconv(x_ref[s * seq:(s + 1) * seq, :].astype(F32), cx_ref[...], conv_k)
        o_ref[s * seq:(s + 1) * seq, :] = (x * (y + skip_ref[...] * zs[s])).astype(o_ref.dtype)


def _hyconv(z, z_col0, conv_z, proj, x_col0, hre, him, order, skip, conv_w, fwd, inv, dst,
            *, row0, n_seq, seq, d, out_rows, out_row0, out_dtype):
    conv_k = conv_w.shape[0]
    n_sub = max(s for s in range(1, max(1, HYENA_ROWS_PER_STEP // seq) + 1)
                if n_seq % s == 0 and row0 % (s * seq) == 0 and out_row0 % (s * seq) == 0)
    blk = n_sub * seq
    tc = _tile(d, 512 if blk <= 1024 else 256, V7X_LANES)
    nt = d // tc
    r0 = row0 // blk
    zr0 = r0 if conv_z else 0
    zc, xc = z_col0 // tc, x_col0 // tc
    in_specs = [pl.BlockSpec((blk, tc), lambda j, b: (b + zr0, zc + j)),
                pl.BlockSpec((blk, tc), lambda j, b: (b + r0, xc + j)),
                pl.BlockSpec((None, seq, tc), lambda j, b: (order, 0, j)),
                pl.BlockSpec((None, seq, tc), lambda j, b: (order, 0, j)),
                pl.BlockSpec((1, tc), lambda j, b: (0, j)),
                pl.BlockSpec((conv_k, tc), lambda j, b: (0, zc + j)),
                pl.BlockSpec((conv_k, tc), lambda j, b: (0, xc + j)),
                pl.BlockSpec((2 * seq, seq), lambda j, b: (0, 0), pipeline_mode=_RESIDENT),
                pl.BlockSpec((seq, 2 * seq), lambda j, b: (0, 0), pipeline_mode=_RESIDENT)]
    args = [z, proj, hre, him, skip.reshape(1, d), conv_w, conv_w, fwd, inv]
    aliases = {}
    if dst is not None:
        in_specs.append(pl.BlockSpec(memory_space=pl.ANY))
        args.append(dst)
        aliases = {len(args) - 1: 0}
    out_r0 = out_row0 // blk
    return pl.pallas_call(
        functools.partial(_hyconv_kernel, conv_z=conv_z, conv_k=conv_k, has_dst=dst is not None),
        out_shape=jax.ShapeDtypeStruct((out_rows, d), out_dtype),
        grid=(nt, n_seq // n_sub),
        in_specs=in_specs,
        out_specs=pl.BlockSpec((blk, tc), lambda j, b: (b + out_r0, j)),
        input_output_aliases=aliases,
        compiler_params=_cparams("parallel", "parallel"),
        name="hyena_conv",
    )(*args)


def _grid_pos_emb(n_tokens, d):
    rows = n_tokens // GRID_W
    r, col = jnp.meshgrid(jnp.arange(rows), jnp.arange(GRID_W), indexing='ij')
    quarter = d // 4
    omega = 1.0 / (POS_BASE ** (jnp.arange(quarter, dtype=F32) / quarter))

    def emb1d(p):
        a = p.reshape(-1, 1).astype(F32) * omega[None, :]
        return jnp.concatenate([jnp.sin(a), jnp.cos(a)], axis=-1)

    return jnp.concatenate([emb1d(r), emb1d(col)], axis=-1)


def kernel(x_prompt, x_sample, state_delta, c, c_ctx, ada_w, ada_b, norm1_g, norm2_g, gdn_w_in, gdn_conv, gdn_a_log, gdn_dt_bias, gdn_onorm, gdn_w_out, hy_w_in, hy_b_in, hy_conv, hy_f_w1, hy_f_b1, hy_f_w2, hy_f_b2, hy_f_w3, hy_freq, hy_skip, hy_w_out, hy_b_out, ffn_w_gu, ffn_w_down, final_g):
    bc, lc, d = x_prompt.shape
    bl, ll, _ = x_sample.shape
    depth = ada_w.shape[0]
    n_heads, dk, dv = state_delta.shape[3:]
    tc_rows, tl_rows = bc * lc, bl * ll
    t = tc_rows + tl_rows
    assert tc_rows % ll == 0 and ll % lc == 0
    rows = _Rows(tc_rows, ll, t, _tile(math.gcd(tc_rows, ll), 512, V7X_SUBLANES))
    rows_small = _Rows(tc_rows, ll, t, _tile(math.gcd(tc_rows, ll), 256, V7X_SUBLANES))

    bm = 1 + bl
    bm_pad = -(-bm // V7X_SUBLANES) * V7X_SUBLANES
    cvec = jnp.concatenate([c_ctx[None, :], c, jnp.zeros((bm_pad - bm, d), F32)], axis=0)
    mods_all = _ada(cvec, ada_w, ada_b).reshape(depth, bm_pad, 6, d)

    x = _embed(x_prompt.reshape(tc_rows, d), x_sample.reshape(tl_rows, d), _grid_pos_emb(ll, d), rows_small)

    n_hy = hy_w_in.shape[0]
    dft = {}
    for seq in (lc, ll):
        cos_m, sin_m = _odd_dft(seq)
        fwd_f32 = jnp.concatenate([cos_m, sin_m], axis=0)
        fwd = fwd_f32.astype(BF16)
        fwd_lo = (fwd_f32 - fwd.astype(F32)).astype(BF16)
        inv = (jnp.concatenate([cos_m.T, -sin_m.T], axis=1) / seq).astype(BF16)
        spectra = [_hyena_filters(seq, d, hy_f_w1[j], hy_f_b1[j], hy_f_w2[j], hy_f_b2[j], hy_f_w3[j],
                                  hy_freq[j], fwd, fwd_lo) for j in range(n_hy)]
        dft[seq] = (fwd, inv, spectra)

    zero_b = jnp.zeros((d,), F32)
    mixed = jnp.zeros((t, n_heads * dv), BF16)
    mixed_hy = jnp.zeros((t, d), BF16)
    new_state_delta = jnp.zeros((bc, gdn_w_in.shape[0], 2, n_heads, dk, dv), F32)
    n_mixers = 2
    n_gdn = gdn_w_in.shape[0]
    gdn_w_in_b = jnp.pad(gdn_w_in, ((0, 0), (0, 0), (0, V7X_LANES - 4 * n_heads))).astype(BF16)
    gdn_w_out_b, hy_w_in_b, hy_w_out_b = (w.astype(BF16) for w in (gdn_w_out, hy_w_in, hy_w_out))
    ffn_w_gu_b, ffn_w_down_b = ffn_w_gu.astype(BF16), ffn_w_down.astype(BF16)
    for layer in range(depth):
        j = layer // n_mixers
        if layer % n_mixers == 0:
            proj, ab = _in_proj(x, norm1_g[layer], mods_all, layer, gdn_w_in_b, j,
                                jnp.zeros((gdn_w_in_b.shape[2],), F32), rows, n_side=V7X_LANES)
            gdn = dict(t_total=t, n_heads=n_heads, dk=dk, dv=dv)
            weights = (gdn_conv[j], gdn_a_log[j], gdn_dt_bias[j], gdn_onorm[j])
            mixed, new_state_delta = _gdn_core(proj, ab, *weights, None, mixed, new_state_delta, row0=0, n_seq=bc,
                                               seq=lc, s0_spec=None, state_slot=(j, n_gdn), **gdn)
            s0_spec = lambda hb, j=j: pl.BlockSpec((None, None, 2, hb, dk, dv), lambda b, h: (b, j, 0, h, 0, 0))
            mixed, _ = _gdn_core(proj, ab, *weights, state_delta, mixed, None, row0=tc_rows, n_seq=bl, seq=ll,
                                 s0_spec=s0_spec, state_slot=(0, 1), **gdn)
            mixer = (mixed, gdn_w_out_b, j, zero_b)
        else:
            proj = _in_proj(x, norm1_g[layer], mods_all, layer, hy_w_in_b, j, hy_b_in[j], rows)
            for row0, n_seq, seq in ((0, bc, lc), (tc_rows, bl, ll)):
                fwd, inv, spectra = dft[seq]
                hre, him = spectra[j]
                z = proj
                for n in range(HY_ORDER):
                    last = n == HY_ORDER - 1
                    z = _hyconv(z, 0, n == 0, proj, (n + 1) * d, hre, him, n, hy_skip[j, n], hy_conv[j],
                                fwd, inv, mixed_hy if last else None, row0=row0, n_seq=n_seq, seq=seq, d=d,
                                out_rows=t if last else n_seq * seq, out_row0=row0 if last else 0,
                                out_dtype=BF16 if last else F32)
                mixed_hy = z
            mixer = (mixed_hy, hy_w_out_b, j, hy_b_out[j])
        ffn = functools.partial(_ffn, x, norm2_g[layer], mods_all, ffn_w_gu_b, ffn_w_down_b, layer, rows,
                                mixer=mixer)
        if layer < depth - 1:
            x = ffn()
        else:
            y_prompt = ffn(final_g=final_g, tile0=0, n_tiles=rows.n_ctx_tiles)
            y_sample = ffn(final_g=final_g, tile0=rows.n_ctx_tiles, n_tiles=rows.n_tiles - rows.n_ctx_tiles)
    return (y_prompt.reshape(bc, lc, d), y_sample.reshape(bl, ll, d), new_state_delta)
```

```python
import functools
import math

import jax
import jax.numpy as jnp
from jax import lax
from jax.experimental import pallas as pl
from jax.experimental.pallas import tpu as pltpu

GRID_W = 64
CHUNK = 64
HY_ORDER = 2
HY_TARGET = 1e-2
HY_SHORT_PCT = 0.3
HY_LONG_PCT = 1.5
POS_BASE = 10000.0
EPS = 1e-6

V7X_LANES = 128
V7X_SUBLANES = 8
V7X_VMEM_LIMIT_BYTES = 48 * 1024 * 1024

BF16 = jnp.bfloat16
F32 = jnp.float32
HIGHEST = lax.Precision.HIGHEST


def _cparams(*sem):
    return pltpu.CompilerParams(dimension_semantics=sem, vmem_limit_bytes=V7X_VMEM_LIMIT_BYTES)


def _tile(n, target, align):
    if n <= target:
        return n
    best = None
    for t in range(align, target + 1, align):
        if n % t == 0:
            best = t
    assert best is not None, (n, target, align)
    return best


def _dot(a, b):
    return jnp.dot(a.astype(BF16), b.astype(BF16), preferred_element_type=F32)


def _dot_nt(a, b):
    return lax.dot_general(a.astype(BF16), b.astype(BF16), (((1,), (1,)), ((), ())),
                           preferred_element_type=F32)


def _dot_hi(a, b):
    return jnp.dot(a, b, preferred_element_type=F32, precision=HIGHEST)


def _silu(x):
    return x * jax.nn.sigmoid(x)


def _norm_mod(x, g, shift, scale):
    ms = jnp.mean(x * x, axis=-1, keepdims=True)
    return (x * lax.rsqrt(ms + EPS) * g) * (1.0 + scale) + shift


class _Rows:
    def __init__(self, tc, ll, t, tm):
        assert tc % tm == 0 and ll % tm == 0 and t % tm == 0
        self.n_ctx_tiles = tc // tm
        self.tiles_per_lat = ll // tm
        self.n_tiles = t // tm
        self.tm = tm

    def mod_index(self, i):
        lat = 1 + (i - self.n_ctx_tiles) // self.tiles_per_lat
        return jnp.where(i < self.n_ctx_tiles, 0, lat)


def _ada_kernel(c_ref, w_ref, b_ref, o_ref):
    o_ref[...] = _dot(_silu(c_ref[...]), w_ref[...]) + b_ref[...]


def _ada(cvec, ada_w, ada_b):
    depth, d, n = ada_w.shape
    bm = cvec.shape[0]
    tn = _tile(n, 1536, V7X_LANES)
    return pl.pallas_call(
        _ada_kernel,
        out_shape=jax.ShapeDtypeStruct((depth, bm, n), F32),
        grid=(depth, n // tn),
        in_specs=[pl.BlockSpec((bm, d), lambda l, j: (0, 0)),
                  pl.BlockSpec((None, d, tn), lambda l, j: (l, 0, j)),
                  pl.BlockSpec((None, 1, tn), lambda l, j: (l, 0, j))],
        out_specs=pl.BlockSpec((None, bm, tn), lambda l, j: (l, 0, j)),
        compiler_params=_cparams("parallel", "parallel"),
        name="ada",
    )(cvec, ada_w, ada_b.reshape(depth, 1, n))


def _embed_kernel(xp_ref, xs_ref, pos_ref, o_ref, *, n_ctx_tiles):
    i = pl.program_id(0)

    @pl.when(i < n_ctx_tiles)
    def _():
        o_ref[...] = xp_ref[...]

    @pl.when(i >= n_ctx_tiles)
    def _():
        o_ref[...] = xs_ref[...] + pos_ref[...]


def _embed(xp, xs, pos, rows):
    t, d = xp.shape[0] + xs.shape[0], xp.shape[1]
    tm, nct = rows.tm, rows.n_ctx_tiles
    npos = pos.shape[0] // tm
    return pl.pallas_call(
        functools.partial(_embed_kernel, n_ctx_tiles=nct),
        out_shape=jax.ShapeDtypeStruct((t, d), F32),
        grid=(rows.n_tiles,),
        in_specs=[pl.BlockSpec((tm, d), lambda i: (jnp.minimum(i, nct - 1), 0)),
                  pl.BlockSpec((tm, d), lambda i: (jnp.maximum(i - nct, 0), 0)),
                  pl.BlockSpec((tm, d), lambda i: (jnp.maximum(i - nct, 0) % npos, 0))],
        out_specs=pl.BlockSpec((tm, d), lambda i: (i, 0)),
        compiler_params=_cparams("parallel"),
        name="embed",
    )(xp, xs, pos)


_RESIDENT = pl.Buffered(1)


def _in_kernel(x_ref, g_ref, mod_ref, w_ref, b_ref, o_ref, *side_ref, chunk):
    m = mod_ref[...]
    h = _norm_mod(x_ref[...], g_ref[...], m[0:1, :], m[1:2, :]).astype(BF16)
    n = o_ref.shape[1]
    for c0 in range(0, n, chunk):
        cols = slice(c0, c0 + chunk)
        y = jnp.dot(h, w_ref[:, cols], preferred_element_type=F32) + b_ref[:, cols]
        o_ref[:, cols] = y.astype(o_ref.dtype)
    if side_ref:
        side_ref[0][...] = jnp.dot(h, w_ref[:, n:], preferred_element_type=F32) + b_ref[:, n:]


def _in_proj(x, g, mods_all, layer, w_all, w_index, b, rows, n_side=0):
    t, d = x.shape
    n = w_all.shape[2] - n_side
    tm = rows.tm
    chunk = _tile(n, 512, V7X_LANES)
    out_shape = [jax.ShapeDtypeStruct((t, n), BF16)]
    out_specs = [pl.BlockSpec((tm, n), lambda i: (i, 0))]
    if n_side:
        out_shape.append(jax.ShapeDtypeStruct((t, n_side), F32))
        out_specs.append(pl.BlockSpec((tm, n_side), lambda i: (i, 0)))
    out = pl.pallas_call(
        functools.partial(_in_kernel, chunk=chunk),
        out_shape=out_shape,
        grid=(rows.n_tiles,),
        in_specs=[pl.BlockSpec((tm, d), lambda i: (i, 0)),
                  pl.BlockSpec((1, d), lambda i: (0, 0), pipeline_mode=_RESIDENT),
                  pl.BlockSpec((None, None, 6, d), lambda i: (layer, rows.mod_index(i), 0, 0)),
                  pl.BlockSpec((None, d, n + n_side), lambda i: (w_index, 0, 0), pipeline_mode=_RESIDENT),
                  pl.BlockSpec((1, n + n_side), lambda i: (0, 0), pipeline_mode=_RESIDENT)],
        out_specs=out_specs,
        compiler_params=_cparams("parallel"),
        name="in_proj",
    )(x, g.reshape(1, d), mods_all, w_all, b.reshape(1, n + n_side))
    return out if n_side else out[0]


def _ffn_kernel(x_ref, g_ref, mod_ref, wgu_ref, wd_ref, *rest, chunk, has_mixer, has_final):
    o_ref = rest[-1]
    x = x_ref[...]
    m = mod_ref[...]
    if has_mixer:
        a_ref, wo_ref, bo_ref = rest[:3]
        x = x + m[2:3, :] * (_dot(a_ref[...], wo_ref[...]) + bo_ref[...])
    h = _norm_mod(x, g_ref[...], m[3:4, :], m[4:5, :]).astype(BF16)
    f = wd_ref.shape[0]
    n_chunks = f // chunk

    def gate_up(k):
        gate = jnp.dot(h, wgu_ref[:, k * chunk:(k + 1) * chunk], preferred_element_type=F32)
        up = jnp.dot(h, wgu_ref[:, f + k * chunk:f + (k + 1) * chunk], preferred_element_type=F32)
        return gate, up

    y = None
    pending = gate_up(0)
    for k in range(n_chunks):
        following = gate_up(k + 1) if k + 1 < n_chunks else None
        act = (_silu(pending[0]) * pending[1]).astype(BF16)
        part = jnp.dot(act, wd_ref[k * chunk:(k + 1) * chunk, :], preferred_element_type=F32)
        y = part if y is None else y + part
        pending = following
    out = x + m[5:6, :] * y
    if has_final:
        out = out * lax.rsqrt(jnp.mean(out * out, axis=-1, keepdims=True) + EPS) * rest[-2][...]
    o_ref[...] = out


def _ffn(x, g, mods_all, w_gu_all, w_down_all, layer, rows, mixer=None, final_g=None, tile0=0, n_tiles=None):
    d = x.shape[1]
    f = w_down_all.shape[1]
    tm = rows.tm
    n_tiles = rows.n_tiles if n_tiles is None else n_tiles
    chunk = _tile(f, 256, V7X_LANES)
    resident = lambda *s: pl.BlockSpec(s, lambda i: (0,) * len(s), pipeline_mode=_RESIDENT)
    in_specs = [pl.BlockSpec((tm, d), lambda i: (i + tile0, 0)),
                resident(1, d),
                pl.BlockSpec((None, None, 6, d), lambda i: (layer, rows.mod_index(i + tile0), 0, 0)),
                pl.BlockSpec((None, d, 2 * f), lambda i: (layer, 0, 0), pipeline_mode=_RESIDENT),
                pl.BlockSpec((None, f, d), lambda i: (layer, 0, 0), pipeline_mode=_RESIDENT)]
    args = [x, g.reshape(1, d), mods_all, w_gu_all, w_down_all]
    if mixer is not None:
        a, wo_all, wo_index, bo = mixer
        k = a.shape[1]
        in_specs += [pl.BlockSpec((tm, k), lambda i: (i + tile0, 0)),
                     pl.BlockSpec((None, k, d), lambda i: (wo_index, 0, 0), pipeline_mode=_RESIDENT),
                     resident(1, d)]
        args += [a, wo_all, bo.reshape(1, d)]
    if final_g is not None:
        in_specs.append(resident(1, d))
        args.append(final_g.reshape(1, d))
    return pl.pallas_call(
        functools.partial(_ffn_kernel, chunk=chunk, has_mixer=mixer is not None, has_final=final_g is not None),
        out_shape=jax.ShapeDtypeStruct((n_tiles * tm, d), F32),
        grid=(n_tiles,),
        in_specs=in_specs,
        out_specs=pl.BlockSpec((tm, d), lambda i: (i, 0)),
        compiler_params=_cparams("parallel"),
        name="ffn",
    )(*args)


def _dwconv(x, w, k):
    n = x.shape[0]
    half = k // 2
    e = V7X_SUBLANES
    assert half <= e and n >= 4 * e
    taps = [w[j:j + 1, :] for j in range(k)]
    shifts = [s for s in range(-half, half + 1) if s != 0]

    def conv(v, mask):
        acc = v * taps[half]
        for s in shifts:
            shifted = pltpu.roll(v, (-s) % v.shape[0], axis=0)
            acc = acc + (shifted if mask is None else jnp.where(mask(s), shifted, 0.0)) * taps[s + half]
        return acc

    row = lax.broadcasted_iota(jnp.int32, (2 * e, x.shape[1]), 0)
    top = conv(x[:2 * e], lambda s: row + s >= 0)[:e]
    bottom = conv(x[n - 2 * e:], lambda s: row + s < 2 * e)[e:]
    return jnp.concatenate([top, conv(x, None)[e:n - e], bottom], axis=0)


TRI_BASE = 8
GDN_PHASE1_CHAINS = 32
GDN_HEADS_PER_STEP = 4
GDN_ROWS_PER_STEP = 2048


def _unit_tri_inverses_minus_eye(mats, ri, ci):
    c = mats[0].shape[0]

    def same_block(s):
        sh = int(math.log2(s))
        return (ri >> sh) == (ci >> sh)

    ps = [jnp.where(same_block(TRI_BASE), -a, 0.0) for a in mats]
    es = ps
    n_lvl = int(math.log2(TRI_BASE))
    for lvl in range(n_lvl):
        es = [e + _dot(p, e) for p, e in zip(ps, es)]
        if lvl < n_lvl - 1:
            ps = [_dot(p, p) for p in ps]
    s = TRI_BASE
    while s < c:
        mask = jnp.logical_and(same_block(2 * s), jnp.logical_not(same_block(s)))
        offs = [jnp.where(mask, a, 0.0) for a in mats]
        ys = [off + _dot(off, e) for off, e in zip(offs, es)]
        es = [e - (y + _dot(e, y)) for e, y in zip(es, ys)]
        s *= 2
    return es


def _chunk_cumsum(x, pos, reverse):
    n = x.shape[0]
    s = 1
    while s < CHUNK:
        if reverse:
            x = x + jnp.where(pos + s < CHUNK, pltpu.roll(x, n - s, axis=0), 0.0)
        else:
            x = x + jnp.where(pos >= s, pltpu.roll(x, s, axis=0), 0.0)
        s *= 2
    return x


def _gdn_kernel(*refs, n_heads, conv_k, lockstep, n_kept):
    (alog_ref, dtb_ref, q_ref, k_ref, v_ref, gt_ref, ab_ref, cq_ref, ck_ref, cv_ref,
     onorm_ref, s0_ref) = refs[:12]
    (o_ref, sfin_ref, gates_s, qs, ks, gc_s, kb_s, w_s, u_s, qd_s, ktl_s, ak_s, gl_s, o_s,
     st_s) = refs[12 + n_kept:]
    hb, seq, dk = qs.shape
    head0 = pl.program_id(1) * hb
    c = CHUNK
    n_chunks = seq // c
    chains = [(hh, d) for hh in range(hb) for d in range(2)]

    @pl.when(head0 == 0)
    def _():
        ab = ab_ref[...]
        pos = jnp.bitwise_and(lax.broadcasted_iota(jnp.int32, ab.shape, 0), c - 1)
        g = -jnp.exp(alog_ref[...]) * jax.nn.softplus(ab + dtb_ref[...])
        g_fwd = _chunk_cumsum(g, pos, reverse=False)
        g_rev = _chunk_cumsum(g, pos, reverse=True)
        gates_s[0] = g_fwd
        gates_s[1] = g_rev - g
        gates_s[2] = g_rev
        gates_s[3] = g_fwd - g
        gates_s[4] = jax.nn.sigmoid(ab)

    lane = lax.broadcasted_iota(jnp.int32, (1, V7X_LANES), 1)

    def column(i, idx):
        one_hot = (lane == idx).astype(F32)
        col = jnp.sum(gates_s[i] * one_hot, axis=1, keepdims=True)
        return jnp.broadcast_to(col, (seq, V7X_LANES))

    for hh in range(hb):
        cols = slice(hh * dk, (hh + 1) * dk)
        q = _silu(_dwconv(q_ref[:, cols].astype(F32), cq_ref[:, cols], conv_k))
        k = _silu(_dwconv(k_ref[:, cols].astype(F32), ck_ref[:, cols], conv_k))
        v = _silu(_dwconv(v_ref[:, cols].astype(F32), cv_ref[:, cols], conv_k))
        q = q * lax.rsqrt(jnp.sum(q * q, axis=-1, keepdims=True) + EPS) * (dk ** -0.5)
        k = k * lax.rsqrt(jnp.sum(k * k, axis=-1, keepdims=True) + EPS)
        qs[hh] = q
        ks[hh] = k
        for d in range(2):
            ch = 2 * hh + d
            head = head0 + hh
            g_cum = column(2 * d, d * n_heads + head)
            g_tail = column(2 * d + 1, d * n_heads + head)
            beta = column(4, 2 * n_heads + d * n_heads + head)
            e_cum = jnp.exp(g_cum)
            kb = k * beta
            gc_s[ch] = g_cum
            kb_s[ch] = kb
            w_s[ch] = kb * e_cum
            u_s[ch] = v * beta
            qd_s[ch] = q * e_cum
            ktl_s[ch] = k * jnp.exp(g_tail)
            gl_s[ch] = jnp.exp(g_cum + g_tail)

    ri = lax.broadcasted_iota(jnp.int32, (c, c), 0)
    ci = lax.broadcasted_iota(jnp.int32, (c, c), 1)
    incl = (ri >= ci, ri <= ci)
    strict = (ri > ci, ri < ci)

    def phase1(it, carry):
        loaded = []
        for gi in range(lockstep):
            ic = it * lockstep + gi
            sl = pl.ds(pl.multiple_of(ic * c, c), c)
            for hh, d in chains:
                ch = 2 * hh + d
                loaded.append((ic, sl, ch, d, qs[hh, sl, :], ks[hh, sl, :], gc_s[ch, sl, :], kb_s[ch, sl, :],
                               w_s[ch, sl, :], u_s[ch, sl, :], ktl_s[ch, sl, :]))
        decays, kqs = [], []
        for ic, sl, ch, d, qc, kc, gc, kb, w0, u0, ktl in loaded:
            diff = gc[:, :c] - gc.T[:c, :]
            decays.append(jnp.where(incl[d], jnp.exp(jnp.where(incl[d], diff, 0.0)), 0.0))
            kqs.append(_dot_nt(jnp.concatenate([kb, qc], axis=0), kc))
        a_kks = [jnp.where(strict[item[3]], kq[:c] * decay, 0.0)
                 for item, kq, decay in zip(loaded, kqs, decays)]
        es = _unit_tri_inverses_minus_eye(a_kks, ri, ci)
        rhss = [jnp.concatenate([item[8], item[9]], axis=1) for item in loaded]
        wus = [rhs + _dot(e, rhs) for e, rhs in zip(es, rhss)]
        for item, kq, decay, wu in zip(loaded, kqs, decays, wus):
            ic, sl, ch = item[:3]
            w_s[ch, sl, :] = wu[:, :dk]
            u_s[ch, sl, :] = wu[:, dk:]
            ak_s[ch, ic] = jnp.concatenate([kq[c:] * decay, item[10].T], axis=0)
        return carry

    lax.fori_loop(0, n_chunks // lockstep, phase1, 0)

    for hh, d in chains:
        st_s[2 * hh + d] = s0_ref[d, hh]

    def phase2(i, carry):
        n = 2 * hb
        ics = [n_chunks - 1 - i if d else i for _, d in chains]
        r0s = [pl.multiple_of(ic * c, c) for ic in ics]
        sls = [pl.ds(r0, c) for r0 in r0s]
        ss = [st_s[ch] for ch in range(n)]
        wqs = [_dot(jnp.concatenate([w_s[ch, sls[ch], :], qd_s[ch, sls[ch], :]], axis=0), ss[ch]) for ch in range(n)]
        v_news = [u_s[ch, sls[ch], :] - wqs[ch][:c] for ch in range(n)]
        outs = [_dot(ak_s[ch, ics[ch]], v_news[ch]) for ch in range(n)]
        for ch in range(n):
            o_s[ch, sls[ch], :] = wqs[ch][c:] + outs[ch][:c]
            st_s[ch] = ss[ch] * gl_s[ch, pl.ds(r0s[ch], 1), :] + outs[ch][c:]
        return carry

    lax.fori_loop(0, n_chunks, phase2, 0, unroll=2)

    for hh, d in chains:
        sfin_ref[d, hh] = st_s[2 * hh + d]
    for hh in range(hb):
        cols = slice(hh * dk, (hh + 1) * dk)
        o = o_s[2 * hh] + o_s[2 * hh + 1]
        o = o * lax.rsqrt(jnp.mean(o * o, axis=-1, keepdims=True) + EPS)
        o_ref[:, cols] = (o * onorm_ref[...] * _silu(gt_ref[:, cols].astype(F32))).astype(o_ref.dtype)


def _gdn_core(proj, ab, conv_w, a_log, dt_bias, onorm, s0, dst, states_dst, *, t_total, row0, n_seq, seq,
              n_heads, dk, dv, s0_spec, state_slot):
    assert dk == dv == V7X_LANES and row0 % seq == 0
    conv_k = conv_w.shape[0]
    r0 = row0 // seq
    n_chunks = seq // CHUNK
    hb = _tile(n_heads, min(GDN_HEADS_PER_STEP, max(1, GDN_ROWS_PER_STEP // seq)), 1)
    lockstep = _tile(n_chunks, max(1, GDN_PHASE1_CHAINS // (2 * hb)), 1)
    gate_pad = lambda p: jnp.pad(p.reshape(1, 2 * n_heads), ((0, 0), (0, V7X_LANES - 2 * n_heads)))
    lane_vec = pl.BlockSpec((1, V7X_LANES), lambda b, h: (0, 0))
    if s0 is None:
        s0 = jnp.zeros((2, hb, dk, dv), F32)
        s0_in = pl.BlockSpec((2, hb, dk, dv), lambda b, h: (0, 0, 0, 0))
    else:
        s0_in = s0_spec(hb)
    nb = n_heads // hb
    col = lambda sec: pl.BlockSpec((seq, hb * dk), lambda b, h: (b + r0, sec * nb + h))
    cw = lambda sec: pl.BlockSpec((conv_k, hb * dk), lambda b, h: (0, sec * nb + h))
    f32 = lambda *s: pltpu.VMEM(s, F32)
    in_specs = [lane_vec, lane_vec, col(0), col(1), col(2), col(3),
                pl.BlockSpec((seq, V7X_LANES), lambda b, h: (b + r0, 0)),
                cw(0), cw(1), cw(2),
                pl.BlockSpec((1, dv), lambda b, h: (0, 0)),
                s0_in]
    args = [gate_pad(a_log), gate_pad(dt_bias), proj, proj, proj, proj, ab, conv_w, conv_w, conv_w,
            onorm.reshape(1, dv), s0]
    aliases = {}
    for out_idx, kept in enumerate((dst, states_dst)):
        if kept is not None:
            in_specs.append(pl.BlockSpec(memory_space=pl.ANY))
            args.append(kept)
            aliases[len(args) - 1] = out_idx
    nc = 2 * hb
    slot, n_slots = state_slot
    return pl.pallas_call(
        functools.partial(_gdn_kernel, n_heads=n_heads, conv_k=conv_k, lockstep=lockstep,
                          n_kept=len(aliases)),
        out_shape=(jax.ShapeDtypeStruct((t_total, n_heads * dv), BF16),
                   jax.ShapeDtypeStruct((n_seq, n_slots, 2, n_heads, dk, dv), F32)),
        grid=(n_seq, nb),
        in_specs=in_specs,
        out_specs=(pl.BlockSpec((seq, hb * dv), lambda b, h: (b + r0, h)),
                   pl.BlockSpec((None, None, 2, hb, dk, dv), lambda b, h: (b, slot, 0, h, 0, 0))),
        scratch_shapes=[f32(5, seq, V7X_LANES),
                        f32(hb, seq, dk), f32(hb, seq, dk),
                        f32(nc, seq, V7X_LANES), f32(nc, seq, dk),
                        f32(nc, seq, dk), f32(nc, seq, dv), f32(nc, seq, dk),
                        f32(nc, seq, dk),
                        f32(nc, n_chunks, CHUNK + dk, CHUNK),
                        f32(nc, seq, V7X_LANES),
                        f32(nc, seq, dv), f32(nc, dk, dv)],
        input_output_aliases=aliases,
        compiler_params=_cparams("parallel", "arbitrary"),
        name="gdn_core",
    )(*args)


def _odd_dft(seq):
    k = jnp.arange(seq, dtype=jnp.int32)[:, None]
    m = jnp.arange(seq, dtype=jnp.int32)[None, :]
    r = ((2 * k + 1) * m) % (4 * seq)
    ang = r.astype(F32) * (math.pi / (2 * seq))
    return jnp.cos(ang), jnp.sin(ang)


def _filter_kernel(feat_ref, w1_ref, b1_ref, w2_ref, b2_ref, fr_ref, w3f_ref, w3b_ref, dl_ref,
                   dft_hi_ref, dft_lo_ref, hre_ref, him_ref, hid_s):
    seq = feat_ref.shape[0]
    feat = feat_ref[...]

    @pl.when(jnp.logical_and(pl.program_id(0) == 0, pl.program_id(1) == 0))
    def _():
        fr = fr_ref[...]
        hid1 = jnp.sin(fr * (_dot(feat, w1_ref[...]) + b1_ref[...]))
        hid_s[...] = jnp.sin(fr * (_dot(hid1, w2_ref[...]) + b2_ref[...]))

    hid = hid_s[...]
    window = jnp.exp(-feat[:, 0:1] * dl_ref[...])
    hf = _dot(hid, w3f_ref[...]) * window
    hb = _dot(hid, w3b_ref[...]) * window
    row = lax.broadcasted_iota(jnp.int32, hb.shape, 0)
    hb = jnp.where(row == 0, 0.0, hb)

    def dft(rows, val):
        v_hi = val.astype(BF16)
        v_lo = (val - v_hi.astype(F32)).astype(BF16)
        m_hi, m_lo = dft_hi_ref[rows, :], dft_lo_ref[rows, :]
        return (jnp.dot(m_hi, v_hi, preferred_element_type=F32)
                + (jnp.dot(m_hi, v_lo, preferred_element_type=F32)
                   + jnp.dot(m_lo, v_hi, preferred_element_type=F32)))

    hre_ref[...] = dft(slice(0, seq), hf + hb)
    him_ref[...] = dft(slice(seq, 2 * seq), hb - hf)


def _hyena_filters(seq, d, w1, b1, w2, b2, w3, freq, dft_hi, dft_lo):
    emb, hid = w1.shape
    bands = (emb - 1) // 2
    t = jnp.linspace(0.0, 1.0, seq, dtype=F32)[:, None]
    wpos = (2.0 * math.pi / seq) * jnp.arange(seq, dtype=F32)[:, None]
    fb = jnp.linspace(1e-4, bands - 1, bands, dtype=F32)[None, :]
    feat = jnp.concatenate([t, jnp.cos(fb * wpos), -jnp.sin(fb * wpos)], axis=-1)
    feat = jnp.pad(feat, ((0, 0), (0, V7X_LANES - emb)))
    w1p = jnp.pad(w1, ((0, V7X_LANES - emb), (0, 0)))
    max_decay = math.log(HY_TARGET) / HY_SHORT_PCT
    min_decay = math.log(HY_TARGET) / HY_LONG_PCT
    deltas = jnp.abs(jnp.linspace(min_decay, max_decay, d, dtype=F32))[None, :]
    tc = _tile(d, 256, V7X_LANES)
    nt = d // tc
    full = lambda r, c: pl.BlockSpec((r, c), lambda n, j: (0, 0))
    out_spec = pl.BlockSpec((None, seq, tc), lambda n, j: (n, 0, j))
    return pl.pallas_call(
        _filter_kernel,
        out_shape=(jax.ShapeDtypeStruct((HY_ORDER, seq, d), F32),) * 2,
        grid=(HY_ORDER, nt),
        in_specs=[full(seq, V7X_LANES), full(V7X_LANES, hid), full(1, hid), full(hid, hid), full(1, hid),
                  full(1, hid),
                  pl.BlockSpec((hid, tc), lambda n, j: (0, (2 * n) * nt + j)),
                  pl.BlockSpec((hid, tc), lambda n, j: (0, (2 * n + 1) * nt + j)),
                  pl.BlockSpec((1, tc), lambda n, j: (0, j)),
                  full(2 * seq, seq), full(2 * seq, seq)],
        out_specs=(out_spec, out_spec),
        scratch_shapes=[pltpu.VMEM((seq, hid), F32)],
        compiler_params=_cparams("arbitrary", "arbitrary"),
        name="hyena_filter",
    )(feat, w1p, b1.reshape(1, hid), w2, b2.reshape(1, hid), freq.reshape(1, hid), w3, w3, deltas,
      dft_hi, dft_lo)


HYENA_ROWS_PER_STEP = 2048


def _hyconv_kernel(*refs, conv_z, conv_k, has_dst):
    z_ref, x_ref, hre_ref, him_ref, skip_ref, cz_ref, cx_ref, fwd_ref, inv_ref = refs[:9]
    o_ref = refs[9 + has_dst]
    seq = hre_ref.shape[0]
    n_sub = z_ref.shape[0] // seq
    zs = []
    for s in range(n_sub):
        z = z_ref[s * seq:(s + 1) * seq, :].astype(F32)
        zs.append(_dwconv(z, cz_ref[...], conv_k) if conv_z else z)
    pqs = [jnp.dot(fwd_ref[...], z.astype(BF16), preferred_element_type=F32) for z in zs]
    hre, him = hre_ref[...], him_ref[...]
    for s in range(n_sub):
        p, q = pqs[s][:seq], pqs[s][seq:]
        y_spec = jnp.concatenate([p * hre + q * him, p * him - q * hre], axis=0).astype(BF16)
        y = jnp.dot(inv_ref[...], y_spec, preferred_element_type=F32)
        x = _dwconv(x_ref[s * seq:(s + 1) * seq, :].astype(F32), cx_ref[...], conv_k)
        o_ref[s * seq:(s + 1) * seq, :] = (x * (y + skip_ref[...] * zs[s])).astype(o_ref.dtype)


def _hyconv(z, z_col0, conv_z, proj, x_col0, hre, him, order, skip, conv_w, fwd, inv, dst,
            *, row0, n_seq, seq, d, out_rows, out_row0, out_dtype):
    conv_k = conv_w.shape[0]
    n_sub = max(s for s in range(1, max(1, HYENA_ROWS_PER_STEP // seq) + 1)
                if n_seq % s == 0 and row0 % (s * seq) == 0 and out_row0 % (s * seq) == 0)
    blk = n_sub * seq
    tc = _tile(d, 512 if blk <= 1024 else 256, V7X_LANES)
    nt = d // tc
    r0 = row0 // blk
    zr0 = r0 if conv_z else 0
    zc, xc = z_col0 // tc, x_col0 // tc
    in_specs = [pl.BlockSpec((blk, tc), lambda j, b: (b + zr0, zc + j)),
                pl.BlockSpec((blk, tc), lambda j, b: (b + r0, xc + j)),
                pl.BlockSpec((None, seq, tc), lambda j, b: (order, 0, j)),
                pl.BlockSpec((None, seq, tc), lambda j, b: (order, 0, j)),
                pl.BlockSpec((1, tc), lambda j, b: (0, j)),
                pl.BlockSpec((conv_k, tc), lambda j, b: (0, zc + j)),
                pl.BlockSpec((conv_k, tc), lambda j, b: (0, xc + j)),
                pl.BlockSpec((2 * seq, seq), lambda j, b: (0, 0), pipeline_mode=_RESIDENT),
                pl.BlockSpec((seq, 2 * seq), lambda j, b: (0, 0), pipeline_mode=_RESIDENT)]
    args = [z, proj, hre, him, skip.reshape(1, d), conv_w, conv_w, fwd, inv]
    aliases = {}
    if dst is not None:
        in_specs.append(pl.BlockSpec(memory_space=pl.ANY))
        args.append(dst)
        aliases = {len(args) - 1: 0}
    out_r0 = out_row0 // blk
    return pl.pallas_call(
        functools.partial(_hyconv_kernel, conv_z=conv_z, conv_k=conv_k, has_dst=dst is not None),
        out_shape=jax.ShapeDtypeStruct((out_rows, d), out_dtype),
        grid=(nt, n_seq // n_sub),
        in_specs=in_specs,
        out_specs=pl.BlockSpec((blk, tc), lambda j, b: (b + out_r0, j)),
        input_output_aliases=aliases,
        compiler_params=_cparams("parallel", "parallel"),
        name="hyena_conv",
    )(*args)


def _grid_pos_emb(n_tokens, d):
    rows = n_tokens // GRID_W
    r, col = jnp.meshgrid(jnp.arange(rows), jnp.arange(GRID_W), indexing='ij')
    quarter = d // 4
    omega = 1.0 / (POS_BASE ** (jnp.arange(quarter, dtype=F32) / quarter))

    def emb1d(p):
        a = p.reshape(-1, 1).astype(F32) * omega[None, :]
        return jnp.concatenate([jnp.sin(a), jnp.cos(a)], axis=-1)

    return jnp.concatenate([emb1d(r), emb1d(col)], axis=-1)


def kernel(x_prompt, x_sample, state_delta, c, c_ctx, ada_w, ada_b, norm1_g, norm2_g, gdn_w_in, gdn_conv, gdn_a_log, gdn_dt_bias, gdn_onorm, gdn_w_out, hy_w_in, hy_b_in, hy_conv, hy_f_w1, hy_f_b1, hy_f_w2, hy_f_b2, hy_f_w3, hy_freq, hy_skip, hy_w_out, hy_b_out, ffn_w_gu, ffn_w_down, final_g):
    bc, lc, d = x_prompt.shape
    bl, ll, _ = x_sample.shape
    depth = ada_w.shape[0]
    n_heads, dk, dv = state_delta.shape[3:]
    tc_rows, tl_rows = bc * lc, bl * ll
    t = tc_rows + tl_rows
    assert tc_rows % ll == 0 and ll % lc == 0
    rows = _Rows(tc_rows, ll, t, _tile(math.gcd(tc_rows, ll), 512, V7X_SUBLANES))

    bm = 1 + bl
    bm_pad = -(-bm // V7X_SUBLANES) * V7X_SUBLANES
    cvec = jnp.concatenate([c_ctx[None, :], c, jnp.zeros((bm_pad - bm, d), F32)], axis=0)
    mods_all = _ada(cvec, ada_w, ada_b).reshape(depth, bm_pad, 6, d)

    x = _embed(x_prompt.reshape(tc_rows, d), x_sample.reshape(tl_rows, d), _grid_pos_emb(ll, d), rows)

    n_hy = hy_w_in.shape[0]
    dft = {}
    for seq in (lc, ll):
        cos_m, sin_m = _odd_dft(seq)
        fwd_f32 = jnp.concatenate([cos_m, sin_m], axis=0)
        fwd = fwd_f32.astype(BF16)
        fwd_lo = (fwd_f32 - fwd.astype(F32)).astype(BF16)
        inv = (jnp.concatenate([cos_m.T, -sin_m.T], axis=1) / seq).astype(BF16)
        spectra = [_hyena_filters(seq, d, hy_f_w1[j], hy_f_b1[j], hy_f_w2[j], hy_f_b2[j], hy_f_w3[j],
                                  hy_freq[j], fwd, fwd_lo) for j in range(n_hy)]
        dft[seq] = (fwd, inv, spectra)

    zero_b = jnp.zeros((d,), F32)
    mixed = jnp.zeros((t, n_heads * dv), BF16)
    shared = n_heads * dv == d
    mixed_hy = mixed if shared else jnp.zeros((t, d), BF16)
    new_state_delta = jnp.zeros((bc, gdn_w_in.shape[0], 2, n_heads, dk, dv), F32)
    n_mixers = 2
    n_gdn = gdn_w_in.shape[0]
    gdn_w_in_b = jnp.pad(gdn_w_in, ((0, 0), (0, 0), (0, V7X_LANES - 4 * n_heads))).astype(BF16)
    gdn_w_out_b, hy_w_in_b, hy_w_out_b = (w.astype(BF16) for w in (gdn_w_out, hy_w_in, hy_w_out))
    ffn_w_gu_b, ffn_w_down_b = ffn_w_gu.astype(BF16), ffn_w_down.astype(BF16)
    for layer in range(depth):
        j = layer // n_mixers
        if layer % n_mixers == 0:
            proj, ab = _in_proj(x, norm1_g[layer], mods_all, layer, gdn_w_in_b, j,
                                jnp.zeros((gdn_w_in_b.shape[2],), F32), rows, n_side=V7X_LANES)
            gdn = dict(t_total=t, n_heads=n_heads, dk=dk, dv=dv)
            weights = (gdn_conv[j], gdn_a_log[j], gdn_dt_bias[j], gdn_onorm[j])
            mixed, new_state_delta = _gdn_core(proj, ab, *weights, None, mixed, new_state_delta, row0=0, n_seq=bc,
                                               seq=lc, s0_spec=None, state_slot=(j, n_gdn), **gdn)
            s0_spec = lambda hb, j=j: pl.BlockSpec((None, None, 2, hb, dk, dv), lambda b, h: (b, j, 0, h, 0, 0))
            mixed, _ = _gdn_core(proj, ab, *weights, state_delta, mixed, None, row0=tc_rows, n_seq=bl, seq=ll,
                                 s0_spec=s0_spec, state_slot=(0, 1), **gdn)
            mixer = (mixed, gdn_w_out_b, j, zero_b)
            mixed_hy = mixed if shared else mixed_hy
        else:
            proj = _in_proj(x, norm1_g[layer], mods_all, layer, hy_w_in_b, j, hy_b_in[j], rows)
            for row0, n_seq, seq in ((0, bc, lc), (tc_rows, bl, ll)):
                fwd, inv, spectra = dft[seq]
                hre, him = spectra[j]
                z = proj
                for n in range(HY_ORDER):
                    last = n == HY_ORDER - 1
                    z = _hyconv(z, 0, n == 0, proj, (n + 1) * d, hre, him, n, hy_skip[j, n], hy_conv[j],
                                fwd, inv, mixed_hy if last else None, row0=row0, n_seq=n_seq, seq=seq, d=d,
                                out_rows=t if last else n_seq * seq, out_row0=row0 if last else 0,
                                out_dtype=BF16 if last else F32)
                mixed_hy = z
            mixer = (mixed_hy, hy_w_out_b, j, hy_b_out[j])
            mixed = mixed_hy if shared else mixed
        ffn = functools.partial(_ffn, x, norm2_g[layer], mods_all, ffn_w_gu_b, ffn_w_down_b, layer, rows,
                                mixer=mixer)
        if layer < depth - 1:
            x = ffn()
        else:
            y_prompt = ffn(final_g=final_g, tile0=0, n_tiles=rows.n_ctx_tiles)
            y_sample = ffn(final_g=final_g, tile0=rows.n_ctx_tiles, n_tiles=rows.n_tiles - rows.n_ctx_tiles)
    return (y_prompt.reshape(bc, lc, d), y_sample.reshape(bl, ll, d), new_state_delta)
```

```python
import functools
import math

import jax
import jax.numpy as jnp
from jax import lax
from jax.experimental import pallas as pl
from jax.experimental.pallas import tpu as pltpu

GRID_W = 64
CHUNK = 64
HY_ORDER = 2
HY_TARGET = 1e-2
HY_SHORT_PCT = 0.3
HY_LONG_PCT = 1.5
POS_BASE = 10000.0
EPS = 1e-6

V7X_LANES = 128
V7X_SUBLANES = 8
V7X_VMEM_LIMIT_BYTES = 48 * 1024 * 1024

BF16 = jnp.bfloat16
F32 = jnp.float32
HIGHEST = lax.Precision.HIGHEST


def _cparams(*sem):
    return pltpu.CompilerParams(dimension_semantics=sem, vmem_limit_bytes=V7X_VMEM_LIMIT_BYTES)


def _tile(n, target, align):
    if n <= target:
        return n
    best = None
    for t in range(align, target + 1, align):
        if n % t == 0:
            best = t
    assert best is not None, (n, target, align)
    return best


def _dot(a, b):
    return jnp.dot(a.astype(BF16), b.astype(BF16), preferred_element_type=F32)


def _dot_nt(a, b):
    return lax.dot_general(a.astype(BF16), b.astype(BF16), (((1,), (1,)), ((), ())),
                           preferred_element_type=F32)


def _dot_hi(a, b):
    return jnp.dot(a, b, preferred_element_type=F32, precision=HIGHEST)


def _silu(x):
    return x * jax.nn.sigmoid(x)


def _norm_mod(x, g, shift, scale):
    ms = jnp.mean(x * x, axis=-1, keepdims=True)
    return (x * lax.rsqrt(ms + EPS) * g) * (1.0 + scale) + shift


class _Rows:
    def __init__(self, tc, ll, t, tm):
        assert tc % tm == 0 and ll % tm == 0 and t % tm == 0
        self.n_ctx_tiles = tc // tm
        self.tiles_per_lat = ll // tm
        self.n_tiles = t // tm
        self.tm = tm

    def mod_index(self, i):
        lat = 1 + (i - self.n_ctx_tiles) // self.tiles_per_lat
        return jnp.where(i < self.n_ctx_tiles, 0, lat)


def _ada_kernel(c_ref, w_ref, b_ref, o_ref):
    o_ref[...] = _dot(_silu(c_ref[...]), w_ref[...]) + b_ref[...]


def _ada(cvec, ada_w, ada_b):
    depth, d, n = ada_w.shape
    bm = cvec.shape[0]
    tn = _tile(n, 1536, V7X_LANES)
    return pl.pallas_call(
        _ada_kernel,
        out_shape=jax.ShapeDtypeStruct((depth, bm, n), F32),
        grid=(depth, n // tn),
        in_specs=[pl.BlockSpec((bm, d), lambda l, j: (0, 0)),
                  pl.BlockSpec((None, d, tn), lambda l, j: (l, 0, j)),
                  pl.BlockSpec((None, 1, tn), lambda l, j: (l, 0, j))],
        out_specs=pl.BlockSpec((None, bm, tn), lambda l, j: (l, 0, j)),
        compiler_params=_cparams("parallel", "parallel"),
        name="ada",
    )(cvec, ada_w, ada_b.reshape(depth, 1, n))


def _embed_kernel(xp_ref, xs_ref, pos_ref, o_ref, *, n_ctx_tiles):
    i = pl.program_id(0)

    @pl.when(i < n_ctx_tiles)
    def _():
        o_ref[...] = xp_ref[...]

    @pl.when(i >= n_ctx_tiles)
    def _():
        o_ref[...] = xs_ref[...] + pos_ref[...]


def _embed(xp, xs, pos, rows):
    t, d = xp.shape[0] + xs.shape[0], xp.shape[1]
    tm, nct = rows.tm, rows.n_ctx_tiles
    npos = pos.shape[0] // tm
    return pl.pallas_call(
        functools.partial(_embed_kernel, n_ctx_tiles=nct),
        out_shape=jax.ShapeDtypeStruct((t, d), F32),
        grid=(rows.n_tiles,),
        in_specs=[pl.BlockSpec((tm, d), lambda i: (jnp.minimum(i, nct - 1), 0)),
                  pl.BlockSpec((tm, d), lambda i: (jnp.maximum(i - nct, 0), 0)),
                  pl.BlockSpec((tm, d), lambda i: (jnp.maximum(i - nct, 0) % npos, 0))],
        out_specs=pl.BlockSpec((tm, d), lambda i: (i, 0)),
        compiler_params=_cparams("parallel"),
        name="embed",
    )(xp, xs, pos)


_RESIDENT = pl.Buffered(1)


def _in_kernel(x_ref, g_ref, mod_ref, w_ref, b_ref, o_ref, *side_ref, chunk):
    m = mod_ref[...]
    h = _norm_mod(x_ref[...], g_ref[...], m[0:1, :], m[1:2, :]).astype(BF16)
    n = o_ref.shape[1]
    for c0 in range(0, n, chunk):
        cols = slice(c0, c0 + chunk)
        y = jnp.dot(h, w_ref[:, cols], preferred_element_type=F32) + b_ref[:, cols]
        o_ref[:, cols] = y.astype(o_ref.dtype)
    if side_ref:
        side_ref[0][...] = jnp.dot(h, w_ref[:, n:], preferred_element_type=F32) + b_ref[:, n:]


def _in_proj(x, g, mods_all, layer, w_all, w_index, b, rows, n_side=0):
    t, d = x.shape
    n = w_all.shape[2] - n_side
    tm = rows.tm
    chunk = _tile(n, 512, V7X_LANES)
    out_shape = [jax.ShapeDtypeStruct((t, n), BF16)]
    out_specs = [pl.BlockSpec((tm, n), lambda i: (i, 0))]
    if n_side:
        out_shape.append(jax.ShapeDtypeStruct((t, n_side), F32))
        out_specs.append(pl.BlockSpec((tm, n_side), lambda i: (i, 0)))
    out = pl.pallas_call(
        functools.partial(_in_kernel, chunk=chunk),
        out_shape=out_shape,
        grid=(rows.n_tiles,),
        in_specs=[pl.BlockSpec((tm, d), lambda i: (i, 0)),
                  pl.BlockSpec((1, d), lambda i: (0, 0), pipeline_mode=_RESIDENT),
                  pl.BlockSpec((None, None, 6, d), lambda i: (layer, rows.mod_index(i), 0, 0)),
                  pl.BlockSpec((None, d, n + n_side), lambda i: (w_index, 0, 0), pipeline_mode=_RESIDENT),
                  pl.BlockSpec((1, n + n_side), lambda i: (0, 0), pipeline_mode=_RESIDENT)],
        out_specs=out_specs,
        compiler_params=_cparams("parallel"),
        name="in_proj",
    )(x, g.reshape(1, d), mods_all, w_all, b.reshape(1, n + n_side))
    return out if n_side else out[0]


def _ffn_kernel(x_ref, g_ref, mod_ref, wgu_ref, wd_ref, *rest, chunk, has_mixer, has_final):
    o_ref = rest[-1]
    x = x_ref[...]
    m = mod_ref[...]
    if has_mixer:
        a_ref, wo_ref, bo_ref = rest[:3]
        x = x + m[2:3, :] * (_dot(a_ref[...], wo_ref[...]) + bo_ref[...])
    h = _norm_mod(x, g_ref[...], m[3:4, :], m[4:5, :]).astype(BF16)
    f = wd_ref.shape[0]
    n_chunks = f // chunk

    def gate_up(k):
        gate = jnp.dot(h, wgu_ref[:, k * chunk:(k + 1) * chunk], preferred_element_type=F32)
        up = jnp.dot(h, wgu_ref[:, f + k * chunk:f + (k + 1) * chunk], preferred_element_type=F32)
        return gate, up

    y = None
    pending = gate_up(0)
    for k in range(n_chunks):
        following = gate_up(k + 1) if k + 1 < n_chunks else None
        act = (_silu(pending[0]) * pending[1]).astype(BF16)
        part = jnp.dot(act, wd_ref[k * chunk:(k + 1) * chunk, :], preferred_element_type=F32)
        y = part if y is None else y + part
        pending = following
    out = x + m[5:6, :] * y
    if has_final:
        out = out * lax.rsqrt(jnp.mean(out * out, axis=-1, keepdims=True) + EPS) * rest[-2][...]
    o_ref[...] = out


def _ffn(x, g, mods_all, w_gu_all, w_down_all, layer, rows, mixer=None, final_g=None, tile0=0, n_tiles=None):
    d = x.shape[1]
    f = w_down_all.shape[1]
    tm = rows.tm
    n_tiles = rows.n_tiles if n_tiles is None else n_tiles
    chunk = _tile(f, 256, V7X_LANES)
    resident = lambda *s: pl.BlockSpec(s, lambda i: (0,) * len(s), pipeline_mode=_RESIDENT)
    in_specs = [pl.BlockSpec((tm, d), lambda i: (i + tile0, 0)),
                resident(1, d),
                pl.BlockSpec((None, None, 6, d), lambda i: (layer, rows.mod_index(i + tile0), 0, 0)),
                pl.BlockSpec((None, d, 2 * f), lambda i: (layer, 0, 0), pipeline_mode=_RESIDENT),
                pl.BlockSpec((None, f, d), lambda i: (layer, 0, 0), pipeline_mode=_RESIDENT)]
    args = [x, g.reshape(1, d), mods_all, w_gu_all, w_down_all]
    if mixer is not None:
        a, wo_all, wo_index, bo = mixer
        k = a.shape[1]
        in_specs += [pl.BlockSpec((tm, k), lambda i: (i + tile0, 0)),
                     pl.BlockSpec((None, k, d), lambda i: (wo_index, 0, 0), pipeline_mode=_RESIDENT),
                     resident(1, d)]
        args += [a, wo_all, bo.reshape(1, d)]
    if final_g is not None:
        in_specs.append(resident(1, d))
        args.append(final_g.reshape(1, d))
    return pl.pallas_call(
        functools.partial(_ffn_kernel, chunk=chunk, has_mixer=mixer is not None, has_final=final_g is not None),
        out_shape=jax.ShapeDtypeStruct((n_tiles * tm, d), F32),
        grid=(n_tiles,),
        in_specs=in_specs,
        out_specs=pl.BlockSpec((tm, d), lambda i: (i, 0)),
        compiler_params=_cparams("parallel"),
        name="ffn",
    )(*args)


def _dwconv(x, w, k):
    n = x.shape[0]
    half = k // 2
    e = V7X_SUBLANES
    assert half <= e and n >= 4 * e
    taps = [w[j:j + 1, :] for j in range(k)]
    shifts = [s for s in range(-half, half + 1) if s != 0]

    def conv(v, mask):
        acc = v * taps[half]
        for s in shifts:
            shifted = pltpu.roll(v, (-s) % v.shape[0], axis=0)
            acc = acc + (shifted if mask is None else jnp.where(mask(s), shifted, 0.0)) * taps[s + half]
        return acc

    row = lax.broadcasted_iota(jnp.int32, (2 * e, x.shape[1]), 0)
    top = conv(x[:2 * e], lambda s: row + s >= 0)[:e]
    bottom = conv(x[n - 2 * e:], lambda s: row + s < 2 * e)[e:]
    return jnp.concatenate([top, conv(x, None)[e:n - e], bottom], axis=0)


CONV_HALO = 16


def _dwconv_window(ref, cols, w, k, r0, rows, slab_ref):
    n = ref.shape[0]
    if rows == n:
        return _dwconv(ref[:, cols].astype(F32), w, k)
    half = k // 2
    h = CONV_HALO
    assert half <= h <= rows and n % rows == 0
    before = ref[pl.ds(pl.multiple_of(jnp.maximum(r0 - h, 0), h), h), cols].astype(F32)
    after = ref[pl.ds(pl.multiple_of(jnp.minimum(r0 + rows, n - h), h), h), cols].astype(F32)
    slab_ref[0:h, :] = jnp.where(r0 > 0, before, 0.0)
    slab_ref[h:h + rows, :] = ref[pl.ds(r0, rows), cols].astype(F32)
    slab_ref[h + rows:, :] = jnp.where(r0 + rows < n, after, 0.0)
    acc = None
    for s in range(-half, half + 1):
        term = slab_ref[h + s:h + s + rows, :] * w[s + half:s + half + 1, :]
        acc = term if acc is None else acc + term
    return acc


TRI_BASE = 8
GDN_PHASE1_CHAINS = 32
GDN_HEADS_PER_STEP = 4
GDN_ROWS_PER_STEP = 2048


def _unit_tri_inverses_minus_eye(mats, ri, ci):
    c = mats[0].shape[0]

    def same_block(s):
        sh = int(math.log2(s))
        return (ri >> sh) == (ci >> sh)

    ps = [jnp.where(same_block(TRI_BASE), -a, 0.0) for a in mats]
    es = ps
    n_lvl = int(math.log2(TRI_BASE))
    for lvl in range(n_lvl):
        es = [e + _dot(p, e) for p, e in zip(ps, es)]
        if lvl < n_lvl - 1:
            ps = [_dot(p, p) for p in ps]
    s = TRI_BASE
    while s < c:
        mask = jnp.logical_and(same_block(2 * s), jnp.logical_not(same_block(s)))
        offs = [jnp.where(mask, a, 0.0) for a in mats]
        ys = [off + _dot(off, e) for off, e in zip(offs, es)]
        es = [e - (y + _dot(e, y)) for e, y in zip(es, ys)]
        s *= 2
    return es


def _chunk_cumsum(x, pos, reverse):
    n = x.shape[0]
    s = 1
    while s < CHUNK:
        if reverse:
            x = x + jnp.where(pos + s < CHUNK, pltpu.roll(x, n - s, axis=0), 0.0)
        else:
            x = x + jnp.where(pos >= s, pltpu.roll(x, s, axis=0), 0.0)
        s *= 2
    return x


def _gdn_kernel(*refs, n_heads, conv_k, lockstep, n_kept):
    (alog_ref, dtb_ref, q_ref, k_ref, v_ref, gt_ref, ab_ref, cq_ref, ck_ref, cv_ref,
     onorm_ref, s0_ref) = refs[:12]
    o_ref, sfin_ref, gates_s, conv_s, w_s, u_s, qd_s, ak_s, gl_s, o_s, st_s = refs[12 + n_kept:]
    n_chains, seq, dk = w_s.shape
    hb = n_chains // 2
    head0 = pl.program_id(1) * hb
    c = CHUNK
    n_chunks = seq // c
    chains = [(hh, d) for hh in range(hb) for d in range(2)]

    @pl.when(head0 == 0)
    def _():
        ab = ab_ref[...]
        pos = jnp.bitwise_and(lax.broadcasted_iota(jnp.int32, ab.shape, 0), c - 1)
        g = -jnp.exp(alog_ref[...]) * jax.nn.softplus(ab + dtb_ref[...])
        g_fwd = _chunk_cumsum(g, pos, reverse=False)
        g_rev = _chunk_cumsum(g, pos, reverse=True)
        gates_s[0] = g_fwd
        gates_s[1] = g_rev - g
        gates_s[2] = g_rev
        gates_s[3] = g_fwd - g
        gates_s[4] = jax.nn.sigmoid(ab)

    lane = lax.broadcasted_iota(jnp.int32, (1, V7X_LANES), 1)

    rows = lockstep * c

    def column(i, idx, win):
        one_hot = (lane == idx).astype(F32)
        col = jnp.sum(gates_s[i, win, :] * one_hot, axis=1, keepdims=True)
        return jnp.broadcast_to(col, (rows, V7X_LANES))

    ri = lax.broadcasted_iota(jnp.int32, (c, c), 0)
    ci = lax.broadcasted_iota(jnp.int32, (c, c), 1)
    incl = (ri >= ci, ri <= ci)
    strict = (ri > ci, ri < ci)

    def phase1(it, carry):
        r0 = pl.multiple_of(it * rows, rows)
        win = pl.ds(r0, rows)
        prepared = []
        for hh in range(hb):
            cols = slice(hh * dk, (hh + 1) * dk)
            q = _silu(_dwconv_window(q_ref, cols, cq_ref[:, cols], conv_k, r0, rows, conv_s.at[3 * hh]))
            k = _silu(_dwconv_window(k_ref, cols, ck_ref[:, cols], conv_k, r0, rows, conv_s.at[3 * hh + 1]))
            v = _silu(_dwconv_window(v_ref, cols, cv_ref[:, cols], conv_k, r0, rows, conv_s.at[3 * hh + 2]))
            q = q * lax.rsqrt(jnp.sum(q * q, axis=-1, keepdims=True) + EPS) * (dk ** -0.5)
            k = k * lax.rsqrt(jnp.sum(k * k, axis=-1, keepdims=True) + EPS)
            for d in range(2):
                head = head0 + hh
                g_cum = column(2 * d, d * n_heads + head, win)
                g_tail = column(2 * d + 1, d * n_heads + head, win)
                beta = column(4, 2 * n_heads + d * n_heads + head, win)
                e_cum = jnp.exp(g_cum)
                kb = k * beta
                prepared.append((q, k, g_cum, kb, kb * e_cum, v * beta, k * jnp.exp(g_tail)))
                qd_s[2 * hh + d, win, :] = q * e_cum
                gl_s[2 * hh + d, win, :] = jnp.exp(g_cum + g_tail)
        items = []
        for gi in range(lockstep):
            part = slice(gi * c, (gi + 1) * c)
            for ch, (_, d) in enumerate(chains):
                sl = pl.ds(pl.multiple_of(r0 + gi * c, c), c)
                items.append((it * lockstep + gi, sl, ch, d) + tuple(a[part] for a in prepared[ch]))
        decays, kqs = [], []
        for ic, sl, ch, d, qc, kc, gc, kb, w0, u0, ktl in items:
            diff = gc[:, :c] - gc.T[:c, :]
            decays.append(jnp.where(incl[d], jnp.exp(jnp.where(incl[d], diff, 0.0)), 0.0))
            kqs.append(_dot_nt(jnp.concatenate([kb, qc], axis=0), kc))
        a_kks = [jnp.where(strict[item[3]], kq[:c] * decay, 0.0)
                 for item, kq, decay in zip(items, kqs, decays)]
        es = _unit_tri_inverses_minus_eye(a_kks, ri, ci)
        rhss = [jnp.concatenate([item[8], item[9]], axis=1) for item in items]
        wus = [rhs + _dot(e, rhs) for e, rhs in zip(es, rhss)]
        for item, kq, decay, wu in zip(items, kqs, decays, wus):
            ic, sl, ch = item[:3]
            w_s[ch, sl, :] = wu[:, :dk]
            u_s[ch, sl, :] = wu[:, dk:]
            ak_s[ch, ic] = jnp.concatenate([kq[c:] * decay, item[10].T], axis=0)
        return carry

    lax.fori_loop(0, n_chunks // lockstep, phase1, 0)

    for hh, d in chains:
        st_s[2 * hh + d] = s0_ref[d, hh]

    def phase2(i, carry):
        n = 2 * hb
        ics = [n_chunks - 1 - i if d else i for _, d in chains]
        r0s = [pl.multiple_of(ic * c, c) for ic in ics]
        sls = [pl.ds(r0, c) for r0 in r0s]
        ss = [st_s[ch] for ch in range(n)]
        wqs = [_dot(jnp.concatenate([w_s[ch, sls[ch], :], qd_s[ch, sls[ch], :]], axis=0), ss[ch]) for ch in range(n)]
        v_news = [u_s[ch, sls[ch], :] - wqs[ch][:c] for ch in range(n)]
        outs = [_dot(ak_s[ch, ics[ch]], v_news[ch]) for ch in range(n)]
        for ch in range(n):
            o_s[ch, sls[ch], :] = wqs[ch][c:] + outs[ch][:c]
            st_s[ch] = ss[ch] * gl_s[ch, pl.ds(r0s[ch], 1), :] + outs[ch][c:]
        return carry

    lax.fori_loop(0, n_chunks, phase2, 0, unroll=2)

    for hh, d in chains:
        sfin_ref[d, hh] = st_s[2 * hh + d]
    for hh in range(hb):
        cols = slice(hh * dk, (hh + 1) * dk)
        o = o_s[2 * hh] + o_s[2 * hh + 1]
        o = o * lax.rsqrt(jnp.mean(o * o, axis=-1, keepdims=True) + EPS)
        o_ref[:, cols] = (o * onorm_ref[...] * _silu(gt_ref[:, cols].astype(F32))).astype(o_ref.dtype)


def _gdn_core(proj, ab, conv_w, a_log, dt_bias, onorm, s0, dst, states_dst, *, t_total, row0, n_seq, seq,
              n_heads, dk, dv, s0_spec, state_slot):
    assert dk == dv == V7X_LANES and row0 % seq == 0
    conv_k = conv_w.shape[0]
    r0 = row0 // seq
    n_chunks = seq // CHUNK
    hb = _tile(n_heads, min(GDN_HEADS_PER_STEP, max(1, GDN_ROWS_PER_STEP // seq)), 1)
    lockstep = _tile(n_chunks, max(1, GDN_PHASE1_CHAINS // (2 * hb)), 1)
    gate_pad = lambda p: jnp.pad(p.reshape(1, 2 * n_heads), ((0, 0), (0, V7X_LANES - 2 * n_heads)))
    lane_vec = pl.BlockSpec((1, V7X_LANES), lambda b, h: (0, 0))
    if s0 is None:
        s0 = jnp.zeros((2, hb, dk, dv), F32)
        s0_in = pl.BlockSpec((2, hb, dk, dv), lambda b, h: (0, 0, 0, 0))
    else:
        s0_in = s0_spec(hb)
    nb = n_heads // hb
    col = lambda sec: pl.BlockSpec((seq, hb * dk), lambda b, h: (b + r0, sec * nb + h))
    cw = lambda sec: pl.BlockSpec((conv_k, hb * dk), lambda b, h: (0, sec * nb + h))
    f32 = lambda *s: pltpu.VMEM(s, F32)
    in_specs = [lane_vec, lane_vec, col(0), col(1), col(2), col(3),
                pl.BlockSpec((seq, V7X_LANES), lambda b, h: (b + r0, 0)),
                cw(0), cw(1), cw(2),
                pl.BlockSpec((1, dv), lambda b, h: (0, 0)),
                s0_in]
    args = [gate_pad(a_log), gate_pad(dt_bias), proj, proj, proj, proj, ab, conv_w, conv_w, conv_w,
            onorm.reshape(1, dv), s0]
    aliases = {}
    for out_idx, kept in enumerate((dst, states_dst)):
        if kept is not None:
            in_specs.append(pl.BlockSpec(memory_space=pl.ANY))
            args.append(kept)
            aliases[len(args) - 1] = out_idx
    nc = 2 * hb
    slot, n_slots = state_slot
    return pl.pallas_call(
        functools.partial(_gdn_kernel, n_heads=n_heads, conv_k=conv_k, lockstep=lockstep,
                          n_kept=len(aliases)),
        out_shape=(jax.ShapeDtypeStruct((t_total, n_heads * dv), BF16),
                   jax.ShapeDtypeStruct((n_seq, n_slots, 2, n_heads, dk, dv), F32)),
        grid=(n_seq, nb),
        in_specs=in_specs,
        out_specs=(pl.BlockSpec((seq, hb * dv), lambda b, h: (b + r0, h)),
                   pl.BlockSpec((None, None, 2, hb, dk, dv), lambda b, h: (b, slot, 0, h, 0, 0))),
        scratch_shapes=[f32(5, seq, V7X_LANES),
                        f32(3 * hb, lockstep * CHUNK + 2 * CONV_HALO, dk),
                        f32(nc, seq, dk), f32(nc, seq, dv), f32(nc, seq, dk),
                        f32(nc, n_chunks, CHUNK + dk, CHUNK),
                        f32(nc, seq, V7X_LANES),
                        f32(nc, seq, dv), f32(nc, dk, dv)],
        input_output_aliases=aliases,
        compiler_params=_cparams("parallel", "arbitrary"),
        name="gdn_core",
    )(*args)


def _odd_dft(seq):
    k = jnp.arange(seq, dtype=jnp.int32)[:, None]
    m = jnp.arange(seq, dtype=jnp.int32)[None, :]
    r = ((2 * k + 1) * m) % (4 * seq)
    ang = r.astype(F32) * (math.pi / (2 * seq))
    return jnp.cos(ang), jnp.sin(ang)


def _filter_kernel(feat_ref, w1_ref, b1_ref, w2_ref, b2_ref, fr_ref, w3f_ref, w3b_ref, dl_ref,
                   dft_hi_ref, dft_lo_ref, hre_ref, him_ref, hid_s):
    seq = feat_ref.shape[0]
    feat = feat_ref[...]

    @pl.when(jnp.logical_and(pl.program_id(0) == 0, pl.program_id(1) == 0))
    def _():
        fr = fr_ref[...]
        hid1 = jnp.sin(fr * (_dot(feat, w1_ref[...]) + b1_ref[...]))
        hid_s[...] = jnp.sin(fr * (_dot(hid1, w2_ref[...]) + b2_ref[...]))

    hid = hid_s[...]
    window = jnp.exp(-feat[:, 0:1] * dl_ref[...])
    hf = _dot(hid, w3f_ref[...]) * window
    hb = _dot(hid, w3b_ref[...]) * window
    row = lax.broadcasted_iota(jnp.int32, hb.shape, 0)
    hb = jnp.where(row == 0, 0.0, hb)

    def dft(rows, val):
        v_hi = val.astype(BF16)
        v_lo = (val - v_hi.astype(F32)).astype(BF16)
        m_hi, m_lo = dft_hi_ref[rows, :], dft_lo_ref[rows, :]
        return (jnp.dot(m_hi, v_hi, preferred_element_type=F32)
                + (jnp.dot(m_hi, v_lo, preferred_element_type=F32)
                   + jnp.dot(m_lo, v_hi, preferred_element_type=F32)))

    hre_ref[...] = dft(slice(0, seq), hf + hb)
    him_ref[...] = dft(slice(seq, 2 * seq), hb - hf)


def _hyena_filters(seq, d, w1, b1, w2, b2, w3, freq, dft_hi, dft_lo):
    emb, hid = w1.shape
    bands = (emb - 1) // 2
    t = jnp.linspace(0.0, 1.0, seq, dtype=F32)[:, None]
    wpos = (2.0 * math.pi / seq) * jnp.arange(seq, dtype=F32)[:, None]
    fb = jnp.linspace(1e-4, bands - 1, bands, dtype=F32)[None, :]
    feat = jnp.concatenate([t, jnp.cos(fb * wpos), -jnp.sin(fb * wpos)], axis=-1)
    feat = jnp.pad(feat, ((0, 0), (0, V7X_LANES - emb)))
    w1p = jnp.pad(w1, ((0, V7X_LANES - emb), (0, 0)))
    max_decay = math.log(HY_TARGET) / HY_SHORT_PCT
    min_decay = math.log(HY_TARGET) / HY_LONG_PCT
    deltas = jnp.abs(jnp.linspace(min_decay, max_decay, d, dtype=F32))[None, :]
    tc = _tile(d, 256, V7X_LANES)
    nt = d // tc
    full = lambda r, c: pl.BlockSpec((r, c), lambda n, j: (0, 0))
    out_spec = pl.BlockSpec((None, seq, tc), lambda n, j: (n, 0, j))
    return pl.pallas_call(
        _filter_kernel,
        out_shape=(jax.ShapeDtypeStruct((HY_ORDER, seq, d), F32),) * 2,
        grid=(HY_ORDER, nt),
        in_specs=[full(seq, V7X_LANES), full(V7X_LANES, hid), full(1, hid), full(hid, hid), full(1, hid),
                  full(1, hid),
                  pl.BlockSpec((hid, tc), lambda n, j: (0, (2 * n) * nt + j)),
                  pl.BlockSpec((hid, tc), lambda n, j: (0, (2 * n + 1) * nt + j)),
                  pl.BlockSpec((1, tc), lambda n, j: (0, j)),
                  full(2 * seq, seq), full(2 * seq, seq)],
        out_specs=(out_spec, out_spec),
        scratch_shapes=[pltpu.VMEM((seq, hid), F32)],
        compiler_params=_cparams("arbitrary", "arbitrary"),
        name="hyena_filter",
    )(feat, w1p, b1.reshape(1, hid), w2, b2.reshape(1, hid), freq.reshape(1, hid), w3, w3, deltas,
      dft_hi, dft_lo)


HYENA_ROWS_PER_STEP = 2048


def _hyconv_kernel(*refs, conv_z, conv_k, has_dst):
    z_ref, x_ref, hre_ref, him_ref, skip_ref, cz_ref, cx_ref, fwd_ref, inv_ref = refs[:9]
    o_ref = refs[9 + has_dst]
    seq = hre_ref.shape[0]
    n_sub = z_ref.shape[0] // seq
    zs = []
    for s in range(n_sub):
        z = z_ref[s * seq:(s + 1) * seq, :].astype(F32)
        zs.append(_dwconv(z, cz_ref[...], conv_k) if conv_z else z)
    pqs = [jnp.dot(fwd_ref[...], z.astype(BF16), preferred_element_type=F32) for z in zs]
    hre, him = hre_ref[...], him_ref[...]
    for s in range(n_sub):
        p, q = pqs[s][:seq], pqs[s][seq:]
        y_spec = jnp.concatenate([p * hre + q * him, p * him - q * hre], axis=0).astype(BF16)
        y = jnp.dot(inv_ref[...], y_spec, preferred_element_type=F32)
        x = _dwconv(x_ref[s * seq:(s + 1) * seq, :].astype(F32), cx_ref[...], conv_k)
        o_ref[s * seq:(s + 1) * seq, :] = (x * (y + skip_ref[...] * zs[s])).astype(o_ref.dtype)


def _hyconv(z, z_col0, conv_z, proj, x_col0, hre, him, order, skip, conv_w, fwd, inv, dst,
            *, row0, n_seq, seq, d, out_rows, out_row0, out_dtype):
    conv_k = conv_w.shape[0]
    n_sub = max(s for s in range(1, max(1, HYENA_ROWS_PER_STEP // seq) + 1)
                if n_seq % s == 0 and row0 % (s * seq) == 0 and out_row0 % (s * seq) == 0)
    blk = n_sub * seq
    tc = _tile(d, 512 if blk <= 1024 else 256, V7X_LANES)
    nt = d // tc
    r0 = row0 // blk
    zr0 = r0 if conv_z else 0
    zc, xc = z_col0 // tc, x_col0 // tc
    in_specs = [pl.BlockSpec((blk, tc), lambda j, b: (b + zr0, zc + j)),
                pl.BlockSpec((blk, tc), lambda j, b: (b + r0, xc + j)),
                pl.BlockSpec((None, seq, tc), lambda j, b: (order, 0, j)),
                pl.BlockSpec((None, seq, tc), lambda j, b: (order, 0, j)),
                pl.BlockSpec((1, tc), lambda j, b: (0, j)),
                pl.BlockSpec((conv_k, tc), lambda j, b: (0, zc + j)),
                pl.BlockSpec((conv_k, tc), lambda j, b: (0, xc + j)),
                pl.BlockSpec((2 * seq, seq), lambda j, b: (0, 0), pipeline_mode=_RESIDENT),
                pl.BlockSpec((seq, 2 * seq), lambda j, b: (0, 0), pipeline_mode=_RESIDENT)]
    args = [z, proj, hre, him, skip.reshape(1, d), conv_w, conv_w, fwd, inv]
    aliases = {}
    if dst is not None:
        in_specs.append(pl.BlockSpec(memory_space=pl.ANY))
        args.append(dst)
        aliases = {len(args) - 1: 0}
    out_r0 = out_row0 // blk
    return pl.pallas_call(
        functools.partial(_hyconv_kernel, conv_z=conv_z, conv_k=conv_k, has_dst=dst is not None),
        out_shape=jax.ShapeDtypeStruct((out_rows, d), out_dtype),
        grid=(nt, n_seq // n_sub),
        in_specs=in_specs,
        out_specs=pl.BlockSpec((blk, tc), lambda j, b: (b + out_r0, j)),
        input_output_aliases=aliases,
        compiler_params=_cparams("parallel", "parallel"),
        name="hyena_conv",
    )(*args)


def _grid_pos_emb(n_tokens, d):
    rows = n_tokens // GRID_W
    r, col = jnp.meshgrid(jnp.arange(rows), jnp.arange(GRID_W), indexing='ij')
    quarter = d // 4
    omega = 1.0 / (POS_BASE ** (jnp.arange(quarter, dtype=F32) / quarter))

    def emb1d(p):
        a = p.reshape(-1, 1).astype(F32) * omega[None, :]
        return jnp.concatenate([jnp.sin(a), jnp.cos(a)], axis=-1)

    return jnp.concatenate([emb1d(r), emb1d(col)], axis=-1)


def kernel(x_prompt, x_sample, state_delta, c, c_ctx, ada_w, ada_b, norm1_g, norm2_g, gdn_w_in, gdn_conv, gdn_a_log, gdn_dt_bias, gdn_onorm, gdn_w_out, hy_w_in, hy_b_in, hy_conv, hy_f_w1, hy_f_b1, hy_f_w2, hy_f_b2, hy_f_w3, hy_freq, hy_skip, hy_w_out, hy_b_out, ffn_w_gu, ffn_w_down, final_g):
    bc, lc, d = x_prompt.shape
    bl, ll, _ = x_sample.shape
    depth = ada_w.shape[0]
    n_heads, dk, dv = state_delta.shape[3:]
    tc_rows, tl_rows = bc * lc, bl * ll
    t = tc_rows + tl_rows
    assert tc_rows % ll == 0 and ll % lc == 0
    rows = _Rows(tc_rows, ll, t, _tile(math.gcd(tc_rows, ll), 512, V7X_SUBLANES))

    bm = 1 + bl
    bm_pad = -(-bm // V7X_SUBLANES) * V7X_SUBLANES
    cvec = jnp.concatenate([c_ctx[None, :], c, jnp.zeros((bm_pad - bm, d), F32)], axis=0)
    mods_all = _ada(cvec, ada_w, ada_b).reshape(depth, bm_pad, 6, d)

    x = _embed(x_prompt.reshape(tc_rows, d), x_sample.reshape(tl_rows, d), _grid_pos_emb(ll, d), rows)

    n_hy = hy_w_in.shape[0]
    dft = {}
    for seq in (lc, ll):
        cos_m, sin_m = _odd_dft(seq)
        fwd_f32 = jnp.concatenate([cos_m, sin_m], axis=0)
        fwd = fwd_f32.astype(BF16)
        fwd_lo = (fwd_f32 - fwd.astype(F32)).astype(BF16)
        inv = (jnp.concatenate([cos_m.T, -sin_m.T], axis=1) / seq).astype(BF16)
        spectra = [_hyena_filters(seq, d, hy_f_w1[j], hy_f_b1[j], hy_f_w2[j], hy_f_b2[j], hy_f_w3[j],
                                  hy_freq[j], fwd, fwd_lo) for j in range(n_hy)]
        dft[seq] = (fwd, inv, spectra)

    zero_b = jnp.zeros((d,), F32)
    mixed = jnp.zeros((t, n_heads * dv), BF16)
    shared = n_heads * dv == d
    mixed_hy = mixed if shared else jnp.zeros((t, d), BF16)
    new_state_delta = jnp.zeros((bc, gdn_w_in.shape[0], 2, n_heads, dk, dv), F32)
    n_mixers = 2
    n_gdn = gdn_w_in.shape[0]
    gdn_w_in_b = jnp.pad(gdn_w_in, ((0, 0), (0, 0), (0, V7X_LANES - 4 * n_heads))).astype(BF16)
    gdn_w_out_b, hy_w_in_b, hy_w_out_b = (w.astype(BF16) for w in (gdn_w_out, hy_w_in, hy_w_out))
    ffn_w_gu_b, ffn_w_down_b = ffn_w_gu.astype(BF16), ffn_w_down.astype(BF16)
    for layer in range(depth):
        j = layer // n_mixers
        if layer % n_mixers == 0:
            proj, ab = _in_proj(x, norm1_g[layer], mods_all, layer, gdn_w_in_b, j,
                                jnp.zeros((gdn_w_in_b.shape[2],), F32), rows, n_side=V7X_LANES)
            gdn = dict(t_total=t, n_heads=n_heads, dk=dk, dv=dv)
            weights = (gdn_conv[j], gdn_a_log[j], gdn_dt_bias[j], gdn_onorm[j])
            mixed, new_state_delta = _gdn_core(proj, ab, *weights, None, mixed, new_state_delta, row0=0, n_seq=bc,
                                               seq=lc, s0_spec=None, state_slot=(j, n_gdn), **gdn)
            s0_spec = lambda hb, j=j: pl.BlockSpec((None, None, 2, hb, dk, dv), lambda b, h: (b, j, 0, h, 0, 0))
            mixed, _ = _gdn_core(proj, ab, *weights, state_delta, mixed, None, row0=tc_rows, n_seq=bl, seq=ll,
                                 s0_spec=s0_spec, state_slot=(0, 1), **gdn)
            mixer = (mixed, gdn_w_out_b, j, zero_b)
            mixed_hy = mixed if shared else mixed_hy
        else:
            proj = _in_proj(x, norm1_g[layer], mods_all, layer, hy_w_in_b, j, hy_b_in[j], rows)
            for row0, n_seq, seq in ((0, bc, lc), (tc_rows, bl, ll)):
                fwd, inv, spectra = dft[seq]
                hre, him = spectra[j]
                z = proj
                for n in range(HY_ORDER):
                    last = n == HY_ORDER - 1
                    z = _hyconv(z, 0, n == 0, proj, (n + 1) * d, hre, him, n, hy_skip[j, n], hy_conv[j],
                                fwd, inv, mixed_hy if last else None, row0=row0, n_seq=n_seq, seq=seq, d=d,
                                out_rows=t if last else n_seq * seq, out_row0=row0 if last else 0,
                                out_dtype=BF16 if last else F32)
                mixed_hy = z
            mixer = (mixed_hy, hy_w_out_b, j, hy_b_out[j])
            mixed = mixed_hy if shared else mixed
        ffn = functools.partial(_ffn, x, norm2_g[layer], mods_all, ffn_w_gu_b, ffn_w_down_b, layer, rows,
                                mixer=mixer)
        if layer < depth - 1:
            x = ffn()
        else:
            y_prompt = ffn(final_g=final_g, tile0=0, n_tiles=rows.n_ctx_tiles)
            y_sample = ffn(final_g=final_g, tile0=rows.n_ctx_tiles, n_tiles=rows.n_tiles - rows.n_ctx_tiles)
    return (y_prompt.reshape(bc, lc, d), y_sample.reshape(bl, ll, d), new_state_delta)
```

```python
import functools
import math

import jax
import jax.numpy as jnp
from jax import lax
from jax.experimental import pallas as pl
from jax.experimental.pallas import tpu as pltpu

GRID_W = 64
CHUNK = 64
HY_ORDER = 2
HY_TARGET = 1e-2
HY_SHORT_PCT = 0.3
HY_LONG_PCT = 1.5
POS_BASE = 10000.0
EPS = 1e-6

V7X_LANES = 128
V7X_SUBLANES = 8
V7X_VMEM_LIMIT_BYTES = 48 * 1024 * 1024

BF16 = jnp.bfloat16
F32 = jnp.float32
HIGHEST = lax.Precision.HIGHEST


def _cparams(*sem):
    return pltpu.CompilerParams(dimension_semantics=sem, vmem_limit_bytes=V7X_VMEM_LIMIT_BYTES)


def _tile(n, target, align):
    if n <= target:
        return n
    best = None
    for t in range(align, target + 1, align):
        if n % t == 0:
            best = t
    assert best is not None, (n, target, align)
    return best


def _dot(a, b):
    return jnp.dot(a.astype(BF16), b.astype(BF16), preferred_element_type=F32)


def _dot_nt(a, b):
    return lax.dot_general(a.astype(BF16), b.astype(BF16), (((1,), (1,)), ((), ())),
                           preferred_element_type=F32)


def _dot_hi(a, b):
    return jnp.dot(a, b, preferred_element_type=F32, precision=HIGHEST)


def _silu(x):
    return x * jax.nn.sigmoid(x)


def _norm_mod(x, g, shift, scale):
    ms = jnp.mean(x * x, axis=-1, keepdims=True)
    return (x * lax.rsqrt(ms + EPS) * g) * (1.0 + scale) + shift


class _Rows:
    def __init__(self, tc, ll, t, tm):
        assert tc % tm == 0 and ll % tm == 0 and t % tm == 0
        self.n_ctx_tiles = tc // tm
        self.tiles_per_lat = ll // tm
        self.n_tiles = t // tm
        self.tm = tm

    def mod_index(self, i):
        lat = 1 + (i - self.n_ctx_tiles) // self.tiles_per_lat
        return jnp.where(i < self.n_ctx_tiles, 0, lat)


def _ada_kernel(c_ref, w_ref, b_ref, o_ref):
    o_ref[...] = _dot(_silu(c_ref[...]), w_ref[...]) + b_ref[...]


def _ada(cvec, ada_w, ada_b):
    depth, d, n = ada_w.shape
    bm = cvec.shape[0]
    tn = _tile(n, 1536, V7X_LANES)
    return pl.pallas_call(
        _ada_kernel,
        out_shape=jax.ShapeDtypeStruct((depth, bm, n), F32),
        grid=(depth, n // tn),
        in_specs=[pl.BlockSpec((bm, d), lambda l, j: (0, 0)),
                  pl.BlockSpec((None, d, tn), lambda l, j: (l, 0, j)),
                  pl.BlockSpec((None, 1, tn), lambda l, j: (l, 0, j))],
        out_specs=pl.BlockSpec((None, bm, tn), lambda l, j: (l, 0, j)),
        compiler_params=_cparams("parallel", "parallel"),
        name="ada",
    )(cvec, ada_w, ada_b.reshape(depth, 1, n))


def _embed_kernel(xp_ref, xs_ref, pos_ref, o_ref, *, n_ctx_tiles):
    i = pl.program_id(0)

    @pl.when(i < n_ctx_tiles)
    def _():
        o_ref[...] = xp_ref[...]

    @pl.when(i >= n_ctx_tiles)
    def _():
        o_ref[...] = xs_ref[...] + pos_ref[...]


def _embed(xp, xs, pos, rows):
    t, d = xp.shape[0] + xs.shape[0], xp.shape[1]
    tm, nct = rows.tm, rows.n_ctx_tiles
    npos = pos.shape[0] // tm
    return pl.pallas_call(
        functools.partial(_embed_kernel, n_ctx_tiles=nct),
        out_shape=jax.ShapeDtypeStruct((t, d), F32),
        grid=(rows.n_tiles,),
        in_specs=[pl.BlockSpec((tm, d), lambda i: (jnp.minimum(i, nct - 1), 0)),
                  pl.BlockSpec((tm, d), lambda i: (jnp.maximum(i - nct, 0), 0)),
                  pl.BlockSpec((tm, d), lambda i: (jnp.maximum(i - nct, 0) % npos, 0))],
        out_specs=pl.BlockSpec((tm, d), lambda i: (i, 0)),
        compiler_params=_cparams("parallel"),
        name="embed",
    )(xp, xs, pos)


_RESIDENT = pl.Buffered(1)


def _in_kernel(x_ref, g_ref, mod_ref, w_ref, b_ref, o_ref, *side_ref, chunk):
    m = mod_ref[...]
    h = _norm_mod(x_ref[...], g_ref[...], m[0:1, :], m[1:2, :]).astype(BF16)
    n = o_ref.shape[1]
    for c0 in range(0, n, chunk):
        cols = slice(c0, c0 + chunk)
        y = jnp.dot(h, w_ref[:, cols], preferred_element_type=F32) + b_ref[:, cols]
        o_ref[:, cols] = y.astype(o_ref.dtype)
    if side_ref:
        side_ref[0][...] = jnp.dot(h, w_ref[:, n:], preferred_element_type=F32) + b_ref[:, n:]


def _in_proj(x, g, mods_all, layer, w_all, w_index, b, rows, n_side=0):
    t, d = x.shape
    n = w_all.shape[2] - n_side
    tm = rows.tm
    chunk = _tile(n, 512, V7X_LANES)
    out_shape = [jax.ShapeDtypeStruct((t, n), BF16)]
    out_specs = [pl.BlockSpec((tm, n), lambda i: (i, 0))]
    if n_side:
        out_shape.append(jax.ShapeDtypeStruct((t, n_side), F32))
        out_specs.append(pl.BlockSpec((tm, n_side), lambda i: (i, 0)))
    out = pl.pallas_call(
        functools.partial(_in_kernel, chunk=chunk),
        out_shape=out_shape,
        grid=(rows.n_tiles,),
        in_specs=[pl.BlockSpec((tm, d), lambda i: (i, 0)),
                  pl.BlockSpec((1, d), lambda i: (0, 0), pipeline_mode=_RESIDENT),
                  pl.BlockSpec((None, None, 6, d), lambda i: (layer, rows.mod_index(i), 0, 0)),
                  pl.BlockSpec((None, d, n + n_side), lambda i: (w_index, 0, 0), pipeline_mode=_RESIDENT),
                  pl.BlockSpec((1, n + n_side), lambda i: (0, 0), pipeline_mode=_RESIDENT)],
        out_specs=out_specs,
        compiler_params=_cparams("parallel"),
        name="in_proj",
    )(x, g.reshape(1, d), mods_all, w_all, b.reshape(1, n + n_side))
    return out if n_side else out[0]


def _ffn_kernel(x_ref, g_ref, mod_ref, wgu_ref, wd_ref, *rest, chunk, has_mixer, has_final):
    o_ref = rest[-1]
    x = x_ref[...]
    m = mod_ref[...]
    if has_mixer:
        a_ref, wo_ref, bo_ref = rest[:3]
        x = x + m[2:3, :] * (_dot(a_ref[...], wo_ref[...]) + bo_ref[...])
    h = _norm_mod(x, g_ref[...], m[3:4, :], m[4:5, :]).astype(BF16)
    f = wd_ref.shape[0]
    n_chunks = f // chunk

    def gate_up(k):
        gate = jnp.dot(h, wgu_ref[:, k * chunk:(k + 1) * chunk], preferred_element_type=F32)
        up = jnp.dot(h, wgu_ref[:, f + k * chunk:f + (k + 1) * chunk], preferred_element_type=F32)
        return gate, up

    y = None
    pending = gate_up(0)
    for k in range(n_chunks):
        following = gate_up(k + 1) if k + 1 < n_chunks else None
        act = (_silu(pending[0]) * pending[1]).astype(BF16)
        part = jnp.dot(act, wd_ref[k * chunk:(k + 1) * chunk, :], preferred_element_type=F32)
        y = part if y is None else y + part
        pending = following
    out = x + m[5:6, :] * y
    if has_final:
        out = out * lax.rsqrt(jnp.mean(out * out, axis=-1, keepdims=True) + EPS) * rest[-2][...]
    o_ref[...] = out


def _ffn(x, g, mods_all, w_gu_all, w_down_all, layer, rows, mixer=None, final_g=None, tile0=0, n_tiles=None):
    d = x.shape[1]
    f = w_down_all.shape[1]
    tm = rows.tm
    n_tiles = rows.n_tiles if n_tiles is None else n_tiles
    chunk = _tile(f, 256, V7X_LANES)
    resident = lambda *s: pl.BlockSpec(s, lambda i: (0,) * len(s), pipeline_mode=_RESIDENT)
    in_specs = [pl.BlockSpec((tm, d), lambda i: (i + tile0, 0)),
                resident(1, d),
                pl.BlockSpec((None, None, 6, d), lambda i: (layer, rows.mod_index(i + tile0), 0, 0)),
                pl.BlockSpec((None, d, 2 * f), lambda i: (layer, 0, 0), pipeline_mode=_RESIDENT),
                pl.BlockSpec((None, f, d), lambda i: (layer, 0, 0), pipeline_mode=_RESIDENT)]
    args = [x, g.reshape(1, d), mods_all, w_gu_all, w_down_all]
    if mixer is not None:
        a, wo_all, wo_index, bo = mixer
        k = a.shape[1]
        in_specs += [pl.BlockSpec((tm, k), lambda i: (i + tile0, 0)),
                     pl.BlockSpec((None, k, d), lambda i: (wo_index, 0, 0), pipeline_mode=_RESIDENT),
                     resident(1, d)]
        args += [a, wo_all, bo.reshape(1, d)]
    if final_g is not None:
        in_specs.append(resident(1, d))
        args.append(final_g.reshape(1, d))
    return pl.pallas_call(
        functools.partial(_ffn_kernel, chunk=chunk, has_mixer=mixer is not None, has_final=final_g is not None),
        out_shape=jax.ShapeDtypeStruct((n_tiles * tm, d), F32),
        grid=(n_tiles,),
        in_specs=in_specs,
        out_specs=pl.BlockSpec((tm, d), lambda i: (i, 0)),
        compiler_params=_cparams("parallel"),
        name="ffn",
    )(*args)


def _dwconv(x, w, k):
    n = x.shape[0]
    half = k // 2
    e = V7X_SUBLANES
    assert half <= e and n >= 4 * e
    taps = [w[j:j + 1, :] for j in range(k)]
    shifts = [s for s in range(-half, half + 1) if s != 0]

    def conv(v, mask):
        acc = v * taps[half]
        for s in shifts:
            shifted = pltpu.roll(v, (-s) % v.shape[0], axis=0)
            acc = acc + (shifted if mask is None else jnp.where(mask(s), shifted, 0.0)) * taps[s + half]
        return acc

    row = lax.broadcasted_iota(jnp.int32, (2 * e, x.shape[1]), 0)
    top = conv(x[:2 * e], lambda s: row + s >= 0)[:e]
    bottom = conv(x[n - 2 * e:], lambda s: row + s < 2 * e)[e:]
    return jnp.concatenate([top, conv(x, None)[e:n - e], bottom], axis=0)


CONV_HALO = 16


def _dwconv_window(ref, cols, w, k, r0, rows, slab_ref):
    n = ref.shape[0]
    if rows == n:
        return _dwconv(ref[:, cols].astype(F32), w, k)
    half = k // 2
    h = CONV_HALO
    assert half <= h <= rows and n % rows == 0
    before = ref[pl.ds(pl.multiple_of(jnp.maximum(r0 - h, 0), h), h), cols].astype(F32)
    after = ref[pl.ds(pl.multiple_of(jnp.minimum(r0 + rows, n - h), h), h), cols].astype(F32)
    slab_ref[0:h, :] = jnp.where(r0 > 0, before, 0.0)
    slab_ref[h:h + rows, :] = ref[pl.ds(r0, rows), cols].astype(F32)
    slab_ref[h + rows:, :] = jnp.where(r0 + rows < n, after, 0.0)
    acc = None
    for s in range(-half, half + 1):
        term = slab_ref[h + s:h + s + rows, :] * w[s + half:s + half + 1, :]
        acc = term if acc is None else acc + term
    return acc


TRI_BASE = 8
GDN_PHASE1_CHAINS = 32
GDN_HEADS_PER_STEP = 4
GDN_ROWS_PER_STEP = 4096


def _unit_tri_inverses_minus_eye(mats, ri, ci):
    c = mats[0].shape[0]

    def same_block(s):
        sh = int(math.log2(s))
        return (ri >> sh) == (ci >> sh)

    ps = [jnp.where(same_block(TRI_BASE), -a, 0.0) for a in mats]
    es = ps
    n_lvl = int(math.log2(TRI_BASE))
    for lvl in range(n_lvl):
        es = [e + _dot(p, e) for p, e in zip(ps, es)]
        if lvl < n_lvl - 1:
            ps = [_dot(p, p) for p in ps]
    s = TRI_BASE
    while s < c:
        mask = jnp.logical_and(same_block(2 * s), jnp.logical_not(same_block(s)))
        offs = [jnp.where(mask, a, 0.0) for a in mats]
        ys = [off + _dot(off, e) for off, e in zip(offs, es)]
        es = [e - (y + _dot(e, y)) for e, y in zip(es, ys)]
        s *= 2
    return es


def _chunk_cumsum(x, pos, reverse):
    n = x.shape[0]
    s = 1
    while s < CHUNK:
        if reverse:
            x = x + jnp.where(pos + s < CHUNK, pltpu.roll(x, n - s, axis=0), 0.0)
        else:
            x = x + jnp.where(pos >= s, pltpu.roll(x, s, axis=0), 0.0)
        s *= 2
    return x


def _gdn_kernel(*refs, n_heads, conv_k, lockstep, n_kept):
    (alog_ref, dtb_ref, q_ref, k_ref, v_ref, gt_ref, ab_ref, cq_ref, ck_ref, cv_ref,
     onorm_ref, s0_ref) = refs[:12]
    o_ref, sfin_ref, gates_s, conv_s, w_s, u_s, qd_s, ak_s, gl_s, o_s, st_s = refs[12 + n_kept:]
    n_chains, seq, dk = w_s.shape
    hb = n_chains // 2
    head0 = pl.program_id(1) * hb
    c = CHUNK
    n_chunks = seq // c
    chains = [(hh, d) for hh in range(hb) for d in range(2)]

    @pl.when(head0 == 0)
    def _():
        ab = ab_ref[...]
        pos = jnp.bitwise_and(lax.broadcasted_iota(jnp.int32, ab.shape, 0), c - 1)
        g = -jnp.exp(alog_ref[...]) * jax.nn.softplus(ab + dtb_ref[...])
        g_fwd = _chunk_cumsum(g, pos, reverse=False)
        g_rev = _chunk_cumsum(g, pos, reverse=True)
        gates_s[0] = g_fwd
        gates_s[1] = g_rev - g
        gates_s[2] = g_rev
        gates_s[3] = g_fwd - g
        gates_s[4] = jax.nn.sigmoid(ab)

    lane = lax.broadcasted_iota(jnp.int32, (1, V7X_LANES), 1)

    rows = lockstep * c

    def column(i, idx, win):
        one_hot = (lane == idx).astype(F32)
        col = jnp.sum(gates_s[i, win, :] * one_hot, axis=1, keepdims=True)
        return jnp.broadcast_to(col, (rows, V7X_LANES))

    ri = lax.broadcasted_iota(jnp.int32, (c, c), 0)
    ci = lax.broadcasted_iota(jnp.int32, (c, c), 1)
    incl = (ri >= ci, ri <= ci)
    strict = (ri > ci, ri < ci)

    def phase1(it, carry):
        r0 = pl.multiple_of(it * rows, rows)
        win = pl.ds(r0, rows)
        prepared = []
        for hh in range(hb):
            cols = slice(hh * dk, (hh + 1) * dk)
            q = _silu(_dwconv_window(q_ref, cols, cq_ref[:, cols], conv_k, r0, rows, conv_s.at[3 * hh]))
            k = _silu(_dwconv_window(k_ref, cols, ck_ref[:, cols], conv_k, r0, rows, conv_s.at[3 * hh + 1]))
            v = _silu(_dwconv_window(v_ref, cols, cv_ref[:, cols], conv_k, r0, rows, conv_s.at[3 * hh + 2]))
            q = q * lax.rsqrt(jnp.sum(q * q, axis=-1, keepdims=True) + EPS) * (dk ** -0.5)
            k = k * lax.rsqrt(jnp.sum(k * k, axis=-1, keepdims=True) + EPS)
            for d in range(2):
                head = head0 + hh
                g_cum = column(2 * d, d * n_heads + head, win)
                g_tail = column(2 * d + 1, d * n_heads + head, win)
                beta = column(4, 2 * n_heads + d * n_heads + head, win)
                e_cum = jnp.exp(g_cum)
                kb = k * beta
                prepared.append((q, k, g_cum, kb, kb * e_cum, v * beta, k * jnp.exp(g_tail)))
                qd_s[2 * hh + d, win, :] = (q * e_cum).astype(qd_s.dtype)
                g_tot = jnp.exp(g_cum + g_tail)
                for gi in range(lockstep):
                    gl_s[2 * hh + d, pl.ds(it * lockstep + gi, 1), :] = g_tot[gi * c:gi * c + 1, :]
        items = []
        for gi in range(lockstep):
            part = slice(gi * c, (gi + 1) * c)
            for ch, (_, d) in enumerate(chains):
                sl = pl.ds(pl.multiple_of(r0 + gi * c, c), c)
                items.append((it * lockstep + gi, sl, ch, d) + tuple(a[part] for a in prepared[ch]))
        decays, kqs = [], []
        for ic, sl, ch, d, qc, kc, gc, kb, w0, u0, ktl in items:
            diff = gc[:, :c] - gc.T[:c, :]
            decays.append(jnp.where(incl[d], jnp.exp(jnp.where(incl[d], diff, 0.0)), 0.0))
            kqs.append(_dot_nt(jnp.concatenate([kb, qc], axis=0), kc))
        a_kks = [jnp.where(strict[item[3]], kq[:c] * decay, 0.0)
                 for item, kq, decay in zip(items, kqs, decays)]
        es = _unit_tri_inverses_minus_eye(a_kks, ri, ci)
        rhss = [jnp.concatenate([item[8], item[9]], axis=1) for item in items]
        wus = [rhs + _dot(e, rhs) for e, rhs in zip(es, rhss)]
        for item, kq, decay, wu in zip(items, kqs, decays, wus):
            ic, sl, ch = item[:3]
            w_s[ch, sl, :] = wu[:, :dk].astype(w_s.dtype)
            u_s[ch, sl, :] = wu[:, dk:]
            ak_s[ch, ic] = jnp.concatenate([kq[c:] * decay, item[10].T], axis=0).astype(ak_s.dtype)
        return carry

    lax.fori_loop(0, n_chunks // lockstep, phase1, 0)

    for hh, d in chains:
        st_s[2 * hh + d] = s0_ref[d, hh]

    def phase2(i, carry):
        n = 2 * hb
        ics = [n_chunks - 1 - i if d else i for _, d in chains]
        r0s = [pl.multiple_of(ic * c, c) for ic in ics]
        sls = [pl.ds(r0, c) for r0 in r0s]
        ss = [st_s[ch] for ch in range(n)]
        wqs = [_dot(jnp.concatenate([w_s[ch, sls[ch], :], qd_s[ch, sls[ch], :]], axis=0), ss[ch]) for ch in range(n)]
        v_news = [u_s[ch, sls[ch], :] - wqs[ch][:c] for ch in range(n)]
        outs = [_dot(ak_s[ch, ics[ch]], v_news[ch]) for ch in range(n)]
        for ch in range(n):
            o_s[ch, sls[ch], :] = wqs[ch][c:] + outs[ch][:c]
            st_s[ch] = ss[ch] * gl_s[ch, pl.ds(ics[ch], 1), :] + outs[ch][c:]
        return carry

    lax.fori_loop(0, n_chunks, phase2, 0, unroll=2)

    for hh, d in chains:
        sfin_ref[d, hh] = st_s[2 * hh + d]
    for hh in range(hb):
        cols = slice(hh * dk, (hh + 1) * dk)
        o = o_s[2 * hh] + o_s[2 * hh + 1]
        o = o * lax.rsqrt(jnp.mean(o * o, axis=-1, keepdims=True) + EPS)
        o_ref[:, cols] = (o * onorm_ref[...] * _silu(gt_ref[:, cols].astype(F32))).astype(o_ref.dtype)


def _gdn_core(proj, ab, conv_w, a_log, dt_bias, onorm, s0, dst, states_dst, *, t_total, row0, n_seq, seq,
              n_heads, dk, dv, s0_spec, state_slot):
    assert dk == dv == V7X_LANES and row0 % seq == 0
    conv_k = conv_w.shape[0]
    r0 = row0 // seq
    n_chunks = seq // CHUNK
    hb = _tile(n_heads, min(GDN_HEADS_PER_STEP, max(1, GDN_ROWS_PER_STEP // seq)), 1)
    lockstep = _tile(n_chunks, max(1, GDN_PHASE1_CHAINS // (2 * hb)), 1)
    gate_pad = lambda p: jnp.pad(p.reshape(1, 2 * n_heads), ((0, 0), (0, V7X_LANES - 2 * n_heads)))
    lane_vec = pl.BlockSpec((1, V7X_LANES), lambda b, h: (0, 0))
    if s0 is None:
        s0 = jnp.zeros((2, hb, dk, dv), F32)
        s0_in = pl.BlockSpec((2, hb, dk, dv), lambda b, h: (0, 0, 0, 0))
    else:
        s0_in = s0_spec(hb)
    nb = n_heads // hb
    col = lambda sec: pl.BlockSpec((seq, hb * dk), lambda b, h: (b + r0, sec * nb + h))
    cw = lambda sec: pl.BlockSpec((conv_k, hb * dk), lambda b, h: (0, sec * nb + h))
    f32 = lambda *s: pltpu.VMEM(s, F32)
    in_specs = [lane_vec, lane_vec, col(0), col(1), col(2), col(3),
                pl.BlockSpec((seq, V7X_LANES), lambda b, h: (b + r0, 0)),
                cw(0), cw(1), cw(2),
                pl.BlockSpec((1, dv), lambda b, h: (0, 0)),
                s0_in]
    args = [gate_pad(a_log), gate_pad(dt_bias), proj, proj, proj, proj, ab, conv_w, conv_w, conv_w,
            onorm.reshape(1, dv), s0]
    aliases = {}
    for out_idx, kept in enumerate((dst, states_dst)):
        if kept is not None:
            in_specs.append(pl.BlockSpec(memory_space=pl.ANY))
            args.append(kept)
            aliases[len(args) - 1] = out_idx
    nc = 2 * hb
    slot, n_slots = state_slot
    return pl.pallas_call(
        functools.partial(_gdn_kernel, n_heads=n_heads, conv_k=conv_k, lockstep=lockstep,
                          n_kept=len(aliases)),
        out_shape=(jax.ShapeDtypeStruct((t_total, n_heads * dv), BF16),
                   jax.ShapeDtypeStruct((n_seq, n_slots, 2, n_heads, dk, dv), F32)),
        grid=(n_seq, nb),
        in_specs=in_specs,
        out_specs=(pl.BlockSpec((seq, hb * dv), lambda b, h: (b + r0, h)),
                   pl.BlockSpec((None, None, 2, hb, dk, dv), lambda b, h: (b, slot, 0, h, 0, 0))),
        scratch_shapes=[f32(5, seq, V7X_LANES),
                        f32(3 * hb, lockstep * CHUNK + 2 * CONV_HALO, dk),
                        pltpu.VMEM((nc, seq, dk), BF16), f32(nc, seq, dv), pltpu.VMEM((nc, seq, dk), BF16),
                        pltpu.VMEM((nc, n_chunks, CHUNK + dk, CHUNK), BF16),
                        f32(nc, max(n_chunks, V7X_SUBLANES), V7X_LANES),
                        f32(nc, seq, dv), f32(nc, dk, dv)],
        input_output_aliases=aliases,
        compiler_params=_cparams("parallel", "arbitrary"),
        name="gdn_core",
    )(*args)


def _odd_dft(seq):
    k = jnp.arange(seq, dtype=jnp.int32)[:, None]
    m = jnp.arange(seq, dtype=jnp.int32)[None, :]
    r = ((2 * k + 1) * m) % (4 * seq)
    ang = r.astype(F32) * (math.pi / (2 * seq))
    return jnp.cos(ang), jnp.sin(ang)


def _filter_kernel(feat_ref, w1_ref, b1_ref, w2_ref, b2_ref, fr_ref, w3f_ref, w3b_ref, dl_ref,
                   dft_hi_ref, dft_lo_ref, hre_ref, him_ref, hid_s):
    seq = feat_ref.shape[0]
    feat = feat_ref[...]

    @pl.when(jnp.logical_and(pl.program_id(0) == 0, pl.program_id(1) == 0))
    def _():
        fr = fr_ref[...]
        hid1 = jnp.sin(fr * (_dot(feat, w1_ref[...]) + b1_ref[...]))
        hid_s[...] = jnp.sin(fr * (_dot(hid1, w2_ref[...]) + b2_ref[...]))

    hid = hid_s[...]
    window = jnp.exp(-feat[:, 0:1] * dl_ref[...])
    hf = _dot(hid, w3f_ref[...]) * window
    hb = _dot(hid, w3b_ref[...]) * window
    row = lax.broadcasted_iota(jnp.int32, hb.shape, 0)
    hb = jnp.where(row == 0, 0.0, hb)

    def dft(rows, val):
        v_hi = val.astype(BF16)
        v_lo = (val - v_hi.astype(F32)).astype(BF16)
        m_hi, m_lo = dft_hi_ref[rows, :], dft_lo_ref[rows, :]
        return (jnp.dot(m_hi, v_hi, preferred_element_type=F32)
                + (jnp.dot(m_hi, v_lo, preferred_element_type=F32)
                   + jnp.dot(m_lo, v_hi, preferred_element_type=F32)))

    hre_ref[...] = dft(slice(0, seq), hf + hb)
    him_ref[...] = dft(slice(seq, 2 * seq), hb - hf)


def _hyena_filters(seq, d, w1, b1, w2, b2, w3, freq, dft_hi, dft_lo):
    emb, hid = w1.shape
    bands = (emb - 1) // 2
    t = jnp.linspace(0.0, 1.0, seq, dtype=F32)[:, None]
    wpos = (2.0 * math.pi / seq) * jnp.arange(seq, dtype=F32)[:, None]
    fb = jnp.linspace(1e-4, bands - 1, bands, dtype=F32)[None, :]
    feat = jnp.concatenate([t, jnp.cos(fb * wpos), -jnp.sin(fb * wpos)], axis=-1)
    feat = jnp.pad(feat, ((0, 0), (0, V7X_LANES - emb)))
    w1p = jnp.pad(w1, ((0, V7X_LANES - emb), (0, 0)))
    max_decay = math.log(HY_TARGET) / HY_SHORT_PCT
    min_decay = math.log(HY_TARGET) / HY_LONG_PCT
    deltas = jnp.abs(jnp.linspace(min_decay, max_decay, d, dtype=F32))[None, :]
    tc = _tile(d, 256, V7X_LANES)
    nt = d // tc
    full = lambda r, c: pl.BlockSpec((r, c), lambda n, j: (0, 0))
    out_spec = pl.BlockSpec((None, seq, tc), lambda n, j: (n, 0, j))
    return pl.pallas_call(
        _filter_kernel,
        out_shape=(jax.ShapeDtypeStruct((HY_ORDER, seq, d), F32),) * 2,
        grid=(HY_ORDER, nt),
        in_specs=[full(seq, V7X_LANES), full(V7X_LANES, hid), full(1, hid), full(hid, hid), full(1, hid),
                  full(1, hid),
                  pl.BlockSpec((hid, tc), lambda n, j: (0, (2 * n) * nt + j)),
                  pl.BlockSpec((hid, tc), lambda n, j: (0, (2 * n + 1) * nt + j)),
                  pl.BlockSpec((1, tc), lambda n, j: (0, j)),
                  full(2 * seq, seq), full(2 * seq, seq)],
        out_specs=(out_spec, out_spec),
        scratch_shapes=[pltpu.VMEM((seq, hid), F32)],
        compiler_params=_cparams("arbitrary", "arbitrary"),
        name="hyena_filter",
    )(feat, w1p, b1.reshape(1, hid), w2, b2.reshape(1, hid), freq.reshape(1, hid), w3, w3, deltas,
      dft_hi, dft_lo)


HYENA_ROWS_PER_STEP = 2048


def _hyconv_kernel(*refs, conv_z, conv_k, has_dst):
    z_ref, x_ref, hre_ref, him_ref, skip_ref, cz_ref, cx_ref, fwd_ref, inv_ref = refs[:9]
    o_ref = refs[9 + has_dst]
    seq = hre_ref.shape[0]
    n_sub = z_ref.shape[0] // seq
    zs = []
    for s in range(n_sub):
        z = z_ref[s * seq:(s + 1) * seq, :].astype(F32)
        zs.append(_dwconv(z, cz_ref[...], conv_k) if conv_z else z)
    pqs = [jnp.dot(fwd_ref[...], z.astype(BF16), preferred_element_type=F32) for z in zs]
    hre, him = hre_ref[...], him_ref[...]
    for s in range(n_sub):
        p, q = pqs[s][:seq], pqs[s][seq:]
        y_spec = jnp.concatenate([p * hre + q * him, p * him - q * hre], axis=0).astype(BF16)
        y = jnp.dot(inv_ref[...], y_spec, preferred_element_type=F32)
        x = _dwconv(x_ref[s * seq:(s + 1) * seq, :].astype(F32), cx_ref[...], conv_k)
        o_ref[s * seq:(s + 1) * seq, :] = (x * (y + skip_ref[...] * zs[s])).astype(o_ref.dtype)


def _hyconv(z, z_col0, conv_z, proj, x_col0, hre, him, order, skip, conv_w, fwd, inv, dst,
            *, row0, n_seq, seq, d, out_rows, out_row0, out_dtype):
    conv_k = conv_w.shape[0]
    n_sub = max(s for s in range(1, max(1, HYENA_ROWS_PER_STEP // seq) + 1)
                if n_seq % s == 0 and row0 % (s * seq) == 0 and out_row0 % (s * seq) == 0)
    blk = n_sub * seq
    tc = _tile(d, 512 if blk <= 1024 else 256, V7X_LANES)
    nt = d // tc
    r0 = row0 // blk
    zr0 = r0 if conv_z else 0
    zc, xc = z_col0 // tc, x_col0 // tc
    in_specs = [pl.BlockSpec((blk, tc), lambda j, b: (b + zr0, zc + j)),
                pl.BlockSpec((blk, tc), lambda j, b: (b + r0, xc + j)),
                pl.BlockSpec((None, seq, tc), lambda j, b: (order, 0, j)),
                pl.BlockSpec((None, seq, tc), lambda j, b: (order, 0, j)),
                pl.BlockSpec((1, tc), lambda j, b: (0, j)),
                pl.BlockSpec((conv_k, tc), lambda j, b: (0, zc + j)),
                pl.BlockSpec((conv_k, tc), lambda j, b: (0, xc + j)),
                pl.BlockSpec((2 * seq, seq), lambda j, b: (0, 0), pipeline_mode=_RESIDENT),
                pl.BlockSpec((seq, 2 * seq), lambda j, b: (0, 0), pipeline_mode=_RESIDENT)]
    args = [z, proj, hre, him, skip.reshape(1, d), conv_w, conv_w, fwd, inv]
    aliases = {}
    if dst is not None:
        in_specs.append(pl.BlockSpec(memory_space=pl.ANY))
        args.append(dst)
        aliases = {len(args) - 1: 0}
    out_r0 = out_row0 // blk
    return pl.pallas_call(
        functools.partial(_hyconv_kernel, conv_z=conv_z, conv_k=conv_k, has_dst=dst is not None),
        out_shape=jax.ShapeDtypeStruct((out_rows, d), out_dtype),
        grid=(nt, n_seq // n_sub),
        in_specs=in_specs,
        out_specs=pl.BlockSpec((blk, tc), lambda j, b: (b + out_r0, j)),
        input_output_aliases=aliases,
        compiler_params=_cparams("parallel", "parallel"),
        name="hyena_conv",
    )(*args)


def _grid_pos_emb(n_tokens, d):
    rows = n_tokens // GRID_W
    r, col = jnp.meshgrid(jnp.arange(rows), jnp.arange(GRID_W), indexing='ij')
    quarter = d // 4
    omega = 1.0 / (POS_BASE ** (jnp.arange(quarter, dtype=F32) / quarter))

    def emb1d(p):
        a = p.reshape(-1, 1).astype(F32) * omega[None, :]
        return jnp.concatenate([jnp.sin(a), jnp.cos(a)], axis=-1)

    return jnp.concatenate([emb1d(r), emb1d(col)], axis=-1)


def kernel(x_prompt, x_sample, state_delta, c, c_ctx, ada_w, ada_b, norm1_g, norm2_g, gdn_w_in, gdn_conv, gdn_a_log, gdn_dt_bias, gdn_onorm, gdn_w_out, hy_w_in, hy_b_in, hy_conv, hy_f_w1, hy_f_b1, hy_f_w2, hy_f_b2, hy_f_w3, hy_freq, hy_skip, hy_w_out, hy_b_out, ffn_w_gu, ffn_w_down, final_g):
    bc, lc, d = x_prompt.shape
    bl, ll, _ = x_sample.shape
    depth = ada_w.shape[0]
    n_heads, dk, dv = state_delta.shape[3:]
    tc_rows, tl_rows = bc * lc, bl * ll
    t = tc_rows + tl_rows
    assert tc_rows % ll == 0 and ll % lc == 0
    rows = _Rows(tc_rows, ll, t, _tile(math.gcd(tc_rows, ll), 512, V7X_SUBLANES))

    bm = 1 + bl
    bm_pad = -(-bm // V7X_SUBLANES) * V7X_SUBLANES
    cvec = jnp.concatenate([c_ctx[None, :], c, jnp.zeros((bm_pad - bm, d), F32)], axis=0)
    mods_all = _ada(cvec, ada_w, ada_b).reshape(depth, bm_pad, 6, d)

    x = _embed(x_prompt.reshape(tc_rows, d), x_sample.reshape(tl_rows, d), _grid_pos_emb(ll, d), rows)

    n_hy = hy_w_in.shape[0]
    dft = {}
    for seq in (lc, ll):
        cos_m, sin_m = _odd_dft(seq)
        fwd_f32 = jnp.concatenate([cos_m, sin_m], axis=0)
        fwd = fwd_f32.astype(BF16)
        fwd_lo = (fwd_f32 - fwd.astype(F32)).astype(BF16)
        inv = (jnp.concatenate([cos_m.T, -sin_m.T], axis=1) / seq).astype(BF16)
        spectra = [_hyena_filters(seq, d, hy_f_w1[j], hy_f_b1[j], hy_f_w2[j], hy_f_b2[j], hy_f_w3[j],
                                  hy_freq[j], fwd, fwd_lo) for j in range(n_hy)]
        dft[seq] = (fwd, inv, spectra)

    zero_b = jnp.zeros((d,), F32)
    mixed = jnp.zeros((t, n_heads * dv), BF16)
    shared = n_heads * dv == d
    mixed_hy = mixed if shared else jnp.zeros((t, d), BF16)
    new_state_delta = jnp.zeros((bc, gdn_w_in.shape[0], 2, n_heads, dk, dv), F32)
    n_mixers = 2
    n_gdn = gdn_w_in.shape[0]
    gdn_w_in_b = jnp.pad(gdn_w_in, ((0, 0), (0, 0), (0, V7X_LANES - 4 * n_heads))).astype(BF16)
    gdn_w_out_b, hy_w_in_b, hy_w_out_b = (w.astype(BF16) for w in (gdn_w_out, hy_w_in, hy_w_out))
    ffn_w_gu_b, ffn_w_down_b = ffn_w_gu.astype(BF16), ffn_w_down.astype(BF16)
    for layer in range(depth):
        j = layer // n_mixers
        if layer % n_mixers == 0:
            proj, ab = _in_proj(x, norm1_g[layer], mods_all, layer, gdn_w_in_b, j,
                                jnp.zeros((gdn_w_in_b.shape[2],), F32), rows, n_side=V7X_LANES)
            gdn = dict(t_total=t, n_heads=n_heads, dk=dk, dv=dv)
            weights = (gdn_conv[j], gdn_a_log[j], gdn_dt_bias[j], gdn_onorm[j])
            mixed, new_state_delta = _gdn_core(proj, ab, *weights, None, mixed, new_state_delta, row0=0, n_seq=bc,
                                               seq=lc, s0_spec=None, state_slot=(j, n_gdn), **gdn)
            s0_spec = lambda hb, j=j: pl.BlockSpec((None, None, 2, hb, dk, dv), lambda b, h: (b, j, 0, h, 0, 0))
            mixed, _ = _gdn_core(proj, ab, *weights, state_delta, mixed, None, row0=tc_rows, n_seq=bl, seq=ll,
                                 s0_spec=s0_spec, state_slot=(0, 1), **gdn)
            mixer = (mixed, gdn_w_out_b, j, zero_b)
            mixed_hy = mixed if shared else mixed_hy
        else:
            proj = _in_proj(x, norm1_g[layer], mods_all, layer, hy_w_in_b, j, hy_b_in[j], rows)
            for row0, n_seq, seq in ((0, bc, lc), (tc_rows, bl, ll)):
                fwd, inv, spectra = dft[seq]
                hre, him = spectra[j]
                z = proj
                for n in range(HY_ORDER):
                    last = n == HY_ORDER - 1
                    z = _hyconv(z, 0, n == 0, proj, (n + 1) * d, hre, him, n, hy_skip[j, n], hy_conv[j],
                                fwd, inv, mixed_hy if last else None, row0=row0, n_seq=n_seq, seq=seq, d=d,
                                out_rows=t if last else n_seq * seq, out_row0=row0 if last else 0,
                                out_dtype=BF16 if last else F32)
                mixed_hy = z
            mixer = (mixed_hy, hy_w_out_b, j, hy_b_out[j])
            mixed = mixed_hy if shared else mixed
        ffn = functools.partial(_ffn, x, norm2_g[layer], mods_all, ffn_w_gu_b, ffn_w_down_b, layer, rows,
                                mixer=mixer)
        if layer < depth - 1:
            x = ffn()
        else:
            y_prompt = ffn(final_g=final_g, tile0=0, n_tiles=rows.n_ctx_tiles)
            y_sample = ffn(final_g=final_g, tile0=rows.n_ctx_tiles, n_tiles=rows.n_tiles - rows.n_ctx_tiles)
    return (y_prompt.reshape(bc, lc, d), y_sample.reshape(bl, ll, d), new_state_delta)
```

```python
import functools
import math

import jax
import jax.numpy as jnp
import numpy as np
from jax import lax
from jax.experimental import pallas as pl
from jax.experimental.pallas import tpu as pltpu

GRID_W = 64
CHUNK = 64
HY_ORDER = 2
HY_TARGET = 1e-2
HY_SHORT_PCT = 0.3
HY_LONG_PCT = 1.5
POS_BASE = 10000.0
EPS = 1e-6

V7X_LANES = 128
V7X_SUBLANES = 8
V7X_VMEM_LIMIT_BYTES = 48 * 1024 * 1024

BF16 = jnp.bfloat16
F32 = jnp.float32
HIGHEST = lax.Precision.HIGHEST


def _cparams(*sem):
    return pltpu.CompilerParams(dimension_semantics=sem, vmem_limit_bytes=V7X_VMEM_LIMIT_BYTES)


def _tile(n, target, align):
    if n <= target:
        return n
    best = None
    for t in range(align, target + 1, align):
        if n % t == 0:
            best = t
    assert best is not None, (n, target, align)
    return best


def _dot(a, b):
    return jnp.dot(a.astype(BF16), b.astype(BF16), preferred_element_type=F32)


def _dot_nt(a, b):
    return lax.dot_general(a.astype(BF16), b.astype(BF16), (((1,), (1,)), ((), ())),
                           preferred_element_type=F32)


def _dot_hi(a, b):
    return jnp.dot(a, b, preferred_element_type=F32, precision=HIGHEST)


def _silu(x):
    return x * jax.nn.sigmoid(x)


def _norm_mod(x, g, shift, scale):
    ms = jnp.mean(x * x, axis=-1, keepdims=True)
    return (x * lax.rsqrt(ms + EPS) * g) * (1.0 + scale) + shift


class _Rows:
    def __init__(self, tc, ll, t, tm):
        assert tc % tm == 0 and ll % tm == 0 and t % tm == 0
        self.n_ctx_tiles = tc // tm
        self.tiles_per_lat = ll // tm
        self.n_tiles = t // tm
        self.tm = tm

    def mod_index(self, i):
        lat = 1 + (i - self.n_ctx_tiles) // self.tiles_per_lat
        return jnp.where(i < self.n_ctx_tiles, 0, lat)


def _ada_kernel(c_ref, w_ref, b_ref, o_ref):
    o_ref[...] = _dot(_silu(c_ref[...]), w_ref[...]) + b_ref[...]


def _ada(cvec, ada_w, ada_b):
    depth, d, n = ada_w.shape
    bm = cvec.shape[0]
    tn = _tile(n, 1536, V7X_LANES)
    return pl.pallas_call(
        _ada_kernel,
        out_shape=jax.ShapeDtypeStruct((depth, bm, n), F32),
        grid=(depth, n // tn),
        in_specs=[pl.BlockSpec((bm, d), lambda l, j: (0, 0)),
                  pl.BlockSpec((None, d, tn), lambda l, j: (l, 0, j)),
                  pl.BlockSpec((None, 1, tn), lambda l, j: (l, 0, j))],
        out_specs=pl.BlockSpec((None, bm, tn), lambda l, j: (l, 0, j)),
        compiler_params=_cparams("parallel", "parallel"),
        name="ada",
    )(cvec, ada_w, ada_b.reshape(depth, 1, n))


def _embed_kernel(xp_ref, xs_ref, pos_ref, o_ref, *, n_ctx_tiles):
    i = pl.program_id(0)

    @pl.when(i < n_ctx_tiles)
    def _():
        o_ref[...] = xp_ref[...]

    @pl.when(i >= n_ctx_tiles)
    def _():
        o_ref[...] = xs_ref[...] + pos_ref[...]


def _embed(xp, xs, pos, rows):
    t, d = xp.shape[0] + xs.shape[0], xp.shape[1]
    tm, nct = rows.tm, rows.n_ctx_tiles
    npos = pos.shape[0] // tm
    return pl.pallas_call(
        functools.partial(_embed_kernel, n_ctx_tiles=nct),
        out_shape=jax.ShapeDtypeStruct((t, d), F32),
        grid=(rows.n_tiles,),
        in_specs=[pl.BlockSpec((tm, d), lambda i: (jnp.minimum(i, nct - 1), 0)),
                  pl.BlockSpec((tm, d), lambda i: (jnp.maximum(i - nct, 0), 0)),
                  pl.BlockSpec((tm, d), lambda i: (jnp.maximum(i - nct, 0) % npos, 0))],
        out_specs=pl.BlockSpec((tm, d), lambda i: (i, 0)),
        compiler_params=_cparams("parallel"),
        name="embed",
    )(xp, xs, pos)


_RESIDENT = pl.Buffered(1)


def _in_kernel(x_ref, g_ref, mod_ref, w_ref, b_ref, o_ref, *side_ref, chunk):
    m = mod_ref[...]
    h = _norm_mod(x_ref[...], g_ref[...], m[0:1, :], m[1:2, :]).astype(BF16)
    n = o_ref.shape[1]
    for c0 in range(0, n, chunk):
        cols = slice(c0, c0 + chunk)
        y = jnp.dot(h, w_ref[:, cols], preferred_element_type=F32) + b_ref[:, cols]
        o_ref[:, cols] = y.astype(o_ref.dtype)
    if side_ref:
        side_ref[0][...] = jnp.dot(h, w_ref[:, n:], preferred_element_type=F32) + b_ref[:, n:]


def _in_proj(x, g, mods_all, layer, w_all, w_index, b, rows, n_side=0):
    t, d = x.shape
    n = w_all.shape[2] - n_side
    tm = rows.tm
    chunk = _tile(n, 512, V7X_LANES)
    out_shape = [jax.ShapeDtypeStruct((t, n), BF16)]
    out_specs = [pl.BlockSpec((tm, n), lambda i: (i, 0))]
    if n_side:
        out_shape.append(jax.ShapeDtypeStruct((t, n_side), F32))
        out_specs.append(pl.BlockSpec((tm, n_side), lambda i: (i, 0)))
    out = pl.pallas_call(
        functools.partial(_in_kernel, chunk=chunk),
        out_shape=out_shape,
        grid=(rows.n_tiles,),
        in_specs=[pl.BlockSpec((tm, d), lambda i: (i, 0)),
                  pl.BlockSpec((1, d), lambda i: (0, 0), pipeline_mode=_RESIDENT),
                  pl.BlockSpec((None, None, 6, d), lambda i: (layer, rows.mod_index(i), 0, 0)),
                  pl.BlockSpec((None, d, n + n_side), lambda i: (w_index, 0, 0), pipeline_mode=_RESIDENT),
                  pl.BlockSpec((1, n + n_side), lambda i: (0, 0), pipeline_mode=_RESIDENT)],
        out_specs=out_specs,
        compiler_params=_cparams("parallel"),
        name="in_proj",
    )(x, g.reshape(1, d), mods_all, w_all, b.reshape(1, n + n_side))
    return out if n_side else out[0]


def _ffn_kernel(x_ref, g_ref, mod_ref, wgu_ref, wd_ref, *rest, chunk, has_mixer, has_final):
    o_ref = rest[-1]
    x = x_ref[...]
    m = mod_ref[...]
    if has_mixer:
        a_ref, wo_ref, bo_ref = rest[:3]
        x = x + m[2:3, :] * (_dot(a_ref[...], wo_ref[...]) + bo_ref[...])
    h = _norm_mod(x, g_ref[...], m[3:4, :], m[4:5, :]).astype(BF16)
    f = wd_ref.shape[0]
    n_chunks = f // chunk

    def gate_up(k):
        gate = jnp.dot(h, wgu_ref[:, k * chunk:(k + 1) * chunk], preferred_element_type=F32)
        up = jnp.dot(h, wgu_ref[:, f + k * chunk:f + (k + 1) * chunk], preferred_element_type=F32)
        return gate, up

    y = None
    pending = gate_up(0)
    for k in range(n_chunks):
        following = gate_up(k + 1) if k + 1 < n_chunks else None
        act = (_silu(pending[0]) * pending[1]).astype(BF16)
        part = jnp.dot(act, wd_ref[k * chunk:(k + 1) * chunk, :], preferred_element_type=F32)
        y = part if y is None else y + part
        pending = following
    out = x + m[5:6, :] * y
    if has_final:
        out = out * lax.rsqrt(jnp.mean(out * out, axis=-1, keepdims=True) + EPS) * rest[-2][...]
    o_ref[...] = out


def _ffn(x, g, mods_all, w_gu_all, w_down_all, layer, rows, mixer=None, final_g=None, tile0=0, n_tiles=None):
    d = x.shape[1]
    f = w_down_all.shape[1]
    tm = rows.tm
    n_tiles = rows.n_tiles if n_tiles is None else n_tiles
    chunk = _tile(f, 256, V7X_LANES)
    resident = lambda *s: pl.BlockSpec(s, lambda i: (0,) * len(s), pipeline_mode=_RESIDENT)
    in_specs = [pl.BlockSpec((tm, d), lambda i: (i + tile0, 0)),
                resident(1, d),
                pl.BlockSpec((None, None, 6, d), lambda i: (layer, rows.mod_index(i + tile0), 0, 0)),
                pl.BlockSpec((None, d, 2 * f), lambda i: (layer, 0, 0), pipeline_mode=_RESIDENT),
                pl.BlockSpec((None, f, d), lambda i: (layer, 0, 0), pipeline_mode=_RESIDENT)]
    args = [x, g.reshape(1, d), mods_all, w_gu_all, w_down_all]
    if mixer is not None:
        a, wo_all, wo_index, bo = mixer
        k = a.shape[1]
        in_specs += [pl.BlockSpec((tm, k), lambda i: (i + tile0, 0)),
                     pl.BlockSpec((None, k, d), lambda i: (wo_index, 0, 0), pipeline_mode=_RESIDENT),
                     resident(1, d)]
        args += [a, wo_all, bo.reshape(1, d)]
    if final_g is not None:
        in_specs.append(resident(1, d))
        args.append(final_g.reshape(1, d))
    return pl.pallas_call(
        functools.partial(_ffn_kernel, chunk=chunk, has_mixer=mixer is not None, has_final=final_g is not None),
        out_shape=jax.ShapeDtypeStruct((n_tiles * tm, d), F32),
        grid=(n_tiles,),
        in_specs=in_specs,
        out_specs=pl.BlockSpec((tm, d), lambda i: (i, 0)),
        compiler_params=_cparams("parallel"),
        name="ffn",
    )(*args)


def _dwconv(x, w, k):
    n = x.shape[0]
    half = k // 2
    e = V7X_SUBLANES
    assert half <= e and n >= 4 * e
    taps = [w[j:j + 1, :] for j in range(k)]
    shifts = [s for s in range(-half, half + 1) if s != 0]

    def conv(v, mask):
        acc = v * taps[half]
        for s in shifts:
            shifted = pltpu.roll(v, (-s) % v.shape[0], axis=0)
            acc = acc + (shifted if mask is None else jnp.where(mask(s), shifted, 0.0)) * taps[s + half]
        return acc

    row = lax.broadcasted_iota(jnp.int32, (2 * e, x.shape[1]), 0)
    top = conv(x[:2 * e], lambda s: row + s >= 0)[:e]
    bottom = conv(x[n - 2 * e:], lambda s: row + s < 2 * e)[e:]
    return jnp.concatenate([top, conv(x, None)[e:n - e], bottom], axis=0)


CONV_HALO = 16


def _dwconv_window(ref, cols, w, k, r0, rows, slab_ref):
    n = ref.shape[0]
    if rows == n:
        return _dwconv(ref[:, cols].astype(F32), w, k)
    half = k // 2
    h = CONV_HALO
    assert half <= h <= rows and n % rows == 0
    before = ref[pl.ds(pl.multiple_of(jnp.maximum(r0 - h, 0), h), h), cols].astype(F32)
    after = ref[pl.ds(pl.multiple_of(jnp.minimum(r0 + rows, n - h), h), h), cols].astype(F32)
    slab_ref[0:h, :] = jnp.where(r0 > 0, before, 0.0)
    slab_ref[h:h + rows, :] = ref[pl.ds(r0, rows), cols].astype(F32)
    slab_ref[h + rows:, :] = jnp.where(r0 + rows < n, after, 0.0)
    acc = None
    for s in range(-half, half + 1):
        term = slab_ref[h + s:h + s + rows, :] * w[s + half:s + half + 1, :]
        acc = term if acc is None else acc + term
    return acc


TRI_BASE = 8
GDN_PHASE1_CHAINS = 32
GDN_HEADS_PER_STEP = 4
GDN_ROWS_PER_STEP = 4096


def _unit_tri_inverses_minus_eye(mats, ri, ci):
    c = mats[0].shape[0]

    def same_block(s):
        sh = int(math.log2(s))
        return (ri >> sh) == (ci >> sh)

    ps = [jnp.where(same_block(TRI_BASE), -a, 0.0) for a in mats]
    es = ps
    n_lvl = int(math.log2(TRI_BASE))
    for lvl in range(n_lvl):
        es = [e + _dot(p, e) for p, e in zip(ps, es)]
        if lvl < n_lvl - 1:
            ps = [_dot(p, p) for p in ps]
    s = TRI_BASE
    while s < c:
        mask = jnp.logical_and(same_block(2 * s), jnp.logical_not(same_block(s)))
        offs = [jnp.where(mask, a, 0.0) for a in mats]
        ys = [off + _dot(off, e) for off, e in zip(offs, es)]
        es = [e - (y + _dot(e, y)) for e, y in zip(es, ys)]
        s *= 2
    return es


def _chunk_cumsum(x, pos, reverse):
    n = x.shape[0]
    s = 1
    while s < CHUNK:
        if reverse:
            x = x + jnp.where(pos + s < CHUNK, pltpu.roll(x, n - s, axis=0), 0.0)
        else:
            x = x + jnp.where(pos >= s, pltpu.roll(x, s, axis=0), 0.0)
        s *= 2
    return x


def _gdn_kernel(*refs, n_heads, conv_k, lockstep, n_kept):
    (alog_ref, dtb_ref, q_ref, k_ref, v_ref, gt_ref, ab_ref, cq_ref, ck_ref, cv_ref,
     onorm_ref, s0_ref) = refs[:12]
    o_ref, sfin_ref, gates_s, conv_s, w_s, u_s, qd_s, ak_s, gl_s, o_s, st_s = refs[12 + n_kept:]
    n_chains, seq, dk = w_s.shape
    hb = n_chains // 2
    head0 = pl.program_id(1) * hb
    c = CHUNK
    n_chunks = seq // c
    chains = [(hh, d) for hh in range(hb) for d in range(2)]

    @pl.when(head0 == 0)
    def _():
        ab = ab_ref[...]
        pos = jnp.bitwise_and(lax.broadcasted_iota(jnp.int32, ab.shape, 0), c - 1)
        g = -jnp.exp(alog_ref[...]) * jax.nn.softplus(ab + dtb_ref[...])
        g_fwd = _chunk_cumsum(g, pos, reverse=False)
        g_rev = _chunk_cumsum(g, pos, reverse=True)
        gates_s[0] = g_fwd
        gates_s[1] = g_rev - g
        gates_s[2] = g_rev
        gates_s[3] = g_fwd - g
        gates_s[4] = jax.nn.sigmoid(ab)

    lane = lax.broadcasted_iota(jnp.int32, (1, V7X_LANES), 1)

    rows = lockstep * c

    def column(i, idx, win):
        one_hot = (lane == idx).astype(F32)
        col = jnp.sum(gates_s[i, win, :] * one_hot, axis=1, keepdims=True)
        return jnp.broadcast_to(col, (rows, V7X_LANES))

    ri = lax.broadcasted_iota(jnp.int32, (c, c), 0)
    ci = lax.broadcasted_iota(jnp.int32, (c, c), 1)
    incl = (ri >= ci, ri <= ci)
    strict = (ri > ci, ri < ci)

    def phase1(it, carry):
        r0 = pl.multiple_of(it * rows, rows)
        win = pl.ds(r0, rows)
        prepared = []
        for hh in range(hb):
            cols = slice(hh * dk, (hh + 1) * dk)
            q = _silu(_dwconv_window(q_ref, cols, cq_ref[:, cols], conv_k, r0, rows, conv_s.at[3 * hh]))
            k = _silu(_dwconv_window(k_ref, cols, ck_ref[:, cols], conv_k, r0, rows, conv_s.at[3 * hh + 1]))
            v = _silu(_dwconv_window(v_ref, cols, cv_ref[:, cols], conv_k, r0, rows, conv_s.at[3 * hh + 2]))
            q = q * lax.rsqrt(jnp.sum(q * q, axis=-1, keepdims=True) + EPS) * (dk ** -0.5)
            k = k * lax.rsqrt(jnp.sum(k * k, axis=-1, keepdims=True) + EPS)
            for d in range(2):
                head = head0 + hh
                g_cum = column(2 * d, d * n_heads + head, win)
                g_tail = column(2 * d + 1, d * n_heads + head, win)
                beta = column(4, 2 * n_heads + d * n_heads + head, win)
                e_cum = jnp.exp(g_cum)
                kb = k * beta
                prepared.append((q, k, g_cum, kb, kb * e_cum, v * beta, k * jnp.exp(g_tail)))
                qd_s[2 * hh + d, win, :] = (q * e_cum).astype(qd_s.dtype)
                g_tot = jnp.exp(g_cum + g_tail)
                for gi in range(lockstep):
                    gl_s[2 * hh + d, pl.ds(it * lockstep + gi, 1), :] = g_tot[gi * c:gi * c + 1, :]
        items = []
        for gi in range(lockstep):
            part = slice(gi * c, (gi + 1) * c)
            for ch, (_, d) in enumerate(chains):
                sl = pl.ds(pl.multiple_of(r0 + gi * c, c), c)
                items.append((it * lockstep + gi, sl, ch, d) + tuple(a[part] for a in prepared[ch]))
        decays, kqs = [], []
        for ic, sl, ch, d, qc, kc, gc, kb, w0, u0, ktl in items:
            diff = gc[:, :c] - gc.T[:c, :]
            decays.append(jnp.where(incl[d], jnp.exp(jnp.where(incl[d], diff, 0.0)), 0.0))
            kqs.append(_dot_nt(jnp.concatenate([kb, qc], axis=0), kc))
        a_kks = [jnp.where(strict[item[3]], kq[:c] * decay, 0.0)
                 for item, kq, decay in zip(items, kqs, decays)]
        es = _unit_tri_inverses_minus_eye(a_kks, ri, ci)
        rhss = [jnp.concatenate([item[8], item[9]], axis=1) for item in items]
        wus = [rhs + _dot(e, rhs) for e, rhs in zip(es, rhss)]
        for item, kq, decay, wu in zip(items, kqs, decays, wus):
            ic, sl, ch = item[:3]
            w_s[ch, sl, :] = wu[:, :dk].astype(w_s.dtype)
            u_s[ch, sl, :] = wu[:, dk:]
            ak_s[ch, ic] = jnp.concatenate([kq[c:] * decay, item[10].T], axis=0).astype(ak_s.dtype)
        return carry

    lax.fori_loop(0, n_chunks // lockstep, phase1, 0)

    for hh, d in chains:
        st_s[2 * hh + d] = s0_ref[d, hh]

    def phase2(i, carry):
        n = 2 * hb
        ics = [n_chunks - 1 - i if d else i for _, d in chains]
        r0s = [pl.multiple_of(ic * c, c) for ic in ics]
        sls = [pl.ds(r0, c) for r0 in r0s]
        ss = [st_s[ch] for ch in range(n)]
        wqs = [_dot(jnp.concatenate([w_s[ch, sls[ch], :], qd_s[ch, sls[ch], :]], axis=0), ss[ch]) for ch in range(n)]
        v_news = [u_s[ch, sls[ch], :] - wqs[ch][:c] for ch in range(n)]
        outs = [_dot(ak_s[ch, ics[ch]], v_news[ch]) for ch in range(n)]
        for ch in range(n):
            o_s[ch, sls[ch], :] = wqs[ch][c:] + outs[ch][:c]
            st_s[ch] = ss[ch] * gl_s[ch, pl.ds(ics[ch], 1), :] + outs[ch][c:]
        return carry

    lax.fori_loop(0, n_chunks, phase2, 0, unroll=2)

    for hh, d in chains:
        sfin_ref[d, hh] = st_s[2 * hh + d]
    for hh in range(hb):
        cols = slice(hh * dk, (hh + 1) * dk)
        o = o_s[2 * hh] + o_s[2 * hh + 1]
        o = o * lax.rsqrt(jnp.mean(o * o, axis=-1, keepdims=True) + EPS)
        o_ref[:, cols] = (o * onorm_ref[...] * _silu(gt_ref[:, cols].astype(F32))).astype(o_ref.dtype)


def _gdn_core(proj, ab, conv_w, a_log, dt_bias, onorm, s0, dst, states_dst, *, t_total, row0, n_seq, seq,
              n_heads, dk, dv, s0_spec, state_slot):
    assert dk == dv == V7X_LANES and row0 % seq == 0
    conv_k = conv_w.shape[0]
    r0 = row0 // seq
    n_chunks = seq // CHUNK
    hb = _tile(n_heads, min(GDN_HEADS_PER_STEP, max(1, GDN_ROWS_PER_STEP // seq)), 1)
    lockstep = _tile(n_chunks, max(1, GDN_PHASE1_CHAINS // (2 * hb)), 1)
    gate_pad = lambda p: jnp.pad(p.reshape(1, 2 * n_heads), ((0, 0), (0, V7X_LANES - 2 * n_heads)))
    lane_vec = pl.BlockSpec((1, V7X_LANES), lambda b, h: (0, 0))
    if s0 is None:
        s0 = jnp.zeros((2, hb, dk, dv), F32)
        s0_in = pl.BlockSpec((2, hb, dk, dv), lambda b, h: (0, 0, 0, 0))
    else:
        s0_in = s0_spec(hb)
    nb = n_heads // hb
    col = lambda sec: pl.BlockSpec((seq, hb * dk), lambda b, h: (b + r0, sec * nb + h))
    cw = lambda sec: pl.BlockSpec((conv_k, hb * dk), lambda b, h: (0, sec * nb + h))
    f32 = lambda *s: pltpu.VMEM(s, F32)
    in_specs = [lane_vec, lane_vec, col(0), col(1), col(2), col(3),
                pl.BlockSpec((seq, V7X_LANES), lambda b, h: (b + r0, 0)),
                cw(0), cw(1), cw(2),
                pl.BlockSpec((1, dv), lambda b, h: (0, 0)),
                s0_in]
    args = [gate_pad(a_log), gate_pad(dt_bias), proj, proj, proj, proj, ab, conv_w, conv_w, conv_w,
            onorm.reshape(1, dv), s0]
    aliases = {}
    for out_idx, kept in enumerate((dst, states_dst)):
        if kept is not None:
            in_specs.append(pl.BlockSpec(memory_space=pl.ANY))
            args.append(kept)
            aliases[len(args) - 1] = out_idx
    nc = 2 * hb
    slot, n_slots = state_slot
    return pl.pallas_call(
        functools.partial(_gdn_kernel, n_heads=n_heads, conv_k=conv_k, lockstep=lockstep,
                          n_kept=len(aliases)),
        out_shape=(jax.ShapeDtypeStruct((t_total, n_heads * dv), BF16),
                   jax.ShapeDtypeStruct((n_seq, n_slots, 2, n_heads, dk, dv), F32)),
        grid=(n_seq, nb),
        in_specs=in_specs,
        out_specs=(pl.BlockSpec((seq, hb * dv), lambda b, h: (b + r0, h)),
                   pl.BlockSpec((None, None, 2, hb, dk, dv), lambda b, h: (b, slot, 0, h, 0, 0))),
        scratch_shapes=[f32(5, seq, V7X_LANES),
                        f32(3 * hb, lockstep * CHUNK + 2 * CONV_HALO, dk),
                        pltpu.VMEM((nc, seq, dk), BF16), f32(nc, seq, dv), pltpu.VMEM((nc, seq, dk), BF16),
                        pltpu.VMEM((nc, n_chunks, CHUNK + dk, CHUNK), BF16),
                        f32(nc, max(n_chunks, V7X_SUBLANES), V7X_LANES),
                        f32(nc, seq, dv), f32(nc, dk, dv)],
        input_output_aliases=aliases,
        compiler_params=_cparams("parallel", "arbitrary"),
        name="gdn_core",
    )(*args)


def _folded_odd_dft(seq):
    half = seq // 2
    k = np.arange(half, dtype=np.int64)[:, None]
    parts = []
    for parity in (0, 1):
        m = 2 * np.arange(half, dtype=np.int64)[None, :] + parity
        ang = (((2 * k + 1) * m) % (4 * seq)) * (math.pi / (2 * seq))
        parts.append(np.concatenate([np.cos(ang), np.sin(ang)], axis=0))
    return parts


def _bf16_head_tail(m):
    m32 = m.astype(np.float32)
    head = m32.astype(jnp.bfloat16)
    tail = (m32 - head.astype(np.float32)).astype(jnp.bfloat16)
    return jnp.asarray(head), jnp.asarray(tail)


def _filter_kernel(feat_ref, w1_ref, b1_ref, w2_ref, b2_ref, fr_ref, w3f_ref, w3b_ref, dl_ref,
                   fe_hi_ref, fe_lo_ref, fo_hi_ref, fo_lo_ref, hre_ref, him_ref, hid_s):
    seq = feat_ref.shape[0]
    half = seq // 2
    n_hid = w2_ref.shape[0]

    @pl.when(jnp.logical_and(pl.program_id(0) == 0, pl.program_id(1) == 0))
    def _():
        fr = fr_ref[...]
        hid1 = jnp.sin(fr * (_dot(feat_ref[...], w1_ref[...]) + b1_ref[...]))
        hid_s[...] = jnp.zeros_like(hid_s)
        hid_s[:, :n_hid] = jnp.sin(fr * (_dot(hid1, w2_ref[...]) + b2_ref[...]))

    def taps(parity):
        lags = pl.ds(parity, half, stride=2)
        hid = hid_s[lags, :][:, :n_hid]
        window = jnp.exp(-feat_ref[lags, :][:, 0:1] * dl_ref[...])
        hf = _dot(hid, w3f_ref[...]) * window
        hb = _dot(hid, w3b_ref[...]) * window
        if parity == 0:
            hb = jnp.where(lax.broadcasted_iota(jnp.int32, hb.shape, 0) == 0, 0.0, hb)
        return hf + hb, hb - hf

    def dft(m_hi, m_lo, val):
        v_hi = val.astype(BF16)
        v_lo = (val - v_hi.astype(F32)).astype(BF16)
        return (jnp.dot(m_hi, v_hi, preferred_element_type=F32)
                + (jnp.dot(m_hi, v_lo, preferred_element_type=F32)
                   + jnp.dot(m_lo, v_hi, preferred_element_type=F32)))

    (sum_e, dif_e), (sum_o, dif_o) = taps(0), taps(1)
    cos_rows, sin_rows = slice(0, half), slice(half, seq)
    re_e = dft(fe_hi_ref[cos_rows, :], fe_lo_ref[cos_rows, :], sum_e)
    re_o = dft(fo_hi_ref[cos_rows, :], fo_lo_ref[cos_rows, :], sum_o)
    im_e = dft(fe_hi_ref[sin_rows, :], fe_lo_ref[sin_rows, :], dif_e)
    im_o = dft(fo_hi_ref[sin_rows, :], fo_lo_ref[sin_rows, :], dif_o)
    hre_ref[0:half, :] = re_e + re_o
    hre_ref[half:, :] = re_e - re_o
    him_ref[0:half, :] = im_e + im_o
    him_ref[half:, :] = im_o - im_e


def _hyena_filters(seq, d, w1, b1, w2, b2, w3, freq, dft_parts):
    emb, hid = w1.shape
    bands = (emb - 1) // 2
    t = jnp.linspace(0.0, 1.0, seq, dtype=F32)[:, None]
    wpos = (2.0 * math.pi / seq) * jnp.arange(seq, dtype=F32)[:, None]
    fb = jnp.linspace(1e-4, bands - 1, bands, dtype=F32)[None, :]
    feat = jnp.concatenate([t, jnp.cos(fb * wpos), -jnp.sin(fb * wpos)], axis=-1)
    feat = jnp.pad(feat, ((0, 0), (0, V7X_LANES - emb)))
    w1p = jnp.pad(w1, ((0, V7X_LANES - emb), (0, 0)))
    max_decay = math.log(HY_TARGET) / HY_SHORT_PCT
    min_decay = math.log(HY_TARGET) / HY_LONG_PCT
    deltas = jnp.abs(jnp.linspace(min_decay, max_decay, d, dtype=F32))[None, :]
    tc = _tile(d, 256, V7X_LANES)
    nt = d // tc
    full = lambda r, c: pl.BlockSpec((r, c), lambda n, j: (0, 0))
    out_spec = pl.BlockSpec((None, seq, tc), lambda n, j: (n, 0, j))
    return pl.pallas_call(
        _filter_kernel,
        out_shape=(jax.ShapeDtypeStruct((HY_ORDER, seq, d), F32),) * 2,
        grid=(HY_ORDER, nt),
        in_specs=[full(seq, V7X_LANES), full(V7X_LANES, hid), full(1, hid), full(hid, hid), full(1, hid),
                  full(1, hid),
                  pl.BlockSpec((hid, tc), lambda n, j: (0, (2 * n) * nt + j)),
                  pl.BlockSpec((hid, tc), lambda n, j: (0, (2 * n + 1) * nt + j)),
                  pl.BlockSpec((1, tc), lambda n, j: (0, j))] + [full(seq, seq // 2)] * 4,
        out_specs=(out_spec, out_spec),
        scratch_shapes=[pltpu.VMEM((seq, V7X_LANES), F32)],
        compiler_params=_cparams("arbitrary", "arbitrary"),
        name="hyena_filter",
    )(feat, w1p, b1.reshape(1, hid), w2, b2.reshape(1, hid), freq.reshape(1, hid), w3, w3, deltas,
      *dft_parts)


HYENA_ROWS_PER_STEP = 2048


def _hyconv_kernel(*refs, conv_z, conv_k, has_dst):
    (z_ref, x_ref, hre_ref, him_ref, skip_ref, cz_ref, cx_ref,
     fwd_e_ref, fwd_o_ref, inv_e_ref, inv_o_ref) = refs[:11]
    o_ref = refs[11 + has_dst]
    zs_s, xs_s, os_s = refs[-3:]
    seq = hre_ref.shape[0]
    half = seq // 2
    n_sub = z_ref.shape[0] // seq
    even, odd = pl.ds(0, half, stride=2), pl.ds(1, half, stride=2)
    n_lane_tiles = zs_s.shape[1]

    def stage(dst, s, val):
        for l in range(n_lane_tiles):
            dst[s, l] = val[:, l * V7X_LANES:(l + 1) * V7X_LANES]

    def rows_of(src, s, rows):
        return jnp.concatenate([src[s, l, rows, :] for l in range(n_lane_tiles)], axis=1)

    for s in range(n_sub):
        z = z_ref[s * seq:(s + 1) * seq, :].astype(F32)
        stage(zs_s, s, _dwconv(z, cz_ref[...], conv_k) if conv_z else z)
        stage(xs_s, s, _dwconv(x_ref[s * seq:(s + 1) * seq, :].astype(F32), cx_ref[...], conv_k))
    z_parts = [(rows_of(zs_s, s, even), rows_of(zs_s, s, odd)) for s in range(n_sub)]
    fwds = [(jnp.dot(fwd_e_ref[...], ze.astype(BF16), preferred_element_type=F32),
             jnp.dot(fwd_o_ref[...], zo.astype(BF16), preferred_element_type=F32)) for ze, zo in z_parts]
    hre_a, hre_b = hre_ref[0:half, :], hre_ref[half:, :]
    him_a, him_b = him_ref[0:half, :], him_ref[half:, :]
    skip = skip_ref[...]
    for s in range(n_sub):
        (fe, fo), (ze, zo) = fwds[s], z_parts[s]
        pe, qe, po, qo = fe[:half], fe[half:], fo[:half], fo[half:]
        pa, pb, qa, qb = pe + po, pe - po, qe + qo, qo - qe
        yre_a, yim_a = pa * hre_a + qa * him_a, pa * him_a - qa * hre_a
        yre_b, yim_b = pb * hre_b + qb * him_b, pb * him_b - qb * hre_b
        y_e = jnp.dot(inv_e_ref[...], jnp.concatenate([yre_a + yre_b, yim_a - yim_b], axis=0).astype(BF16),
                      preferred_element_type=F32)
        y_o = jnp.dot(inv_o_ref[...], jnp.concatenate([yre_a - yre_b, yim_a + yim_b], axis=0).astype(BF16),
                      preferred_element_type=F32)
        out_e = rows_of(xs_s, s, even) * (y_e + skip * ze)
        out_o = rows_of(xs_s, s, odd) * (y_o + skip * zo)
        for l in range(n_lane_tiles):
            lanes = slice(l * V7X_LANES, (l + 1) * V7X_LANES)
            os_s[s, l, even, :] = out_e[:, lanes]
            os_s[s, l, odd, :] = out_o[:, lanes]
        o_ref[s * seq:(s + 1) * seq, :] = rows_of(os_s, s, slice(None)).astype(o_ref.dtype)


def _hyconv(z, z_col0, conv_z, proj, x_col0, hre, him, order, skip, conv_w, dft, dst,
            *, row0, n_seq, seq, d, out_rows, out_row0, out_dtype):
    conv_k = conv_w.shape[0]
    n_sub = max(s for s in range(1, max(1, HYENA_ROWS_PER_STEP // seq) + 1)
                if n_seq % s == 0 and row0 % (s * seq) == 0 and out_row0 % (s * seq) == 0)
    blk = n_sub * seq
    tc = _tile(d, 512 if blk <= 1024 else 256, V7X_LANES)
    nt = d // tc
    r0 = row0 // blk
    zr0 = r0 if conv_z else 0
    zc, xc = z_col0 // tc, x_col0 // tc
    in_specs = [pl.BlockSpec((blk, tc), lambda j, b: (b + zr0, zc + j)),
                pl.BlockSpec((blk, tc), lambda j, b: (b + r0, xc + j)),
                pl.BlockSpec((None, seq, tc), lambda j, b: (order, 0, j)),
                pl.BlockSpec((None, seq, tc), lambda j, b: (order, 0, j)),
                pl.BlockSpec((1, tc), lambda j, b: (0, j)),
                pl.BlockSpec((conv_k, tc), lambda j, b: (0, zc + j)),
                pl.BlockSpec((conv_k, tc), lambda j, b: (0, xc + j)),
                pl.BlockSpec((seq, seq // 2), lambda j, b: (0, 0), pipeline_mode=_RESIDENT),
                pl.BlockSpec((seq, seq // 2), lambda j, b: (0, 0), pipeline_mode=_RESIDENT),
                pl.BlockSpec((seq // 2, seq), lambda j, b: (0, 0), pipeline_mode=_RESIDENT),
                pl.BlockSpec((seq // 2, seq), lambda j, b: (0, 0), pipeline_mode=_RESIDENT)]
    args = [z, proj, hre, him, skip.reshape(1, d), conv_w, conv_w, *dft]
    aliases = {}
    if dst is not None:
        in_specs.append(pl.BlockSpec(memory_space=pl.ANY))
        args.append(dst)
        aliases = {len(args) - 1: 0}
    out_r0 = out_row0 // blk
    return pl.pallas_call(
        functools.partial(_hyconv_kernel, conv_z=conv_z, conv_k=conv_k, has_dst=dst is not None),
        out_shape=jax.ShapeDtypeStruct((out_rows, d), out_dtype),
        grid=(nt, n_seq // n_sub),
        in_specs=in_specs,
        out_specs=pl.BlockSpec((blk, tc), lambda j, b: (b + out_r0, j)),
        scratch_shapes=[pltpu.VMEM((n_sub, tc // V7X_LANES, seq, V7X_LANES), F32)] * 3,
        input_output_aliases=aliases,
        compiler_params=_cparams("parallel", "parallel"),
        name="hyena_conv",
    )(*args)


def _grid_pos_emb(n_tokens, d):
    rows = n_tokens // GRID_W
    r, col = jnp.meshgrid(jnp.arange(rows), jnp.arange(GRID_W), indexing='ij')
    quarter = d // 4
    omega = 1.0 / (POS_BASE ** (jnp.arange(quarter, dtype=F32) / quarter))

    def emb1d(p):
        a = p.reshape(-1, 1).astype(F32) * omega[None, :]
        return jnp.concatenate([jnp.sin(a), jnp.cos(a)], axis=-1)

    return jnp.concatenate([emb1d(r), emb1d(col)], axis=-1)


def kernel(x_prompt, x_sample, state_delta, c, c_ctx, ada_w, ada_b, norm1_g, norm2_g, gdn_w_in, gdn_conv, gdn_a_log, gdn_dt_bias, gdn_onorm, gdn_w_out, hy_w_in, hy_b_in, hy_conv, hy_f_w1, hy_f_b1, hy_f_w2, hy_f_b2, hy_f_w3, hy_freq, hy_skip, hy_w_out, hy_b_out, ffn_w_gu, ffn_w_down, final_g):
    bc, lc, d = x_prompt.shape
    bl, ll, _ = x_sample.shape
    depth = ada_w.shape[0]
    n_heads, dk, dv = state_delta.shape[3:]
    tc_rows, tl_rows = bc * lc, bl * ll
    t = tc_rows + tl_rows
    assert tc_rows % ll == 0 and ll % lc == 0
    rows = _Rows(tc_rows, ll, t, _tile(math.gcd(tc_rows, ll), 512, V7X_SUBLANES))

    bm = 1 + bl
    bm_pad = -(-bm // V7X_SUBLANES) * V7X_SUBLANES
    cvec = jnp.concatenate([c_ctx[None, :], c, jnp.zeros((bm_pad - bm, d), F32)], axis=0)
    mods_all = _ada(cvec, ada_w, ada_b).reshape(depth, bm_pad, 6, d)

    x = _embed(x_prompt.reshape(tc_rows, d), x_sample.reshape(tl_rows, d), _grid_pos_emb(ll, d), rows)

    n_hy = hy_w_in.shape[0]
    dft = {}
    for seq in (lc, ll):
        half = seq // 2
        fwd_e, fwd_o = _folded_odd_dft(seq)
        (fe_hi, fe_lo), (fo_hi, fo_lo) = _bf16_head_tail(fwd_e), _bf16_head_tail(fwd_o)
        inv_e, _ = _bf16_head_tail(np.concatenate([fwd_e[:half].T, -fwd_e[half:].T], axis=1) / seq)
        inv_o, _ = _bf16_head_tail(np.concatenate([fwd_o[:half].T, -fwd_o[half:].T], axis=1) / seq)
        spectra = [_hyena_filters(seq, d, hy_f_w1[j], hy_f_b1[j], hy_f_w2[j], hy_f_b2[j], hy_f_w3[j],
                                  hy_freq[j], (fe_hi, fe_lo, fo_hi, fo_lo)) for j in range(n_hy)]
        dft[seq] = ((fe_hi, fo_hi, inv_e, inv_o), spectra)

    zero_b = jnp.zeros((d,), F32)
    mixed = jnp.zeros((t, n_heads * dv), BF16)
    shared = n_heads * dv == d
    mixed_hy = mixed if shared else jnp.zeros((t, d), BF16)
    new_state_delta = jnp.zeros((bc, gdn_w_in.shape[0], 2, n_heads, dk, dv), F32)
    n_mixers = 2
    n_gdn = gdn_w_in.shape[0]
    gdn_w_in_b = jnp.pad(gdn_w_in, ((0, 0), (0, 0), (0, V7X_LANES - 4 * n_heads))).astype(BF16)
    gdn_w_out_b, hy_w_in_b, hy_w_out_b = (w.astype(BF16) for w in (gdn_w_out, hy_w_in, hy_w_out))
    ffn_w_gu_b, ffn_w_down_b = ffn_w_gu.astype(BF16), ffn_w_down.astype(BF16)
    for layer in range(depth):
        j = layer // n_mixers
        if layer % n_mixers == 0:
            proj, ab = _in_proj(x, norm1_g[layer], mods_all, layer, gdn_w_in_b, j,
                                jnp.zeros((gdn_w_in_b.shape[2],), F32), rows, n_side=V7X_LANES)
            gdn = dict(t_total=t, n_heads=n_heads, dk=dk, dv=dv)
            weights = (gdn_conv[j], gdn_a_log[j], gdn_dt_bias[j], gdn_onorm[j])
            mixed, new_state_delta = _gdn_core(proj, ab, *weights, None, mixed, new_state_delta, row0=0, n_seq=bc,
                                               seq=lc, s0_spec=None, state_slot=(j, n_gdn), **gdn)
            s0_spec = lambda hb, j=j: pl.BlockSpec((None, None, 2, hb, dk, dv), lambda b, h: (b, j, 0, h, 0, 0))
            mixed, _ = _gdn_core(proj, ab, *weights, state_delta, mixed, None, row0=tc_rows, n_seq=bl, seq=ll,
                                 s0_spec=s0_spec, state_slot=(0, 1), **gdn)
            mixer = (mixed, gdn_w_out_b, j, zero_b)
            mixed_hy = mixed if shared else mixed_hy
        else:
            proj = _in_proj(x, norm1_g[layer], mods_all, layer, hy_w_in_b, j, hy_b_in[j], rows)
            for row0, n_seq, seq in ((0, bc, lc), (tc_rows, bl, ll)):
                dft_mats, spectra = dft[seq]
                hre, him = spectra[j]
                z = proj
                for n in range(HY_ORDER):
                    last = n == HY_ORDER - 1
                    z = _hyconv(z, 0, n == 0, proj, (n + 1) * d, hre, him, n, hy_skip[j, n], hy_conv[j],
                                dft_mats, mixed_hy if last else None, row0=row0, n_seq=n_seq, seq=seq, d=d,
                                out_rows=t if last else n_seq * seq, out_row0=row0 if last else 0,
                                out_dtype=BF16 if last else F32)
                mixed_hy = z
            mixer = (mixed_hy, hy_w_out_b, j, hy_b_out[j])
            mixed = mixed_hy if shared else mixed
        ffn = functools.partial(_ffn, x, norm2_g[layer], mods_all, ffn_w_gu_b, ffn_w_down_b, layer, rows,
                                mixer=mixer)
        if layer < depth - 1:
            x = ffn()
        else:
            y_prompt = ffn(final_g=final_g, tile0=0, n_tiles=rows.n_ctx_tiles)
            y_sample = ffn(final_g=final_g, tile0=rows.n_ctx_tiles, n_tiles=rows.n_tiles - rows.n_ctx_tiles)
    return (y_prompt.reshape(bc, lc, d), y_sample.reshape(bl, ll, d), new_state_delta)
```

```python
import functools
import math

import jax
import jax.numpy as jnp
import numpy as np
from jax import lax
from jax.experimental import pallas as pl
from jax.experimental.pallas import tpu as pltpu

GRID_W = 64
CHUNK = 64
HY_ORDER = 2
HY_TARGET = 1e-2
HY_SHORT_PCT = 0.3
HY_LONG_PCT = 1.5
POS_BASE = 10000.0
EPS = 1e-6

V7X_LANES = 128
V7X_SUBLANES = 8
V7X_VMEM_LIMIT_BYTES = 48 * 1024 * 1024

BF16 = jnp.bfloat16
F32 = jnp.float32
HIGHEST = lax.Precision.HIGHEST


def _cparams(*sem):
    return pltpu.CompilerParams(dimension_semantics=sem, vmem_limit_bytes=V7X_VMEM_LIMIT_BYTES)


def _tile(n, target, align):
    if n <= target:
        return n
    best = None
    for t in range(align, target + 1, align):
        if n % t == 0:
            best = t
    assert best is not None, (n, target, align)
    return best


def _dot(a, b):
    return jnp.dot(a.astype(BF16), b.astype(BF16), preferred_element_type=F32)


def _dot_nt(a, b):
    return lax.dot_general(a.astype(BF16), b.astype(BF16), (((1,), (1,)), ((), ())),
                           preferred_element_type=F32)


def _dot_hi(a, b):
    return jnp.dot(a, b, preferred_element_type=F32, precision=HIGHEST)


def _silu(x):
    return x * jax.nn.sigmoid(x)


def _norm_mod(x, g, shift, scale):
    ms = jnp.mean(x * x, axis=-1, keepdims=True)
    return (x * lax.rsqrt(ms + EPS) * g) * (1.0 + scale) + shift


class _Rows:
    def __init__(self, tc, ll, t, tm):
        assert tc % tm == 0 and ll % tm == 0 and t % tm == 0
        self.n_ctx_tiles = tc // tm
        self.tiles_per_lat = ll // tm
        self.n_tiles = t // tm
        self.tm = tm

    def mod_index(self, i):
        lat = 1 + (i - self.n_ctx_tiles) // self.tiles_per_lat
        return jnp.where(i < self.n_ctx_tiles, 0, lat)


def _ada_kernel(c_ref, w_ref, b_ref, o_ref):
    o_ref[...] = _dot(_silu(c_ref[...]), w_ref[...]) + b_ref[...]


def _ada(cvec, ada_w, ada_b):
    depth, d, n = ada_w.shape
    bm = cvec.shape[0]
    tn = _tile(n, 1536, V7X_LANES)
    return pl.pallas_call(
        _ada_kernel,
        out_shape=jax.ShapeDtypeStruct((depth, bm, n), F32),
        grid=(depth, n // tn),
        in_specs=[pl.BlockSpec((bm, d), lambda l, j: (0, 0)),
                  pl.BlockSpec((None, d, tn), lambda l, j: (l, 0, j)),
                  pl.BlockSpec((None, 1, tn), lambda l, j: (l, 0, j))],
        out_specs=pl.BlockSpec((None, bm, tn), lambda l, j: (l, 0, j)),
        compiler_params=_cparams("parallel", "parallel"),
        name="ada",
    )(cvec, ada_w, ada_b.reshape(depth, 1, n))


def _embed_kernel(xp_ref, xs_ref, pos_ref, o_ref, *, n_ctx_tiles):
    i = pl.program_id(0)

    @pl.when(i < n_ctx_tiles)
    def _():
        o_ref[...] = xp_ref[...]

    @pl.when(i >= n_ctx_tiles)
    def _():
        o_ref[...] = xs_ref[...] + pos_ref[...]


def _embed(xp, xs, pos, rows):
    t, d = xp.shape[0] + xs.shape[0], xp.shape[1]
    tm, nct = rows.tm, rows.n_ctx_tiles
    npos = pos.shape[0] // tm
    return pl.pallas_call(
        functools.partial(_embed_kernel, n_ctx_tiles=nct),
        out_shape=jax.ShapeDtypeStruct((t, d), F32),
        grid=(rows.n_tiles,),
        in_specs=[pl.BlockSpec((tm, d), lambda i: (jnp.minimum(i, nct - 1), 0)),
                  pl.BlockSpec((tm, d), lambda i: (jnp.maximum(i - nct, 0), 0)),
                  pl.BlockSpec((tm, d), lambda i: (jnp.maximum(i - nct, 0) % npos, 0))],
        out_specs=pl.BlockSpec((tm, d), lambda i: (i, 0)),
        compiler_params=_cparams("parallel"),
        name="embed",
    )(xp, xs, pos)


_RESIDENT = pl.Buffered(1)


def _in_kernel(x_ref, g_ref, mod_ref, w_ref, b_ref, o_ref, *side_ref, chunk):
    m = mod_ref[...]
    h = _norm_mod(x_ref[...], g_ref[...], m[0:1, :], m[1:2, :]).astype(BF16)
    n = o_ref.shape[1]
    for c0 in range(0, n, chunk):
        cols = slice(c0, c0 + chunk)
        y = jnp.dot(h, w_ref[:, cols], preferred_element_type=F32) + b_ref[:, cols]
        o_ref[:, cols] = y.astype(o_ref.dtype)
    if side_ref:
        side_ref[0][...] = jnp.dot(h, w_ref[:, n:], preferred_element_type=F32) + b_ref[:, n:]


def _in_proj(x, g, mods_all, layer, w_all, w_index, b, rows, n_side=0):
    t, d = x.shape
    n = w_all.shape[2] - n_side
    tm = rows.tm
    chunk = _tile(n, 512, V7X_LANES)
    out_shape = [jax.ShapeDtypeStruct((t, n), BF16)]
    out_specs = [pl.BlockSpec((tm, n), lambda i: (i, 0))]
    if n_side:
        out_shape.append(jax.ShapeDtypeStruct((t, n_side), F32))
        out_specs.append(pl.BlockSpec((tm, n_side), lambda i: (i, 0)))
    out = pl.pallas_call(
        functools.partial(_in_kernel, chunk=chunk),
        out_shape=out_shape,
        grid=(rows.n_tiles,),
        in_specs=[pl.BlockSpec((tm, d), lambda i: (i, 0)),
                  pl.BlockSpec((1, d), lambda i: (0, 0), pipeline_mode=_RESIDENT),
                  pl.BlockSpec((None, None, 6, d), lambda i: (layer, rows.mod_index(i), 0, 0)),
                  pl.BlockSpec((None, d, n + n_side), lambda i: (w_index, 0, 0), pipeline_mode=_RESIDENT),
                  pl.BlockSpec((1, n + n_side), lambda i: (0, 0), pipeline_mode=_RESIDENT)],
        out_specs=out_specs,
        compiler_params=_cparams("parallel"),
        name="in_proj",
    )(x, g.reshape(1, d), mods_all, w_all, b.reshape(1, n + n_side))
    return out if n_side else out[0]


def _ffn_kernel(x_ref, g_ref, mod_ref, wgu_ref, wd_ref, *rest, chunk, has_mixer, has_final):
    o_ref = rest[-1]
    x = x_ref[...]
    m = mod_ref[...]
    if has_mixer:
        a_ref, wo_ref, bo_ref = rest[:3]
        x = x + m[2:3, :] * (_dot(a_ref[...], wo_ref[...]) + bo_ref[...])
    h = _norm_mod(x, g_ref[...], m[3:4, :], m[4:5, :]).astype(BF16)
    f = wd_ref.shape[0]
    n_chunks = f // chunk

    def gate_up(k):
        gate = jnp.dot(h, wgu_ref[:, k * chunk:(k + 1) * chunk], preferred_element_type=F32)
        up = jnp.dot(h, wgu_ref[:, f + k * chunk:f + (k + 1) * chunk], preferred_element_type=F32)
        return gate, up

    y = None
    pending = gate_up(0)
    for k in range(n_chunks):
        following = gate_up(k + 1) if k + 1 < n_chunks else None
        act = (_silu(pending[0]) * pending[1]).astype(BF16)
        part = jnp.dot(act, wd_ref[k * chunk:(k + 1) * chunk, :], preferred_element_type=F32)
        y = part if y is None else y + part
        pending = following
    out = x + m[5:6, :] * y
    if has_final:
        out = out * lax.rsqrt(jnp.mean(out * out, axis=-1, keepdims=True) + EPS) * rest[-2][...]
    o_ref[...] = out


def _ffn(x, g, mods_all, w_gu_all, w_down_all, layer, rows, mixer=None, final_g=None, tile0=0, n_tiles=None):
    d = x.shape[1]
    f = w_down_all.shape[1]
    tm = rows.tm
    n_tiles = rows.n_tiles if n_tiles is None else n_tiles
    chunk = _tile(f, 256, V7X_LANES)
    resident = lambda *s: pl.BlockSpec(s, lambda i: (0,) * len(s), pipeline_mode=_RESIDENT)
    in_specs = [pl.BlockSpec((tm, d), lambda i: (i + tile0, 0)),
                resident(1, d),
                pl.BlockSpec((None, None, 6, d), lambda i: (layer, rows.mod_index(i + tile0), 0, 0)),
                pl.BlockSpec((None, d, 2 * f), lambda i: (layer, 0, 0), pipeline_mode=_RESIDENT),
                pl.BlockSpec((None, f, d), lambda i: (layer, 0, 0), pipeline_mode=_RESIDENT)]
    args = [x, g.reshape(1, d), mods_all, w_gu_all, w_down_all]
    if mixer is not None:
        a, wo_all, wo_index, bo = mixer
        k = a.shape[1]
        in_specs += [pl.BlockSpec((tm, k), lambda i: (i + tile0, 0)),
                     pl.BlockSpec((None, k, d), lambda i: (wo_index, 0, 0), pipeline_mode=_RESIDENT),
                     resident(1, d)]
        args += [a, wo_all, bo.reshape(1, d)]
    if final_g is not None:
        in_specs.append(resident(1, d))
        args.append(final_g.reshape(1, d))
    return pl.pallas_call(
        functools.partial(_ffn_kernel, chunk=chunk, has_mixer=mixer is not None, has_final=final_g is not None),
        out_shape=jax.ShapeDtypeStruct((n_tiles * tm, d), F32),
        grid=(n_tiles,),
        in_specs=in_specs,
        out_specs=pl.BlockSpec((tm, d), lambda i: (i, 0)),
        compiler_params=_cparams("parallel"),
        name="ffn",
    )(*args)


def _dwconv(x, w, k):
    n = x.shape[0]
    half = k // 2
    e = V7X_SUBLANES
    assert half <= e and n >= 4 * e
    taps = [w[j:j + 1, :] for j in range(k)]
    shifts = [s for s in range(-half, half + 1) if s != 0]

    def conv(v, mask):
        acc = v * taps[half]
        for s in shifts:
            shifted = pltpu.roll(v, (-s) % v.shape[0], axis=0)
            acc = acc + (shifted if mask is None else jnp.where(mask(s), shifted, 0.0)) * taps[s + half]
        return acc

    row = lax.broadcasted_iota(jnp.int32, (2 * e, x.shape[1]), 0)
    top = conv(x[:2 * e], lambda s: row + s >= 0)[:e]
    bottom = conv(x[n - 2 * e:], lambda s: row + s < 2 * e)[e:]
    return jnp.concatenate([top, conv(x, None)[e:n - e], bottom], axis=0)


CONV_HALO = 16


def _dwconv_window(ref, cols, w, k, r0, rows, slab_ref):
    n = ref.shape[0]
    if rows == n:
        return _dwconv(ref[:, cols].astype(F32), w, k)
    half = k // 2
    h = CONV_HALO
    assert half <= h <= rows and n % rows == 0
    before = ref[pl.ds(pl.multiple_of(jnp.maximum(r0 - h, 0), h), h), cols].astype(F32)
    after = ref[pl.ds(pl.multiple_of(jnp.minimum(r0 + rows, n - h), h), h), cols].astype(F32)
    slab_ref[0:h, :] = jnp.where(r0 > 0, before, 0.0)
    slab_ref[h:h + rows, :] = ref[pl.ds(r0, rows), cols].astype(F32)
    slab_ref[h + rows:, :] = jnp.where(r0 + rows < n, after, 0.0)
    acc = None
    for s in range(-half, half + 1):
        term = slab_ref[h + s:h + s + rows, :] * w[s + half:s + half + 1, :]
        acc = term if acc is None else acc + term
    return acc


TRI_BASE = 8
GDN_PHASE1_CHAINS = 32
GDN_HEADS_PER_STEP = 8
GDN_ROWS_PER_STEP = 4096


def _unit_tri_inverses_minus_eye(mats, ri, ci):
    c = mats[0].shape[0]

    def same_block(s):
        sh = int(math.log2(s))
        return (ri >> sh) == (ci >> sh)

    ps = [jnp.where(same_block(TRI_BASE), -a, 0.0) for a in mats]
    es = ps
    n_lvl = int(math.log2(TRI_BASE))
    for lvl in range(n_lvl):
        es = [e + _dot(p, e) for p, e in zip(ps, es)]
        if lvl < n_lvl - 1:
            ps = [_dot(p, p) for p in ps]
    s = TRI_BASE
    while s < c:
        mask = jnp.logical_and(same_block(2 * s), jnp.logical_not(same_block(s)))
        offs = [jnp.where(mask, a, 0.0) for a in mats]
        ys = [off + _dot(off, e) for off, e in zip(offs, es)]
        es = [e - (y + _dot(e, y)) for e, y in zip(es, ys)]
        s *= 2
    return es


def _chunk_cumsum(x, pos, reverse):
    n = x.shape[0]
    s = 1
    while s < CHUNK:
        if reverse:
            x = x + jnp.where(pos + s < CHUNK, pltpu.roll(x, n - s, axis=0), 0.0)
        else:
            x = x + jnp.where(pos >= s, pltpu.roll(x, s, axis=0), 0.0)
        s *= 2
    return x


def _gdn_kernel(*refs, n_heads, conv_k, lockstep, n_kept):
    (alog_ref, dtb_ref, q_ref, k_ref, v_ref, gt_ref, ab_ref, cq_ref, ck_ref, cv_ref,
     onorm_ref, s0_ref) = refs[:12]
    o_ref, sfin_ref, gates_s, conv_s, w_s, u_s, qd_s, ak_s, gl_s, o_s, st_s = refs[12 + n_kept:]
    n_chains, seq, dk = w_s.shape
    hb = n_chains // 2
    head0 = pl.program_id(1) * hb
    c = CHUNK
    n_chunks = seq // c
    chains = [(hh, d) for hh in range(hb) for d in range(2)]

    @pl.when(head0 == 0)
    def _():
        ab = ab_ref[...]
        pos = jnp.bitwise_and(lax.broadcasted_iota(jnp.int32, ab.shape, 0), c - 1)
        g = -jnp.exp(alog_ref[...]) * jax.nn.softplus(ab + dtb_ref[...])
        g_fwd = _chunk_cumsum(g, pos, reverse=False)
        g_rev = _chunk_cumsum(g, pos, reverse=True)
        gates_s[0] = g_fwd
        gates_s[1] = g_rev - g
        gates_s[2] = g_rev
        gates_s[3] = g_fwd - g
        gates_s[4] = jax.nn.sigmoid(ab)

    lane = lax.broadcasted_iota(jnp.int32, (1, V7X_LANES), 1)

    rows = lockstep * c

    def column(i, idx, win):
        one_hot = (lane == idx).astype(F32)
        col = jnp.sum(gates_s[i, win, :] * one_hot, axis=1, keepdims=True)
        return jnp.broadcast_to(col, (rows, V7X_LANES))

    ri = lax.broadcasted_iota(jnp.int32, (c, c), 0)
    ci = lax.broadcasted_iota(jnp.int32, (c, c), 1)
    incl = (ri >= ci, ri <= ci)
    strict = (ri > ci, ri < ci)

    def phase1(it, carry):
        r0 = pl.multiple_of(it * rows, rows)
        win = pl.ds(r0, rows)
        prepared = []
        for hh in range(hb):
            cols = slice(hh * dk, (hh + 1) * dk)
            q = _silu(_dwconv_window(q_ref, cols, cq_ref[:, cols], conv_k, r0, rows, conv_s.at[3 * hh]))
            k = _silu(_dwconv_window(k_ref, cols, ck_ref[:, cols], conv_k, r0, rows, conv_s.at[3 * hh + 1]))
            v = _silu(_dwconv_window(v_ref, cols, cv_ref[:, cols], conv_k, r0, rows, conv_s.at[3 * hh + 2]))
            q = q * lax.rsqrt(jnp.sum(q * q, axis=-1, keepdims=True) + EPS) * (dk ** -0.5)
            k = k * lax.rsqrt(jnp.sum(k * k, axis=-1, keepdims=True) + EPS)
            for d in range(2):
                head = head0 + hh
                g_cum = column(2 * d, d * n_heads + head, win)
                g_tail = column(2 * d + 1, d * n_heads + head, win)
                beta = column(4, 2 * n_heads + d * n_heads + head, win)
                e_cum = jnp.exp(g_cum)
                kb = k * beta
                prepared.append((q, k, g_cum, kb, kb * e_cum, v * beta, k * jnp.exp(g_tail)))
                qd_s[2 * hh + d, win, :] = (q * e_cum).astype(qd_s.dtype)
                g_tot = jnp.exp(g_cum + g_tail)
                for gi in range(lockstep):
                    gl_s[2 * hh + d, pl.ds(it * lockstep + gi, 1), :] = g_tot[gi * c:gi * c + 1, :]
        items = []
        for gi in range(lockstep):
            part = slice(gi * c, (gi + 1) * c)
            for ch, (_, d) in enumerate(chains):
                sl = pl.ds(pl.multiple_of(r0 + gi * c, c), c)
                items.append((it * lockstep + gi, sl, ch, d) + tuple(a[part] for a in prepared[ch]))
        decays, kqs = [], []
        for ic, sl, ch, d, qc, kc, gc, kb, w0, u0, ktl in items:
            diff = gc[:, :c] - gc.T[:c, :]
            decays.append(jnp.where(incl[d], jnp.exp(jnp.where(incl[d], diff, 0.0)), 0.0))
            kqs.append(_dot_nt(jnp.concatenate([kb, qc], axis=0), kc))
        a_kks = [jnp.where(strict[item[3]], kq[:c] * decay, 0.0)
                 for item, kq, decay in zip(items, kqs, decays)]
        es = _unit_tri_inverses_minus_eye(a_kks, ri, ci)
        rhss = [jnp.concatenate([item[8], item[9]], axis=1) for item in items]
        wus = [rhs + _dot(e, rhs) for e, rhs in zip(es, rhss)]
        for item, kq, decay, wu in zip(items, kqs, decays, wus):
            ic, sl, ch = item[:3]
            w_s[ch, sl, :] = wu[:, :dk].astype(w_s.dtype)
            u_s[ch, sl, :] = wu[:, dk:]
            ak_s[ch, ic] = jnp.concatenate([kq[c:] * decay, item[10].T], axis=0).astype(ak_s.dtype)
        return carry

    lax.fori_loop(0, n_chunks // lockstep, phase1, 0)

    for hh, d in chains:
        st_s[2 * hh + d] = s0_ref[d, hh]

    def phase2(i, carry):
        n = 2 * hb
        ics = [n_chunks - 1 - i if d else i for _, d in chains]
        r0s = [pl.multiple_of(ic * c, c) for ic in ics]
        sls = [pl.ds(r0, c) for r0 in r0s]
        ss = [st_s[ch] for ch in range(n)]
        wqs = [_dot(jnp.concatenate([w_s[ch, sls[ch], :], qd_s[ch, sls[ch], :]], axis=0), ss[ch]) for ch in range(n)]
        v_news = [u_s[ch, sls[ch], :] - wqs[ch][:c] for ch in range(n)]
        outs = [_dot(ak_s[ch, ics[ch]], v_news[ch]) for ch in range(n)]
        for ch in range(n):
            o_s[ch, sls[ch], :] = wqs[ch][c:] + outs[ch][:c]
            st_s[ch] = ss[ch] * gl_s[ch, pl.ds(ics[ch], 1), :] + outs[ch][c:]
        return carry

    lax.fori_loop(0, n_chunks, phase2, 0, unroll=2)

    for hh, d in chains:
        sfin_ref[d, hh] = st_s[2 * hh + d]
    for hh in range(hb):
        cols = slice(hh * dk, (hh + 1) * dk)
        o = o_s[2 * hh] + o_s[2 * hh + 1]
        o = o * lax.rsqrt(jnp.mean(o * o, axis=-1, keepdims=True) + EPS)
        o_ref[:, cols] = (o * onorm_ref[...] * _silu(gt_ref[:, cols].astype(F32))).astype(o_ref.dtype)


def _gdn_core(proj, ab, conv_w, a_log, dt_bias, onorm, s0, dst, states_dst, *, t_total, row0, n_seq, seq,
              n_heads, dk, dv, s0_spec, state_slot):
    assert dk == dv == V7X_LANES and row0 % seq == 0
    conv_k = conv_w.shape[0]
    r0 = row0 // seq
    n_chunks = seq // CHUNK
    hb = _tile(n_heads, min(GDN_HEADS_PER_STEP, max(1, GDN_ROWS_PER_STEP // seq)), 1)
    lockstep = _tile(n_chunks, max(1, GDN_PHASE1_CHAINS // (2 * hb)), 1)
    gate_pad = lambda p: jnp.pad(p.reshape(1, 2 * n_heads), ((0, 0), (0, V7X_LANES - 2 * n_heads)))
    lane_vec = pl.BlockSpec((1, V7X_LANES), lambda b, h: (0, 0))
    if s0 is None:
        s0 = jnp.zeros((2, hb, dk, dv), F32)
        s0_in = pl.BlockSpec((2, hb, dk, dv), lambda b, h: (0, 0, 0, 0))
    else:
        s0_in = s0_spec(hb)
    nb = n_heads // hb
    col = lambda sec: pl.BlockSpec((seq, hb * dk), lambda b, h: (b + r0, sec * nb + h))
    cw = lambda sec: pl.BlockSpec((conv_k, hb * dk), lambda b, h: (0, sec * nb + h))
    f32 = lambda *s: pltpu.VMEM(s, F32)
    in_specs = [lane_vec, lane_vec, col(0), col(1), col(2), col(3),
                pl.BlockSpec((seq, V7X_LANES), lambda b, h: (b + r0, 0)),
                cw(0), cw(1), cw(2),
                pl.BlockSpec((1, dv), lambda b, h: (0, 0)),
                s0_in]
    args = [gate_pad(a_log), gate_pad(dt_bias), proj, proj, proj, proj, ab, conv_w, conv_w, conv_w,
            onorm.reshape(1, dv), s0]
    aliases = {}
    for out_idx, kept in enumerate((dst, states_dst)):
        if kept is not None:
            in_specs.append(pl.BlockSpec(memory_space=pl.ANY))
            args.append(kept)
            aliases[len(args) - 1] = out_idx
    nc = 2 * hb
    slot, n_slots = state_slot
    return pl.pallas_call(
        functools.partial(_gdn_kernel, n_heads=n_heads, conv_k=conv_k, lockstep=lockstep,
                          n_kept=len(aliases)),
        out_shape=(jax.ShapeDtypeStruct((t_total, n_heads * dv), BF16),
                   jax.ShapeDtypeStruct((n_seq, n_slots, 2, n_heads, dk, dv), F32)),
        grid=(n_seq, nb),
        in_specs=in_specs,
        out_specs=(pl.BlockSpec((seq, hb * dv), lambda b, h: (b + r0, h)),
                   pl.BlockSpec((None, None, 2, hb, dk, dv), lambda b, h: (b, slot, 0, h, 0, 0))),
        scratch_shapes=[f32(5, seq, V7X_LANES),
                        f32(3 * hb, lockstep * CHUNK + 2 * CONV_HALO, dk),
                        pltpu.VMEM((nc, seq, dk), BF16), f32(nc, seq, dv), pltpu.VMEM((nc, seq, dk), BF16),
                        pltpu.VMEM((nc, n_chunks, CHUNK + dk, CHUNK), BF16),
                        f32(nc, max(n_chunks, V7X_SUBLANES), V7X_LANES),
                        f32(nc, seq, dv), f32(nc, dk, dv)],
        input_output_aliases=aliases,
        compiler_params=_cparams("parallel", "arbitrary"),
        name="gdn_core",
    )(*args)


def _folded_odd_dft(seq):
    half = seq // 2
    k = np.arange(half, dtype=np.int64)[:, None]
    parts = []
    for parity in (0, 1):
        m = 2 * np.arange(half, dtype=np.int64)[None, :] + parity
        ang = (((2 * k + 1) * m) % (4 * seq)) * (math.pi / (2 * seq))
        parts.append(np.concatenate([np.cos(ang), np.sin(ang)], axis=0))
    return parts


def _bf16_head_tail(m):
    m32 = m.astype(np.float32)
    head = m32.astype(jnp.bfloat16)
    tail = (m32 - head.astype(np.float32)).astype(jnp.bfloat16)
    return jnp.asarray(head), jnp.asarray(tail)


def _filter_kernel(feat_ref, w1_ref, b1_ref, w2_ref, b2_ref, fr_ref, w3f_ref, w3b_ref, dl_ref,
                   fe_hi_ref, fe_lo_ref, fo_hi_ref, fo_lo_ref, hre_ref, him_ref, hid_s):
    seq = feat_ref.shape[0]
    half = seq // 2
    n_hid = w2_ref.shape[0]

    @pl.when(jnp.logical_and(pl.program_id(0) == 0, pl.program_id(1) == 0))
    def _():
        fr = fr_ref[...]
        hid1 = jnp.sin(fr * (_dot(feat_ref[...], w1_ref[...]) + b1_ref[...]))
        hid_s[...] = jnp.zeros_like(hid_s)
        hid_s[:, :n_hid] = jnp.sin(fr * (_dot(hid1, w2_ref[...]) + b2_ref[...]))

    def taps(parity):
        lags = pl.ds(parity, half, stride=2)
        hid = hid_s[lags, :][:, :n_hid]
        window = jnp.exp(-feat_ref[lags, :][:, 0:1] * dl_ref[...])
        hf = _dot(hid, w3f_ref[...]) * window
        hb = _dot(hid, w3b_ref[...]) * window
        if parity == 0:
            hb = jnp.where(lax.broadcasted_iota(jnp.int32, hb.shape, 0) == 0, 0.0, hb)
        return hf + hb, hb - hf

    def dft(m_hi, m_lo, val):
        v_hi = val.astype(BF16)
        v_lo = (val - v_hi.astype(F32)).astype(BF16)
        return (jnp.dot(m_hi, v_hi, preferred_element_type=F32)
                + (jnp.dot(m_hi, v_lo, preferred_element_type=F32)
                   + jnp.dot(m_lo, v_hi, preferred_element_type=F32)))

    (sum_e, dif_e), (sum_o, dif_o) = taps(0), taps(1)
    cos_rows, sin_rows = slice(0, half), slice(half, seq)
    re_e = dft(fe_hi_ref[cos_rows, :], fe_lo_ref[cos_rows, :], sum_e)
    re_o = dft(fo_hi_ref[cos_rows, :], fo_lo_ref[cos_rows, :], sum_o)
    im_e = dft(fe_hi_ref[sin_rows, :], fe_lo_ref[sin_rows, :], dif_e)
    im_o = dft(fo_hi_ref[sin_rows, :], fo_lo_ref[sin_rows, :], dif_o)
    hre_ref[0:half, :] = re_e + re_o
    hre_ref[half:, :] = re_e - re_o
    him_ref[0:half, :] = im_e + im_o
    him_ref[half:, :] = im_o - im_e


def _hyena_filters(seq, d, w1, b1, w2, b2, w3, freq, dft_parts):
    emb, hid = w1.shape
    bands = (emb - 1) // 2
    t = jnp.linspace(0.0, 1.0, seq, dtype=F32)[:, None]
    wpos = (2.0 * math.pi / seq) * jnp.arange(seq, dtype=F32)[:, None]
    fb = jnp.linspace(1e-4, bands - 1, bands, dtype=F32)[None, :]
    feat = jnp.concatenate([t, jnp.cos(fb * wpos), -jnp.sin(fb * wpos)], axis=-1)
    feat = jnp.pad(feat, ((0, 0), (0, V7X_LANES - emb)))
    w1p = jnp.pad(w1, ((0, V7X_LANES - emb), (0, 0)))
    max_decay = math.log(HY_TARGET) / HY_SHORT_PCT
    min_decay = math.log(HY_TARGET) / HY_LONG_PCT
    deltas = jnp.abs(jnp.linspace(min_decay, max_decay, d, dtype=F32))[None, :]
    tc = _tile(d, 256, V7X_LANES)
    nt = d // tc
    full = lambda r, c: pl.BlockSpec((r, c), lambda n, j: (0, 0))
    out_spec = pl.BlockSpec((None, seq, tc), lambda n, j: (n, 0, j))
    return pl.pallas_call(
        _filter_kernel,
        out_shape=(jax.ShapeDtypeStruct((HY_ORDER, seq, d), F32),) * 2,
        grid=(HY_ORDER, nt),
        in_specs=[full(seq, V7X_LANES), full(V7X_LANES, hid), full(1, hid), full(hid, hid), full(1, hid),
                  full(1, hid),
                  pl.BlockSpec((hid, tc), lambda n, j: (0, (2 * n) * nt + j)),
                  pl.BlockSpec((hid, tc), lambda n, j: (0, (2 * n + 1) * nt + j)),
                  pl.BlockSpec((1, tc), lambda n, j: (0, j))] + [full(seq, seq // 2)] * 4,
        out_specs=(out_spec, out_spec),
        scratch_shapes=[pltpu.VMEM((seq, V7X_LANES), F32)],
        compiler_params=_cparams("arbitrary", "arbitrary"),
        name="hyena_filter",
    )(feat, w1p, b1.reshape(1, hid), w2, b2.reshape(1, hid), freq.reshape(1, hid), w3, w3, deltas,
      *dft_parts)


HYENA_ROWS_PER_STEP = 2048


def _hyconv_kernel(*refs, conv_z, conv_k, has_dst):
    (z_ref, x_ref, hre_ref, him_ref, skip_ref, cz_ref, cx_ref,
     fwd_e_ref, fwd_o_ref, inv_e_ref, inv_o_ref) = refs[:11]
    o_ref = refs[11 + has_dst]
    zs_s, xs_s, os_s = refs[-3:]
    seq = hre_ref.shape[0]
    half = seq // 2
    n_sub = z_ref.shape[0] // seq
    even, odd = pl.ds(0, half, stride=2), pl.ds(1, half, stride=2)
    n_lane_tiles = zs_s.shape[1]

    def stage(dst, s, val):
        for l in range(n_lane_tiles):
            dst[s, l] = val[:, l * V7X_LANES:(l + 1) * V7X_LANES]

    def rows_of(src, s, rows):
        return jnp.concatenate([src[s, l, rows, :] for l in range(n_lane_tiles)], axis=1)

    for s in range(n_sub):
        z = z_ref[s * seq:(s + 1) * seq, :].astype(F32)
        stage(zs_s, s, _dwconv(z, cz_ref[...], conv_k) if conv_z else z)
        stage(xs_s, s, _dwconv(x_ref[s * seq:(s + 1) * seq, :].astype(F32), cx_ref[...], conv_k))
    z_parts = [(rows_of(zs_s, s, even), rows_of(zs_s, s, odd)) for s in range(n_sub)]
    fwds = [(jnp.dot(fwd_e_ref[...], ze.astype(BF16), preferred_element_type=F32),
             jnp.dot(fwd_o_ref[...], zo.astype(BF16), preferred_element_type=F32)) for ze, zo in z_parts]
    hre_a, hre_b = hre_ref[0:half, :], hre_ref[half:, :]
    him_a, him_b = him_ref[0:half, :], him_ref[half:, :]
    skip = skip_ref[...]
    for s in range(n_sub):
        (fe, fo), (ze, zo) = fwds[s], z_parts[s]
        pe, qe, po, qo = fe[:half], fe[half:], fo[:half], fo[half:]
        pa, pb, qa, qb = pe + po, pe - po, qe + qo, qo - qe
        yre_a, yim_a = pa * hre_a + qa * him_a, pa * him_a - qa * hre_a
        yre_b, yim_b = pb * hre_b + qb * him_b, pb * him_b - qb * hre_b
        y_e = jnp.dot(inv_e_ref[...], jnp.concatenate([yre_a + yre_b, yim_a - yim_b], axis=0).astype(BF16),
                      preferred_element_type=F32)
        y_o = jnp.dot(inv_o_ref[...], jnp.concatenate([yre_a - yre_b, yim_a + yim_b], axis=0).astype(BF16),
                      preferred_element_type=F32)
        out_e = rows_of(xs_s, s, even) * (y_e + skip * ze)
        out_o = rows_of(xs_s, s, odd) * (y_o + skip * zo)
        for l in range(n_lane_tiles):
            lanes = slice(l * V7X_LANES, (l + 1) * V7X_LANES)
            os_s[s, l, even, :] = out_e[:, lanes]
            os_s[s, l, odd, :] = out_o[:, lanes]
        o_ref[s * seq:(s + 1) * seq, :] = rows_of(os_s, s, slice(None)).astype(o_ref.dtype)


def _hyconv(z, z_col0, conv_z, proj, x_col0, hre, him, order, skip, conv_w, dft, dst,
            *, row0, n_seq, seq, d, out_rows, out_row0, out_dtype):
    conv_k = conv_w.shape[0]
    n_sub = max(s for s in range(1, max(1, HYENA_ROWS_PER_STEP // seq) + 1)
                if n_seq % s == 0 and row0 % (s * seq) == 0 and out_row0 % (s * seq) == 0)
    blk = n_sub * seq
    tc = _tile(d, 512 if blk <= 1024 else 256, V7X_LANES)
    nt = d // tc
    r0 = row0 // blk
    zr0 = r0 if conv_z else 0
    zc, xc = z_col0 // tc, x_col0 // tc
    in_specs = [pl.BlockSpec((blk, tc), lambda j, b: (b + zr0, zc + j)),
                pl.BlockSpec((blk, tc), lambda j, b: (b + r0, xc + j)),
                pl.BlockSpec((None, seq, tc), lambda j, b: (order, 0, j)),
                pl.BlockSpec((None, seq, tc), lambda j, b: (order, 0, j)),
                pl.BlockSpec((1, tc), lambda j, b: (0, j)),
                pl.BlockSpec((conv_k, tc), lambda j, b: (0, zc + j)),
                pl.BlockSpec((conv_k, tc), lambda j, b: (0, xc + j)),
                pl.BlockSpec((seq, seq // 2), lambda j, b: (0, 0), pipeline_mode=_RESIDENT),
                pl.BlockSpec((seq, seq // 2), lambda j, b: (0, 0), pipeline_mode=_RESIDENT),
                pl.BlockSpec((seq // 2, seq), lambda j, b: (0, 0), pipeline_mode=_RESIDENT),
                pl.BlockSpec((seq // 2, seq), lambda j, b: (0, 0), pipeline_mode=_RESIDENT)]
    args = [z, proj, hre, him, skip.reshape(1, d), conv_w, conv_w, *dft]
    aliases = {}
    if dst is not None:
        in_specs.append(pl.BlockSpec(memory_space=pl.ANY))
        args.append(dst)
        aliases = {len(args) - 1: 0}
    out_r0 = out_row0 // blk
    return pl.pallas_call(
        functools.partial(_hyconv_kernel, conv_z=conv_z, conv_k=conv_k, has_dst=dst is not None),
        out_shape=jax.ShapeDtypeStruct((out_rows, d), out_dtype),
        grid=(nt, n_seq // n_sub),
        in_specs=in_specs,
        out_specs=pl.BlockSpec((blk, tc), lambda j, b: (b + out_r0, j)),
        scratch_shapes=[pltpu.VMEM((n_sub, tc // V7X_LANES, seq, V7X_LANES), F32)] * 3,
        input_output_aliases=aliases,
        compiler_params=_cparams("parallel", "parallel"),
        name="hyena_conv",
    )(*args)


def _grid_pos_emb(n_tokens, d):
    rows = n_tokens // GRID_W
    r, col = jnp.meshgrid(jnp.arange(rows), jnp.arange(GRID_W), indexing='ij')
    quarter = d // 4
    omega = 1.0 / (POS_BASE ** (jnp.arange(quarter, dtype=F32) / quarter))

    def emb1d(p):
        a = p.reshape(-1, 1).astype(F32) * omega[None, :]
        return jnp.concatenate([jnp.sin(a), jnp.cos(a)], axis=-1)

    return jnp.concatenate([emb1d(r), emb1d(col)], axis=-1)


def kernel(x_prompt, x_sample, state_delta, c, c_ctx, ada_w, ada_b, norm1_g, norm2_g, gdn_w_in, gdn_conv, gdn_a_log, gdn_dt_bias, gdn_onorm, gdn_w_out, hy_w_in, hy_b_in, hy_conv, hy_f_w1, hy_f_b1, hy_f_w2, hy_f_b2, hy_f_w3, hy_freq, hy_skip, hy_w_out, hy_b_out, ffn_w_gu, ffn_w_down, final_g):
    bc, lc, d = x_prompt.shape
    bl, ll, _ = x_sample.shape
    depth = ada_w.shape[0]
    n_heads, dk, dv = state_delta.shape[3:]
    tc_rows, tl_rows = bc * lc, bl * ll
    t = tc_rows + tl_rows
    assert tc_rows % ll == 0 and ll % lc == 0
    rows = _Rows(tc_rows, ll, t, _tile(math.gcd(tc_rows, ll), 512, V7X_SUBLANES))

    bm = 1 + bl
    bm_pad = -(-bm // V7X_SUBLANES) * V7X_SUBLANES
    cvec = jnp.concatenate([c_ctx[None, :], c, jnp.zeros((bm_pad - bm, d), F32)], axis=0)
    mods_all = _ada(cvec, ada_w, ada_b).reshape(depth, bm_pad, 6, d)

    x = _embed(x_prompt.reshape(tc_rows, d), x_sample.reshape(tl_rows, d), _grid_pos_emb(ll, d), rows)

    n_hy = hy_w_in.shape[0]
    dft = {}
    for seq in (lc, ll):
        half = seq // 2
        fwd_e, fwd_o = _folded_odd_dft(seq)
        (fe_hi, fe_lo), (fo_hi, fo_lo) = _bf16_head_tail(fwd_e), _bf16_head_tail(fwd_o)
        inv_e, _ = _bf16_head_tail(np.concatenate([fwd_e[:half].T, -fwd_e[half:].T], axis=1) / seq)
        inv_o, _ = _bf16_head_tail(np.concatenate([fwd_o[:half].T, -fwd_o[half:].T], axis=1) / seq)
        spectra = [_hyena_filters(seq, d, hy_f_w1[j], hy_f_b1[j], hy_f_w2[j], hy_f_b2[j], hy_f_w3[j],
                                  hy_freq[j], (fe_hi, fe_lo, fo_hi, fo_lo)) for j in range(n_hy)]
        dft[seq] = ((fe_hi, fo_hi, inv_e, inv_o), spectra)

    zero_b = jnp.zeros((d,), F32)
    mixed = jnp.zeros((t, n_heads * dv), BF16)
    shared = n_heads * dv == d
    mixed_hy = mixed if shared else jnp.zeros((t, d), BF16)
    new_state_delta = jnp.zeros((bc, gdn_w_in.shape[0], 2, n_heads, dk, dv), F32)
    n_mixers = 2
    n_gdn = gdn_w_in.shape[0]
    gdn_w_in_b = jnp.pad(gdn_w_in.astype(BF16), ((0, 0), (0, 0), (0, V7X_LANES - 4 * n_heads)))
    gdn_w_out_b, hy_w_in_b, hy_w_out_b = (w.astype(BF16) for w in (gdn_w_out, hy_w_in, hy_w_out))
    ffn_w_gu_b, ffn_w_down_b = ffn_w_gu.astype(BF16), ffn_w_down.astype(BF16)
    for layer in range(depth):
        j = layer // n_mixers
        if layer % n_mixers == 0:
            proj, ab = _in_proj(x, norm1_g[layer], mods_all, layer, gdn_w_in_b, j,
                                jnp.zeros((gdn_w_in_b.shape[2],), F32), rows, n_side=V7X_LANES)
            gdn = dict(t_total=t, n_heads=n_heads, dk=dk, dv=dv)
            weights = (gdn_conv[j], gdn_a_log[j], gdn_dt_bias[j], gdn_onorm[j])
            mixed, new_state_delta = _gdn_core(proj, ab, *weights, None, mixed, new_state_delta, row0=0, n_seq=bc,
                                               seq=lc, s0_spec=None, state_slot=(j, n_gdn), **gdn)
            s0_spec = lambda hb, j=j: pl.BlockSpec((None, None, 2, hb, dk, dv), lambda b, h: (b, j, 0, h, 0, 0))
            mixed, _ = _gdn_core(proj, ab, *weights, state_delta, mixed, None, row0=tc_rows, n_seq=bl, seq=ll,
                                 s0_spec=s0_spec, state_slot=(0, 1), **gdn)
            mixer = (mixed, gdn_w_out_b, j, zero_b)
            mixed_hy = mixed if shared else mixed_hy
        else:
            proj = _in_proj(x, norm1_g[layer], mods_all, layer, hy_w_in_b, j, hy_b_in[j], rows)
            for row0, n_seq, seq in ((0, bc, lc), (tc_rows, bl, ll)):
                dft_mats, spectra = dft[seq]
                hre, him = spectra[j]
                z = proj
                for n in range(HY_ORDER):
                    last = n == HY_ORDER - 1
                    z = _hyconv(z, 0, n == 0, proj, (n + 1) * d, hre, him, n, hy_skip[j, n], hy_conv[j],
                                dft_mats, mixed_hy if last else None, row0=row0, n_seq=n_seq, seq=seq, d=d,
                                out_rows=t if last else n_seq * seq, out_row0=row0 if last else 0,
                                out_dtype=BF16 if last else F32)
                mixed_hy = z
            mixer = (mixed_hy, hy_w_out_b, j, hy_b_out[j])
            mixed = mixed_hy if shared else mixed
        ffn = functools.partial(_ffn, x, norm2_g[layer], mods_all, ffn_w_gu_b, ffn_w_down_b, layer, rows,
                                mixer=mixer)
        if layer < depth - 1:
            x = ffn()
        else:
            y_prompt = ffn(final_g=final_g, tile0=0, n_tiles=rows.n_ctx_tiles)
            y_sample = ffn(final_g=final_g, tile0=rows.n_ctx_tiles, n_tiles=rows.n_tiles - rows.n_ctx_tiles)
    return (y_prompt.reshape(bc, lc, d), y_sample.reshape(bl, ll, d), new_state_delta)
```

```python
import functools
import math

import jax
import jax.numpy as jnp
import numpy as np
from jax import lax
from jax.experimental import pallas as pl
from jax.experimental.pallas import tpu as pltpu

GRID_W = 64
CHUNK = 64
HY_ORDER = 2
HY_TARGET = 1e-2
HY_SHORT_PCT = 0.3
HY_LONG_PCT = 1.5
POS_BASE = 10000.0
EPS = 1e-6

V7X_LANES = 128
V7X_SUBLANES = 8
V7X_VMEM_LIMIT_BYTES = 48 * 1024 * 1024

BF16 = jnp.bfloat16
F32 = jnp.float32


def _cparams(*sem):
    return pltpu.CompilerParams(dimension_semantics=sem, vmem_limit_bytes=V7X_VMEM_LIMIT_BYTES)


def _tile(n, target, align):
    if n <= target:
        return n
    best = None
    for t in range(align, target + 1, align):
        if n % t == 0:
            best = t
    assert best is not None, (n, target, align)
    return best


def _dot(a, b):
    return jnp.dot(a.astype(BF16), b.astype(BF16), preferred_element_type=F32)


def _dot_nt(a, b):
    return lax.dot_general(a.astype(BF16), b.astype(BF16), (((1,), (1,)), ((), ())),
                           preferred_element_type=F32)


def _silu(x):
    return x * jax.nn.sigmoid(x)


def _norm_mod(x, g, shift, scale):
    ms = jnp.mean(x * x, axis=-1, keepdims=True)
    return (x * lax.rsqrt(ms + EPS) * g) * (1.0 + scale) + shift


class _Rows:
    def __init__(self, tc, ll, t, tm):
        assert tc % tm == 0 and ll % tm == 0 and t % tm == 0
        self.n_ctx_tiles = tc // tm
        self.tiles_per_lat = ll // tm
        self.n_tiles = t // tm
        self.tm = tm

    def mod_index(self, i):
        lat = 1 + (i - self.n_ctx_tiles) // self.tiles_per_lat
        return jnp.where(i < self.n_ctx_tiles, 0, lat)


def _ada_kernel(c_ref, w_ref, b_ref, o_ref):
    o_ref[...] = _dot(_silu(c_ref[...]), w_ref[...]) + b_ref[...]


def _ada(cvec, ada_w, ada_b):
    depth, d, n = ada_w.shape
    bm = cvec.shape[0]
    tn = _tile(n, 1536, V7X_LANES)
    return pl.pallas_call(
        _ada_kernel,
        out_shape=jax.ShapeDtypeStruct((depth, bm, n), F32),
        grid=(depth, n // tn),
        in_specs=[pl.BlockSpec((bm, d), lambda l, j: (0, 0)),
                  pl.BlockSpec((None, d, tn), lambda l, j: (l, 0, j)),
                  pl.BlockSpec((None, 1, tn), lambda l, j: (l, 0, j))],
        out_specs=pl.BlockSpec((None, bm, tn), lambda l, j: (l, 0, j)),
        compiler_params=_cparams("parallel", "parallel"),
        name="ada",
    )(cvec, ada_w, ada_b.reshape(depth, 1, n))


def _embed_kernel(xp_ref, xs_ref, pos_ref, o_ref, *, n_ctx_tiles):
    i = pl.program_id(0)

    @pl.when(i < n_ctx_tiles)
    def _():
        o_ref[...] = xp_ref[...]

    @pl.when(i >= n_ctx_tiles)
    def _():
        o_ref[...] = xs_ref[...] + pos_ref[...]


def _embed(xp, xs, pos, rows):
    t, d = xp.shape[0] + xs.shape[0], xp.shape[1]
    tm, nct = rows.tm, rows.n_ctx_tiles
    npos = pos.shape[0] // tm
    return pl.pallas_call(
        functools.partial(_embed_kernel, n_ctx_tiles=nct),
        out_shape=jax.ShapeDtypeStruct((t, d), F32),
        grid=(rows.n_tiles,),
        in_specs=[pl.BlockSpec((tm, d), lambda i: (jnp.minimum(i, nct - 1), 0)),
                  pl.BlockSpec((tm, d), lambda i: (jnp.maximum(i - nct, 0), 0)),
                  pl.BlockSpec((tm, d), lambda i: (jnp.maximum(i - nct, 0) % npos, 0))],
        out_specs=pl.BlockSpec((tm, d), lambda i: (i, 0)),
        compiler_params=_cparams("parallel"),
        name="embed",
    )(xp, xs, pos)


_RESIDENT = pl.Buffered(1)


def _in_kernel(x_ref, g_ref, mod_ref, w_ref, b_ref, o_ref, *side_ref, chunk):
    m = mod_ref[...]
    h = _norm_mod(x_ref[...], g_ref[...], m[0:1, :], m[1:2, :]).astype(BF16)
    n = o_ref.shape[1]
    for c0 in range(0, n, chunk):
        cols = slice(c0, c0 + chunk)
        y = jnp.dot(h, w_ref[:, cols], preferred_element_type=F32) + b_ref[:, cols]
        o_ref[:, cols] = y.astype(o_ref.dtype)
    if side_ref:
        side_ref[0][...] = jnp.dot(h, w_ref[:, n:], preferred_element_type=F32) + b_ref[:, n:]


def _in_proj(x, g, mods_all, layer, w_all, w_index, b, rows, n_side=0):
    t, d = x.shape
    n = w_all.shape[2] - n_side
    tm = rows.tm
    chunk = _tile(n, 512, V7X_LANES)
    out_shape = [jax.ShapeDtypeStruct((t, n), BF16)]
    out_specs = [pl.BlockSpec((tm, n), lambda i: (i, 0))]
    if n_side:
        out_shape.append(jax.ShapeDtypeStruct((t, n_side), F32))
        out_specs.append(pl.BlockSpec((tm, n_side), lambda i: (i, 0)))
    out = pl.pallas_call(
        functools.partial(_in_kernel, chunk=chunk),
        out_shape=out_shape,
        grid=(rows.n_tiles,),
        in_specs=[pl.BlockSpec((tm, d), lambda i: (i, 0)),
                  pl.BlockSpec((1, d), lambda i: (0, 0), pipeline_mode=_RESIDENT),
                  pl.BlockSpec((None, None, 6, d), lambda i: (layer, rows.mod_index(i), 0, 0)),
                  pl.BlockSpec((None, d, n + n_side), lambda i: (w_index, 0, 0), pipeline_mode=_RESIDENT),
                  pl.BlockSpec((1, n + n_side), lambda i: (0, 0), pipeline_mode=_RESIDENT)],
        out_specs=out_specs,
        compiler_params=_cparams("parallel"),
        name="in_proj",
    )(x, g.reshape(1, d), mods_all, w_all, b.reshape(1, n + n_side))
    return out if n_side else out[0]


def _ffn_kernel(x_ref, g_ref, mod_ref, wgu_ref, wd_ref, *rest, chunk, has_mixer, has_final):
    o_ref = rest[-1]
    x = x_ref[...]
    m = mod_ref[...]
    if has_mixer:
        a_ref, wo_ref, bo_ref = rest[:3]
        x = x + m[2:3, :] * (_dot(a_ref[...], wo_ref[...]) + bo_ref[...])
    h = _norm_mod(x, g_ref[...], m[3:4, :], m[4:5, :]).astype(BF16)
    f = wd_ref.shape[0]
    n_chunks = f // chunk

    def gate_up(k):
        gate = jnp.dot(h, wgu_ref[:, k * chunk:(k + 1) * chunk], preferred_element_type=F32)
        up = jnp.dot(h, wgu_ref[:, f + k * chunk:f + (k + 1) * chunk], preferred_element_type=F32)
        return gate, up

    y = None
    pending = gate_up(0)
    for k in range(n_chunks):
        following = gate_up(k + 1) if k + 1 < n_chunks else None
        act = (_silu(pending[0]) * pending[1]).astype(BF16)
        part = jnp.dot(act, wd_ref[k * chunk:(k + 1) * chunk, :], preferred_element_type=F32)
        y = part if y is None else y + part
        pending = following
    out = x + m[5:6, :] * y
    if has_final:
        out = out * lax.rsqrt(jnp.mean(out * out, axis=-1, keepdims=True) + EPS) * rest[-2][...]
    o_ref[...] = out


def _ffn(x, g, mods_all, w_gu_all, w_down_all, layer, rows, mixer=None, final_g=None, tile0=0, n_tiles=None):
    d = x.shape[1]
    f = w_down_all.shape[1]
    tm = rows.tm
    n_tiles = rows.n_tiles if n_tiles is None else n_tiles
    chunk = _tile(f, 256, V7X_LANES)
    resident = lambda *s: pl.BlockSpec(s, lambda i: (0,) * len(s), pipeline_mode=_RESIDENT)
    in_specs = [pl.BlockSpec((tm, d), lambda i: (i + tile0, 0)),
                resident(1, d),
                pl.BlockSpec((None, None, 6, d), lambda i: (layer, rows.mod_index(i + tile0), 0, 0)),
                pl.BlockSpec((None, d, 2 * f), lambda i: (layer, 0, 0), pipeline_mode=_RESIDENT),
                pl.BlockSpec((None, f, d), lambda i: (layer, 0, 0), pipeline_mode=_RESIDENT)]
    args = [x, g.reshape(1, d), mods_all, w_gu_all, w_down_all]
    if mixer is not None:
        a, wo_all, wo_index, bo = mixer
        k = a.shape[1]
        in_specs += [pl.BlockSpec((tm, k), lambda i: (i + tile0, 0)),
                     pl.BlockSpec((None, k, d), lambda i: (wo_index, 0, 0), pipeline_mode=_RESIDENT),
                     resident(1, d)]
        args += [a, wo_all, bo.reshape(1, d)]
    if final_g is not None:
        in_specs.append(resident(1, d))
        args.append(final_g.reshape(1, d))
    return pl.pallas_call(
        functools.partial(_ffn_kernel, chunk=chunk, has_mixer=mixer is not None, has_final=final_g is not None),
        out_shape=jax.ShapeDtypeStruct((n_tiles * tm, d), F32),
        grid=(n_tiles,),
        in_specs=in_specs,
        out_specs=pl.BlockSpec((tm, d), lambda i: (i, 0)),
        compiler_params=_cparams("parallel"),
        name="ffn",
    )(*args)


def _dwconv(x, w, k):
    n = x.shape[0]
    half = k // 2
    e = V7X_SUBLANES
    assert half <= e and n >= 4 * e
    taps = [w[j:j + 1, :] for j in range(k)]
    shifts = [s for s in range(-half, half + 1) if s != 0]

    def conv(v, mask):
        acc = v * taps[half]
        for s in shifts:
            shifted = pltpu.roll(v, (-s) % v.shape[0], axis=0)
            acc = acc + (shifted if mask is None else jnp.where(mask(s), shifted, 0.0)) * taps[s + half]
        return acc

    row = lax.broadcasted_iota(jnp.int32, (2 * e, x.shape[1]), 0)
    top = conv(x[:2 * e], lambda s: row + s >= 0)[:e]
    bottom = conv(x[n - 2 * e:], lambda s: row + s < 2 * e)[e:]
    return jnp.concatenate([top, conv(x, None)[e:n - e], bottom], axis=0)


CONV_HALO = 16


def _dwconv_window(ref, cols, w, k, r0, rows, slab_ref):
    n = ref.shape[0]
    if rows == n:
        return _dwconv(ref[:, cols].astype(F32), w, k)
    half = k // 2
    h = CONV_HALO
    assert half <= h <= rows and n % rows == 0
    before = ref[pl.ds(pl.multiple_of(jnp.maximum(r0 - h, 0), h), h), cols].astype(F32)
    after = ref[pl.ds(pl.multiple_of(jnp.minimum(r0 + rows, n - h), h), h), cols].astype(F32)
    slab_ref[0:h, :] = jnp.where(r0 > 0, before, 0.0)
    slab_ref[h:h + rows, :] = ref[pl.ds(r0, rows), cols].astype(F32)
    slab_ref[h + rows:, :] = jnp.where(r0 + rows < n, after, 0.0)
    acc = None
    for s in range(-half, half + 1):
        term = slab_ref[h + s:h + s + rows, :] * w[s + half:s + half + 1, :]
        acc = term if acc is None else acc + term
    return acc


TRI_BASE = 8
GDN_PHASE1_CHAINS = 32
GDN_HEADS_PER_STEP = 8
GDN_ROWS_PER_STEP = 4096


def _unit_tri_inverses_minus_eye(mats, ri, ci):
    c = mats[0].shape[0]

    def same_block(s):
        sh = int(math.log2(s))
        return (ri >> sh) == (ci >> sh)

    ps = [jnp.where(same_block(TRI_BASE), -a, 0.0) for a in mats]
    es = ps
    n_lvl = int(math.log2(TRI_BASE))
    for lvl in range(n_lvl):
        es = [e + _dot(p, e) for p, e in zip(ps, es)]
        if lvl < n_lvl - 1:
            ps = [_dot(p, p) for p in ps]
    s = TRI_BASE
    while s < c:
        mask = jnp.logical_and(same_block(2 * s), jnp.logical_not(same_block(s)))
        offs = [jnp.where(mask, a, 0.0) for a in mats]
        ys = [off + _dot(off, e) for off, e in zip(offs, es)]
        es = [e - (y + _dot(e, y)) for e, y in zip(es, ys)]
        s *= 2
    return es


def _chunk_cumsum(x, pos, reverse):
    n = x.shape[0]
    s = 1
    while s < CHUNK:
        if reverse:
            x = x + jnp.where(pos + s < CHUNK, pltpu.roll(x, n - s, axis=0), 0.0)
        else:
            x = x + jnp.where(pos >= s, pltpu.roll(x, s, axis=0), 0.0)
        s *= 2
    return x


def _gdn_kernel(*refs, n_heads, conv_k, lockstep, n_kept):
    (alog_ref, dtb_ref, q_ref, k_ref, v_ref, gt_ref, ab_ref, cq_ref, ck_ref, cv_ref,
     onorm_ref, s0_ref) = refs[:12]
    o_ref, sfin_ref, gates_s, conv_s, w_s, u_s, qd_s, ak_s, gl_s, o_s, st_s = refs[12 + n_kept:]
    n_chains, seq, dk = w_s.shape
    hb = n_chains // 2
    head0 = pl.program_id(1) * hb
    c = CHUNK
    n_chunks = seq // c
    chains = [(hh, d) for hh in range(hb) for d in range(2)]

    @pl.when(head0 == 0)
    def _():
        ab = ab_ref[...]
        pos = jnp.bitwise_and(lax.broadcasted_iota(jnp.int32, ab.shape, 0), c - 1)
        g = -jnp.exp(alog_ref[...]) * jax.nn.softplus(ab + dtb_ref[...])
        g_fwd = _chunk_cumsum(g, pos, reverse=False)
        g_rev = _chunk_cumsum(g, pos, reverse=True)
        gates_s[0] = g_fwd
        gates_s[1] = g_rev - g
        gates_s[2] = g_rev
        gates_s[3] = g_fwd - g
        gates_s[4] = jax.nn.sigmoid(ab)

    lane = lax.broadcasted_iota(jnp.int32, (1, V7X_LANES), 1)

    rows = lockstep * c

    def column(i, idx, win):
        one_hot = (lane == idx).astype(F32)
        col = jnp.sum(gates_s[i, win, :] * one_hot, axis=1, keepdims=True)
        return jnp.broadcast_to(col, (rows, V7X_LANES))

    ri = lax.broadcasted_iota(jnp.int32, (c, c), 0)
    ci = lax.broadcasted_iota(jnp.int32, (c, c), 1)
    incl = (ri >= ci, ri <= ci)
    strict = (ri > ci, ri < ci)

    def phase1(it, carry):
        r0 = pl.multiple_of(it * rows, rows)
        win = pl.ds(r0, rows)
        prepared = []
        for hh in range(hb):
            cols = slice(hh * dk, (hh + 1) * dk)
            q = _silu(_dwconv_window(q_ref, cols, cq_ref[:, cols], conv_k, r0, rows, conv_s.at[3 * hh]))
            k = _silu(_dwconv_window(k_ref, cols, ck_ref[:, cols], conv_k, r0, rows, conv_s.at[3 * hh + 1]))
            v = _silu(_dwconv_window(v_ref, cols, cv_ref[:, cols], conv_k, r0, rows, conv_s.at[3 * hh + 2]))
            q = q * lax.rsqrt(jnp.sum(q * q, axis=-1, keepdims=True) + EPS) * (dk ** -0.5)
            k = k * lax.rsqrt(jnp.sum(k * k, axis=-1, keepdims=True) + EPS)
            for d in range(2):
                head = head0 + hh
                g_cum = column(2 * d, d * n_heads + head, win)
                g_tail = column(2 * d + 1, d * n_heads + head, win)
                beta = column(4, 2 * n_heads + d * n_heads + head, win)
                e_cum = jnp.exp(g_cum)
                kb = k * beta
                prepared.append((q, k, g_cum, kb, kb * e_cum, v * beta, k * jnp.exp(g_tail)))
                qd_s[2 * hh + d, win, :] = (q * e_cum).astype(qd_s.dtype)
                g_tot = jnp.exp(g_cum + g_tail)
                for gi in range(lockstep):
                    gl_s[2 * hh + d, pl.ds(it * lockstep + gi, 1), :] = g_tot[gi * c:gi * c + 1, :]
        items = []
        for gi in range(lockstep):
            part = slice(gi * c, (gi + 1) * c)
            for ch, (_, d) in enumerate(chains):
                sl = pl.ds(pl.multiple_of(r0 + gi * c, c), c)
                items.append((it * lockstep + gi, sl, ch, d) + tuple(a[part] for a in prepared[ch]))
        decays, kqs = [], []
        for ic, sl, ch, d, qc, kc, gc, kb, w0, u0, ktl in items:
            diff = gc[:, :c] - gc.T[:c, :]
            decays.append(jnp.where(incl[d], jnp.exp(jnp.where(incl[d], diff, 0.0)), 0.0))
            kqs.append(_dot_nt(jnp.concatenate([kb, qc], axis=0), kc))
        a_kks = [jnp.where(strict[item[3]], kq[:c] * decay, 0.0)
                 for item, kq, decay in zip(items, kqs, decays)]
        es = _unit_tri_inverses_minus_eye(a_kks, ri, ci)
        rhss = [jnp.concatenate([item[8], item[9]], axis=1) for item in items]
        wus = [rhs + _dot(e, rhs) for e, rhs in zip(es, rhss)]
        for item, kq, decay, wu in zip(items, kqs, decays, wus):
            ic, sl, ch = item[:3]
            w_s[ch, sl, :] = wu[:, :dk].astype(w_s.dtype)
            u_s[ch, sl, :] = wu[:, dk:]
            ak_s[ch, ic] = jnp.concatenate([kq[c:] * decay, item[10].T], axis=0).astype(ak_s.dtype)
        return carry

    lax.fori_loop(0, n_chunks // lockstep, phase1, 0)

    for hh, d in chains:
        st_s[2 * hh + d] = s0_ref[d, hh]

    def phase2(i, carry):
        n = 2 * hb
        ics = [n_chunks - 1 - i if d else i for _, d in chains]
        r0s = [pl.multiple_of(ic * c, c) for ic in ics]
        sls = [pl.ds(r0, c) for r0 in r0s]
        ss = [st_s[ch] for ch in range(n)]
        wqs = [_dot(jnp.concatenate([w_s[ch, sls[ch], :], qd_s[ch, sls[ch], :]], axis=0), ss[ch]) for ch in range(n)]
        v_news = [u_s[ch, sls[ch], :] - wqs[ch][:c] for ch in range(n)]
        outs = [_dot(ak_s[ch, ics[ch]], v_news[ch]) for ch in range(n)]
        for ch in range(n):
            o_s[ch, sls[ch], :] = wqs[ch][c:] + outs[ch][:c]
            st_s[ch] = ss[ch] * gl_s[ch, pl.ds(ics[ch], 1), :] + outs[ch][c:]
        return carry

    lax.fori_loop(0, n_chunks, phase2, 0, unroll=2)

    for hh, d in chains:
        sfin_ref[d, hh] = st_s[2 * hh + d]
    for hh in range(hb):
        cols = slice(hh * dk, (hh + 1) * dk)
        o = o_s[2 * hh] + o_s[2 * hh + 1]
        o = o * lax.rsqrt(jnp.mean(o * o, axis=-1, keepdims=True) + EPS)
        o_ref[:, cols] = (o * onorm_ref[...] * _silu(gt_ref[:, cols].astype(F32))).astype(o_ref.dtype)


def _gdn_core(proj, ab, conv_w, a_log, dt_bias, onorm, s0, dst, states_dst, *, t_total, row0, n_seq, seq,
              n_heads, dk, dv, s0_spec, state_slot):
    assert dk == dv == V7X_LANES and row0 % seq == 0
    conv_k = conv_w.shape[0]
    r0 = row0 // seq
    n_chunks = seq // CHUNK
    hb = _tile(n_heads, min(GDN_HEADS_PER_STEP, max(1, GDN_ROWS_PER_STEP // seq)), 1)
    lockstep = _tile(n_chunks, max(1, GDN_PHASE1_CHAINS // (2 * hb)), 1)
    gate_pad = lambda p: jnp.pad(p.reshape(1, 2 * n_heads), ((0, 0), (0, V7X_LANES - 2 * n_heads)))
    lane_vec = pl.BlockSpec((1, V7X_LANES), lambda b, h: (0, 0))
    if s0 is None:
        s0 = jnp.zeros((2, hb, dk, dv), F32)
        s0_in = pl.BlockSpec((2, hb, dk, dv), lambda b, h: (0, 0, 0, 0))
    else:
        s0_in = s0_spec(hb)
    nb = n_heads // hb
    col = lambda sec: pl.BlockSpec((seq, hb * dk), lambda b, h: (b + r0, sec * nb + h))
    cw = lambda sec: pl.BlockSpec((conv_k, hb * dk), lambda b, h: (0, sec * nb + h))
    f32 = lambda *s: pltpu.VMEM(s, F32)
    in_specs = [lane_vec, lane_vec, col(0), col(1), col(2), col(3),
                pl.BlockSpec((seq, V7X_LANES), lambda b, h: (b + r0, 0)),
                cw(0), cw(1), cw(2),
                pl.BlockSpec((1, dv), lambda b, h: (0, 0)),
                s0_in]
    args = [gate_pad(a_log), gate_pad(dt_bias), proj, proj, proj, proj, ab, conv_w, conv_w, conv_w,
            onorm.reshape(1, dv), s0]
    aliases = {}
    for out_idx, kept in enumerate((dst, states_dst)):
        if kept is not None:
            in_specs.append(pl.BlockSpec(memory_space=pl.ANY))
            args.append(kept)
            aliases[len(args) - 1] = out_idx
    nc = 2 * hb
    slot, n_slots = state_slot
    return pl.pallas_call(
        functools.partial(_gdn_kernel, n_heads=n_heads, conv_k=conv_k, lockstep=lockstep,
                          n_kept=len(aliases)),
        out_shape=(jax.ShapeDtypeStruct((t_total, n_heads * dv), BF16),
                   jax.ShapeDtypeStruct((n_seq, n_slots, 2, n_heads, dk, dv), F32)),
        grid=(n_seq, nb),
        in_specs=in_specs,
        out_specs=(pl.BlockSpec((seq, hb * dv), lambda b, h: (b + r0, h)),
                   pl.BlockSpec((None, None, 2, hb, dk, dv), lambda b, h: (b, slot, 0, h, 0, 0))),
        scratch_shapes=[f32(5, seq, V7X_LANES),
                        f32(3 * hb, lockstep * CHUNK + 2 * CONV_HALO, dk),
                        pltpu.VMEM((nc, seq, dk), BF16), f32(nc, seq, dv), pltpu.VMEM((nc, seq, dk), BF16),
                        pltpu.VMEM((nc, n_chunks, CHUNK + dk, CHUNK), BF16),
                        f32(nc, max(n_chunks, V7X_SUBLANES), V7X_LANES),
                        f32(nc, seq, dv), f32(nc, dk, dv)],
        input_output_aliases=aliases,
        compiler_params=_cparams("parallel", "arbitrary"),
        name="gdn_core",
    )(*args)


def _folded_odd_dft(seq):
    half = seq // 2
    k = np.arange(half, dtype=np.int64)[:, None]
    parts = []
    for parity in (0, 1):
        m = 2 * np.arange(half, dtype=np.int64)[None, :] + parity
        ang = (((2 * k + 1) * m) % (4 * seq)) * (math.pi / (2 * seq))
        parts.append(np.concatenate([np.cos(ang), np.sin(ang)], axis=0))
    return parts


def _bf16_head_tail(m):
    m32 = m.astype(np.float32)
    head = m32.astype(jnp.bfloat16)
    tail = (m32 - head.astype(np.float32)).astype(jnp.bfloat16)
    return jnp.asarray(head), jnp.asarray(tail)


def _filter_kernel(feat_ref, w1_ref, b1_ref, w2_ref, b2_ref, fr_ref, w3f_ref, w3b_ref, dl_ref,
                   fe_hi_ref, fe_lo_ref, fo_hi_ref, fo_lo_ref, hre_ref, him_ref, hid_s):
    seq = feat_ref.shape[0]
    half = seq // 2
    n_hid = w2_ref.shape[0]

    @pl.when(jnp.logical_and(pl.program_id(0) == 0, pl.program_id(1) == 0))
    def _():
        fr = fr_ref[...]
        hid1 = jnp.sin(fr * (_dot(feat_ref[...], w1_ref[...]) + b1_ref[...]))
        hid_s[...] = jnp.zeros_like(hid_s)
        hid_s[:, :n_hid] = jnp.sin(fr * (_dot(hid1, w2_ref[...]) + b2_ref[...]))

    def taps(parity):
        lags = pl.ds(parity, half, stride=2)
        hid = hid_s[lags, :][:, :n_hid]
        window = jnp.exp(-feat_ref[lags, :][:, 0:1] * dl_ref[...])
        hf = _dot(hid, w3f_ref[...]) * window
        hb = _dot(hid, w3b_ref[...]) * window
        if parity == 0:
            hb = jnp.where(lax.broadcasted_iota(jnp.int32, hb.shape, 0) == 0, 0.0, hb)
        return hf + hb, hb - hf

    def dft(m_hi, m_lo, val):
        v_hi = val.astype(BF16)
        v_lo = (val - v_hi.astype(F32)).astype(BF16)
        return (jnp.dot(m_hi, v_hi, preferred_element_type=F32)
                + (jnp.dot(m_hi, v_lo, preferred_element_type=F32)
                   + jnp.dot(m_lo, v_hi, preferred_element_type=F32)))

    (sum_e, dif_e), (sum_o, dif_o) = taps(0), taps(1)
    cos_rows, sin_rows = slice(0, half), slice(half, seq)
    re_e = dft(fe_hi_ref[cos_rows, :], fe_lo_ref[cos_rows, :], sum_e)
    re_o = dft(fo_hi_ref[cos_rows, :], fo_lo_ref[cos_rows, :], sum_o)
    im_e = dft(fe_hi_ref[sin_rows, :], fe_lo_ref[sin_rows, :], dif_e)
    im_o = dft(fo_hi_ref[sin_rows, :], fo_lo_ref[sin_rows, :], dif_o)
    hre_ref[0:half, :] = re_e + re_o
    hre_ref[half:, :] = re_e - re_o
    him_ref[0:half, :] = im_e + im_o
    him_ref[half:, :] = im_o - im_e


def _hyena_filters(seq, d, w1, b1, w2, b2, w3, freq, dft_parts):
    emb, hid = w1.shape
    bands = (emb - 1) // 2
    t = jnp.linspace(0.0, 1.0, seq, dtype=F32)[:, None]
    wpos = (2.0 * math.pi / seq) * jnp.arange(seq, dtype=F32)[:, None]
    fb = jnp.linspace(1e-4, bands - 1, bands, dtype=F32)[None, :]
    feat = jnp.concatenate([t, jnp.cos(fb * wpos), -jnp.sin(fb * wpos)], axis=-1)
    feat = jnp.pad(feat, ((0, 0), (0, V7X_LANES - emb)))
    w1p = jnp.pad(w1, ((0, V7X_LANES - emb), (0, 0)))
    max_decay = math.log(HY_TARGET) / HY_SHORT_PCT
    min_decay = math.log(HY_TARGET) / HY_LONG_PCT
    deltas = jnp.abs(jnp.linspace(min_decay, max_decay, d, dtype=F32))[None, :]
    tc = _tile(d, 256, V7X_LANES)
    nt = d // tc
    full = lambda r, c: pl.BlockSpec((r, c), lambda n, j: (0, 0))
    out_spec = pl.BlockSpec((None, seq, tc), lambda n, j: (n, 0, j))
    return pl.pallas_call(
        _filter_kernel,
        out_shape=(jax.ShapeDtypeStruct((HY_ORDER, seq, d), F32),) * 2,
        grid=(HY_ORDER, nt),
        in_specs=[full(seq, V7X_LANES), full(V7X_LANES, hid), full(1, hid), full(hid, hid), full(1, hid),
                  full(1, hid),
                  pl.BlockSpec((hid, tc), lambda n, j: (0, (2 * n) * nt + j)),
                  pl.BlockSpec((hid, tc), lambda n, j: (0, (2 * n + 1) * nt + j)),
                  pl.BlockSpec((1, tc), lambda n, j: (0, j))] + [full(seq, seq // 2)] * 4,
        out_specs=(out_spec, out_spec),
        scratch_shapes=[pltpu.VMEM((seq, V7X_LANES), F32)],
        compiler_params=_cparams("arbitrary", "arbitrary"),
        name="hyena_filter",
    )(feat, w1p, b1.reshape(1, hid), w2, b2.reshape(1, hid), freq.reshape(1, hid), w3, w3, deltas,
      *dft_parts)


HYENA_ROWS_PER_STEP = 2048


def _hyconv_kernel(*refs, conv_z, conv_k, has_dst):
    (z_ref, x_ref, hre_ref, him_ref, skip_ref, cz_ref, cx_ref,
     fwd_e_ref, fwd_o_ref, inv_e_ref, inv_o_ref) = refs[:11]
    o_ref = refs[11 + has_dst]
    zs_s, xs_s, os_s = refs[-3:]
    seq = hre_ref.shape[0]
    half = seq // 2
    n_sub = z_ref.shape[0] // seq
    even, odd = pl.ds(0, half, stride=2), pl.ds(1, half, stride=2)
    n_lane_tiles = zs_s.shape[1]

    def stage(dst, s, val):
        for l in range(n_lane_tiles):
            dst[s, l] = val[:, l * V7X_LANES:(l + 1) * V7X_LANES]

    def rows_of(src, s, rows):
        return jnp.concatenate([src[s, l, rows, :] for l in range(n_lane_tiles)], axis=1)

    for s in range(n_sub):
        z = z_ref[s * seq:(s + 1) * seq, :].astype(F32)
        stage(zs_s, s, _dwconv(z, cz_ref[...], conv_k) if conv_z else z)
        stage(xs_s, s, _dwconv(x_ref[s * seq:(s + 1) * seq, :].astype(F32), cx_ref[...], conv_k))
    z_parts = [(rows_of(zs_s, s, even), rows_of(zs_s, s, odd)) for s in range(n_sub)]
    fwds = [(jnp.dot(fwd_e_ref[...], ze.astype(BF16), preferred_element_type=F32),
             jnp.dot(fwd_o_ref[...], zo.astype(BF16), preferred_element_type=F32)) for ze, zo in z_parts]
    hre_a, hre_b = hre_ref[0:half, :], hre_ref[half:, :]
    him_a, him_b = him_ref[0:half, :], him_ref[half:, :]
    skip = skip_ref[...]
    for s in range(n_sub):
        (fe, fo), (ze, zo) = fwds[s], z_parts[s]
        pe, qe, po, qo = fe[:half], fe[half:], fo[:half], fo[half:]
        pa, pb, qa, qb = pe + po, pe - po, qe + qo, qo - qe
        yre_a, yim_a = pa * hre_a + qa * him_a, pa * him_a - qa * hre_a
        yre_b, yim_b = pb * hre_b + qb * him_b, pb * him_b - qb * hre_b
        y_e = jnp.dot(inv_e_ref[...], jnp.concatenate([yre_a + yre_b, yim_a - yim_b], axis=0).astype(BF16),
                      preferred_element_type=F32)
        y_o = jnp.dot(inv_o_ref[...], jnp.concatenate([yre_a - yre_b, yim_a + yim_b], axis=0).astype(BF16),
                      preferred_element_type=F32)
        out_e = rows_of(xs_s, s, even) * (y_e + skip * ze)
        out_o = rows_of(xs_s, s, odd) * (y_o + skip * zo)
        for l in range(n_lane_tiles):
            lanes = slice(l * V7X_LANES, (l + 1) * V7X_LANES)
            os_s[s, l, even, :] = out_e[:, lanes]
            os_s[s, l, odd, :] = out_o[:, lanes]
        o_ref[s * seq:(s + 1) * seq, :] = rows_of(os_s, s, slice(None)).astype(o_ref.dtype)


def _hyconv(z, z_col0, conv_z, proj, x_col0, hre, him, order, skip, conv_w, dft, dst,
            *, row0, n_seq, seq, d, out_rows, out_row0, out_dtype):
    conv_k = conv_w.shape[0]
    n_sub = max(s for s in range(1, max(1, HYENA_ROWS_PER_STEP // seq) + 1)
                if n_seq % s == 0 and row0 % (s * seq) == 0 and out_row0 % (s * seq) == 0)
    blk = n_sub * seq
    tc = _tile(d, 512 if blk <= 1024 else 256, V7X_LANES)
    nt = d // tc
    r0 = row0 // blk
    zr0 = r0 if conv_z else 0
    zc, xc = z_col0 // tc, x_col0 // tc
    in_specs = [pl.BlockSpec((blk, tc), lambda j, b: (b + zr0, zc + j)),
                pl.BlockSpec((blk, tc), lambda j, b: (b + r0, xc + j)),
                pl.BlockSpec((None, seq, tc), lambda j, b: (order, 0, j)),
                pl.BlockSpec((None, seq, tc), lambda j, b: (order, 0, j)),
                pl.BlockSpec((1, tc), lambda j, b: (0, j)),
                pl.BlockSpec((conv_k, tc), lambda j, b: (0, zc + j)),
                pl.BlockSpec((conv_k, tc), lambda j, b: (0, xc + j)),
                pl.BlockSpec((seq, seq // 2), lambda j, b: (0, 0), pipeline_mode=_RESIDENT),
                pl.BlockSpec((seq, seq // 2), lambda j, b: (0, 0), pipeline_mode=_RESIDENT),
                pl.BlockSpec((seq // 2, seq), lambda j, b: (0, 0), pipeline_mode=_RESIDENT),
                pl.BlockSpec((seq // 2, seq), lambda j, b: (0, 0), pipeline_mode=_RESIDENT)]
    args = [z, proj, hre, him, skip.reshape(1, d), conv_w, conv_w, *dft]
    aliases = {}
    if dst is not None:
        in_specs.append(pl.BlockSpec(memory_space=pl.ANY))
        args.append(dst)
        aliases = {len(args) - 1: 0}
    out_r0 = out_row0 // blk
    return pl.pallas_call(
        functools.partial(_hyconv_kernel, conv_z=conv_z, conv_k=conv_k, has_dst=dst is not None),
        out_shape=jax.ShapeDtypeStruct((out_rows, d), out_dtype),
        grid=(nt, n_seq // n_sub),
        in_specs=in_specs,
        out_specs=pl.BlockSpec((blk, tc), lambda j, b: (b + out_r0, j)),
        scratch_shapes=[pltpu.VMEM((n_sub, tc // V7X_LANES, seq, V7X_LANES), F32)] * 3,
        input_output_aliases=aliases,
        compiler_params=_cparams("parallel", "parallel"),
        name="hyena_conv",
    )(*args)


def _grid_pos_emb(n_tokens, d):
    rows = n_tokens // GRID_W
    r, col = jnp.meshgrid(jnp.arange(rows), jnp.arange(GRID_W), indexing='ij')
    quarter = d // 4
    omega = 1.0 / (POS_BASE ** (jnp.arange(quarter, dtype=F32) / quarter))

    def emb1d(p):
        a = p.reshape(-1, 1).astype(F32) * omega[None, :]
        return jnp.concatenate([jnp.sin(a), jnp.cos(a)], axis=-1)

    return jnp.concatenate([emb1d(r), emb1d(col)], axis=-1)


def kernel(x_prompt, x_sample, state_delta, c, c_ctx, ada_w, ada_b, norm1_g, norm2_g, gdn_w_in, gdn_conv, gdn_a_log, gdn_dt_bias, gdn_onorm, gdn_w_out, hy_w_in, hy_b_in, hy_conv, hy_f_w1, hy_f_b1, hy_f_w2, hy_f_b2, hy_f_w3, hy_freq, hy_skip, hy_w_out, hy_b_out, ffn_w_gu, ffn_w_down, final_g):
    bc, lc, d = x_prompt.shape
    bl, ll, _ = x_sample.shape
    depth = ada_w.shape[0]
    n_heads, dk, dv = state_delta.shape[3:]
    tc_rows, tl_rows = bc * lc, bl * ll
    t = tc_rows + tl_rows
    assert tc_rows % ll == 0 and ll % lc == 0
    rows = _Rows(tc_rows, ll, t, _tile(math.gcd(tc_rows, ll), 512, V7X_SUBLANES))

    bm = 1 + bl
    bm_pad = -(-bm // V7X_SUBLANES) * V7X_SUBLANES
    cvec = jnp.concatenate([c_ctx[None, :], c, jnp.zeros((bm_pad - bm, d), F32)], axis=0)
    mods_all = _ada(cvec, ada_w, ada_b).reshape(depth, bm_pad, 6, d)

    x = _embed(x_prompt.reshape(tc_rows, d), x_sample.reshape(tl_rows, d), _grid_pos_emb(ll, d), rows)

    n_hy = hy_w_in.shape[0]
    dft = {}
    for seq in (lc, ll):
        half = seq // 2
        fwd_e, fwd_o = _folded_odd_dft(seq)
        (fe_hi, fe_lo), (fo_hi, fo_lo) = _bf16_head_tail(fwd_e), _bf16_head_tail(fwd_o)
        inv_e, _ = _bf16_head_tail(np.concatenate([fwd_e[:half].T, -fwd_e[half:].T], axis=1) / seq)
        inv_o, _ = _bf16_head_tail(np.concatenate([fwd_o[:half].T, -fwd_o[half:].T], axis=1) / seq)
        spectra = [_hyena_filters(seq, d, hy_f_w1[j], hy_f_b1[j], hy_f_w2[j], hy_f_b2[j], hy_f_w3[j],
                                  hy_freq[j], (fe_hi, fe_lo, fo_hi, fo_lo)) for j in range(n_hy)]
        dft[seq] = ((fe_hi, fo_hi, inv_e, inv_o), spectra)

    zero_b = jnp.zeros((d,), F32)
    mixed = jnp.zeros((t, n_heads * dv), BF16)
    shared = n_heads * dv == d
    mixed_hy = mixed if shared else jnp.zeros((t, d), BF16)
    new_state_delta = jnp.zeros((bc, gdn_w_in.shape[0], 2, n_heads, dk, dv), F32)
    n_mixers = 2
    n_gdn = gdn_w_in.shape[0]
    gdn_w_in_b = jnp.pad(gdn_w_in.astype(BF16), ((0, 0), (0, 0), (0, V7X_LANES - 4 * n_heads)))
    gdn_w_out_b, hy_w_in_b, hy_w_out_b = (w.astype(BF16) for w in (gdn_w_out, hy_w_in, hy_w_out))
    ffn_w_gu_b, ffn_w_down_b = ffn_w_gu.astype(BF16), ffn_w_down.astype(BF16)
    for layer in range(depth):
        j = layer // n_mixers
        if layer % n_mixers == 0:
            proj, ab = _in_proj(x, norm1_g[layer], mods_all, layer, gdn_w_in_b, j,
                                jnp.zeros((gdn_w_in_b.shape[2],), F32), rows, n_side=V7X_LANES)
            gdn = dict(t_total=t, n_heads=n_heads, dk=dk, dv=dv)
            weights = (gdn_conv[j], gdn_a_log[j], gdn_dt_bias[j], gdn_onorm[j])
            mixed, new_state_delta = _gdn_core(proj, ab, *weights, None, mixed, new_state_delta, row0=0, n_seq=bc,
                                               seq=lc, s0_spec=None, state_slot=(j, n_gdn), **gdn)
            s0_spec = lambda hb, j=j: pl.BlockSpec((None, None, 2, hb, dk, dv), lambda b, h: (b, j, 0, h, 0, 0))
            mixed, _ = _gdn_core(proj, ab, *weights, state_delta, mixed, None, row0=tc_rows, n_seq=bl, seq=ll,
                                 s0_spec=s0_spec, state_slot=(0, 1), **gdn)
            mixer = (mixed, gdn_w_out_b, j, zero_b)
            mixed_hy = mixed if shared else mixed_hy
        else:
            proj = _in_proj(x, norm1_g[layer], mods_all, layer, hy_w_in_b, j, hy_b_in[j], rows)
            for row0, n_seq, seq in ((0, bc, lc), (tc_rows, bl, ll)):
                dft_mats, spectra = dft[seq]
                hre, him = spectra[j]
                z = proj
                for n in range(HY_ORDER):
                    last = n == HY_ORDER - 1
                    z = _hyconv(z, 0, n == 0, proj, (n + 1) * d, hre, him, n, hy_skip[j, n], hy_conv[j],
                                dft_mats, mixed_hy if last else None, row0=row0, n_seq=n_seq, seq=seq, d=d,
                                out_rows=t if last else n_seq * seq, out_row0=row0 if last else 0,
                                out_dtype=BF16 if last else F32)
                mixed_hy = z
            mixer = (mixed_hy, hy_w_out_b, j, hy_b_out[j])
            mixed = mixed_hy if shared else mixed
        ffn = functools.partial(_ffn, x, norm2_g[layer], mods_all, ffn_w_gu_b, ffn_w_down_b, layer, rows,
                                mixer=mixer)
        if layer < depth - 1:
            x = ffn()
        else:
            y_prompt = ffn(final_g=final_g, tile0=0, n_tiles=rows.n_ctx_tiles)
            y_sample = ffn(final_g=final_g, tile0=rows.n_ctx_tiles, n_tiles=rows.n_tiles - rows.n_ctx_tiles)
    return (y_prompt.reshape(bc, lc, d), y_sample.reshape(bl, ll, d), new_state_delta)
```

```python
import functools
import math

import jax
import jax.numpy as jnp
import numpy as np
from jax import lax
from jax.experimental import pallas as pl
from jax.experimental.pallas import tpu as pltpu

GRID_W = 64
CHUNK = 64
HY_ORDER = 2
HY_TARGET = 1e-2
HY_SHORT_PCT = 0.3
HY_LONG_PCT = 1.5
POS_BASE = 10000.0
EPS = 1e-6

V7X_LANES = 128
V7X_SUBLANES = 8
V7X_VMEM_LIMIT_BYTES = 48 * 1024 * 1024

BF16 = jnp.bfloat16
F32 = jnp.float32


def _cparams(*sem):
    return pltpu.CompilerParams(dimension_semantics=sem, vmem_limit_bytes=V7X_VMEM_LIMIT_BYTES)


def _tile(n, target, align):
    if n <= target:
        return n
    best = None
    for t in range(align, target + 1, align):
        if n % t == 0:
            best = t
    assert best is not None, (n, target, align)
    return best


def _dot(a, b):
    return jnp.dot(a.astype(BF16), b.astype(BF16), preferred_element_type=F32)


def _dot_nt(a, b):
    return lax.dot_general(a.astype(BF16), b.astype(BF16), (((1,), (1,)), ((), ())),
                           preferred_element_type=F32)


def _silu(x):
    return x * jax.nn.sigmoid(x)


def _norm_mod(x, g, shift, scale):
    ms = jnp.mean(x * x, axis=-1, keepdims=True)
    return (x * lax.rsqrt(ms + EPS) * g) * (1.0 + scale) + shift


class _Rows:
    def __init__(self, tc, ll, t, tm):
        assert tc % tm == 0 and ll % tm == 0 and t % tm == 0
        self.n_ctx_tiles = tc // tm
        self.tiles_per_lat = ll // tm
        self.n_tiles = t // tm
        self.tm = tm

    def mod_index(self, i):
        lat = 1 + (i - self.n_ctx_tiles) // self.tiles_per_lat
        return jnp.where(i < self.n_ctx_tiles, 0, lat)


def _ada_kernel(c_ref, w_ref, b_ref, o_ref):
    o_ref[...] = _dot(_silu(c_ref[...]), w_ref[...]) + b_ref[...]


def _ada(cvec, ada_w, ada_b):
    depth, d, n = ada_w.shape
    bm = cvec.shape[0]
    tn = _tile(n, 1536, V7X_LANES)
    return pl.pallas_call(
        _ada_kernel,
        out_shape=jax.ShapeDtypeStruct((depth, bm, n), F32),
        grid=(depth, n // tn),
        in_specs=[pl.BlockSpec((bm, d), lambda l, j: (0, 0)),
                  pl.BlockSpec((None, d, tn), lambda l, j: (l, 0, j)),
                  pl.BlockSpec((None, 1, tn), lambda l, j: (l, 0, j))],
        out_specs=pl.BlockSpec((None, bm, tn), lambda l, j: (l, 0, j)),
        compiler_params=_cparams("parallel", "parallel"),
        name="ada",
    )(cvec, ada_w, ada_b.reshape(depth, 1, n))


def _embed_kernel(xp_ref, xs_ref, pos_ref, o_ref, *, n_ctx_tiles):
    i = pl.program_id(0)

    @pl.when(i < n_ctx_tiles)
    def _():
        o_ref[...] = xp_ref[...]

    @pl.when(i >= n_ctx_tiles)
    def _():
        o_ref[...] = xs_ref[...] + pos_ref[...]


def _embed(xp, xs, pos, rows):
    t, d = xp.shape[0] + xs.shape[0], xp.shape[1]
    tm, nct = rows.tm, rows.n_ctx_tiles
    npos = pos.shape[0] // tm
    return pl.pallas_call(
        functools.partial(_embed_kernel, n_ctx_tiles=nct),
        out_shape=jax.ShapeDtypeStruct((t, d), F32),
        grid=(rows.n_tiles,),
        in_specs=[pl.BlockSpec((tm, d), lambda i: (jnp.minimum(i, nct - 1), 0)),
                  pl.BlockSpec((tm, d), lambda i: (jnp.maximum(i - nct, 0), 0)),
                  pl.BlockSpec((tm, d), lambda i: (jnp.maximum(i - nct, 0) % npos, 0))],
        out_specs=pl.BlockSpec((tm, d), lambda i: (i, 0)),
        compiler_params=_cparams("parallel"),
        name="embed",
    )(xp, xs, pos)


_RESIDENT = pl.Buffered(1)


def _in_kernel(x_ref, g_ref, mod_ref, w_ref, b_ref, o_ref, *side_ref, chunk):
    m = mod_ref[...]
    h = _norm_mod(x_ref[...], g_ref[...], m[0:1, :], m[1:2, :]).astype(BF16)
    n = o_ref.shape[1]
    for c0 in range(0, n, chunk):
        cols = slice(c0, c0 + chunk)
        y = jnp.dot(h, w_ref[:, cols], preferred_element_type=F32) + b_ref[:, cols]
        o_ref[:, cols] = y.astype(o_ref.dtype)
    if side_ref:
        side_ref[0][...] = jnp.dot(h, w_ref[:, n:], preferred_element_type=F32) + b_ref[:, n:]


def _in_proj(x, g, mods_all, layer, w_all, w_index, b, rows, n_side=0):
    t, d = x.shape
    n = w_all.shape[2] - n_side
    tm = rows.tm
    chunk = _tile(n, 512, V7X_LANES)
    out_shape = [jax.ShapeDtypeStruct((t, n), BF16)]
    out_specs = [pl.BlockSpec((tm, n), lambda i: (i, 0))]
    if n_side:
        out_shape.append(jax.ShapeDtypeStruct((t, n_side), F32))
        out_specs.append(pl.BlockSpec((tm, n_side), lambda i: (i, 0)))
    out = pl.pallas_call(
        functools.partial(_in_kernel, chunk=chunk),
        out_shape=out_shape,
        grid=(rows.n_tiles,),
        in_specs=[pl.BlockSpec((tm, d), lambda i: (i, 0)),
                  pl.BlockSpec((1, d), lambda i: (0, 0), pipeline_mode=_RESIDENT),
                  pl.BlockSpec((None, None, 6, d), lambda i: (layer, rows.mod_index(i), 0, 0)),
                  pl.BlockSpec((None, d, n + n_side), lambda i: (w_index, 0, 0), pipeline_mode=_RESIDENT),
                  pl.BlockSpec((1, n + n_side), lambda i: (0, 0), pipeline_mode=_RESIDENT)],
        out_specs=out_specs,
        compiler_params=_cparams("parallel"),
        name="in_proj",
    )(x, g.reshape(1, d), mods_all, w_all, b.reshape(1, n + n_side))
    return out if n_side else out[0]


def _ffn_kernel(x_ref, g_ref, mod_ref, wgu_ref, wd_ref, *rest, chunk, has_mixer, has_final):
    o_ref = rest[-1]
    x = x_ref[...]
    m = mod_ref[...]
    if has_mixer:
        a_ref, wo_ref, bo_ref = rest[:3]
        x = x + m[2:3, :] * (_dot(a_ref[...], wo_ref[...]) + bo_ref[...])
    h = _norm_mod(x, g_ref[...], m[3:4, :], m[4:5, :]).astype(BF16)
    f = wd_ref.shape[0]
    n_chunks = f // chunk

    def gate_up(k):
        gate = jnp.dot(h, wgu_ref[:, k * chunk:(k + 1) * chunk], preferred_element_type=F32)
        up = jnp.dot(h, wgu_ref[:, f + k * chunk:f + (k + 1) * chunk], preferred_element_type=F32)
        return gate, up

    y = None
    pending = gate_up(0)
    for k in range(n_chunks):
        following = gate_up(k + 1) if k + 1 < n_chunks else None
        act = (_silu(pending[0]) * pending[1]).astype(BF16)
        part = jnp.dot(act, wd_ref[k * chunk:(k + 1) * chunk, :], preferred_element_type=F32)
        y = part if y is None else y + part
        pending = following
    out = x + m[5:6, :] * y
    if has_final:
        out = out * lax.rsqrt(jnp.mean(out * out, axis=-1, keepdims=True) + EPS) * rest[-2][...]
    o_ref[...] = out


def _ffn(x, g, mods_all, w_gu_all, w_down_all, layer, rows, mixer=None, final_g=None, tile0=0, n_tiles=None):
    d = x.shape[1]
    f = w_down_all.shape[1]
    tm = rows.tm
    n_tiles = rows.n_tiles if n_tiles is None else n_tiles
    chunk = _tile(f, 256, V7X_LANES)
    resident = lambda *s: pl.BlockSpec(s, lambda i: (0,) * len(s), pipeline_mode=_RESIDENT)
    in_specs = [pl.BlockSpec((tm, d), lambda i: (i + tile0, 0)),
                resident(1, d),
                pl.BlockSpec((None, None, 6, d), lambda i: (layer, rows.mod_index(i + tile0), 0, 0)),
                pl.BlockSpec((None, d, 2 * f), lambda i: (layer, 0, 0), pipeline_mode=_RESIDENT),
                pl.BlockSpec((None, f, d), lambda i: (layer, 0, 0), pipeline_mode=_RESIDENT)]
    args = [x, g.reshape(1, d), mods_all, w_gu_all, w_down_all]
    if mixer is not None:
        a, wo_all, wo_index, bo = mixer
        k = a.shape[1]
        in_specs += [pl.BlockSpec((tm, k), lambda i: (i + tile0, 0)),
                     pl.BlockSpec((None, k, d), lambda i: (wo_index, 0, 0), pipeline_mode=_RESIDENT),
                     resident(1, d)]
        args += [a, wo_all, bo.reshape(1, d)]
    if final_g is not None:
        in_specs.append(resident(1, d))
        args.append(final_g.reshape(1, d))
    return pl.pallas_call(
        functools.partial(_ffn_kernel, chunk=chunk, has_mixer=mixer is not None, has_final=final_g is not None),
        out_shape=jax.ShapeDtypeStruct((n_tiles * tm, d), F32),
        grid=(n_tiles,),
        in_specs=in_specs,
        out_specs=pl.BlockSpec((tm, d), lambda i: (i, 0)),
        compiler_params=_cparams("parallel"),
        name="ffn",
    )(*args)


def _dwconv(x, w, k):
    n = x.shape[0]
    half = k // 2
    e = V7X_SUBLANES
    assert half <= e and n >= 4 * e
    taps = [w[j:j + 1, :] for j in range(k)]
    shifts = [s for s in range(-half, half + 1) if s != 0]

    def conv(v, mask):
        acc = v * taps[half]
        for s in shifts:
            shifted = pltpu.roll(v, (-s) % v.shape[0], axis=0)
            acc = acc + (shifted if mask is None else jnp.where(mask(s), shifted, 0.0)) * taps[s + half]
        return acc

    row = lax.broadcasted_iota(jnp.int32, (2 * e, x.shape[1]), 0)
    top = conv(x[:2 * e], lambda s: row + s >= 0)[:e]
    bottom = conv(x[n - 2 * e:], lambda s: row + s < 2 * e)[e:]
    return jnp.concatenate([top, conv(x, None)[e:n - e], bottom], axis=0)


CONV_HALO = 16


def _dwconv_window(ref, cols, w, k, r0, rows, slab_ref):
    n = ref.shape[0]
    if rows == n:
        return _dwconv(ref[:, cols].astype(F32), w, k)
    half = k // 2
    h = CONV_HALO
    assert half <= h <= rows and n % rows == 0
    before = ref[pl.ds(pl.multiple_of(jnp.maximum(r0 - h, 0), h), h), cols].astype(F32)
    after = ref[pl.ds(pl.multiple_of(jnp.minimum(r0 + rows, n - h), h), h), cols].astype(F32)
    slab_ref[0:h, :] = jnp.where(r0 > 0, before, 0.0)
    slab_ref[h:h + rows, :] = ref[pl.ds(r0, rows), cols].astype(F32)
    slab_ref[h + rows:, :] = jnp.where(r0 + rows < n, after, 0.0)
    acc = None
    for s in range(-half, half + 1):
        term = slab_ref[h + s:h + s + rows, :] * w[s + half:s + half + 1, :]
        acc = term if acc is None else acc + term
    return acc


TRI_BASE = 8
GDN_PHASE1_CHAINS = 32
GDN_HEADS_PER_STEP = 8
GDN_ROWS_PER_STEP = 4096


def _unit_tri_inverses_minus_eye(mats, ri, ci):
    c = mats[0].shape[0]

    def same_block(s):
        sh = int(math.log2(s))
        return (ri >> sh) == (ci >> sh)

    ps = [jnp.where(same_block(TRI_BASE), -a, 0.0) for a in mats]
    es = ps
    n_lvl = int(math.log2(TRI_BASE))
    for lvl in range(n_lvl):
        es = [e + _dot(p, e) for p, e in zip(ps, es)]
        if lvl < n_lvl - 1:
            ps = [_dot(p, p) for p in ps]
    s = TRI_BASE
    while s < c:
        mask = jnp.logical_and(same_block(2 * s), jnp.logical_not(same_block(s)))
        offs = [jnp.where(mask, a, 0.0) for a in mats]
        ys = [off + _dot(off, e) for off, e in zip(offs, es)]
        es = [e - (y + _dot(e, y)) for e, y in zip(es, ys)]
        s *= 2
    return es


def _chunk_cumsum(x, pos, reverse):
    n = x.shape[0]
    s = 1
    while s < CHUNK:
        if reverse:
            x = x + jnp.where(pos + s < CHUNK, pltpu.roll(x, n - s, axis=0), 0.0)
        else:
            x = x + jnp.where(pos >= s, pltpu.roll(x, s, axis=0), 0.0)
        s *= 2
    return x


def _gdn_kernel(*refs, n_heads, conv_k, lockstep, n_kept):
    (alog_ref, dtb_ref, q_ref, k_ref, v_ref, gt_ref, ab_ref, cq_ref, ck_ref, cv_ref,
     onorm_ref, s0_ref) = refs[:12]
    o_ref, sfin_ref, gates_s, conv_s, w_s, u_s, qd_s, ak_s, gl_s, o_s, st_s = refs[12 + n_kept:]
    n_chains, seq, dk = w_s.shape
    hb = n_chains // 2
    head0 = pl.program_id(1) * hb
    c = CHUNK
    n_chunks = seq // c
    chains = [(hh, d) for hh in range(hb) for d in range(2)]

    @pl.when(head0 == 0)
    def _():
        ab = ab_ref[...]
        pos = jnp.bitwise_and(lax.broadcasted_iota(jnp.int32, ab.shape, 0), c - 1)
        g = -jnp.exp(alog_ref[...]) * jax.nn.softplus(ab + dtb_ref[...])
        g_fwd = _chunk_cumsum(g, pos, reverse=False)
        g_rev = _chunk_cumsum(g, pos, reverse=True)
        gates_s[0] = g_fwd
        gates_s[1] = g_rev - g
        gates_s[2] = g_rev
        gates_s[3] = g_fwd - g
        gates_s[4] = jax.nn.sigmoid(ab)

    lane = lax.broadcasted_iota(jnp.int32, (1, V7X_LANES), 1)

    rows = lockstep * c

    def column(i, idx, win):
        one_hot = (lane == idx).astype(F32)
        col = jnp.sum(gates_s[i, win, :] * one_hot, axis=1, keepdims=True)
        return jnp.broadcast_to(col, (rows, V7X_LANES))

    ri = lax.broadcasted_iota(jnp.int32, (c, c), 0)
    ci = lax.broadcasted_iota(jnp.int32, (c, c), 1)
    incl = (ri >= ci, ri <= ci)
    strict = (ri > ci, ri < ci)

    def phase1(it, carry):
        r0 = pl.multiple_of(it * rows, rows)
        win = pl.ds(r0, rows)
        prepared = []
        for hh in range(hb):
            cols = slice(hh * dk, (hh + 1) * dk)
            q = _silu(_dwconv_window(q_ref, cols, cq_ref[:, cols], conv_k, r0, rows, conv_s.at[3 * hh]))
            k = _silu(_dwconv_window(k_ref, cols, ck_ref[:, cols], conv_k, r0, rows, conv_s.at[3 * hh + 1]))
            v = _silu(_dwconv_window(v_ref, cols, cv_ref[:, cols], conv_k, r0, rows, conv_s.at[3 * hh + 2]))
            q = q * lax.rsqrt(jnp.sum(q * q, axis=-1, keepdims=True) + EPS) * (dk ** -0.5)
            k = k * lax.rsqrt(jnp.sum(k * k, axis=-1, keepdims=True) + EPS)
            for d in range(2):
                head = head0 + hh
                g_cum = column(2 * d, d * n_heads + head, win)
                g_tail = column(2 * d + 1, d * n_heads + head, win)
                beta = column(4, 2 * n_heads + d * n_heads + head, win)
                e_cum = jnp.exp(g_cum)
                kb = k * beta
                prepared.append((q, k, g_cum, kb, kb * e_cum, v * beta, k * jnp.exp(g_tail)))
                qd_s[2 * hh + d, win, :] = (q * e_cum).astype(qd_s.dtype)
                g_tot = jnp.exp(g_cum + g_tail)
                for gi in range(lockstep):
                    gl_s[2 * hh + d, pl.ds(it * lockstep + gi, 1), :] = g_tot[gi * c:gi * c + 1, :]
        items = []
        for gi in range(lockstep):
            part = slice(gi * c, (gi + 1) * c)
            for ch, (_, d) in enumerate(chains):
                sl = pl.ds(pl.multiple_of(r0 + gi * c, c), c)
                items.append((it * lockstep + gi, sl, ch, d) + tuple(a[part] for a in prepared[ch]))
        decays, kqs = [], []
        for ic, sl, ch, d, qc, kc, gc, kb, w0, u0, ktl in items:
            diff = gc[:, :c] - gc.T[:c, :]
            decays.append(jnp.where(incl[d], jnp.exp(jnp.where(incl[d], diff, 0.0)), 0.0))
            kqs.append(_dot_nt(jnp.concatenate([kb, qc], axis=0), kc))
        a_kks = [jnp.where(strict[item[3]], kq[:c] * decay, 0.0)
                 for item, kq, decay in zip(items, kqs, decays)]
        es = _unit_tri_inverses_minus_eye(a_kks, ri, ci)
        rhss = [jnp.concatenate([item[8], item[9]], axis=1) for item in items]
        wus = [rhs + _dot(e, rhs) for e, rhs in zip(es, rhss)]
        for item, kq, decay, wu in zip(items, kqs, decays, wus):
            ic, sl, ch = item[:3]
            w_s[ch, sl, :] = wu[:, :dk].astype(w_s.dtype)
            u_s[ch, sl, :] = wu[:, dk:]
            ak_s[ch, ic] = jnp.concatenate([kq[c:] * decay, item[10].T], axis=0).astype(ak_s.dtype)
        return carry

    lax.fori_loop(0, n_chunks // lockstep, phase1, 0)

    for hh, d in chains:
        st_s[2 * hh + d] = s0_ref[d, hh]

    def phase2(i, carry):
        n = 2 * hb
        ics = [n_chunks - 1 - i if d else i for _, d in chains]
        r0s = [pl.multiple_of(ic * c, c) for ic in ics]
        sls = [pl.ds(r0, c) for r0 in r0s]
        ss = [st_s[ch] for ch in range(n)]
        wqs = [_dot(jnp.concatenate([w_s[ch, sls[ch], :], qd_s[ch, sls[ch], :]], axis=0), ss[ch]) for ch in range(n)]
        v_news = [u_s[ch, sls[ch], :] - wqs[ch][:c] for ch in range(n)]
        outs = [_dot(ak_s[ch, ics[ch]], v_news[ch]) for ch in range(n)]
        for ch in range(n):
            o_s[ch, sls[ch], :] = wqs[ch][c:] + outs[ch][:c]
            st_s[ch] = ss[ch] * gl_s[ch, pl.ds(ics[ch], 1), :] + outs[ch][c:]
        return carry

    lax.fori_loop(0, n_chunks, phase2, 0, unroll=4)

    for hh, d in chains:
        sfin_ref[d, hh] = st_s[2 * hh + d]
    for hh in range(hb):
        cols = slice(hh * dk, (hh + 1) * dk)
        o = o_s[2 * hh] + o_s[2 * hh + 1]
        o = o * lax.rsqrt(jnp.mean(o * o, axis=-1, keepdims=True) + EPS)
        o_ref[:, cols] = (o * onorm_ref[...] * _silu(gt_ref[:, cols].astype(F32))).astype(o_ref.dtype)


def _gdn_core(proj, ab, conv_w, a_log, dt_bias, onorm, s0, dst, states_dst, *, t_total, row0, n_seq, seq,
              n_heads, dk, dv, s0_spec, state_slot):
    assert dk == dv == V7X_LANES and row0 % seq == 0
    conv_k = conv_w.shape[0]
    r0 = row0 // seq
    n_chunks = seq // CHUNK
    hb = _tile(n_heads, min(GDN_HEADS_PER_STEP, max(1, GDN_ROWS_PER_STEP // seq)), 1)
    lockstep = _tile(n_chunks, max(1, GDN_PHASE1_CHAINS // (2 * hb)), 1)
    gate_pad = lambda p: jnp.pad(p.reshape(1, 2 * n_heads), ((0, 0), (0, V7X_LANES - 2 * n_heads)))
    lane_vec = pl.BlockSpec((1, V7X_LANES), lambda b, h: (0, 0))
    if s0 is None:
        s0 = jnp.zeros((2, hb, dk, dv), F32)
        s0_in = pl.BlockSpec((2, hb, dk, dv), lambda b, h: (0, 0, 0, 0))
    else:
        s0_in = s0_spec(hb)
    nb = n_heads // hb
    col = lambda sec: pl.BlockSpec((seq, hb * dk), lambda b, h: (b + r0, sec * nb + h))
    cw = lambda sec: pl.BlockSpec((conv_k, hb * dk), lambda b, h: (0, sec * nb + h))
    f32 = lambda *s: pltpu.VMEM(s, F32)
    in_specs = [lane_vec, lane_vec, col(0), col(1), col(2), col(3),
                pl.BlockSpec((seq, V7X_LANES), lambda b, h: (b + r0, 0)),
                cw(0), cw(1), cw(2),
                pl.BlockSpec((1, dv), lambda b, h: (0, 0)),
                s0_in]
    args = [gate_pad(a_log), gate_pad(dt_bias), proj, proj, proj, proj, ab, conv_w, conv_w, conv_w,
            onorm.reshape(1, dv), s0]
    aliases = {}
    for out_idx, kept in enumerate((dst, states_dst)):
        if kept is not None:
            in_specs.append(pl.BlockSpec(memory_space=pl.ANY))
            args.append(kept)
            aliases[len(args) - 1] = out_idx
    nc = 2 * hb
    slot, n_slots = state_slot
    return pl.pallas_call(
        functools.partial(_gdn_kernel, n_heads=n_heads, conv_k=conv_k, lockstep=lockstep,
                          n_kept=len(aliases)),
        out_shape=(jax.ShapeDtypeStruct((t_total, n_heads * dv), BF16),
                   jax.ShapeDtypeStruct((n_seq, n_slots, 2, n_heads, dk, dv), F32)),
        grid=(n_seq, nb),
        in_specs=in_specs,
        out_specs=(pl.BlockSpec((seq, hb * dv), lambda b, h: (b + r0, h)),
                   pl.BlockSpec((None, None, 2, hb, dk, dv), lambda b, h: (b, slot, 0, h, 0, 0))),
        scratch_shapes=[f32(5, seq, V7X_LANES),
                        f32(3 * hb, lockstep * CHUNK + 2 * CONV_HALO, dk),
                        pltpu.VMEM((nc, seq, dk), BF16), f32(nc, seq, dv), pltpu.VMEM((nc, seq, dk), BF16),
                        pltpu.VMEM((nc, n_chunks, CHUNK + dk, CHUNK), BF16),
                        f32(nc, max(n_chunks, V7X_SUBLANES), V7X_LANES),
                        f32(nc, seq, dv), f32(nc, dk, dv)],
        input_output_aliases=aliases,
        compiler_params=_cparams("parallel", "arbitrary"),
        name="gdn_core",
    )(*args)


def _folded_odd_dft(seq):
    half = seq // 2
    k = np.arange(half, dtype=np.int64)[:, None]
    parts = []
    for parity in (0, 1):
        m = 2 * np.arange(half, dtype=np.int64)[None, :] + parity
        ang = (((2 * k + 1) * m) % (4 * seq)) * (math.pi / (2 * seq))
        parts.append(np.concatenate([np.cos(ang), np.sin(ang)], axis=0))
    return parts


def _bf16_head_tail(m):
    m32 = m.astype(np.float32)
    head = m32.astype(jnp.bfloat16)
    tail = (m32 - head.astype(np.float32)).astype(jnp.bfloat16)
    return jnp.asarray(head), jnp.asarray(tail)


def _filter_kernel(feat_ref, w1_ref, b1_ref, w2_ref, b2_ref, fr_ref, w3f_ref, w3b_ref, dl_ref,
                   fe_hi_ref, fe_lo_ref, fo_hi_ref, fo_lo_ref, hre_ref, him_ref, hid_s):
    seq = feat_ref.shape[0]
    half = seq // 2
    n_hid = w2_ref.shape[0]

    @pl.when(jnp.logical_and(pl.program_id(0) == 0, pl.program_id(1) == 0))
    def _():
        fr = fr_ref[...]
        hid1 = jnp.sin(fr * (_dot(feat_ref[...], w1_ref[...]) + b1_ref[...]))
        hid_s[...] = jnp.zeros_like(hid_s)
        hid_s[:, :n_hid] = jnp.sin(fr * (_dot(hid1, w2_ref[...]) + b2_ref[...]))

    def taps(parity):
        lags = pl.ds(parity, half, stride=2)
        hid = hid_s[lags, :][:, :n_hid]
        window = jnp.exp(-feat_ref[lags, :][:, 0:1] * dl_ref[...])
        hf = _dot(hid, w3f_ref[...]) * window
        hb = _dot(hid, w3b_ref[...]) * window
        if parity == 0:
            hb = jnp.where(lax.broadcasted_iota(jnp.int32, hb.shape, 0) == 0, 0.0, hb)
        return hf + hb, hb - hf

    def dft(m_hi, m_lo, val):
        v_hi = val.astype(BF16)
        v_lo = (val - v_hi.astype(F32)).astype(BF16)
        return (jnp.dot(m_hi, v_hi, preferred_element_type=F32)
                + (jnp.dot(m_hi, v_lo, preferred_element_type=F32)
                   + jnp.dot(m_lo, v_hi, preferred_element_type=F32)))

    (sum_e, dif_e), (sum_o, dif_o) = taps(0), taps(1)
    cos_rows, sin_rows = slice(0, half), slice(half, seq)
    re_e = dft(fe_hi_ref[cos_rows, :], fe_lo_ref[cos_rows, :], sum_e)
    re_o = dft(fo_hi_ref[cos_rows, :], fo_lo_ref[cos_rows, :], sum_o)
    im_e = dft(fe_hi_ref[sin_rows, :], fe_lo_ref[sin_rows, :], dif_e)
    im_o = dft(fo_hi_ref[sin_rows, :], fo_lo_ref[sin_rows, :], dif_o)
    hre_ref[0:half, :] = re_e + re_o
    hre_ref[half:, :] = re_e - re_o
    him_ref[0:half, :] = im_e + im_o
    him_ref[half:, :] = im_o - im_e


def _hyena_filters(seq, d, w1, b1, w2, b2, w3, freq, dft_parts):
    emb, hid = w1.shape
    bands = (emb - 1) // 2
    t = jnp.linspace(0.0, 1.0, seq, dtype=F32)[:, None]
    wpos = (2.0 * math.pi / seq) * jnp.arange(seq, dtype=F32)[:, None]
    fb = jnp.linspace(1e-4, bands - 1, bands, dtype=F32)[None, :]
    feat = jnp.concatenate([t, jnp.cos(fb * wpos), -jnp.sin(fb * wpos)], axis=-1)
    feat = jnp.pad(feat, ((0, 0), (0, V7X_LANES - emb)))
    w1p = jnp.pad(w1, ((0, V7X_LANES - emb), (0, 0)))
    max_decay = math.log(HY_TARGET) / HY_SHORT_PCT
    min_decay = math.log(HY_TARGET) / HY_LONG_PCT
    deltas = jnp.abs(jnp.linspace(min_decay, max_decay, d, dtype=F32))[None, :]
    tc = _tile(d, 256, V7X_LANES)
    nt = d // tc
    full = lambda r, c: pl.BlockSpec((r, c), lambda n, j: (0, 0))
    out_spec = pl.BlockSpec((None, seq, tc), lambda n, j: (n, 0, j))
    return pl.pallas_call(
        _filter_kernel,
        out_shape=(jax.ShapeDtypeStruct((HY_ORDER, seq, d), F32),) * 2,
        grid=(HY_ORDER, nt),
        in_specs=[full(seq, V7X_LANES), full(V7X_LANES, hid), full(1, hid), full(hid, hid), full(1, hid),
                  full(1, hid),
                  pl.BlockSpec((hid, tc), lambda n, j: (0, (2 * n) * nt + j)),
                  pl.BlockSpec((hid, tc), lambda n, j: (0, (2 * n + 1) * nt + j)),
                  pl.BlockSpec((1, tc), lambda n, j: (0, j))] + [full(seq, seq // 2)] * 4,
        out_specs=(out_spec, out_spec),
        scratch_shapes=[pltpu.VMEM((seq, V7X_LANES), F32)],
        compiler_params=_cparams("arbitrary", "arbitrary"),
        name="hyena_filter",
    )(feat, w1p, b1.reshape(1, hid), w2, b2.reshape(1, hid), freq.reshape(1, hid), w3, w3, deltas,
      *dft_parts)


HYENA_ROWS_PER_STEP = 2048


def _hyconv_kernel(*refs, conv_z, conv_k, has_dst):
    (z_ref, x_ref, hre_ref, him_ref, skip_ref, cz_ref, cx_ref,
     fwd_e_ref, fwd_o_ref, inv_e_ref, inv_o_ref) = refs[:11]
    o_ref = refs[11 + has_dst]
    zs_s, xs_s, os_s = refs[-3:]
    seq = hre_ref.shape[0]
    half = seq // 2
    n_sub = z_ref.shape[0] // seq
    even, odd = pl.ds(0, half, stride=2), pl.ds(1, half, stride=2)
    n_lane_tiles = zs_s.shape[1]

    def stage(dst, s, val):
        for l in range(n_lane_tiles):
            dst[s, l] = val[:, l * V7X_LANES:(l + 1) * V7X_LANES]

    def rows_of(src, s, rows):
        return jnp.concatenate([src[s, l, rows, :] for l in range(n_lane_tiles)], axis=1)

    for s in range(n_sub):
        z = z_ref[s * seq:(s + 1) * seq, :].astype(F32)
        stage(zs_s, s, _dwconv(z, cz_ref[...], conv_k) if conv_z else z)
        stage(xs_s, s, _dwconv(x_ref[s * seq:(s + 1) * seq, :].astype(F32), cx_ref[...], conv_k))
    z_parts = [(rows_of(zs_s, s, even), rows_of(zs_s, s, odd)) for s in range(n_sub)]
    fwds = [(jnp.dot(fwd_e_ref[...], ze.astype(BF16), preferred_element_type=F32),
             jnp.dot(fwd_o_ref[...], zo.astype(BF16), preferred_element_type=F32)) for ze, zo in z_parts]
    hre_a, hre_b = hre_ref[0:half, :], hre_ref[half:, :]
    him_a, him_b = him_ref[0:half, :], him_ref[half:, :]
    skip = skip_ref[...]
    for s in range(n_sub):
        (fe, fo), (ze, zo) = fwds[s], z_parts[s]
        pe, qe, po, qo = fe[:half], fe[half:], fo[:half], fo[half:]
        pa, pb, qa, qb = pe + po, pe - po, qe + qo, qo - qe
        yre_a, yim_a = pa * hre_a + qa * him_a, pa * him_a - qa * hre_a
        yre_b, yim_b = pb * hre_b + qb * him_b, pb * him_b - qb * hre_b
        y_e = jnp.dot(inv_e_ref[...], jnp.concatenate([yre_a + yre_b, yim_a - yim_b], axis=0).astype(BF16),
                      preferred_element_type=F32)
        y_o = jnp.dot(inv_o_ref[...], jnp.concatenate([yre_a - yre_b, yim_a + yim_b], axis=0).astype(BF16),
                      preferred_element_type=F32)
        out_e = rows_of(xs_s, s, even) * (y_e + skip * ze)
        out_o = rows_of(xs_s, s, odd) * (y_o + skip * zo)
        for l in range(n_lane_tiles):
            lanes = slice(l * V7X_LANES, (l + 1) * V7X_LANES)
            os_s[s, l, even, :] = out_e[:, lanes]
            os_s[s, l, odd, :] = out_o[:, lanes]
        o_ref[s * seq:(s + 1) * seq, :] = rows_of(os_s, s, slice(None)).astype(o_ref.dtype)


def _hyconv(z, z_col0, conv_z, proj, x_col0, hre, him, order, skip, conv_w, dft, dst,
            *, row0, n_seq, seq, d, out_rows, out_row0, out_dtype):
    conv_k = conv_w.shape[0]
    n_sub = max(s for s in range(1, max(1, HYENA_ROWS_PER_STEP // seq) + 1)
                if n_seq % s == 0 and row0 % (s * seq) == 0 and out_row0 % (s * seq) == 0)
    blk = n_sub * seq
    tc = _tile(d, 512 if blk <= 1024 else 256, V7X_LANES)
    nt = d // tc
    r0 = row0 // blk
    zr0 = r0 if conv_z else 0
    zc, xc = z_col0 // tc, x_col0 // tc
    in_specs = [pl.BlockSpec((blk, tc), lambda j, b: (b + zr0, zc + j)),
                pl.BlockSpec((blk, tc), lambda j, b: (b + r0, xc + j)),
                pl.BlockSpec((None, seq, tc), lambda j, b: (order, 0, j)),
                pl.BlockSpec((None, seq, tc), lambda j, b: (order, 0, j)),
                pl.BlockSpec((1, tc), lambda j, b: (0, j)),
                pl.BlockSpec((conv_k, tc), lambda j, b: (0, zc + j)),
                pl.BlockSpec((conv_k, tc), lambda j, b: (0, xc + j)),
                pl.BlockSpec((seq, seq // 2), lambda j, b: (0, 0), pipeline_mode=_RESIDENT),
                pl.BlockSpec((seq, seq // 2), lambda j, b: (0, 0), pipeline_mode=_RESIDENT),
                pl.BlockSpec((seq // 2, seq), lambda j, b: (0, 0), pipeline_mode=_RESIDENT),
                pl.BlockSpec((seq // 2, seq), lambda j, b: (0, 0), pipeline_mode=_RESIDENT)]
    args = [z, proj, hre, him, skip.reshape(1, d), conv_w, conv_w, *dft]
    aliases = {}
    if dst is not None:
        in_specs.append(pl.BlockSpec(memory_space=pl.ANY))
        args.append(dst)
        aliases = {len(args) - 1: 0}
    out_r0 = out_row0 // blk
    return pl.pallas_call(
        functools.partial(_hyconv_kernel, conv_z=conv_z, conv_k=conv_k, has_dst=dst is not None),
        out_shape=jax.ShapeDtypeStruct((out_rows, d), out_dtype),
        grid=(nt, n_seq // n_sub),
        in_specs=in_specs,
        out_specs=pl.BlockSpec((blk, tc), lambda j, b: (b + out_r0, j)),
        scratch_shapes=[pltpu.VMEM((n_sub, tc // V7X_LANES, seq, V7X_LANES), F32)] * 3,
        input_output_aliases=aliases,
        compiler_params=_cparams("parallel", "parallel"),
        name="hyena_conv",
    )(*args)


def _grid_pos_emb(n_tokens, d):
    rows = n_tokens // GRID_W
    r, col = jnp.meshgrid(jnp.arange(rows), jnp.arange(GRID_W), indexing='ij')
    quarter = d // 4
    omega = 1.0 / (POS_BASE ** (jnp.arange(quarter, dtype=F32) / quarter))

    def emb1d(p):
        a = p.reshape(-1, 1).astype(F32) * omega[None, :]
        return jnp.concatenate([jnp.sin(a), jnp.cos(a)], axis=-1)

    return jnp.concatenate([emb1d(r), emb1d(col)], axis=-1)


def kernel(x_prompt, x_sample, state_delta, c, c_ctx, ada_w, ada_b, norm1_g, norm2_g, gdn_w_in, gdn_conv, gdn_a_log, gdn_dt_bias, gdn_onorm, gdn_w_out, hy_w_in, hy_b_in, hy_conv, hy_f_w1, hy_f_b1, hy_f_w2, hy_f_b2, hy_f_w3, hy_freq, hy_skip, hy_w_out, hy_b_out, ffn_w_gu, ffn_w_down, final_g):
    bc, lc, d = x_prompt.shape
    bl, ll, _ = x_sample.shape
    depth = ada_w.shape[0]
    n_heads, dk, dv = state_delta.shape[3:]
    tc_rows, tl_rows = bc * lc, bl * ll
    t = tc_rows + tl_rows
    assert tc_rows % ll == 0 and ll % lc == 0
    rows = _Rows(tc_rows, ll, t, _tile(math.gcd(tc_rows, ll), 512, V7X_SUBLANES))

    bm = 1 + bl
    bm_pad = -(-bm // V7X_SUBLANES) * V7X_SUBLANES
    cvec = jnp.concatenate([c_ctx[None, :], c, jnp.zeros((bm_pad - bm, d), F32)], axis=0)
    mods_all = _ada(cvec, ada_w, ada_b).reshape(depth, bm_pad, 6, d)

    x = _embed(x_prompt.reshape(tc_rows, d), x_sample.reshape(tl_rows, d), _grid_pos_emb(ll, d), rows)

    n_hy = hy_w_in.shape[0]
    dft = {}
    for seq in (lc, ll):
        half = seq // 2
        fwd_e, fwd_o = _folded_odd_dft(seq)
        (fe_hi, fe_lo), (fo_hi, fo_lo) = _bf16_head_tail(fwd_e), _bf16_head_tail(fwd_o)
        inv_e, _ = _bf16_head_tail(np.concatenate([fwd_e[:half].T, -fwd_e[half:].T], axis=1) / seq)
        inv_o, _ = _bf16_head_tail(np.concatenate([fwd_o[:half].T, -fwd_o[half:].T], axis=1) / seq)
        spectra = [_hyena_filters(seq, d, hy_f_w1[j], hy_f_b1[j], hy_f_w2[j], hy_f_b2[j], hy_f_w3[j],
                                  hy_freq[j], (fe_hi, fe_lo, fo_hi, fo_lo)) for j in range(n_hy)]
        dft[seq] = ((fe_hi, fo_hi, inv_e, inv_o), spectra)

    zero_b = jnp.zeros((d,), F32)
    mixed = jnp.zeros((t, n_heads * dv), BF16)
    shared = n_heads * dv == d
    mixed_hy = mixed if shared else jnp.zeros((t, d), BF16)
    new_state_delta = jnp.zeros((bc, gdn_w_in.shape[0], 2, n_heads, dk, dv), F32)
    n_mixers = 2
    n_gdn = gdn_w_in.shape[0]
    gdn_w_in_b = jnp.pad(gdn_w_in.astype(BF16), ((0, 0), (0, 0), (0, V7X_LANES - 4 * n_heads)))
    gdn_w_out_b, hy_w_in_b, hy_w_out_b = (w.astype(BF16) for w in (gdn_w_out, hy_w_in, hy_w_out))
    ffn_w_gu_b, ffn_w_down_b = ffn_w_gu.astype(BF16), ffn_w_down.astype(BF16)
    for layer in range(depth):
        j = layer // n_mixers
        if layer % n_mixers == 0:
            proj, ab = _in_proj(x, norm1_g[layer], mods_all, layer, gdn_w_in_b, j,
                                jnp.zeros((gdn_w_in_b.shape[2],), F32), rows, n_side=V7X_LANES)
            gdn = dict(t_total=t, n_heads=n_heads, dk=dk, dv=dv)
            weights = (gdn_conv[j], gdn_a_log[j], gdn_dt_bias[j], gdn_onorm[j])
            mixed, new_state_delta = _gdn_core(proj, ab, *weights, None, mixed, new_state_delta, row0=0, n_seq=bc,
                                               seq=lc, s0_spec=None, state_slot=(j, n_gdn), **gdn)
            s0_spec = lambda hb, j=j: pl.BlockSpec((None, None, 2, hb, dk, dv), lambda b, h: (b, j, 0, h, 0, 0))
            mixed, _ = _gdn_core(proj, ab, *weights, state_delta, mixed, None, row0=tc_rows, n_seq=bl, seq=ll,
                                 s0_spec=s0_spec, state_slot=(0, 1), **gdn)
            mixer = (mixed, gdn_w_out_b, j, zero_b)
            mixed_hy = mixed if shared else mixed_hy
        else:
            proj = _in_proj(x, norm1_g[layer], mods_all, layer, hy_w_in_b, j, hy_b_in[j], rows)
            for row0, n_seq, seq in ((0, bc, lc), (tc_rows, bl, ll)):
                dft_mats, spectra = dft[seq]
                hre, him = spectra[j]
                z = proj
                for n in range(HY_ORDER):
                    last = n == HY_ORDER - 1
                    z = _hyconv(z, 0, n == 0, proj, (n + 1) * d, hre, him, n, hy_skip[j, n], hy_conv[j],
                                dft_mats, mixed_hy if last else None, row0=row0, n_seq=n_seq, seq=seq, d=d,
                                out_rows=t if last else n_seq * seq, out_row0=row0 if last else 0,
                                out_dtype=BF16 if last else F32)
                mixed_hy = z
            mixer = (mixed_hy, hy_w_out_b, j, hy_b_out[j])
            mixed = mixed_hy if shared else mixed
        ffn = functools.partial(_ffn, x, norm2_g[layer], mods_all, ffn_w_gu_b, ffn_w_down_b, layer, rows,
                                mixer=mixer)
        if layer < depth - 1:
            x = ffn()
        else:
            y_prompt = ffn(final_g=final_g, tile0=0, n_tiles=rows.n_ctx_tiles)
            y_sample = ffn(final_g=final_g, tile0=rows.n_ctx_tiles, n_tiles=rows.n_tiles - rows.n_ctx_tiles)
    return (y_prompt.reshape(bc, lc, d), y_sample.reshape(bl, ll, d), new_state_delta)
```

```python
import functools
import math

import jax
import jax.numpy as jnp
import numpy as np
from jax import lax
from jax.experimental import pallas as pl
from jax.experimental.pallas import tpu as pltpu

GRID_W = 64
CHUNK = 64
HY_ORDER = 2
HY_TARGET = 1e-2
HY_SHORT_PCT = 0.3
HY_LONG_PCT = 1.5
POS_BASE = 10000.0
EPS = 1e-6

V7X_LANES = 128
V7X_SUBLANES = 8
V7X_VMEM_LIMIT_BYTES = 48 * 1024 * 1024

BF16 = jnp.bfloat16
F32 = jnp.float32


def _cparams(*sem):
    return pltpu.CompilerParams(dimension_semantics=sem, vmem_limit_bytes=V7X_VMEM_LIMIT_BYTES)


def _tile(n, target, align):
    if n <= target:
        return n
    best = None
    for t in range(align, target + 1, align):
        if n % t == 0:
            best = t
    assert best is not None, (n, target, align)
    return best


def _dot(a, b):
    return jnp.dot(a.astype(BF16), b.astype(BF16), preferred_element_type=F32)


def _dot_nt(a, b):
    return lax.dot_general(a.astype(BF16), b.astype(BF16), (((1,), (1,)), ((), ())),
                           preferred_element_type=F32)


def _silu(x):
    return x * jax.nn.sigmoid(x)


def _norm_mod(x, g, shift, scale):
    ms = jnp.mean(x * x, axis=-1, keepdims=True)
    return (x * lax.rsqrt(ms + EPS) * g) * (1.0 + scale) + shift


class _Rows:
    def __init__(self, tc, ll, t, tm):
        assert tc % tm == 0 and ll % tm == 0 and t % tm == 0
        self.n_ctx_tiles = tc // tm
        self.tiles_per_lat = ll // tm
        self.n_tiles = t // tm
        self.tm = tm

    def mod_index(self, i):
        lat = 1 + (i - self.n_ctx_tiles) // self.tiles_per_lat
        return jnp.where(i < self.n_ctx_tiles, 0, lat)


def _ada_kernel(c_ref, w_ref, b_ref, o_ref):
    o_ref[...] = _dot(_silu(c_ref[...]), w_ref[...]) + b_ref[...]


def _ada(cvec, ada_w, ada_b):
    depth, d, n = ada_w.shape
    bm = cvec.shape[0]
    tn = _tile(n, 1536, V7X_LANES)
    return pl.pallas_call(
        _ada_kernel,
        out_shape=jax.ShapeDtypeStruct((depth, bm, n), F32),
        grid=(depth, n // tn),
        in_specs=[pl.BlockSpec((bm, d), lambda l, j: (0, 0)),
                  pl.BlockSpec((None, d, tn), lambda l, j: (l, 0, j)),
                  pl.BlockSpec((None, 1, tn), lambda l, j: (l, 0, j))],
        out_specs=pl.BlockSpec((None, bm, tn), lambda l, j: (l, 0, j)),
        compiler_params=_cparams("parallel", "parallel"),
        name="ada",
    )(cvec, ada_w, ada_b.reshape(depth, 1, n))


def _embed_kernel(xp_ref, xs_ref, pos_ref, o_ref, *, n_ctx_tiles):
    i = pl.program_id(0)

    @pl.when(i < n_ctx_tiles)
    def _():
        o_ref[...] = xp_ref[...]

    @pl.when(i >= n_ctx_tiles)
    def _():
        o_ref[...] = xs_ref[...] + pos_ref[...]


def _embed(xp, xs, pos, rows):
    t, d = xp.shape[0] + xs.shape[0], xp.shape[1]
    tm, nct = rows.tm, rows.n_ctx_tiles
    npos = pos.shape[0] // tm
    return pl.pallas_call(
        functools.partial(_embed_kernel, n_ctx_tiles=nct),
        out_shape=jax.ShapeDtypeStruct((t, d), F32),
        grid=(rows.n_tiles,),
        in_specs=[pl.BlockSpec((tm, d), lambda i: (jnp.minimum(i, nct - 1), 0)),
                  pl.BlockSpec((tm, d), lambda i: (jnp.maximum(i - nct, 0), 0)),
                  pl.BlockSpec((tm, d), lambda i: (jnp.maximum(i - nct, 0) % npos, 0))],
        out_specs=pl.BlockSpec((tm, d), lambda i: (i, 0)),
        compiler_params=_cparams("parallel"),
        name="embed",
    )(xp, xs, pos)


_RESIDENT = pl.Buffered(1)


def _in_kernel(x_ref, g_ref, mod_ref, w_ref, b_ref, o_ref, *side_ref, chunk):
    m = mod_ref[...]
    h = _norm_mod(x_ref[...], g_ref[...], m[0:1, :], m[1:2, :]).astype(BF16)
    n = o_ref.shape[1]
    for c0 in range(0, n, chunk):
        cols = slice(c0, c0 + chunk)
        y = jnp.dot(h, w_ref[:, cols], preferred_element_type=F32) + b_ref[:, cols]
        o_ref[:, cols] = y.astype(o_ref.dtype)
    if side_ref:
        side_ref[0][...] = jnp.dot(h, w_ref[:, n:], preferred_element_type=F32) + b_ref[:, n:]


def _in_proj(x, g, mods_all, layer, w_all, w_index, b, rows, n_side=0):
    t, d = x.shape
    n = w_all.shape[2] - n_side
    tm = rows.tm
    chunk = _tile(n, 512, V7X_LANES)
    out_shape = [jax.ShapeDtypeStruct((t, n), BF16)]
    out_specs = [pl.BlockSpec((tm, n), lambda i: (i, 0))]
    if n_side:
        out_shape.append(jax.ShapeDtypeStruct((t, n_side), F32))
        out_specs.append(pl.BlockSpec((tm, n_side), lambda i: (i, 0)))
    out = pl.pallas_call(
        functools.partial(_in_kernel, chunk=chunk),
        out_shape=out_shape,
        grid=(rows.n_tiles,),
        in_specs=[pl.BlockSpec((tm, d), lambda i: (i, 0)),
                  pl.BlockSpec((1, d), lambda i: (0, 0), pipeline_mode=_RESIDENT),
                  pl.BlockSpec((None, None, 6, d), lambda i: (layer, rows.mod_index(i), 0, 0)),
                  pl.BlockSpec((None, d, n + n_side), lambda i: (w_index, 0, 0), pipeline_mode=_RESIDENT),
                  pl.BlockSpec((1, n + n_side), lambda i: (0, 0), pipeline_mode=_RESIDENT)],
        out_specs=out_specs,
        compiler_params=_cparams("parallel"),
        name="in_proj",
    )(x, g.reshape(1, d), mods_all, w_all, b.reshape(1, n + n_side))
    return out if n_side else out[0]


def _ffn_kernel(x_ref, g_ref, mod_ref, wgu_ref, wd_ref, *rest, chunk, has_mixer, has_final):
    o_ref = rest[-1]
    x = x_ref[...]
    m = mod_ref[...]
    if has_mixer:
        a_ref, wo_ref, bo_ref = rest[:3]
        x = x + m[2:3, :] * (_dot(a_ref[...], wo_ref[...]) + bo_ref[...])
    h = _norm_mod(x, g_ref[...], m[3:4, :], m[4:5, :]).astype(BF16)
    f = wd_ref.shape[0]
    n_chunks = f // chunk

    def gate_up(k):
        gate = jnp.dot(h, wgu_ref[:, k * chunk:(k + 1) * chunk], preferred_element_type=F32)
        up = jnp.dot(h, wgu_ref[:, f + k * chunk:f + (k + 1) * chunk], preferred_element_type=F32)
        return gate, up

    y = None
    pending = gate_up(0)
    for k in range(n_chunks):
        following = gate_up(k + 1) if k + 1 < n_chunks else None
        act = (_silu(pending[0]) * pending[1]).astype(BF16)
        part = jnp.dot(act, wd_ref[k * chunk:(k + 1) * chunk, :], preferred_element_type=F32)
        y = part if y is None else y + part
        pending = following
    out = x + m[5:6, :] * y
    if has_final:
        out = out * lax.rsqrt(jnp.mean(out * out, axis=-1, keepdims=True) + EPS) * rest[-2][...]
    o_ref[...] = out


def _ffn(x, g, mods_all, w_gu_all, w_down_all, layer, rows, mixer=None, final_g=None, tile0=0, n_tiles=None):
    d = x.shape[1]
    f = w_down_all.shape[1]
    tm = rows.tm
    n_tiles = rows.n_tiles if n_tiles is None else n_tiles
    chunk = _tile(f, 256, V7X_LANES)
    resident = lambda *s: pl.BlockSpec(s, lambda i: (0,) * len(s), pipeline_mode=_RESIDENT)
    in_specs = [pl.BlockSpec((tm, d), lambda i: (i + tile0, 0)),
                resident(1, d),
                pl.BlockSpec((None, None, 6, d), lambda i: (layer, rows.mod_index(i + tile0), 0, 0)),
                pl.BlockSpec((None, d, 2 * f), lambda i: (layer, 0, 0), pipeline_mode=_RESIDENT),
                pl.BlockSpec((None, f, d), lambda i: (layer, 0, 0), pipeline_mode=_RESIDENT)]
    args = [x, g.reshape(1, d), mods_all, w_gu_all, w_down_all]
    if mixer is not None:
        a, wo_all, wo_index, bo = mixer
        k = a.shape[1]
        in_specs += [pl.BlockSpec((tm, k), lambda i: (i + tile0, 0)),
                     pl.BlockSpec((None, k, d), lambda i: (wo_index, 0, 0), pipeline_mode=_RESIDENT),
                     resident(1, d)]
        args += [a, wo_all, bo.reshape(1, d)]
    if final_g is not None:
        in_specs.append(resident(1, d))
        args.append(final_g.reshape(1, d))
    return pl.pallas_call(
        functools.partial(_ffn_kernel, chunk=chunk, has_mixer=mixer is not None, has_final=final_g is not None),
        out_shape=jax.ShapeDtypeStruct((n_tiles * tm, d), F32),
        grid=(n_tiles,),
        in_specs=in_specs,
        out_specs=pl.BlockSpec((tm, d), lambda i: (i, 0)),
        compiler_params=_cparams("parallel"),
        name="ffn",
    )(*args)


def _dwconv(x, w, k):
    n = x.shape[0]
    half = k // 2
    e = V7X_SUBLANES
    assert half <= e and n >= 4 * e
    taps = [w[j:j + 1, :] for j in range(k)]
    shifts = [s for s in range(-half, half + 1) if s != 0]

    def conv(v, mask):
        acc = v * taps[half]
        for s in shifts:
            shifted = pltpu.roll(v, (-s) % v.shape[0], axis=0)
            acc = acc + (shifted if mask is None else jnp.where(mask(s), shifted, 0.0)) * taps[s + half]
        return acc

    row = lax.broadcasted_iota(jnp.int32, (2 * e, x.shape[1]), 0)
    top = conv(x[:2 * e], lambda s: row + s >= 0)[:e]
    bottom = conv(x[n - 2 * e:], lambda s: row + s < 2 * e)[e:]
    return jnp.concatenate([top, conv(x, None)[e:n - e], bottom], axis=0)


CONV_HALO = 16


def _dwconv_window(ref, cols, w, k, r0, rows, slab_ref):
    n = ref.shape[0]
    if rows == n:
        return _dwconv(ref[:, cols].astype(F32), w, k)
    half = k // 2
    h = CONV_HALO
    assert half <= h <= rows and n % rows == 0
    before = ref[pl.ds(pl.multiple_of(jnp.maximum(r0 - h, 0), h), h), cols].astype(F32)
    after = ref[pl.ds(pl.multiple_of(jnp.minimum(r0 + rows, n - h), h), h), cols].astype(F32)
    slab_ref[0:h, :] = jnp.where(r0 > 0, before, 0.0)
    slab_ref[h:h + rows, :] = ref[pl.ds(r0, rows), cols].astype(F32)
    slab_ref[h + rows:, :] = jnp.where(r0 + rows < n, after, 0.0)
    acc = None
    for s in range(-half, half + 1):
        term = slab_ref[h + s:h + s + rows, :] * w[s + half:s + half + 1, :]
        acc = term if acc is None else acc + term
    return acc


TRI_BASE = 8
GDN_PHASE1_CHAINS = 32
GDN_HEADS_PER_STEP = 8
GDN_ROWS_PER_STEP = 4096


def _unit_tri_inverses_minus_eye(mats, ri, ci):
    c = mats[0].shape[0]

    def same_block(s):
        sh = int(math.log2(s))
        return (ri >> sh) == (ci >> sh)

    ps = [jnp.where(same_block(TRI_BASE), -a, 0.0) for a in mats]
    es = ps
    n_lvl = int(math.log2(TRI_BASE))
    for lvl in range(n_lvl):
        es = [e + _dot(p, e) for p, e in zip(ps, es)]
        if lvl < n_lvl - 1:
            ps = [_dot(p, p) for p in ps]
    s = TRI_BASE
    while s < c:
        mask = jnp.logical_and(same_block(2 * s), jnp.logical_not(same_block(s)))
        offs = [jnp.where(mask, a, 0.0) for a in mats]
        ys = [off + _dot(off, e) for off, e in zip(offs, es)]
        es = [e - (y + _dot(e, y)) for e, y in zip(es, ys)]
        s *= 2
    return es


def _chunk_cumsum(x, pos, reverse):
    n = x.shape[0]
    s = 1
    while s < CHUNK:
        if reverse:
            x = x + jnp.where(pos + s < CHUNK, pltpu.roll(x, n - s, axis=0), 0.0)
        else:
            x = x + jnp.where(pos >= s, pltpu.roll(x, s, axis=0), 0.0)
        s *= 2
    return x


def _gdn_kernel(*refs, n_heads, conv_k, lockstep, n_kept):
    (alog_ref, dtb_ref, q_ref, k_ref, v_ref, gt_ref, ab_ref, cq_ref, ck_ref, cv_ref,
     onorm_ref, s0_ref) = refs[:12]
    o_ref, sfin_ref, gates_s, conv_s, w_s, u_s, qd_s, ak_s, gl_s, o_s, st_s = refs[12 + n_kept:]
    n_chains, seq, dk = w_s.shape
    hb = n_chains // 2
    head0 = pl.program_id(1) * hb
    c = CHUNK
    n_chunks = seq // c
    chains = [(hh, d) for hh in range(hb) for d in range(2)]

    @pl.when(head0 == 0)
    def _():
        ab = ab_ref[...]
        pos = jnp.bitwise_and(lax.broadcasted_iota(jnp.int32, ab.shape, 0), c - 1)
        g = -jnp.exp(alog_ref[...]) * jax.nn.softplus(ab + dtb_ref[...])
        g_fwd = _chunk_cumsum(g, pos, reverse=False)
        g_rev = _chunk_cumsum(g, pos, reverse=True)
        gates_s[0] = g_fwd
        gates_s[1] = g_rev - g
        gates_s[2] = g_rev
        gates_s[3] = g_fwd - g
        gates_s[4] = jax.nn.sigmoid(ab)

    lane = lax.broadcasted_iota(jnp.int32, (1, V7X_LANES), 1)

    rows = lockstep * c

    def column(i, idx, win):
        one_hot = (lane == idx).astype(F32)
        col = jnp.sum(gates_s[i, win, :] * one_hot, axis=1, keepdims=True)
        return jnp.broadcast_to(col, (rows, V7X_LANES))

    ri = lax.broadcasted_iota(jnp.int32, (c, c), 0)
    ci = lax.broadcasted_iota(jnp.int32, (c, c), 1)
    incl = (ri >= ci, ri <= ci)
    strict = (ri > ci, ri < ci)

    def phase1(it, carry):
        r0 = pl.multiple_of(it * rows, rows)
        win = pl.ds(r0, rows)
        prepared = []
        for hh in range(hb):
            cols = slice(hh * dk, (hh + 1) * dk)
            q = _silu(_dwconv_window(q_ref, cols, cq_ref[:, cols], conv_k, r0, rows, conv_s.at[3 * hh]))
            k = _silu(_dwconv_window(k_ref, cols, ck_ref[:, cols], conv_k, r0, rows, conv_s.at[3 * hh + 1]))
            v = _silu(_dwconv_window(v_ref, cols, cv_ref[:, cols], conv_k, r0, rows, conv_s.at[3 * hh + 2]))
            q = q * lax.rsqrt(jnp.sum(q * q, axis=-1, keepdims=True) + EPS) * (dk ** -0.5)
            k = k * lax.rsqrt(jnp.sum(k * k, axis=-1, keepdims=True) + EPS)
            for d in range(2):
                head = head0 + hh
                g_cum = column(2 * d, d * n_heads + head, win)
                g_tail = column(2 * d + 1, d * n_heads + head, win)
                beta = column(4, 2 * n_heads + d * n_heads + head, win)
                e_cum = jnp.exp(g_cum)
                kb = k * beta
                prepared.append((q, k, g_cum, kb, kb * e_cum, v * beta, k * jnp.exp(g_tail)))
                qd_s[2 * hh + d, win, :] = (q * e_cum).astype(qd_s.dtype)
                g_tot = jnp.exp(g_cum + g_tail)
                for gi in range(lockstep):
                    gl_s[2 * hh + d, pl.ds(it * lockstep + gi, 1), :] = g_tot[gi * c:gi * c + 1, :]
        items = []
        for gi in range(lockstep):
            part = slice(gi * c, (gi + 1) * c)
            for ch, (_, d) in enumerate(chains):
                sl = pl.ds(pl.multiple_of(r0 + gi * c, c), c)
                items.append((it * lockstep + gi, sl, ch, d) + tuple(a[part] for a in prepared[ch]))
        decays, kqs = [], []
        for ic, sl, ch, d, qc, kc, gc, kb, w0, u0, ktl in items:
            diff = gc[:, :c] - gc.T[:c, :]
            decays.append(jnp.where(incl[d], jnp.exp(jnp.where(incl[d], diff, 0.0)), 0.0))
            kqs.append(_dot_nt(jnp.concatenate([kb, qc], axis=0), kc))
        a_kks = [jnp.where(strict[item[3]], kq[:c] * decay, 0.0)
                 for item, kq, decay in zip(items, kqs, decays)]
        es = _unit_tri_inverses_minus_eye(a_kks, ri, ci)
        rhss = [jnp.concatenate([item[8], item[9]], axis=1) for item in items]
        wus = [rhs + _dot(e, rhs) for e, rhs in zip(es, rhss)]
        for item, kq, decay, wu in zip(items, kqs, decays, wus):
            ic, sl, ch = item[:3]
            w_s[ch, sl, :] = wu[:, :dk].astype(w_s.dtype)
            u_s[ch, sl, :] = wu[:, dk:]
            ak_s[ch, ic] = jnp.concatenate([kq[c:] * decay, item[10].T], axis=0).astype(ak_s.dtype)
        return carry

    lax.fori_loop(0, n_chunks // lockstep, phase1, 0)

    for hh, d in chains:
        st_s[2 * hh + d] = s0_ref[d, hh]

    def phase2(i, carry):
        n = 2 * hb
        ics = [n_chunks - 1 - i if d else i for _, d in chains]
        r0s = [pl.multiple_of(ic * c, c) for ic in ics]
        sls = [pl.ds(r0, c) for r0 in r0s]
        ss = [st_s[ch] for ch in range(n)]
        wqs = [_dot(jnp.concatenate([w_s[ch, sls[ch], :], qd_s[ch, sls[ch], :]], axis=0), ss[ch]) for ch in range(n)]
        v_news = [u_s[ch, sls[ch], :] - wqs[ch][:c] for ch in range(n)]
        outs = [_dot(ak_s[ch, ics[ch]], v_news[ch]) for ch in range(n)]
        for ch in range(n):
            o_s[ch, sls[ch], :] = wqs[ch][c:] + outs[ch][:c]
            st_s[ch] = ss[ch] * gl_s[ch, pl.ds(ics[ch], 1), :] + outs[ch][c:]
        return carry

    lax.fori_loop(0, n_chunks, phase2, 0, unroll=8)

    for hh, d in chains:
        sfin_ref[d, hh] = st_s[2 * hh + d]
    for hh in range(hb):
        cols = slice(hh * dk, (hh + 1) * dk)
        o = o_s[2 * hh] + o_s[2 * hh + 1]
        o = o * lax.rsqrt(jnp.mean(o * o, axis=-1, keepdims=True) + EPS)
        o_ref[:, cols] = (o * onorm_ref[...] * _silu(gt_ref[:, cols].astype(F32))).astype(o_ref.dtype)


def _gdn_core(proj, ab, conv_w, a_log, dt_bias, onorm, s0, dst, states_dst, *, t_total, row0, n_seq, seq,
              n_heads, dk, dv, s0_spec, state_slot):
    assert dk == dv == V7X_LANES and row0 % seq == 0
    conv_k = conv_w.shape[0]
    r0 = row0 // seq
    n_chunks = seq // CHUNK
    hb = _tile(n_heads, min(GDN_HEADS_PER_STEP, max(1, GDN_ROWS_PER_STEP // seq)), 1)
    lockstep = _tile(n_chunks, max(1, GDN_PHASE1_CHAINS // (2 * hb)), 1)
    gate_pad = lambda p: jnp.pad(p.reshape(1, 2 * n_heads), ((0, 0), (0, V7X_LANES - 2 * n_heads)))
    lane_vec = pl.BlockSpec((1, V7X_LANES), lambda b, h: (0, 0))
    if s0 is None:
        s0 = jnp.zeros((2, hb, dk, dv), F32)
        s0_in = pl.BlockSpec((2, hb, dk, dv), lambda b, h: (0, 0, 0, 0))
    else:
        s0_in = s0_spec(hb)
    nb = n_heads // hb
    col = lambda sec: pl.BlockSpec((seq, hb * dk), lambda b, h: (b + r0, sec * nb + h))
    cw = lambda sec: pl.BlockSpec((conv_k, hb * dk), lambda b, h: (0, sec * nb + h))
    f32 = lambda *s: pltpu.VMEM(s, F32)
    in_specs = [lane_vec, lane_vec, col(0), col(1), col(2), col(3),
                pl.BlockSpec((seq, V7X_LANES), lambda b, h: (b + r0, 0)),
                cw(0), cw(1), cw(2),
                pl.BlockSpec((1, dv), lambda b, h: (0, 0)),
                s0_in]
    args = [gate_pad(a_log), gate_pad(dt_bias), proj, proj, proj, proj, ab, conv_w, conv_w, conv_w,
            onorm.reshape(1, dv), s0]
    aliases = {}
    for out_idx, kept in enumerate((dst, states_dst)):
        if kept is not None:
            in_specs.append(pl.BlockSpec(memory_space=pl.ANY))
            args.append(kept)
            aliases[len(args) - 1] = out_idx
    nc = 2 * hb
    slot, n_slots = state_slot
    return pl.pallas_call(
        functools.partial(_gdn_kernel, n_heads=n_heads, conv_k=conv_k, lockstep=lockstep,
                          n_kept=len(aliases)),
        out_shape=(jax.ShapeDtypeStruct((t_total, n_heads * dv), BF16),
                   jax.ShapeDtypeStruct((n_seq, n_slots, 2, n_heads, dk, dv), F32)),
        grid=(n_seq, nb),
        in_specs=in_specs,
        out_specs=(pl.BlockSpec((seq, hb * dv), lambda b, h: (b + r0, h)),
                   pl.BlockSpec((None, None, 2, hb, dk, dv), lambda b, h: (b, slot, 0, h, 0, 0))),
        scratch_shapes=[f32(5, seq, V7X_LANES),
                        f32(3 * hb, lockstep * CHUNK + 2 * CONV_HALO, dk),
                        pltpu.VMEM((nc, seq, dk), BF16), f32(nc, seq, dv), pltpu.VMEM((nc, seq, dk), BF16),
                        pltpu.VMEM((nc, n_chunks, CHUNK + dk, CHUNK), BF16),
                        f32(nc, max(n_chunks, V7X_SUBLANES), V7X_LANES),
                        f32(nc, seq, dv), f32(nc, dk, dv)],
        input_output_aliases=aliases,
        compiler_params=_cparams("parallel", "arbitrary"),
        name="gdn_core",
    )(*args)


def _folded_odd_dft(seq):
    half = seq // 2
    k = np.arange(half, dtype=np.int64)[:, None]
    parts = []
    for parity in (0, 1):
        m = 2 * np.arange(half, dtype=np.int64)[None, :] + parity
        ang = (((2 * k + 1) * m) % (4 * seq)) * (math.pi / (2 * seq))
        parts.append(np.concatenate([np.cos(ang), np.sin(ang)], axis=0))
    return parts


def _bf16_head_tail(m):
    m32 = m.astype(np.float32)
    head = m32.astype(jnp.bfloat16)
    tail = (m32 - head.astype(np.float32)).astype(jnp.bfloat16)
    return jnp.asarray(head), jnp.asarray(tail)


def _filter_kernel(feat_ref, w1_ref, b1_ref, w2_ref, b2_ref, fr_ref, w3f_ref, w3b_ref, dl_ref,
                   fe_hi_ref, fe_lo_ref, fo_hi_ref, fo_lo_ref, hre_ref, him_ref, hid_s):
    seq = feat_ref.shape[0]
    half = seq // 2
    n_hid = w2_ref.shape[0]

    @pl.when(jnp.logical_and(pl.program_id(0) == 0, pl.program_id(1) == 0))
    def _():
        fr = fr_ref[...]
        hid1 = jnp.sin(fr * (_dot(feat_ref[...], w1_ref[...]) + b1_ref[...]))
        hid_s[...] = jnp.zeros_like(hid_s)
        hid_s[:, :n_hid] = jnp.sin(fr * (_dot(hid1, w2_ref[...]) + b2_ref[...]))

    def taps(parity):
        lags = pl.ds(parity, half, stride=2)
        hid = hid_s[lags, :][:, :n_hid]
        window = jnp.exp(-feat_ref[lags, :][:, 0:1] * dl_ref[...])
        hf = _dot(hid, w3f_ref[...]) * window
        hb = _dot(hid, w3b_ref[...]) * window
        if parity == 0:
            hb = jnp.where(lax.broadcasted_iota(jnp.int32, hb.shape, 0) == 0, 0.0, hb)
        return hf + hb, hb - hf

    def dft(m_hi, m_lo, val):
        v_hi = val.astype(BF16)
        v_lo = (val - v_hi.astype(F32)).astype(BF16)
        return (jnp.dot(m_hi, v_hi, preferred_element_type=F32)
                + (jnp.dot(m_hi, v_lo, preferred_element_type=F32)
                   + jnp.dot(m_lo, v_hi, preferred_element_type=F32)))

    (sum_e, dif_e), (sum_o, dif_o) = taps(0), taps(1)
    cos_rows, sin_rows = slice(0, half), slice(half, seq)
    re_e = dft(fe_hi_ref[cos_rows, :], fe_lo_ref[cos_rows, :], sum_e)
    re_o = dft(fo_hi_ref[cos_rows, :], fo_lo_ref[cos_rows, :], sum_o)
    im_e = dft(fe_hi_ref[sin_rows, :], fe_lo_ref[sin_rows, :], dif_e)
    im_o = dft(fo_hi_ref[sin_rows, :], fo_lo_ref[sin_rows, :], dif_o)
    hre_ref[0:half, :] = re_e + re_o
    hre_ref[half:, :] = re_e - re_o
    him_ref[0:half, :] = im_e + im_o
    him_ref[half:, :] = im_o - im_e


def _hyena_filters(seq, d, w1, b1, w2, b2, w3, freq, dft_parts):
    emb, hid = w1.shape
    bands = (emb - 1) // 2
    t = jnp.linspace(0.0, 1.0, seq, dtype=F32)[:, None]
    wpos = (2.0 * math.pi / seq) * jnp.arange(seq, dtype=F32)[:, None]
    fb = jnp.linspace(1e-4, bands - 1, bands, dtype=F32)[None, :]
    feat = jnp.concatenate([t, jnp.cos(fb * wpos), -jnp.sin(fb * wpos)], axis=-1)
    feat = jnp.pad(feat, ((0, 0), (0, V7X_LANES - emb)))
    w1p = jnp.pad(w1, ((0, V7X_LANES - emb), (0, 0)))
    max_decay = math.log(HY_TARGET) / HY_SHORT_PCT
    min_decay = math.log(HY_TARGET) / HY_LONG_PCT
    deltas = jnp.abs(jnp.linspace(min_decay, max_decay, d, dtype=F32))[None, :]
    tc = _tile(d, 256, V7X_LANES)
    nt = d // tc
    full = lambda r, c: pl.BlockSpec((r, c), lambda n, j: (0, 0))
    out_spec = pl.BlockSpec((None, seq, tc), lambda n, j: (n, 0, j))
    return pl.pallas_call(
        _filter_kernel,
        out_shape=(jax.ShapeDtypeStruct((HY_ORDER, seq, d), F32),) * 2,
        grid=(HY_ORDER, nt),
        in_specs=[full(seq, V7X_LANES), full(V7X_LANES, hid), full(1, hid), full(hid, hid), full(1, hid),
                  full(1, hid),
                  pl.BlockSpec((hid, tc), lambda n, j: (0, (2 * n) * nt + j)),
                  pl.BlockSpec((hid, tc), lambda n, j: (0, (2 * n + 1) * nt + j)),
                  pl.BlockSpec((1, tc), lambda n, j: (0, j))] + [full(seq, seq // 2)] * 4,
        out_specs=(out_spec, out_spec),
        scratch_shapes=[pltpu.VMEM((seq, V7X_LANES), F32)],
        compiler_params=_cparams("arbitrary", "arbitrary"),
        name="hyena_filter",
    )(feat, w1p, b1.reshape(1, hid), w2, b2.reshape(1, hid), freq.reshape(1, hid), w3, w3, deltas,
      *dft_parts)


HYENA_ROWS_PER_STEP = 4096


def _hyconv_kernel(*refs, conv_z, conv_k, has_dst):
    (z_ref, x_ref, hre_ref, him_ref, skip_ref, cz_ref, cx_ref,
     fwd_e_ref, fwd_o_ref, inv_e_ref, inv_o_ref) = refs[:11]
    o_ref = refs[11 + has_dst]
    zs_s, xs_s, os_s = refs[-3:]
    seq = hre_ref.shape[0]
    half = seq // 2
    n_sub = z_ref.shape[0] // seq
    even, odd = pl.ds(0, half, stride=2), pl.ds(1, half, stride=2)
    n_lane_tiles = zs_s.shape[1]

    def stage(dst, s, val):
        for l in range(n_lane_tiles):
            dst[s, l] = val[:, l * V7X_LANES:(l + 1) * V7X_LANES]

    def rows_of(src, s, rows):
        return jnp.concatenate([src[s, l, rows, :] for l in range(n_lane_tiles)], axis=1)

    for s in range(n_sub):
        z = z_ref[s * seq:(s + 1) * seq, :].astype(F32)
        stage(zs_s, s, _dwconv(z, cz_ref[...], conv_k) if conv_z else z)
        stage(xs_s, s, _dwconv(x_ref[s * seq:(s + 1) * seq, :].astype(F32), cx_ref[...], conv_k))
    z_parts = [(rows_of(zs_s, s, even), rows_of(zs_s, s, odd)) for s in range(n_sub)]
    fwds = [(jnp.dot(fwd_e_ref[...], ze.astype(BF16), preferred_element_type=F32),
             jnp.dot(fwd_o_ref[...], zo.astype(BF16), preferred_element_type=F32)) for ze, zo in z_parts]
    hre_a, hre_b = hre_ref[0:half, :], hre_ref[half:, :]
    him_a, him_b = him_ref[0:half, :], him_ref[half:, :]
    skip = skip_ref[...]
    for s in range(n_sub):
        (fe, fo), (ze, zo) = fwds[s], z_parts[s]
        pe, qe, po, qo = fe[:half], fe[half:], fo[:half], fo[half:]
        pa, pb, qa, qb = pe + po, pe - po, qe + qo, qo - qe
        yre_a, yim_a = pa * hre_a + qa * him_a, pa * him_a - qa * hre_a
        yre_b, yim_b = pb * hre_b + qb * him_b, pb * him_b - qb * hre_b
        y_e = jnp.dot(inv_e_ref[...], jnp.concatenate([yre_a + yre_b, yim_a - yim_b], axis=0).astype(BF16),
                      preferred_element_type=F32)
        y_o = jnp.dot(inv_o_ref[...], jnp.concatenate([yre_a - yre_b, yim_a + yim_b], axis=0).astype(BF16),
                      preferred_element_type=F32)
        out_e = rows_of(xs_s, s, even) * (y_e + skip * ze)
        out_o = rows_of(xs_s, s, odd) * (y_o + skip * zo)
        for l in range(n_lane_tiles):
            lanes = slice(l * V7X_LANES, (l + 1) * V7X_LANES)
            os_s[s, l, even, :] = out_e[:, lanes]
            os_s[s, l, odd, :] = out_o[:, lanes]
        o_ref[s * seq:(s + 1) * seq, :] = rows_of(os_s, s, slice(None)).astype(o_ref.dtype)


def _hyconv(z, z_col0, conv_z, proj, x_col0, hre, him, order, skip, conv_w, dft, dst,
            *, row0, n_seq, seq, d, out_rows, out_row0, out_dtype):
    conv_k = conv_w.shape[0]
    n_sub = max(s for s in range(1, max(1, HYENA_ROWS_PER_STEP // seq) + 1)
                if n_seq % s == 0 and row0 % (s * seq) == 0 and out_row0 % (s * seq) == 0)
    blk = n_sub * seq
    tc = _tile(d, 512 if blk <= 1024 else 256, V7X_LANES)
    nt = d // tc
    r0 = row0 // blk
    zr0 = r0 if conv_z else 0
    zc, xc = z_col0 // tc, x_col0 // tc
    in_specs = [pl.BlockSpec((blk, tc), lambda j, b: (b + zr0, zc + j)),
                pl.BlockSpec((blk, tc), lambda j, b: (b + r0, xc + j)),
                pl.BlockSpec((None, seq, tc), lambda j, b: (order, 0, j)),
                pl.BlockSpec((None, seq, tc), lambda j, b: (order, 0, j)),
                pl.BlockSpec((1, tc), lambda j, b: (0, j)),
                pl.BlockSpec((conv_k, tc), lambda j, b: (0, zc + j)),
                pl.BlockSpec((conv_k, tc), lambda j, b: (0, xc + j)),
                pl.BlockSpec((seq, seq // 2), lambda j, b: (0, 0), pipeline_mode=_RESIDENT),
                pl.BlockSpec((seq, seq // 2), lambda j, b: (0, 0), pipeline_mode=_RESIDENT),
                pl.BlockSpec((seq // 2, seq), lambda j, b: (0, 0), pipeline_mode=_RESIDENT),
                pl.BlockSpec((seq // 2, seq), lambda j, b: (0, 0), pipeline_mode=_RESIDENT)]
    args = [z, proj, hre, him, skip.reshape(1, d), conv_w, conv_w, *dft]
    aliases = {}
    if dst is not None:
        in_specs.append(pl.BlockSpec(memory_space=pl.ANY))
        args.append(dst)
        aliases = {len(args) - 1: 0}
    out_r0 = out_row0 // blk
    return pl.pallas_call(
        functools.partial(_hyconv_kernel, conv_z=conv_z, conv_k=conv_k, has_dst=dst is not None),
        out_shape=jax.ShapeDtypeStruct((out_rows, d), out_dtype),
        grid=(nt, n_seq // n_sub),
        in_specs=in_specs,
        out_specs=pl.BlockSpec((blk, tc), lambda j, b: (b + out_r0, j)),
        scratch_shapes=[pltpu.VMEM((n_sub, tc // V7X_LANES, seq, V7X_LANES), F32)] * 3,
        input_output_aliases=aliases,
        compiler_params=_cparams("parallel", "parallel"),
        name="hyena_conv",
    )(*args)


def _grid_pos_emb(n_tokens, d):
    rows = n_tokens // GRID_W
    r, col = jnp.meshgrid(jnp.arange(rows), jnp.arange(GRID_W), indexing='ij')
    quarter = d // 4
    omega = 1.0 / (POS_BASE ** (jnp.arange(quarter, dtype=F32) / quarter))

    def emb1d(p):
        a = p.reshape(-1, 1).astype(F32) * omega[None, :]
        return jnp.concatenate([jnp.sin(a), jnp.cos(a)], axis=-1)

    return jnp.concatenate([emb1d(r), emb1d(col)], axis=-1)


def kernel(x_prompt, x_sample, state_delta, c, c_ctx, ada_w, ada_b, norm1_g, norm2_g, gdn_w_in, gdn_conv, gdn_a_log, gdn_dt_bias, gdn_onorm, gdn_w_out, hy_w_in, hy_b_in, hy_conv, hy_f_w1, hy_f_b1, hy_f_w2, hy_f_b2, hy_f_w3, hy_freq, hy_skip, hy_w_out, hy_b_out, ffn_w_gu, ffn_w_down, final_g):
    bc, lc, d = x_prompt.shape
    bl, ll, _ = x_sample.shape
    depth = ada_w.shape[0]
    n_heads, dk, dv = state_delta.shape[3:]
    tc_rows, tl_rows = bc * lc, bl * ll
    t = tc_rows + tl_rows
    assert tc_rows % ll == 0 and ll % lc == 0
    rows = _Rows(tc_rows, ll, t, _tile(math.gcd(tc_rows, ll), 512, V7X_SUBLANES))

    bm = 1 + bl
    bm_pad = -(-bm // V7X_SUBLANES) * V7X_SUBLANES
    cvec = jnp.concatenate([c_ctx[None, :], c, jnp.zeros((bm_pad - bm, d), F32)], axis=0)
    mods_all = _ada(cvec, ada_w, ada_b).reshape(depth, bm_pad, 6, d)

    x = _embed(x_prompt.reshape(tc_rows, d), x_sample.reshape(tl_rows, d), _grid_pos_emb(ll, d), rows)

    n_hy = hy_w_in.shape[0]
    dft = {}
    for seq in (lc, ll):
        half = seq // 2
        fwd_e, fwd_o = _folded_odd_dft(seq)
        (fe_hi, fe_lo), (fo_hi, fo_lo) = _bf16_head_tail(fwd_e), _bf16_head_tail(fwd_o)
        inv_e, _ = _bf16_head_tail(np.concatenate([fwd_e[:half].T, -fwd_e[half:].T], axis=1) / seq)
        inv_o, _ = _bf16_head_tail(np.concatenate([fwd_o[:half].T, -fwd_o[half:].T], axis=1) / seq)
        spectra = [_hyena_filters(seq, d, hy_f_w1[j], hy_f_b1[j], hy_f_w2[j], hy_f_b2[j], hy_f_w3[j],
                                  hy_freq[j], (fe_hi, fe_lo, fo_hi, fo_lo)) for j in range(n_hy)]
        dft[seq] = ((fe_hi, fo_hi, inv_e, inv_o), spectra)

    zero_b = jnp.zeros((d,), F32)
    mixed = jnp.zeros((t, n_heads * dv), BF16)
    shared = n_heads * dv == d
    mixed_hy = mixed if shared else jnp.zeros((t, d), BF16)
    new_state_delta = jnp.zeros((bc, gdn_w_in.shape[0], 2, n_heads, dk, dv), F32)
    n_mixers = 2
    n_gdn = gdn_w_in.shape[0]
    gdn_w_in_b = jnp.pad(gdn_w_in.astype(BF16), ((0, 0), (0, 0), (0, V7X_LANES - 4 * n_heads)))
    gdn_w_out_b, hy_w_in_b, hy_w_out_b = (w.astype(BF16) for w in (gdn_w_out, hy_w_in, hy_w_out))
    ffn_w_gu_b, ffn_w_down_b = ffn_w_gu.astype(BF16), ffn_w_down.astype(BF16)
    for layer in range(depth):
        j = layer // n_mixers
        if layer % n_mixers == 0:
            proj, ab = _in_proj(x, norm1_g[layer], mods_all, layer, gdn_w_in_b, j,
                                jnp.zeros((gdn_w_in_b.shape[2],), F32), rows, n_side=V7X_LANES)
            gdn = dict(t_total=t, n_heads=n_heads, dk=dk, dv=dv)
            weights = (gdn_conv[j], gdn_a_log[j], gdn_dt_bias[j], gdn_onorm[j])
            mixed, new_state_delta = _gdn_core(proj, ab, *weights, None, mixed, new_state_delta, row0=0, n_seq=bc,
                                               seq=lc, s0_spec=None, state_slot=(j, n_gdn), **gdn)
            s0_spec = lambda hb, j=j: pl.BlockSpec((None, None, 2, hb, dk, dv), lambda b, h: (b, j, 0, h, 0, 0))
            mixed, _ = _gdn_core(proj, ab, *weights, state_delta, mixed, None, row0=tc_rows, n_seq=bl, seq=ll,
                                 s0_spec=s0_spec, state_slot=(0, 1), **gdn)
            mixer = (mixed, gdn_w_out_b, j, zero_b)
            mixed_hy = mixed if shared else mixed_hy
        else:
            proj = _in_proj(x, norm1_g[layer], mods_all, layer, hy_w_in_b, j, hy_b_in[j], rows)
            for row0, n_seq, seq in ((0, bc, lc), (tc_rows, bl, ll)):
                dft_mats, spectra = dft[seq]
                hre, him = spectra[j]
                z = proj
                for n in range(HY_ORDER):
                    last = n == HY_ORDER - 1
                    z = _hyconv(z, 0, n == 0, proj, (n + 1) * d, hre, him, n, hy_skip[j, n], hy_conv[j],
                                dft_mats, mixed_hy if last else None, row0=row0, n_seq=n_seq, seq=seq, d=d,
                                out_rows=t if last else n_seq * seq, out_row0=row0 if last else 0,
                                out_dtype=BF16 if last else F32)
                mixed_hy = z
            mixer = (mixed_hy, hy_w_out_b, j, hy_b_out[j])
            mixed = mixed_hy if shared else mixed
        ffn = functools.partial(_ffn, x, norm2_g[layer], mods_all, ffn_w_gu_b, ffn_w_down_b, layer, rows,
                                mixer=mixer)
        if layer < depth - 1:
            x = ffn()
        else:
            y_prompt = ffn(final_g=final_g, tile0=0, n_tiles=rows.n_ctx_tiles)
            y_sample = ffn(final_g=final_g, tile0=rows.n_ctx_tiles, n_tiles=rows.n_tiles - rows.n_ctx_tiles)
    return (y_prompt.reshape(bc, lc, d), y_sample.reshape(bl, ll, d), new_state_delta)
```

```python
import functools
import math

import jax
import jax.numpy as jnp
import numpy as np
from jax import lax
from jax.experimental import pallas as pl
from jax.experimental.pallas import tpu as pltpu

GRID_W = 64
CHUNK = 64
HY_ORDER = 2
HY_TARGET = 1e-2
HY_SHORT_PCT = 0.3
HY_LONG_PCT = 1.5
POS_BASE = 10000.0
EPS = 1e-6

V7X_LANES = 128
V7X_SUBLANES = 8
V7X_VMEM_LIMIT_BYTES = 48 * 1024 * 1024

BF16 = jnp.bfloat16
F32 = jnp.float32


def _cparams(*sem):
    return pltpu.CompilerParams(dimension_semantics=sem, vmem_limit_bytes=V7X_VMEM_LIMIT_BYTES)


def _tile(n, target, align):
    if n <= target:
        return n
    best = None
    for t in range(align, target + 1, align):
        if n % t == 0:
            best = t
    assert best is not None, (n, target, align)
    return best


def _dot(a, b):
    return jnp.dot(a.astype(BF16), b.astype(BF16), preferred_element_type=F32)


def _dot_nt(a, b):
    return lax.dot_general(a.astype(BF16), b.astype(BF16), (((1,), (1,)), ((), ())),
                           preferred_element_type=F32)


def _silu(x):
    return x * jax.nn.sigmoid(x)


def _norm_mod(x, g, shift, scale):
    ms = jnp.mean(x * x, axis=-1, keepdims=True)
    return (x * lax.rsqrt(ms + EPS)) * (g * (1.0 + scale)) + shift


class _Rows:
    def __init__(self, tc, ll, t, tm):
        assert tc % tm == 0 and ll % tm == 0 and t % tm == 0
        self.n_ctx_tiles = tc // tm
        self.tiles_per_lat = ll // tm
        self.n_tiles = t // tm
        self.tm = tm

    def mod_index(self, i):
        lat = 1 + (i - self.n_ctx_tiles) // self.tiles_per_lat
        return jnp.where(i < self.n_ctx_tiles, 0, lat)


def _ada_kernel(c_ref, w_ref, b_ref, o_ref):
    o_ref[...] = _dot(_silu(c_ref[...]), w_ref[...]) + b_ref[...]


def _ada(cvec, ada_w, ada_b):
    depth, d, n = ada_w.shape
    bm = cvec.shape[0]
    tn = _tile(n, 1536, V7X_LANES)
    return pl.pallas_call(
        _ada_kernel,
        out_shape=jax.ShapeDtypeStruct((depth, bm, n), F32),
        grid=(depth, n // tn),
        in_specs=[pl.BlockSpec((bm, d), lambda l, j: (0, 0)),
                  pl.BlockSpec((None, d, tn), lambda l, j: (l, 0, j)),
                  pl.BlockSpec((None, 1, tn), lambda l, j: (l, 0, j))],
        out_specs=pl.BlockSpec((None, bm, tn), lambda l, j: (l, 0, j)),
        compiler_params=_cparams("parallel", "parallel"),
        name="ada",
    )(cvec, ada_w, ada_b.reshape(depth, 1, n))


def _embed_kernel(xp_ref, xs_ref, pos_ref, o_ref, *, n_ctx_tiles):
    i = pl.program_id(0)

    @pl.when(i < n_ctx_tiles)
    def _():
        o_ref[...] = xp_ref[...]

    @pl.when(i >= n_ctx_tiles)
    def _():
        o_ref[...] = xs_ref[...] + pos_ref[...]


def _embed(xp, xs, pos, rows):
    t, d = xp.shape[0] + xs.shape[0], xp.shape[1]
    tm, nct = rows.tm, rows.n_ctx_tiles
    npos = pos.shape[0] // tm
    return pl.pallas_call(
        functools.partial(_embed_kernel, n_ctx_tiles=nct),
        out_shape=jax.ShapeDtypeStruct((t, d), F32),
        grid=(rows.n_tiles,),
        in_specs=[pl.BlockSpec((tm, d), lambda i: (jnp.minimum(i, nct - 1), 0)),
                  pl.BlockSpec((tm, d), lambda i: (jnp.maximum(i - nct, 0), 0)),
                  pl.BlockSpec((tm, d), lambda i: (jnp.maximum(i - nct, 0) % npos, 0))],
        out_specs=pl.BlockSpec((tm, d), lambda i: (i, 0)),
        compiler_params=_cparams("parallel"),
        name="embed",
    )(xp, xs, pos)


_RESIDENT = pl.Buffered(1)


def _in_kernel(x_ref, g_ref, mod_ref, w_ref, b_ref, o_ref, *side_ref, chunk):
    m = mod_ref[...]
    h = _norm_mod(x_ref[...], g_ref[...], m[0:1, :], m[1:2, :]).astype(BF16)
    n = o_ref.shape[1]
    for c0 in range(0, n, chunk):
        cols = slice(c0, c0 + chunk)
        y = jnp.dot(h, w_ref[:, cols], preferred_element_type=F32) + b_ref[:, cols]
        o_ref[:, cols] = y.astype(o_ref.dtype)
    if side_ref:
        side_ref[0][...] = jnp.dot(h, w_ref[:, n:], preferred_element_type=F32) + b_ref[:, n:]


def _in_proj(x, g, mods_all, layer, w_all, w_index, b, rows, n_side=0):
    t, d = x.shape
    n = w_all.shape[2] - n_side
    tm = rows.tm
    chunk = _tile(n, 512, V7X_LANES)
    out_shape = [jax.ShapeDtypeStruct((t, n), BF16)]
    out_specs = [pl.BlockSpec((tm, n), lambda i: (i, 0))]
    if n_side:
        out_shape.append(jax.ShapeDtypeStruct((t, n_side), F32))
        out_specs.append(pl.BlockSpec((tm, n_side), lambda i: (i, 0)))
    out = pl.pallas_call(
        functools.partial(_in_kernel, chunk=chunk),
        out_shape=out_shape,
        grid=(rows.n_tiles,),
        in_specs=[pl.BlockSpec((tm, d), lambda i: (i, 0)),
                  pl.BlockSpec((1, d), lambda i: (0, 0), pipeline_mode=_RESIDENT),
                  pl.BlockSpec((None, None, 6, d), lambda i: (layer, rows.mod_index(i), 0, 0)),
                  pl.BlockSpec((None, d, n + n_side), lambda i: (w_index, 0, 0), pipeline_mode=_RESIDENT),
                  pl.BlockSpec((1, n + n_side), lambda i: (0, 0), pipeline_mode=_RESIDENT)],
        out_specs=out_specs,
        compiler_params=_cparams("parallel"),
        name="in_proj",
    )(x, g.reshape(1, d), mods_all, w_all, b.reshape(1, n + n_side))
    return out if n_side else out[0]


def _ffn_kernel(x_ref, g_ref, mod_ref, wgu_ref, wd_ref, *rest, chunk, has_mixer, has_final):
    o_ref = rest[-1]
    x = x_ref[...]
    m = mod_ref[...]
    if has_mixer:
        a_ref, wo_ref, bo_ref = rest[:3]
        x = x + m[2:3, :] * (_dot(a_ref[...], wo_ref[...]) + bo_ref[...])
    h = _norm_mod(x, g_ref[...], m[3:4, :], m[4:5, :]).astype(BF16)
    f = wd_ref.shape[0]
    n_chunks = f // chunk

    def gate_up(k):
        gate = jnp.dot(h, wgu_ref[:, k * chunk:(k + 1) * chunk], preferred_element_type=F32)
        up = jnp.dot(h, wgu_ref[:, f + k * chunk:f + (k + 1) * chunk], preferred_element_type=F32)
        return gate, up

    y = None
    pending = gate_up(0)
    for k in range(n_chunks):
        following = gate_up(k + 1) if k + 1 < n_chunks else None
        act = (_silu(pending[0]) * pending[1]).astype(BF16)
        part = jnp.dot(act, wd_ref[k * chunk:(k + 1) * chunk, :], preferred_element_type=F32)
        y = part if y is None else y + part
        pending = following
    out = x + m[5:6, :] * y
    if has_final:
        out = out * lax.rsqrt(jnp.mean(out * out, axis=-1, keepdims=True) + EPS) * rest[-2][...]
    o_ref[...] = out


def _ffn(x, g, mods_all, w_gu_all, w_down_all, layer, rows, mixer=None, final_g=None, tile0=0, n_tiles=None):
    d = x.shape[1]
    f = w_down_all.shape[1]
    tm = rows.tm
    n_tiles = rows.n_tiles if n_tiles is None else n_tiles
    chunk = _tile(f, 256, V7X_LANES)
    resident = lambda *s: pl.BlockSpec(s, lambda i: (0,) * len(s), pipeline_mode=_RESIDENT)
    in_specs = [pl.BlockSpec((tm, d), lambda i: (i + tile0, 0)),
                resident(1, d),
                pl.BlockSpec((None, None, 6, d), lambda i: (layer, rows.mod_index(i + tile0), 0, 0)),
                pl.BlockSpec((None, d, 2 * f), lambda i: (layer, 0, 0), pipeline_mode=_RESIDENT),
                pl.BlockSpec((None, f, d), lambda i: (layer, 0, 0), pipeline_mode=_RESIDENT)]
    args = [x, g.reshape(1, d), mods_all, w_gu_all, w_down_all]
    if mixer is not None:
        a, wo_all, wo_index, bo = mixer
        k = a.shape[1]
        in_specs += [pl.BlockSpec((tm, k), lambda i: (i + tile0, 0)),
                     pl.BlockSpec((None, k, d), lambda i: (wo_index, 0, 0), pipeline_mode=_RESIDENT),
                     resident(1, d)]
        args += [a, wo_all, bo.reshape(1, d)]
    if final_g is not None:
        in_specs.append(resident(1, d))
        args.append(final_g.reshape(1, d))
    return pl.pallas_call(
        functools.partial(_ffn_kernel, chunk=chunk, has_mixer=mixer is not None, has_final=final_g is not None),
        out_shape=jax.ShapeDtypeStruct((n_tiles * tm, d), F32),
        grid=(n_tiles,),
        in_specs=in_specs,
        out_specs=pl.BlockSpec((tm, d), lambda i: (i, 0)),
        compiler_params=_cparams("parallel"),
        name="ffn",
    )(*args)


def _dwconv(x, w, k):
    n = x.shape[0]
    half = k // 2
    e = V7X_SUBLANES
    assert half <= e and n >= 4 * e
    taps = [w[j:j + 1, :] for j in range(k)]
    shifts = [s for s in range(-half, half + 1) if s != 0]

    def conv(v, mask):
        acc = v * taps[half]
        for s in shifts:
            shifted = pltpu.roll(v, (-s) % v.shape[0], axis=0)
            acc = acc + (shifted if mask is None else jnp.where(mask(s), shifted, 0.0)) * taps[s + half]
        return acc

    row = lax.broadcasted_iota(jnp.int32, (2 * e, x.shape[1]), 0)
    top = conv(x[:2 * e], lambda s: row + s >= 0)[:e]
    bottom = conv(x[n - 2 * e:], lambda s: row + s < 2 * e)[e:]
    return jnp.concatenate([top, conv(x, None)[e:n - e], bottom], axis=0)


CONV_HALO = 16


def _dwconv_window(ref, cols, w, k, r0, rows, slab_ref):
    n = ref.shape[0]
    if rows == n:
        return _dwconv(ref[:, cols].astype(F32), w, k)
    half = k // 2
    h = CONV_HALO
    assert half <= h <= rows and n % rows == 0
    before = ref[pl.ds(pl.multiple_of(jnp.maximum(r0 - h, 0), h), h), cols].astype(F32)
    after = ref[pl.ds(pl.multiple_of(jnp.minimum(r0 + rows, n - h), h), h), cols].astype(F32)
    slab_ref[0:h, :] = jnp.where(r0 > 0, before, 0.0)
    slab_ref[h:h + rows, :] = ref[pl.ds(r0, rows), cols].astype(F32)
    slab_ref[h + rows:, :] = jnp.where(r0 + rows < n, after, 0.0)
    acc = None
    for s in range(-half, half + 1):
        term = slab_ref[h + s:h + s + rows, :] * w[s + half:s + half + 1, :]
        acc = term if acc is None else acc + term
    return acc


TRI_BASE = 8
GDN_PHASE1_CHAINS = 32
GDN_HEADS_PER_STEP = 8
GDN_ROWS_PER_STEP = 4096


def _unit_tri_inverses_minus_eye(mats, ri, ci):
    c = mats[0].shape[0]

    def same_block(s):
        sh = int(math.log2(s))
        return (ri >> sh) == (ci >> sh)

    ps = [jnp.where(same_block(TRI_BASE), -a, 0.0) for a in mats]
    es = ps
    n_lvl = int(math.log2(TRI_BASE))
    for lvl in range(n_lvl):
        es = [e + _dot(p, e) for p, e in zip(ps, es)]
        if lvl < n_lvl - 1:
            ps = [_dot(p, p) for p in ps]
    s = TRI_BASE
    while s < c:
        mask = jnp.logical_and(same_block(2 * s), jnp.logical_not(same_block(s)))
        offs = [jnp.where(mask, a, 0.0) for a in mats]
        ys = [off + _dot(off, e) for off, e in zip(offs, es)]
        es = [e - (y + _dot(e, y)) for e, y in zip(es, ys)]
        s *= 2
    return es


def _chunk_cumsum(x, pos, reverse):
    n = x.shape[0]
    s = 1
    while s < CHUNK:
        if reverse:
            x = x + jnp.where(pos + s < CHUNK, pltpu.roll(x, n - s, axis=0), 0.0)
        else:
            x = x + jnp.where(pos >= s, pltpu.roll(x, s, axis=0), 0.0)
        s *= 2
    return x


def _gdn_kernel(*refs, n_heads, conv_k, lockstep, n_kept):
    (alog_ref, dtb_ref, q_ref, k_ref, v_ref, gt_ref, ab_ref, cq_ref, ck_ref, cv_ref,
     onorm_ref, s0_ref) = refs[:12]
    o_ref, sfin_ref, gates_s, conv_s, w_s, u_s, qd_s, ak_s, gl_s, o_s, st_s = refs[12 + n_kept:]
    n_chains, seq, dk = w_s.shape
    hb = n_chains // 2
    head0 = pl.program_id(1) * hb
    c = CHUNK
    n_chunks = seq // c
    chains = [(hh, d) for hh in range(hb) for d in range(2)]

    @pl.when(head0 == 0)
    def _():
        ab = ab_ref[...]
        pos = jnp.bitwise_and(lax.broadcasted_iota(jnp.int32, ab.shape, 0), c - 1)
        g = -jnp.exp(alog_ref[...]) * jax.nn.softplus(ab + dtb_ref[...])
        g_fwd = _chunk_cumsum(g, pos, reverse=False)
        g_rev = _chunk_cumsum(g, pos, reverse=True)
        gates_s[0] = g_fwd
        gates_s[1] = g_rev - g
        gates_s[2] = g_rev
        gates_s[3] = g_fwd - g
        gates_s[4] = jax.nn.sigmoid(ab)

    lane = lax.broadcasted_iota(jnp.int32, (1, V7X_LANES), 1)

    rows = lockstep * c

    def column(i, idx, win):
        one_hot = (lane == idx).astype(F32)
        col = jnp.sum(gates_s[i, win, :] * one_hot, axis=1, keepdims=True)
        return jnp.broadcast_to(col, (rows, V7X_LANES))

    ri = lax.broadcasted_iota(jnp.int32, (c, c), 0)
    ci = lax.broadcasted_iota(jnp.int32, (c, c), 1)
    incl = (ri >= ci, ri <= ci)
    strict = (ri > ci, ri < ci)

    def phase1(it, carry):
        r0 = pl.multiple_of(it * rows, rows)
        win = pl.ds(r0, rows)
        prepared = []
        for hh in range(hb):
            cols = slice(hh * dk, (hh + 1) * dk)
            q = _silu(_dwconv_window(q_ref, cols, cq_ref[:, cols], conv_k, r0, rows, conv_s.at[3 * hh]))
            k = _silu(_dwconv_window(k_ref, cols, ck_ref[:, cols], conv_k, r0, rows, conv_s.at[3 * hh + 1]))
            v = _silu(_dwconv_window(v_ref, cols, cv_ref[:, cols], conv_k, r0, rows, conv_s.at[3 * hh + 2]))
            q = q * (lax.rsqrt(jnp.sum(q * q, axis=-1, keepdims=True) + EPS) * (dk ** -0.5))
            k = k * lax.rsqrt(jnp.sum(k * k, axis=-1, keepdims=True) + EPS)
            for d in range(2):
                head = head0 + hh
                g_cum = column(2 * d, d * n_heads + head, win)
                g_tail = column(2 * d + 1, d * n_heads + head, win)
                beta = column(4, 2 * n_heads + d * n_heads + head, win)
                e_cum = jnp.exp(g_cum)
                kb = k * beta
                prepared.append((q, k, g_cum, kb, kb * e_cum, v * beta, k * jnp.exp(g_tail)))
                qd_s[2 * hh + d, win, :] = (q * e_cum).astype(qd_s.dtype)
                g_tot = jnp.exp(g_cum + g_tail)
                for gi in range(lockstep):
                    gl_s[2 * hh + d, pl.ds(it * lockstep + gi, 1), :] = g_tot[gi * c:gi * c + 1, :]
        items = []
        for gi in range(lockstep):
            part = slice(gi * c, (gi + 1) * c)
            for ch, (_, d) in enumerate(chains):
                sl = pl.ds(pl.multiple_of(r0 + gi * c, c), c)
                items.append((it * lockstep + gi, sl, ch, d) + tuple(a[part] for a in prepared[ch]))
        decays, kqs = [], []
        for ic, sl, ch, d, qc, kc, gc, kb, w0, u0, ktl in items:
            diff = gc[:, :c] - gc.T[:c, :]
            decays.append(jnp.where(incl[d], jnp.exp(jnp.where(incl[d], diff, 0.0)), 0.0))
            kqs.append(_dot_nt(jnp.concatenate([kb, qc], axis=0), kc))
        a_kks = [jnp.where(strict[item[3]], kq[:c] * decay, 0.0)
                 for item, kq, decay in zip(items, kqs, decays)]
        es = _unit_tri_inverses_minus_eye(a_kks, ri, ci)
        rhss = [jnp.concatenate([item[8], item[9]], axis=1) for item in items]
        wus = [rhs + _dot(e, rhs) for e, rhs in zip(es, rhss)]
        for item, kq, decay, wu in zip(items, kqs, decays, wus):
            ic, sl, ch = item[:3]
            w_s[ch, sl, :] = wu[:, :dk].astype(w_s.dtype)
            u_s[ch, sl, :] = wu[:, dk:]
            ak_s[ch, ic] = jnp.concatenate([kq[c:] * decay, item[10].T], axis=0).astype(ak_s.dtype)
        return carry

    lax.fori_loop(0, n_chunks // lockstep, phase1, 0)

    for hh, d in chains:
        st_s[2 * hh + d] = s0_ref[d, hh]

    def phase2(i, carry):
        n = 2 * hb
        ics = [n_chunks - 1 - i if d else i for _, d in chains]
        r0s = [pl.multiple_of(ic * c, c) for ic in ics]
        sls = [pl.ds(r0, c) for r0 in r0s]
        ss = [st_s[ch] for ch in range(n)]
        wqs = [_dot(jnp.concatenate([w_s[ch, sls[ch], :], qd_s[ch, sls[ch], :]], axis=0), ss[ch]) for ch in range(n)]
        v_news = [u_s[ch, sls[ch], :] - wqs[ch][:c] for ch in range(n)]
        outs = [_dot(ak_s[ch, ics[ch]], v_news[ch]) for ch in range(n)]
        for ch in range(n):
            o_s[ch, sls[ch], :] = wqs[ch][c:] + outs[ch][:c]
            st_s[ch] = ss[ch] * gl_s[ch, pl.ds(ics[ch], 1), :] + outs[ch][c:]
        return carry

    lax.fori_loop(0, n_chunks, phase2, 0, unroll=4)

    for hh, d in chains:
        sfin_ref[d, hh] = st_s[2 * hh + d]
    for hh in range(hb):
        cols = slice(hh * dk, (hh + 1) * dk)
        o = o_s[2 * hh] + o_s[2 * hh + 1]
        o = o * lax.rsqrt(jnp.mean(o * o, axis=-1, keepdims=True) + EPS)
        o_ref[:, cols] = (o * onorm_ref[...] * _silu(gt_ref[:, cols].astype(F32))).astype(o_ref.dtype)


def _gdn_core(proj, ab, conv_w, a_log, dt_bias, onorm, s0, dst, states_dst, *, t_total, row0, n_seq, seq,
              n_heads, dk, dv, s0_spec, state_slot):
    assert dk == dv == V7X_LANES and row0 % seq == 0
    conv_k = conv_w.shape[0]
    r0 = row0 // seq
    n_chunks = seq // CHUNK
    hb = _tile(n_heads, min(GDN_HEADS_PER_STEP, max(1, GDN_ROWS_PER_STEP // seq)), 1)
    lockstep = _tile(n_chunks, max(1, GDN_PHASE1_CHAINS // (2 * hb)), 1)
    gate_pad = lambda p: jnp.pad(p.reshape(1, 2 * n_heads), ((0, 0), (0, V7X_LANES - 2 * n_heads)))
    lane_vec = pl.BlockSpec((1, V7X_LANES), lambda b, h: (0, 0))
    if s0 is None:
        s0 = jnp.zeros((2, hb, dk, dv), F32)
        s0_in = pl.BlockSpec((2, hb, dk, dv), lambda b, h: (0, 0, 0, 0))
    else:
        s0_in = s0_spec(hb)
    nb = n_heads // hb
    col = lambda sec: pl.BlockSpec((seq, hb * dk), lambda b, h: (b + r0, sec * nb + h))
    cw = lambda sec: pl.BlockSpec((conv_k, hb * dk), lambda b, h: (0, sec * nb + h))
    f32 = lambda *s: pltpu.VMEM(s, F32)
    in_specs = [lane_vec, lane_vec, col(0), col(1), col(2), col(3),
                pl.BlockSpec((seq, V7X_LANES), lambda b, h: (b + r0, 0)),
                cw(0), cw(1), cw(2),
                pl.BlockSpec((1, dv), lambda b, h: (0, 0)),
                s0_in]
    args = [gate_pad(a_log), gate_pad(dt_bias), proj, proj, proj, proj, ab, conv_w, conv_w, conv_w,
            onorm.reshape(1, dv), s0]
    aliases = {}
    for out_idx, kept in enumerate((dst, states_dst)):
        if kept is not None:
            in_specs.append(pl.BlockSpec(memory_space=pl.ANY))
            args.append(kept)
            aliases[len(args) - 1] = out_idx
    nc = 2 * hb
    slot, n_slots = state_slot
    return pl.pallas_call(
        functools.partial(_gdn_kernel, n_heads=n_heads, conv_k=conv_k, lockstep=lockstep,
                          n_kept=len(aliases)),
        out_shape=(jax.ShapeDtypeStruct((t_total, n_heads * dv), BF16),
                   jax.ShapeDtypeStruct((n_seq, n_slots, 2, n_heads, dk, dv), F32)),
        grid=(n_seq, nb),
        in_specs=in_specs,
        out_specs=(pl.BlockSpec((seq, hb * dv), lambda b, h: (b + r0, h)),
                   pl.BlockSpec((None, None, 2, hb, dk, dv), lambda b, h: (b, slot, 0, h, 0, 0))),
        scratch_shapes=[f32(5, seq, V7X_LANES),
                        f32(3 * hb, lockstep * CHUNK + 2 * CONV_HALO, dk),
                        pltpu.VMEM((nc, seq, dk), BF16), f32(nc, seq, dv), pltpu.VMEM((nc, seq, dk), BF16),
                        pltpu.VMEM((nc, n_chunks, CHUNK + dk, CHUNK), BF16),
                        f32(nc, max(n_chunks, V7X_SUBLANES), V7X_LANES),
                        f32(nc, seq, dv), f32(nc, dk, dv)],
        input_output_aliases=aliases,
        compiler_params=_cparams("parallel", "arbitrary"),
        name="gdn_core",
    )(*args)


def _folded_odd_dft(seq):
    half = seq // 2
    k = np.arange(half, dtype=np.int64)[:, None]
    parts = []
    for parity in (0, 1):
        m = 2 * np.arange(half, dtype=np.int64)[None, :] + parity
        ang = (((2 * k + 1) * m) % (4 * seq)) * (math.pi / (2 * seq))
        parts.append(np.concatenate([np.cos(ang), np.sin(ang)], axis=0))
    return parts


def _bf16_head_tail(m):
    m32 = m.astype(np.float32)
    head = m32.astype(jnp.bfloat16)
    tail = (m32 - head.astype(np.float32)).astype(jnp.bfloat16)
    return jnp.asarray(head), jnp.asarray(tail)


def _filter_kernel(feat_ref, w1_ref, b1_ref, w2_ref, b2_ref, fr_ref, w3f_ref, w3b_ref, dl_ref,
                   fe_hi_ref, fe_lo_ref, fo_hi_ref, fo_lo_ref, hre_ref, him_ref, hid_s):
    seq = feat_ref.shape[0]
    half = seq // 2
    n_hid = w2_ref.shape[0]

    @pl.when(jnp.logical_and(pl.program_id(0) == 0, pl.program_id(1) == 0))
    def _():
        fr = fr_ref[...]
        hid1 = jnp.sin(fr * (_dot(feat_ref[...], w1_ref[...]) + b1_ref[...]))
        hid_s[...] = jnp.zeros_like(hid_s)
        hid_s[:, :n_hid] = jnp.sin(fr * (_dot(hid1, w2_ref[...]) + b2_ref[...]))

    def taps(parity):
        lags = pl.ds(parity, half, stride=2)
        hid = hid_s[lags, :][:, :n_hid]
        window = jnp.exp(-feat_ref[lags, :][:, 0:1] * dl_ref[...])
        hf = _dot(hid, w3f_ref[...]) * window
        hb = _dot(hid, w3b_ref[...]) * window
        if parity == 0:
            hb = jnp.where(lax.broadcasted_iota(jnp.int32, hb.shape, 0) == 0, 0.0, hb)
        return hf + hb, hb - hf

    def dft(m_hi, m_lo, val):
        v_hi = val.astype(BF16)
        v_lo = (val - v_hi.astype(F32)).astype(BF16)
        return (jnp.dot(m_hi, v_hi, preferred_element_type=F32)
                + (jnp.dot(m_hi, v_lo, preferred_element_type=F32)
                   + jnp.dot(m_lo, v_hi, preferred_element_type=F32)))

    (sum_e, dif_e), (sum_o, dif_o) = taps(0), taps(1)
    cos_rows, sin_rows = slice(0, half), slice(half, seq)
    re_e = dft(fe_hi_ref[cos_rows, :], fe_lo_ref[cos_rows, :], sum_e)
    re_o = dft(fo_hi_ref[cos_rows, :], fo_lo_ref[cos_rows, :], sum_o)
    im_e = dft(fe_hi_ref[sin_rows, :], fe_lo_ref[sin_rows, :], dif_e)
    im_o = dft(fo_hi_ref[sin_rows, :], fo_lo_ref[sin_rows, :], dif_o)
    hre_ref[0:half, :] = re_e + re_o
    hre_ref[half:, :] = re_e - re_o
    him_ref[0:half, :] = im_e + im_o
    him_ref[half:, :] = im_o - im_e


def _hyena_filters(seq, d, w1, b1, w2, b2, w3, freq, dft_parts):
    emb, hid = w1.shape
    bands = (emb - 1) // 2
    t = jnp.linspace(0.0, 1.0, seq, dtype=F32)[:, None]
    wpos = (2.0 * math.pi / seq) * jnp.arange(seq, dtype=F32)[:, None]
    fb = jnp.linspace(1e-4, bands - 1, bands, dtype=F32)[None, :]
    feat = jnp.concatenate([t, jnp.cos(fb * wpos), -jnp.sin(fb * wpos)], axis=-1)
    feat = jnp.pad(feat, ((0, 0), (0, V7X_LANES - emb)))
    w1p = jnp.pad(w1, ((0, V7X_LANES - emb), (0, 0)))
    max_decay = math.log(HY_TARGET) / HY_SHORT_PCT
    min_decay = math.log(HY_TARGET) / HY_LONG_PCT
    deltas = jnp.abs(jnp.linspace(min_decay, max_decay, d, dtype=F32))[None, :]
    tc = _tile(d, 256, V7X_LANES)
    nt = d // tc
    full = lambda r, c: pl.BlockSpec((r, c), lambda n, j: (0, 0))
    out_spec = pl.BlockSpec((None, seq, tc), lambda n, j: (n, 0, j))
    return pl.pallas_call(
        _filter_kernel,
        out_shape=(jax.ShapeDtypeStruct((HY_ORDER, seq, d), F32),) * 2,
        grid=(HY_ORDER, nt),
        in_specs=[full(seq, V7X_LANES), full(V7X_LANES, hid), full(1, hid), full(hid, hid), full(1, hid),
                  full(1, hid),
                  pl.BlockSpec((hid, tc), lambda n, j: (0, (2 * n) * nt + j)),
                  pl.BlockSpec((hid, tc), lambda n, j: (0, (2 * n + 1) * nt + j)),
                  pl.BlockSpec((1, tc), lambda n, j: (0, j))] + [full(seq, seq // 2)] * 4,
        out_specs=(out_spec, out_spec),
        scratch_shapes=[pltpu.VMEM((seq, V7X_LANES), F32)],
        compiler_params=_cparams("arbitrary", "arbitrary"),
        name="hyena_filter",
    )(feat, w1p, b1.reshape(1, hid), w2, b2.reshape(1, hid), freq.reshape(1, hid), w3, w3, deltas,
      *dft_parts)


HYENA_ROWS_PER_STEP = 2048


def _hyconv_kernel(*refs, conv_z, conv_k, has_dst):
    (z_ref, x_ref, hre_ref, him_ref, skip_ref, cz_ref, cx_ref,
     fwd_e_ref, fwd_o_ref, inv_e_ref, inv_o_ref) = refs[:11]
    o_ref = refs[11 + has_dst]
    zs_s, xs_s, os_s = refs[-3:]
    seq = hre_ref.shape[0]
    half = seq // 2
    n_sub = z_ref.shape[0] // seq
    even, odd = pl.ds(0, half, stride=2), pl.ds(1, half, stride=2)
    n_lane_tiles = zs_s.shape[1]

    def stage(dst, s, val):
        for l in range(n_lane_tiles):
            dst[s, l] = val[:, l * V7X_LANES:(l + 1) * V7X_LANES]

    def rows_of(src, s, rows):
        return jnp.concatenate([src[s, l, rows, :] for l in range(n_lane_tiles)], axis=1)

    for s in range(n_sub):
        z = z_ref[s * seq:(s + 1) * seq, :].astype(F32)
        stage(zs_s, s, _dwconv(z, cz_ref[...], conv_k) if conv_z else z)
        stage(xs_s, s, _dwconv(x_ref[s * seq:(s + 1) * seq, :].astype(F32), cx_ref[...], conv_k))
    z_parts = [(rows_of(zs_s, s, even), rows_of(zs_s, s, odd)) for s in range(n_sub)]
    fwds = [(jnp.dot(fwd_e_ref[...], ze.astype(BF16), preferred_element_type=F32),
             jnp.dot(fwd_o_ref[...], zo.astype(BF16), preferred_element_type=F32)) for ze, zo in z_parts]
    hre_a, hre_b = hre_ref[0:half, :], hre_ref[half:, :]
    him_a, him_b = him_ref[0:half, :], him_ref[half:, :]
    skip = skip_ref[...]
    for s in range(n_sub):
        (fe, fo), (ze, zo) = fwds[s], z_parts[s]
        pe, qe, po, qo = fe[:half], fe[half:], fo[:half], fo[half:]
        pa, pb, qa, qb = pe + po, pe - po, qe + qo, qo - qe
        yre_a, yim_a = pa * hre_a + qa * him_a, pa * him_a - qa * hre_a
        yre_b, yim_b = pb * hre_b + qb * him_b, pb * him_b - qb * hre_b
        y_e = jnp.dot(inv_e_ref[...], jnp.concatenate([yre_a + yre_b, yim_a - yim_b], axis=0).astype(BF16),
                      preferred_element_type=F32)
        y_o = jnp.dot(inv_o_ref[...], jnp.concatenate([yre_a - yre_b, yim_a + yim_b], axis=0).astype(BF16),
                      preferred_element_type=F32)
        out_e = rows_of(xs_s, s, even) * (y_e + skip * ze)
        out_o = rows_of(xs_s, s, odd) * (y_o + skip * zo)
        for l in range(n_lane_tiles):
            lanes = slice(l * V7X_LANES, (l + 1) * V7X_LANES)
            os_s[s, l, even, :] = out_e[:, lanes]
            os_s[s, l, odd, :] = out_o[:, lanes]
        o_ref[s * seq:(s + 1) * seq, :] = rows_of(os_s, s, slice(None)).astype(o_ref.dtype)


def _hyconv(z, z_col0, conv_z, proj, x_col0, hre, him, order, skip, conv_w, dft, dst,
            *, row0, n_seq, seq, d, out_rows, out_row0, out_dtype):
    conv_k = conv_w.shape[0]
    n_sub = max(s for s in range(1, max(1, HYENA_ROWS_PER_STEP // seq) + 1)
                if n_seq % s == 0 and row0 % (s * seq) == 0 and out_row0 % (s * seq) == 0)
    blk = n_sub * seq
    tc = _tile(d, 512 if blk <= 1024 else 256, V7X_LANES)
    nt = d // tc
    r0 = row0 // blk
    zr0 = r0 if conv_z else 0
    zc, xc = z_col0 // tc, x_col0 // tc
    in_specs = [pl.BlockSpec((blk, tc), lambda j, b: (b + zr0, zc + j)),
                pl.BlockSpec((blk, tc), lambda j, b: (b + r0, xc + j)),
                pl.BlockSpec((None, seq, tc), lambda j, b: (order, 0, j)),
                pl.BlockSpec((None, seq, tc), lambda j, b: (order, 0, j)),
                pl.BlockSpec((1, tc), lambda j, b: (0, j)),
                pl.BlockSpec((conv_k, tc), lambda j, b: (0, zc + j)),
                pl.BlockSpec((conv_k, tc), lambda j, b: (0, xc + j)),
                pl.BlockSpec((seq, seq // 2), lambda j, b: (0, 0), pipeline_mode=_RESIDENT),
                pl.BlockSpec((seq, seq // 2), lambda j, b: (0, 0), pipeline_mode=_RESIDENT),
                pl.BlockSpec((seq // 2, seq), lambda j, b: (0, 0), pipeline_mode=_RESIDENT),
                pl.BlockSpec((seq // 2, seq), lambda j, b: (0, 0), pipeline_mode=_RESIDENT)]
    args = [z, proj, hre, him, skip.reshape(1, d), conv_w, conv_w, *dft]
    aliases = {}
    if dst is not None:
        in_specs.append(pl.BlockSpec(memory_space=pl.ANY))
        args.append(dst)
        aliases = {len(args) - 1: 0}
    out_r0 = out_row0 // blk
    return pl.pallas_call(
        functools.partial(_hyconv_kernel, conv_z=conv_z, conv_k=conv_k, has_dst=dst is not None),
        out_shape=jax.ShapeDtypeStruct((out_rows, d), out_dtype),
        grid=(nt, n_seq // n_sub),
        in_specs=in_specs,
        out_specs=pl.BlockSpec((blk, tc), lambda j, b: (b + out_r0, j)),
        scratch_shapes=[pltpu.VMEM((n_sub, tc // V7X_LANES, seq, V7X_LANES), F32)] * 3,
        input_output_aliases=aliases,
        compiler_params=_cparams("parallel", "parallel"),
        name="hyena_conv",
    )(*args)


def _grid_pos_emb(n_tokens, d):
    rows = n_tokens // GRID_W
    r, col = jnp.meshgrid(jnp.arange(rows), jnp.arange(GRID_W), indexing='ij')
    quarter = d // 4
    omega = 1.0 / (POS_BASE ** (jnp.arange(quarter, dtype=F32) / quarter))

    def emb1d(p):
        a = p.reshape(-1, 1).astype(F32) * omega[None, :]
        return jnp.concatenate([jnp.sin(a), jnp.cos(a)], axis=-1)

    return jnp.concatenate([emb1d(r), emb1d(col)], axis=-1)


def kernel(x_prompt, x_sample, state_delta, c, c_ctx, ada_w, ada_b, norm1_g, norm2_g, gdn_w_in, gdn_conv, gdn_a_log, gdn_dt_bias, gdn_onorm, gdn_w_out, hy_w_in, hy_b_in, hy_conv, hy_f_w1, hy_f_b1, hy_f_w2, hy_f_b2, hy_f_w3, hy_freq, hy_skip, hy_w_out, hy_b_out, ffn_w_gu, ffn_w_down, final_g):
    bc, lc, d = x_prompt.shape
    bl, ll, _ = x_sample.shape
    depth = ada_w.shape[0]
    n_heads, dk, dv = state_delta.shape[3:]
    tc_rows, tl_rows = bc * lc, bl * ll
    t = tc_rows + tl_rows
    assert tc_rows % ll == 0 and ll % lc == 0
    rows = _Rows(tc_rows, ll, t, _tile(math.gcd(tc_rows, ll), 512, V7X_SUBLANES))

    bm = 1 + bl
    bm_pad = -(-bm // V7X_SUBLANES) * V7X_SUBLANES
    cvec = jnp.concatenate([c_ctx[None, :], c, jnp.zeros((bm_pad - bm, d), F32)], axis=0)
    mods_all = _ada(cvec, ada_w, ada_b).reshape(depth, bm_pad, 6, d)

    x = _embed(x_prompt.reshape(tc_rows, d), x_sample.reshape(tl_rows, d), _grid_pos_emb(ll, d), rows)

    n_hy = hy_w_in.shape[0]
    dft = {}
    for seq in (lc, ll):
        half = seq // 2
        fwd_e, fwd_o = _folded_odd_dft(seq)
        (fe_hi, fe_lo), (fo_hi, fo_lo) = _bf16_head_tail(fwd_e), _bf16_head_tail(fwd_o)
        inv_e, _ = _bf16_head_tail(np.concatenate([fwd_e[:half].T, -fwd_e[half:].T], axis=1) / seq)
        inv_o, _ = _bf16_head_tail(np.concatenate([fwd_o[:half].T, -fwd_o[half:].T], axis=1) / seq)
        spectra = [_hyena_filters(seq, d, hy_f_w1[j], hy_f_b1[j], hy_f_w2[j], hy_f_b2[j], hy_f_w3[j],
                                  hy_freq[j], (fe_hi, fe_lo, fo_hi, fo_lo)) for j in range(n_hy)]
        dft[seq] = ((fe_hi, fo_hi, inv_e, inv_o), spectra)

    zero_b = jnp.zeros((d,), F32)
    mixed = jnp.zeros((t, n_heads * dv), BF16)
    shared = n_heads * dv == d
    mixed_hy = mixed if shared else jnp.zeros((t, d), BF16)
    new_state_delta = jnp.zeros((bc, gdn_w_in.shape[0], 2, n_heads, dk, dv), F32)
    n_mixers = 2
    n_gdn = gdn_w_in.shape[0]
    gdn_w_in_b = jnp.pad(gdn_w_in.astype(BF16), ((0, 0), (0, 0), (0, V7X_LANES - 4 * n_heads)))
    gdn_w_out_b, hy_w_in_b, hy_w_out_b = (w.astype(BF16) for w in (gdn_w_out, hy_w_in, hy_w_out))
    ffn_w_gu_b, ffn_w_down_b = ffn_w_gu.astype(BF16), ffn_w_down.astype(BF16)
    for layer in range(depth):
        j = layer // n_mixers
        if layer % n_mixers == 0:
            proj, ab = _in_proj(x, norm1_g[layer], mods_all, layer, gdn_w_in_b, j,
                                jnp.zeros((gdn_w_in_b.shape[2],), F32), rows, n_side=V7X_LANES)
            gdn = dict(t_total=t, n_heads=n_heads, dk=dk, dv=dv)
            weights = (gdn_conv[j], gdn_a_log[j], gdn_dt_bias[j], gdn_onorm[j])
            mixed, new_state_delta = _gdn_core(proj, ab, *weights, None, mixed, new_state_delta, row0=0, n_seq=bc,
                                               seq=lc, s0_spec=None, state_slot=(j, n_gdn), **gdn)
            s0_spec = lambda hb, j=j: pl.BlockSpec((None, None, 2, hb, dk, dv), lambda b, h: (b, j, 0, h, 0, 0))
            mixed, _ = _gdn_core(proj, ab, *weights, state_delta, mixed, None, row0=tc_rows, n_seq=bl, seq=ll,
                                 s0_spec=s0_spec, state_slot=(0, 1), **gdn)
            mixer = (mixed, gdn_w_out_b, j, zero_b)
            mixed_hy = mixed if shared else mixed_hy
        else:
            proj = _in_proj(x, norm1_g[layer], mods_all, layer, hy_w_in_b, j, hy_b_in[j], rows)
            for row0, n_seq, seq in ((0, bc, lc), (tc_rows, bl, ll)):
                dft_mats, spectra = dft[seq]
                hre, him = spectra[j]
                z = proj
                for n in range(HY_ORDER):
                    last = n == HY_ORDER - 1
                    z = _hyconv(z, 0, n == 0, proj, (n + 1) * d, hre, him, n, hy_skip[j, n], hy_conv[j],
                                dft_mats, mixed_hy if last else None, row0=row0, n_seq=n_seq, seq=seq, d=d,
                                out_rows=t if last else n_seq * seq, out_row0=row0 if last else 0,
                                out_dtype=BF16 if last else F32)
                mixed_hy = z
            mixer = (mixed_hy, hy_w_out_b, j, hy_b_out[j])
            mixed = mixed_hy if shared else mixed
        ffn = functools.partial(_ffn, x, norm2_g[layer], mods_all, ffn_w_gu_b, ffn_w_down_b, layer, rows,
                                mixer=mixer)
        if layer < depth - 1:
            x = ffn()
        else:
            y_prompt = ffn(final_g=final_g, tile0=0, n_tiles=rows.n_ctx_tiles)
            y_sample = ffn(final_g=final_g, tile0=rows.n_ctx_tiles, n_tiles=rows.n_tiles - rows.n_ctx_tiles)
    return (y_prompt.reshape(bc, lc, d), y_sample.reshape(bl, ll, d), new_state_delta)
```

```python
import functools
import math

import jax
import jax.numpy as jnp
import numpy as np
from jax import lax
from jax.experimental import pallas as pl
from jax.experimental.pallas import tpu as pltpu

GRID_W = 64
CHUNK = 64
HY_ORDER = 2
HY_TARGET = 1e-2
HY_SHORT_PCT = 0.3
HY_LONG_PCT = 1.5
POS_BASE = 10000.0
EPS = 1e-6

V7X_LANES = 128
V7X_SUBLANES = 8
V7X_VMEM_LIMIT_BYTES = 48 * 1024 * 1024

BF16 = jnp.bfloat16
F32 = jnp.float32


def _cparams(*sem):
    return pltpu.CompilerParams(dimension_semantics=sem, vmem_limit_bytes=V7X_VMEM_LIMIT_BYTES)


def _tile(n, target, align):
    if n <= target:
        return n
    best = None
    for t in range(align, target + 1, align):
        if n % t == 0:
            best = t
    assert best is not None, (n, target, align)
    return best


def _dot(a, b):
    return jnp.dot(a.astype(BF16), b.astype(BF16), preferred_element_type=F32)


def _dot_nt(a, b):
    return lax.dot_general(a.astype(BF16), b.astype(BF16), (((1,), (1,)), ((), ())),
                           preferred_element_type=F32)


def _silu(x):
    half = 0.5 * x
    return half + half * jnp.tanh(half)


def _norm_mod(x, g, shift, scale):
    ms = jnp.mean(x * x, axis=-1, keepdims=True)
    return (x * lax.rsqrt(ms + EPS)) * (g * (1.0 + scale)) + shift


class _Rows:
    def __init__(self, tc, ll, t, tm):
        assert tc % tm == 0 and ll % tm == 0 and t % tm == 0
        self.n_ctx_tiles = tc // tm
        self.tiles_per_lat = ll // tm
        self.n_tiles = t // tm
        self.tm = tm

    def mod_index(self, i):
        lat = 1 + (i - self.n_ctx_tiles) // self.tiles_per_lat
        return jnp.where(i < self.n_ctx_tiles, 0, lat)


def _ada_kernel(c_ref, w_ref, b_ref, o_ref):
    o_ref[...] = _dot(_silu(c_ref[...]), w_ref[...]) + b_ref[...]


def _ada(cvec, ada_w, ada_b):
    depth, d, n = ada_w.shape
    bm = cvec.shape[0]
    tn = _tile(n, 1536, V7X_LANES)
    return pl.pallas_call(
        _ada_kernel,
        out_shape=jax.ShapeDtypeStruct((depth, bm, n), F32),
        grid=(depth, n // tn),
        in_specs=[pl.BlockSpec((bm, d), lambda l, j: (0, 0)),
                  pl.BlockSpec((None, d, tn), lambda l, j: (l, 0, j)),
                  pl.BlockSpec((None, 1, tn), lambda l, j: (l, 0, j))],
        out_specs=pl.BlockSpec((None, bm, tn), lambda l, j: (l, 0, j)),
        compiler_params=_cparams("parallel", "parallel"),
        name="ada",
    )(cvec, ada_w, ada_b.reshape(depth, 1, n))


def _embed_kernel(xp_ref, xs_ref, pos_ref, o_ref, *, n_ctx_tiles):
    i = pl.program_id(0)

    @pl.when(i < n_ctx_tiles)
    def _():
        o_ref[...] = xp_ref[...]

    @pl.when(i >= n_ctx_tiles)
    def _():
        o_ref[...] = xs_ref[...] + pos_ref[...]


def _embed(xp, xs, pos, rows):
    t, d = xp.shape[0] + xs.shape[0], xp.shape[1]
    tm, nct = rows.tm, rows.n_ctx_tiles
    npos = pos.shape[0] // tm
    return pl.pallas_call(
        functools.partial(_embed_kernel, n_ctx_tiles=nct),
        out_shape=jax.ShapeDtypeStruct((t, d), F32),
        grid=(rows.n_tiles,),
        in_specs=[pl.BlockSpec((tm, d), lambda i: (jnp.minimum(i, nct - 1), 0)),
                  pl.BlockSpec((tm, d), lambda i: (jnp.maximum(i - nct, 0), 0)),
                  pl.BlockSpec((tm, d), lambda i: (jnp.maximum(i - nct, 0) % npos, 0))],
        out_specs=pl.BlockSpec((tm, d), lambda i: (i, 0)),
        compiler_params=_cparams("parallel"),
        name="embed",
    )(xp, xs, pos)


_RESIDENT = pl.Buffered(1)


def _in_kernel(x_ref, g_ref, mod_ref, w_ref, b_ref, o_ref, *side_ref, chunk):
    m = mod_ref[...]
    h = _norm_mod(x_ref[...], g_ref[...], m[0:1, :], m[1:2, :]).astype(BF16)
    n = o_ref.shape[1]
    for c0 in range(0, n, chunk):
        cols = slice(c0, c0 + chunk)
        y = jnp.dot(h, w_ref[:, cols], preferred_element_type=F32) + b_ref[:, cols]
        o_ref[:, cols] = y.astype(o_ref.dtype)
    if side_ref:
        side_ref[0][...] = jnp.dot(h, w_ref[:, n:], preferred_element_type=F32) + b_ref[:, n:]


def _in_proj(x, g, mods_all, layer, w_all, w_index, b, rows, n_side=0):
    t, d = x.shape
    n = w_all.shape[2] - n_side
    tm = rows.tm
    chunk = _tile(n, 512, V7X_LANES)
    out_shape = [jax.ShapeDtypeStruct((t, n), BF16)]
    out_specs = [pl.BlockSpec((tm, n), lambda i: (i, 0))]
    if n_side:
        out_shape.append(jax.ShapeDtypeStruct((t, n_side), F32))
        out_specs.append(pl.BlockSpec((tm, n_side), lambda i: (i, 0)))
    out = pl.pallas_call(
        functools.partial(_in_kernel, chunk=chunk),
        out_shape=out_shape,
        grid=(rows.n_tiles,),
        in_specs=[pl.BlockSpec((tm, d), lambda i: (i, 0)),
                  pl.BlockSpec((1, d), lambda i: (0, 0), pipeline_mode=_RESIDENT),
                  pl.BlockSpec((None, None, 6, d), lambda i: (layer, rows.mod_index(i), 0, 0)),
                  pl.BlockSpec((None, d, n + n_side), lambda i: (w_index, 0, 0), pipeline_mode=_RESIDENT),
                  pl.BlockSpec((1, n + n_side), lambda i: (0, 0), pipeline_mode=_RESIDENT)],
        out_specs=out_specs,
        compiler_params=_cparams("parallel"),
        name="in_proj",
    )(x, g.reshape(1, d), mods_all, w_all, b.reshape(1, n + n_side))
    return out if n_side else out[0]


def _ffn_kernel(x_ref, g_ref, mod_ref, wgu_ref, wd_ref, *rest, chunk, has_mixer, has_final):
    o_ref = rest[-1]
    x = x_ref[...]
    m = mod_ref[...]
    if has_mixer:
        a_ref, wo_ref, bo_ref = rest[:3]
        x = x + m[2:3, :] * (_dot(a_ref[...], wo_ref[...]) + bo_ref[...])
    h = _norm_mod(x, g_ref[...], m[3:4, :], m[4:5, :]).astype(BF16)
    f = wd_ref.shape[0]
    n_chunks = f // chunk

    def gate_up(k):
        gate = jnp.dot(h, wgu_ref[:, k * chunk:(k + 1) * chunk], preferred_element_type=F32)
        up = jnp.dot(h, wgu_ref[:, f + k * chunk:f + (k + 1) * chunk], preferred_element_type=F32)
        return gate, up

    y = None
    pending = gate_up(0)
    for k in range(n_chunks):
        following = gate_up(k + 1) if k + 1 < n_chunks else None
        act = (_silu(pending[0]) * pending[1]).astype(BF16)
        part = jnp.dot(act, wd_ref[k * chunk:(k + 1) * chunk, :], preferred_element_type=F32)
        y = part if y is None else y + part
        pending = following
    out = x + m[5:6, :] * y
    if has_final:
        out = out * lax.rsqrt(jnp.mean(out * out, axis=-1, keepdims=True) + EPS) * rest[-2][...]
    o_ref[...] = out


def _ffn(x, g, mods_all, w_gu_all, w_down_all, layer, rows, mixer=None, final_g=None, tile0=0, n_tiles=None):
    d = x.shape[1]
    f = w_down_all.shape[1]
    tm = rows.tm
    n_tiles = rows.n_tiles if n_tiles is None else n_tiles
    chunk = _tile(f, 256, V7X_LANES)
    resident = lambda *s: pl.BlockSpec(s, lambda i: (0,) * len(s), pipeline_mode=_RESIDENT)
    in_specs = [pl.BlockSpec((tm, d), lambda i: (i + tile0, 0)),
                resident(1, d),
                pl.BlockSpec((None, None, 6, d), lambda i: (layer, rows.mod_index(i + tile0), 0, 0)),
                pl.BlockSpec((None, d, 2 * f), lambda i: (layer, 0, 0), pipeline_mode=_RESIDENT),
                pl.BlockSpec((None, f, d), lambda i: (layer, 0, 0), pipeline_mode=_RESIDENT)]
    args = [x, g.reshape(1, d), mods_all, w_gu_all, w_down_all]
    if mixer is not None:
        a, wo_all, wo_index, bo = mixer
        k = a.shape[1]
        in_specs += [pl.BlockSpec((tm, k), lambda i: (i + tile0, 0)),
                     pl.BlockSpec((None, k, d), lambda i: (wo_index, 0, 0), pipeline_mode=_RESIDENT),
                     resident(1, d)]
        args += [a, wo_all, bo.reshape(1, d)]
    if final_g is not None:
        in_specs.append(resident(1, d))
        args.append(final_g.reshape(1, d))
    return pl.pallas_call(
        functools.partial(_ffn_kernel, chunk=chunk, has_mixer=mixer is not None, has_final=final_g is not None),
        out_shape=jax.ShapeDtypeStruct((n_tiles * tm, d), F32),
        grid=(n_tiles,),
        in_specs=in_specs,
        out_specs=pl.BlockSpec((tm, d), lambda i: (i, 0)),
        compiler_params=_cparams("parallel"),
        name="ffn",
    )(*args)


def _dwconv(x, w, k):
    n = x.shape[0]
    half = k // 2
    e = V7X_SUBLANES
    assert half <= e and n >= 4 * e
    taps = [w[j:j + 1, :] for j in range(k)]
    shifts = [s for s in range(-half, half + 1) if s != 0]

    def conv(v, mask):
        acc = v * taps[half]
        for s in shifts:
            shifted = pltpu.roll(v, (-s) % v.shape[0], axis=0)
            acc = acc + (shifted if mask is None else jnp.where(mask(s), shifted, 0.0)) * taps[s + half]
        return acc

    row = lax.broadcasted_iota(jnp.int32, (2 * e, x.shape[1]), 0)
    top = conv(x[:2 * e], lambda s: row + s >= 0)[:e]
    bottom = conv(x[n - 2 * e:], lambda s: row + s < 2 * e)[e:]
    return jnp.concatenate([top, conv(x, None)[e:n - e], bottom], axis=0)


CONV_HALO = 16


def _dwconv_window(ref, cols, w, k, r0, rows, slab_ref):
    n = ref.shape[0]
    if rows == n:
        return _dwconv(ref[:, cols].astype(F32), w, k)
    half = k // 2
    h = CONV_HALO
    assert half <= h <= rows and n % rows == 0
    before = ref[pl.ds(pl.multiple_of(jnp.maximum(r0 - h, 0), h), h), cols].astype(F32)
    after = ref[pl.ds(pl.multiple_of(jnp.minimum(r0 + rows, n - h), h), h), cols].astype(F32)
    slab_ref[0:h, :] = jnp.where(r0 > 0, before, 0.0)
    slab_ref[h:h + rows, :] = ref[pl.ds(r0, rows), cols].astype(F32)
    slab_ref[h + rows:, :] = jnp.where(r0 + rows < n, after, 0.0)
    acc = None
    for s in range(-half, half + 1):
        term = slab_ref[h + s:h + s + rows, :] * w[s + half:s + half + 1, :]
        acc = term if acc is None else acc + term
    return acc


TRI_BASE = 8
GDN_PHASE1_CHAINS = 32
GDN_HEADS_PER_STEP = 8
GDN_ROWS_PER_STEP = 4096


def _unit_tri_inverses_minus_eye(mats, ri, ci):
    c = mats[0].shape[0]

    def same_block(s):
        sh = int(math.log2(s))
        return (ri >> sh) == (ci >> sh)

    ps = [jnp.where(same_block(TRI_BASE), -a, 0.0) for a in mats]
    es = ps
    n_lvl = int(math.log2(TRI_BASE))
    for lvl in range(n_lvl):
        es = [e + _dot(p, e) for p, e in zip(ps, es)]
        if lvl < n_lvl - 1:
            ps = [_dot(p, p) for p in ps]
    s = TRI_BASE
    while s < c:
        mask = jnp.logical_and(same_block(2 * s), jnp.logical_not(same_block(s)))
        offs = [jnp.where(mask, a, 0.0) for a in mats]
        ys = [off + _dot(off, e) for off, e in zip(offs, es)]
        es = [e - (y + _dot(e, y)) for e, y in zip(es, ys)]
        s *= 2
    return es


def _chunk_cumsum(x, pos, reverse):
    n = x.shape[0]
    s = 1
    while s < CHUNK:
        if reverse:
            x = x + jnp.where(pos + s < CHUNK, pltpu.roll(x, n - s, axis=0), 0.0)
        else:
            x = x + jnp.where(pos >= s, pltpu.roll(x, s, axis=0), 0.0)
        s *= 2
    return x


def _gdn_kernel(*refs, n_heads, conv_k, lockstep, n_kept):
    (alog_ref, dtb_ref, q_ref, k_ref, v_ref, gt_ref, ab_ref, cq_ref, ck_ref, cv_ref,
     onorm_ref, s0_ref) = refs[:12]
    o_ref, sfin_ref, gates_s, conv_s, w_s, u_s, qd_s, ak_s, gl_s, o_s, st_s = refs[12 + n_kept:]
    n_chains, seq, dk = w_s.shape
    hb = n_chains // 2
    head0 = pl.program_id(1) * hb
    c = CHUNK
    n_chunks = seq // c
    chains = [(hh, d) for hh in range(hb) for d in range(2)]

    @pl.when(head0 == 0)
    def _():
        ab = ab_ref[...]
        pos = jnp.bitwise_and(lax.broadcasted_iota(jnp.int32, ab.shape, 0), c - 1)
        g = -jnp.exp(alog_ref[...]) * jax.nn.softplus(ab + dtb_ref[...])
        g_fwd = _chunk_cumsum(g, pos, reverse=False)
        g_rev = _chunk_cumsum(g, pos, reverse=True)
        gates_s[0] = g_fwd
        gates_s[1] = g_rev - g
        gates_s[2] = g_rev
        gates_s[3] = g_fwd - g
        gates_s[4] = jax.nn.sigmoid(ab)

    lane = lax.broadcasted_iota(jnp.int32, (1, V7X_LANES), 1)

    rows = lockstep * c

    def column(i, idx, win):
        one_hot = (lane == idx).astype(F32)
        col = jnp.sum(gates_s[i, win, :] * one_hot, axis=1, keepdims=True)
        return jnp.broadcast_to(col, (rows, V7X_LANES))

    ri = lax.broadcasted_iota(jnp.int32, (c, c), 0)
    ci = lax.broadcasted_iota(jnp.int32, (c, c), 1)
    incl = (ri >= ci, ri <= ci)
    strict = (ri > ci, ri < ci)

    def phase1(it, carry):
        r0 = pl.multiple_of(it * rows, rows)
        win = pl.ds(r0, rows)
        prepared = []
        for hh in range(hb):
            cols = slice(hh * dk, (hh + 1) * dk)
            q = _silu(_dwconv_window(q_ref, cols, cq_ref[:, cols], conv_k, r0, rows, conv_s.at[3 * hh]))
            k = _silu(_dwconv_window(k_ref, cols, ck_ref[:, cols], conv_k, r0, rows, conv_s.at[3 * hh + 1]))
            v = _silu(_dwconv_window(v_ref, cols, cv_ref[:, cols], conv_k, r0, rows, conv_s.at[3 * hh + 2]))
            q = q * (lax.rsqrt(jnp.sum(q * q, axis=-1, keepdims=True) + EPS) * (dk ** -0.5))
            k = k * lax.rsqrt(jnp.sum(k * k, axis=-1, keepdims=True) + EPS)
            for d in range(2):
                head = head0 + hh
                g_cum = column(2 * d, d * n_heads + head, win)
                g_tail = column(2 * d + 1, d * n_heads + head, win)
                beta = column(4, 2 * n_heads + d * n_heads + head, win)
                e_cum = jnp.exp(g_cum)
                kb = k * beta
                prepared.append((q, k, g_cum, kb, kb * e_cum, v * beta, k * jnp.exp(g_tail)))
                qd_s[2 * hh + d, win, :] = (q * e_cum).astype(qd_s.dtype)
                g_tot = jnp.exp(g_cum + g_tail)
                for gi in range(lockstep):
                    gl_s[2 * hh + d, pl.ds(it * lockstep + gi, 1), :] = g_tot[gi * c:gi * c + 1, :]
        items = []
        for gi in range(lockstep):
            part = slice(gi * c, (gi + 1) * c)
            for ch, (_, d) in enumerate(chains):
                sl = pl.ds(pl.multiple_of(r0 + gi * c, c), c)
                items.append((it * lockstep + gi, sl, ch, d) + tuple(a[part] for a in prepared[ch]))
        decays, kqs = [], []
        for ic, sl, ch, d, qc, kc, gc, kb, w0, u0, ktl in items:
            diff = gc[:, :c] - gc.T[:c, :]
            decays.append(jnp.where(incl[d], jnp.exp(jnp.where(incl[d], diff, 0.0)), 0.0))
            kqs.append(_dot_nt(jnp.concatenate([kb, qc], axis=0), kc))
        a_kks = [jnp.where(strict[item[3]], kq[:c] * decay, 0.0)
                 for item, kq, decay in zip(items, kqs, decays)]
        es = _unit_tri_inverses_minus_eye(a_kks, ri, ci)
        rhss = [jnp.concatenate([item[8], item[9]], axis=1) for item in items]
        wus = [rhs + _dot(e, rhs) for e, rhs in zip(es, rhss)]
        for item, kq, decay, wu in zip(items, kqs, decays, wus):
            ic, sl, ch = item[:3]
            w_s[ch, sl, :] = wu[:, :dk].astype(w_s.dtype)
            u_s[ch, sl, :] = wu[:, dk:]
            ak_s[ch, ic] = jnp.concatenate([kq[c:] * decay, item[10].T], axis=0).astype(ak_s.dtype)
        return carry

    lax.fori_loop(0, n_chunks // lockstep, phase1, 0)

    for hh, d in chains:
        st_s[2 * hh + d] = s0_ref[d, hh]

    def phase2(i, carry):
        n = 2 * hb
        ics = [n_chunks - 1 - i if d else i for _, d in chains]
        r0s = [pl.multiple_of(ic * c, c) for ic in ics]
        sls = [pl.ds(r0, c) for r0 in r0s]
        ss = [st_s[ch] for ch in range(n)]
        wqs = [_dot(jnp.concatenate([w_s[ch, sls[ch], :], qd_s[ch, sls[ch], :]], axis=0), ss[ch]) for ch in range(n)]
        v_news = [u_s[ch, sls[ch], :] - wqs[ch][:c] for ch in range(n)]
        outs = [_dot(ak_s[ch, ics[ch]], v_news[ch]) for ch in range(n)]
        for ch in range(n):
            o_s[ch, sls[ch], :] = wqs[ch][c:] + outs[ch][:c]
            st_s[ch] = ss[ch] * gl_s[ch, pl.ds(ics[ch], 1), :] + outs[ch][c:]
        return carry

    lax.fori_loop(0, n_chunks, phase2, 0, unroll=4)

    for hh, d in chains:
        sfin_ref[d, hh] = st_s[2 * hh + d]
    for hh in range(hb):
        cols = slice(hh * dk, (hh + 1) * dk)
        o = o_s[2 * hh] + o_s[2 * hh + 1]
        o = o * lax.rsqrt(jnp.mean(o * o, axis=-1, keepdims=True) + EPS)
        o_ref[:, cols] = (o * onorm_ref[...] * _silu(gt_ref[:, cols].astype(F32))).astype(o_ref.dtype)


def _gdn_core(proj, ab, conv_w, a_log, dt_bias, onorm, s0, dst, states_dst, *, t_total, row0, n_seq, seq,
              n_heads, dk, dv, s0_spec, state_slot):
    assert dk == dv == V7X_LANES and row0 % seq == 0
    conv_k = conv_w.shape[0]
    r0 = row0 // seq
    n_chunks = seq // CHUNK
    hb = _tile(n_heads, min(GDN_HEADS_PER_STEP, max(1, GDN_ROWS_PER_STEP // seq)), 1)
    lockstep = _tile(n_chunks, max(1, GDN_PHASE1_CHAINS // (2 * hb)), 1)
    gate_pad = lambda p: jnp.pad(p.reshape(1, 2 * n_heads), ((0, 0), (0, V7X_LANES - 2 * n_heads)))
    lane_vec = pl.BlockSpec((1, V7X_LANES), lambda b, h: (0, 0))
    if s0 is None:
        s0 = jnp.zeros((2, hb, dk, dv), F32)
        s0_in = pl.BlockSpec((2, hb, dk, dv), lambda b, h: (0, 0, 0, 0))
    else:
        s0_in = s0_spec(hb)
    nb = n_heads // hb
    col = lambda sec: pl.BlockSpec((seq, hb * dk), lambda b, h: (b + r0, sec * nb + h))
    cw = lambda sec: pl.BlockSpec((conv_k, hb * dk), lambda b, h: (0, sec * nb + h))
    f32 = lambda *s: pltpu.VMEM(s, F32)
    in_specs = [lane_vec, lane_vec, col(0), col(1), col(2), col(3),
                pl.BlockSpec((seq, V7X_LANES), lambda b, h: (b + r0, 0)),
                cw(0), cw(1), cw(2),
                pl.BlockSpec((1, dv), lambda b, h: (0, 0)),
                s0_in]
    args = [gate_pad(a_log), gate_pad(dt_bias), proj, proj, proj, proj, ab, conv_w, conv_w, conv_w,
            onorm.reshape(1, dv), s0]
    aliases = {}
    for out_idx, kept in enumerate((dst, states_dst)):
        if kept is not None:
            in_specs.append(pl.BlockSpec(memory_space=pl.ANY))
            args.append(kept)
            aliases[len(args) - 1] = out_idx
    nc = 2 * hb
    slot, n_slots = state_slot
    return pl.pallas_call(
        functools.partial(_gdn_kernel, n_heads=n_heads, conv_k=conv_k, lockstep=lockstep,
                          n_kept=len(aliases)),
        out_shape=(jax.ShapeDtypeStruct((t_total, n_heads * dv), BF16),
                   jax.ShapeDtypeStruct((n_seq, n_slots, 2, n_heads, dk, dv), F32)),
        grid=(n_seq, nb),
        in_specs=in_specs,
        out_specs=(pl.BlockSpec((seq, hb * dv), lambda b, h: (b + r0, h)),
                   pl.BlockSpec((None, None, 2, hb, dk, dv), lambda b, h: (b, slot, 0, h, 0, 0))),
        scratch_shapes=[f32(5, seq, V7X_LANES),
                        f32(3 * hb, lockstep * CHUNK + 2 * CONV_HALO, dk),
                        pltpu.VMEM((nc, seq, dk), BF16), f32(nc, seq, dv), pltpu.VMEM((nc, seq, dk), BF16),
                        pltpu.VMEM((nc, n_chunks, CHUNK + dk, CHUNK), BF16),
                        f32(nc, max(n_chunks, V7X_SUBLANES), V7X_LANES),
                        f32(nc, seq, dv), f32(nc, dk, dv)],
        input_output_aliases=aliases,
        compiler_params=_cparams("parallel", "arbitrary"),
        name="gdn_core",
    )(*args)


def _folded_odd_dft(seq):
    half = seq // 2
    k = np.arange(half, dtype=np.int64)[:, None]
    parts = []
    for parity in (0, 1):
        m = 2 * np.arange(half, dtype=np.int64)[None, :] + parity
        ang = (((2 * k + 1) * m) % (4 * seq)) * (math.pi / (2 * seq))
        parts.append(np.concatenate([np.cos(ang), np.sin(ang)], axis=0))
    return parts


def _bf16_head_tail(m):
    m32 = m.astype(np.float32)
    head = m32.astype(jnp.bfloat16)
    tail = (m32 - head.astype(np.float32)).astype(jnp.bfloat16)
    return jnp.asarray(head), jnp.asarray(tail)


def _filter_kernel(feat_ref, w1_ref, b1_ref, w2_ref, b2_ref, fr_ref, w3f_ref, w3b_ref, dl_ref,
                   fe_hi_ref, fe_lo_ref, fo_hi_ref, fo_lo_ref, hre_ref, him_ref, hid_s):
    seq = feat_ref.shape[0]
    half = seq // 2
    n_hid = w2_ref.shape[0]

    @pl.when(jnp.logical_and(pl.program_id(0) == 0, pl.program_id(1) == 0))
    def _():
        fr = fr_ref[...]
        hid1 = jnp.sin(fr * (_dot(feat_ref[...], w1_ref[...]) + b1_ref[...]))
        hid_s[...] = jnp.zeros_like(hid_s)
        hid_s[:, :n_hid] = jnp.sin(fr * (_dot(hid1, w2_ref[...]) + b2_ref[...]))

    def taps(parity):
        lags = pl.ds(parity, half, stride=2)
        hid = hid_s[lags, :][:, :n_hid]
        window = jnp.exp(-feat_ref[lags, :][:, 0:1] * dl_ref[...])
        hf = _dot(hid, w3f_ref[...]) * window
        hb = _dot(hid, w3b_ref[...]) * window
        if parity == 0:
            hb = jnp.where(lax.broadcasted_iota(jnp.int32, hb.shape, 0) == 0, 0.0, hb)
        return hf + hb, hb - hf

    def dft(m_hi, m_lo, val):
        v_hi = val.astype(BF16)
        v_lo = (val - v_hi.astype(F32)).astype(BF16)
        return (jnp.dot(m_hi, v_hi, preferred_element_type=F32)
                + (jnp.dot(m_hi, v_lo, preferred_element_type=F32)
                   + jnp.dot(m_lo, v_hi, preferred_element_type=F32)))

    (sum_e, dif_e), (sum_o, dif_o) = taps(0), taps(1)
    cos_rows, sin_rows = slice(0, half), slice(half, seq)
    re_e = dft(fe_hi_ref[cos_rows, :], fe_lo_ref[cos_rows, :], sum_e)
    re_o = dft(fo_hi_ref[cos_rows, :], fo_lo_ref[cos_rows, :], sum_o)
    im_e = dft(fe_hi_ref[sin_rows, :], fe_lo_ref[sin_rows, :], dif_e)
    im_o = dft(fo_hi_ref[sin_rows, :], fo_lo_ref[sin_rows, :], dif_o)
    hre_ref[0:half, :] = re_e + re_o
    hre_ref[half:, :] = re_e - re_o
    him_ref[0:half, :] = im_e + im_o
    him_ref[half:, :] = im_o - im_e


def _hyena_filters(seq, d, w1, b1, w2, b2, w3, freq, dft_parts):
    emb, hid = w1.shape
    bands = (emb - 1) // 2
    t = jnp.linspace(0.0, 1.0, seq, dtype=F32)[:, None]
    wpos = (2.0 * math.pi / seq) * jnp.arange(seq, dtype=F32)[:, None]
    fb = jnp.linspace(1e-4, bands - 1, bands, dtype=F32)[None, :]
    feat = jnp.concatenate([t, jnp.cos(fb * wpos), -jnp.sin(fb * wpos)], axis=-1)
    feat = jnp.pad(feat, ((0, 0), (0, V7X_LANES - emb)))
    w1p = jnp.pad(w1, ((0, V7X_LANES - emb), (0, 0)))
    max_decay = math.log(HY_TARGET) / HY_SHORT_PCT
    min_decay = math.log(HY_TARGET) / HY_LONG_PCT
    deltas = jnp.abs(jnp.linspace(min_decay, max_decay, d, dtype=F32))[None, :]
    tc = _tile(d, 256, V7X_LANES)
    nt = d // tc
    full = lambda r, c: pl.BlockSpec((r, c), lambda n, j: (0, 0))
    out_spec = pl.BlockSpec((None, seq, tc), lambda n, j: (n, 0, j))
    return pl.pallas_call(
        _filter_kernel,
        out_shape=(jax.ShapeDtypeStruct((HY_ORDER, seq, d), F32),) * 2,
        grid=(HY_ORDER, nt),
        in_specs=[full(seq, V7X_LANES), full(V7X_LANES, hid), full(1, hid), full(hid, hid), full(1, hid),
                  full(1, hid),
                  pl.BlockSpec((hid, tc), lambda n, j: (0, (2 * n) * nt + j)),
                  pl.BlockSpec((hid, tc), lambda n, j: (0, (2 * n + 1) * nt + j)),
                  pl.BlockSpec((1, tc), lambda n, j: (0, j))] + [full(seq, seq // 2)] * 4,
        out_specs=(out_spec, out_spec),
        scratch_shapes=[pltpu.VMEM((seq, V7X_LANES), F32)],
        compiler_params=_cparams("arbitrary", "arbitrary"),
        name="hyena_filter",
    )(feat, w1p, b1.reshape(1, hid), w2, b2.reshape(1, hid), freq.reshape(1, hid), w3, w3, deltas,
      *dft_parts)


HYENA_ROWS_PER_STEP = 2048


def _hyconv_kernel(*refs, conv_z, conv_k, has_dst):
    (z_ref, x_ref, hre_ref, him_ref, skip_ref, cz_ref, cx_ref,
     fwd_e_ref, fwd_o_ref, inv_e_ref, inv_o_ref) = refs[:11]
    o_ref = refs[11 + has_dst]
    zs_s, xs_s, os_s = refs[-3:]
    seq = hre_ref.shape[0]
    half = seq // 2
    n_sub = z_ref.shape[0] // seq
    even, odd = pl.ds(0, half, stride=2), pl.ds(1, half, stride=2)
    n_lane_tiles = zs_s.shape[1]

    def stage(dst, s, val):
        for l in range(n_lane_tiles):
            dst[s, l] = val[:, l * V7X_LANES:(l + 1) * V7X_LANES]

    def rows_of(src, s, rows):
        return jnp.concatenate([src[s, l, rows, :] for l in range(n_lane_tiles)], axis=1)

    for s in range(n_sub):
        z = z_ref[s * seq:(s + 1) * seq, :].astype(F32)
        stage(zs_s, s, _dwconv(z, cz_ref[...], conv_k) if conv_z else z)
        stage(xs_s, s, _dwconv(x_ref[s * seq:(s + 1) * seq, :].astype(F32), cx_ref[...], conv_k))
    z_parts = [(rows_of(zs_s, s, even), rows_of(zs_s, s, odd)) for s in range(n_sub)]
    fwds = [(jnp.dot(fwd_e_ref[...], ze.astype(BF16), preferred_element_type=F32),
             jnp.dot(fwd_o_ref[...], zo.astype(BF16), preferred_element_type=F32)) for ze, zo in z_parts]
    hre_a, hre_b = hre_ref[0:half, :], hre_ref[half:, :]
    him_a, him_b = him_ref[0:half, :], him_ref[half:, :]
    skip = skip_ref[...]
    for s in range(n_sub):
        (fe, fo), (ze, zo) = fwds[s], z_parts[s]
        pe, qe, po, qo = fe[:half], fe[half:], fo[:half], fo[half:]
        pa, pb, qa, qb = pe + po, pe - po, qe + qo, qo - qe
        yre_a, yim_a = pa * hre_a + qa * him_a, pa * him_a - qa * hre_a
        yre_b, yim_b = pb * hre_b + qb * him_b, pb * him_b - qb * hre_b
        y_e = jnp.dot(inv_e_ref[...], jnp.concatenate([yre_a + yre_b, yim_a - yim_b], axis=0).astype(BF16),
                      preferred_element_type=F32)
        y_o = jnp.dot(inv_o_ref[...], jnp.concatenate([yre_a - yre_b, yim_a + yim_b], axis=0).astype(BF16),
                      preferred_element_type=F32)
        out_e = rows_of(xs_s, s, even) * (y_e + skip * ze)
        out_o = rows_of(xs_s, s, odd) * (y_o + skip * zo)
        for l in range(n_lane_tiles):
            lanes = slice(l * V7X_LANES, (l + 1) * V7X_LANES)
            os_s[s, l, even, :] = out_e[:, lanes]
            os_s[s, l, odd, :] = out_o[:, lanes]
        o_ref[s * seq:(s + 1) * seq, :] = rows_of(os_s, s, slice(None)).astype(o_ref.dtype)


def _hyconv(z, z_col0, conv_z, proj, x_col0, hre, him, order, skip, conv_w, dft, dst,
            *, row0, n_seq, seq, d, out_rows, out_row0, out_dtype):
    conv_k = conv_w.shape[0]
    n_sub = max(s for s in range(1, max(1, HYENA_ROWS_PER_STEP // seq) + 1)
                if n_seq % s == 0 and row0 % (s * seq) == 0 and out_row0 % (s * seq) == 0)
    blk = n_sub * seq
    tc = _tile(d, 512 if blk <= 1024 else 256, V7X_LANES)
    nt = d // tc
    r0 = row0 // blk
    zr0 = r0 if conv_z else 0
    zc, xc = z_col0 // tc, x_col0 // tc
    in_specs = [pl.BlockSpec((blk, tc), lambda j, b: (b + zr0, zc + j)),
                pl.BlockSpec((blk, tc), lambda j, b: (b + r0, xc + j)),
                pl.BlockSpec((None, seq, tc), lambda j, b: (order, 0, j)),
                pl.BlockSpec((None, seq, tc), lambda j, b: (order, 0, j)),
                pl.BlockSpec((1, tc), lambda j, b: (0, j)),
                pl.BlockSpec((conv_k, tc), lambda j, b: (0, zc + j)),
                pl.BlockSpec((conv_k, tc), lambda j, b: (0, xc + j)),
                pl.BlockSpec((seq, seq // 2), lambda j, b: (0, 0), pipeline_mode=_RESIDENT),
                pl.BlockSpec((seq, seq // 2), lambda j, b: (0, 0), pipeline_mode=_RESIDENT),
                pl.BlockSpec((seq // 2, seq), lambda j, b: (0, 0), pipeline_mode=_RESIDENT),
                pl.BlockSpec((seq // 2, seq), lambda j, b: (0, 0), pipeline_mode=_RESIDENT)]
    args = [z, proj, hre, him, skip.reshape(1, d), conv_w, conv_w, *dft]
    aliases = {}
    if dst is not None:
        in_specs.append(pl.BlockSpec(memory_space=pl.ANY))
        args.append(dst)
        aliases = {len(args) - 1: 0}
    out_r0 = out_row0 // blk
    return pl.pallas_call(
        functools.partial(_hyconv_kernel, conv_z=conv_z, conv_k=conv_k, has_dst=dst is not None),
        out_shape=jax.ShapeDtypeStruct((out_rows, d), out_dtype),
        grid=(nt, n_seq // n_sub),
        in_specs=in_specs,
        out_specs=pl.BlockSpec((blk, tc), lambda j, b: (b + out_r0, j)),
        scratch_shapes=[pltpu.VMEM((n_sub, tc // V7X_LANES, seq, V7X_LANES), F32)] * 3,
        input_output_aliases=aliases,
        compiler_params=_cparams("parallel", "parallel"),
        name="hyena_conv",
    )(*args)


def _grid_pos_emb(n_tokens, d):
    rows = n_tokens // GRID_W
    r, col = jnp.meshgrid(jnp.arange(rows), jnp.arange(GRID_W), indexing='ij')
    quarter = d // 4
    omega = 1.0 / (POS_BASE ** (jnp.arange(quarter, dtype=F32) / quarter))

    def emb1d(p):
        a = p.reshape(-1, 1).astype(F32) * omega[None, :]
        return jnp.concatenate([jnp.sin(a), jnp.cos(a)], axis=-1)

    return jnp.concatenate([emb1d(r), emb1d(col)], axis=-1)


def kernel(x_prompt, x_sample, state_delta, c, c_ctx, ada_w, ada_b, norm1_g, norm2_g, gdn_w_in, gdn_conv, gdn_a_log, gdn_dt_bias, gdn_onorm, gdn_w_out, hy_w_in, hy_b_in, hy_conv, hy_f_w1, hy_f_b1, hy_f_w2, hy_f_b2, hy_f_w3, hy_freq, hy_skip, hy_w_out, hy_b_out, ffn_w_gu, ffn_w_down, final_g):
    bc, lc, d = x_prompt.shape
    bl, ll, _ = x_sample.shape
    depth = ada_w.shape[0]
    n_heads, dk, dv = state_delta.shape[3:]
    tc_rows, tl_rows = bc * lc, bl * ll
    t = tc_rows + tl_rows
    assert tc_rows % ll == 0 and ll % lc == 0
    rows = _Rows(tc_rows, ll, t, _tile(math.gcd(tc_rows, ll), 512, V7X_SUBLANES))

    bm = 1 + bl
    bm_pad = -(-bm // V7X_SUBLANES) * V7X_SUBLANES
    cvec = jnp.concatenate([c_ctx[None, :], c, jnp.zeros((bm_pad - bm, d), F32)], axis=0)
    mods_all = _ada(cvec, ada_w, ada_b).reshape(depth, bm_pad, 6, d)

    x = _embed(x_prompt.reshape(tc_rows, d), x_sample.reshape(tl_rows, d), _grid_pos_emb(ll, d), rows)

    n_hy = hy_w_in.shape[0]
    dft = {}
    for seq in (lc, ll):
        half = seq // 2
        fwd_e, fwd_o = _folded_odd_dft(seq)
        (fe_hi, fe_lo), (fo_hi, fo_lo) = _bf16_head_tail(fwd_e), _bf16_head_tail(fwd_o)
        inv_e, _ = _bf16_head_tail(np.concatenate([fwd_e[:half].T, -fwd_e[half:].T], axis=1) / seq)
        inv_o, _ = _bf16_head_tail(np.concatenate([fwd_o[:half].T, -fwd_o[half:].T], axis=1) / seq)
        spectra = [_hyena_filters(seq, d, hy_f_w1[j], hy_f_b1[j], hy_f_w2[j], hy_f_b2[j], hy_f_w3[j],
                                  hy_freq[j], (fe_hi, fe_lo, fo_hi, fo_lo)) for j in range(n_hy)]
        dft[seq] = ((fe_hi, fo_hi, inv_e, inv_o), spectra)

    zero_b = jnp.zeros((d,), F32)
    mixed = jnp.zeros((t, n_heads * dv), BF16)
    shared = n_heads * dv == d
    mixed_hy = mixed if shared else jnp.zeros((t, d), BF16)
    new_state_delta = jnp.zeros((bc, gdn_w_in.shape[0], 2, n_heads, dk, dv), F32)
    n_mixers = 2
    n_gdn = gdn_w_in.shape[0]
    gdn_w_in_b = jnp.pad(gdn_w_in.astype(BF16), ((0, 0), (0, 0), (0, V7X_LANES - 4 * n_heads)))
    gdn_w_out_b, hy_w_in_b, hy_w_out_b = (w.astype(BF16) for w in (gdn_w_out, hy_w_in, hy_w_out))
    ffn_w_gu_b, ffn_w_down_b = ffn_w_gu.astype(BF16), ffn_w_down.astype(BF16)
    for layer in range(depth):
        j = layer // n_mixers
        if layer % n_mixers == 0:
            proj, ab = _in_proj(x, norm1_g[layer], mods_all, layer, gdn_w_in_b, j,
                                jnp.zeros((gdn_w_in_b.shape[2],), F32), rows, n_side=V7X_LANES)
            gdn = dict(t_total=t, n_heads=n_heads, dk=dk, dv=dv)
            weights = (gdn_conv[j], gdn_a_log[j], gdn_dt_bias[j], gdn_onorm[j])
            mixed, new_state_delta = _gdn_core(proj, ab, *weights, None, mixed, new_state_delta, row0=0, n_seq=bc,
                                               seq=lc, s0_spec=None, state_slot=(j, n_gdn), **gdn)
            s0_spec = lambda hb, j=j: pl.BlockSpec((None, None, 2, hb, dk, dv), lambda b, h: (b, j, 0, h, 0, 0))
            mixed, _ = _gdn_core(proj, ab, *weights, state_delta, mixed, None, row0=tc_rows, n_seq=bl, seq=ll,
                                 s0_spec=s0_spec, state_slot=(0, 1), **gdn)
            mixer = (mixed, gdn_w_out_b, j, zero_b)
            mixed_hy = mixed if shared else mixed_hy
        else:
            proj = _in_proj(x, norm1_g[layer], mods_all, layer, hy_w_in_b, j, hy_b_in[j], rows)
            for row0, n_seq, seq in ((0, bc, lc), (tc_rows, bl, ll)):
                dft_mats, spectra = dft[seq]
                hre, him = spectra[j]
                z = proj
                for n in range(HY_ORDER):
                    last = n == HY_ORDER - 1
                    z = _hyconv(z, 0, n == 0, proj, (n + 1) * d, hre, him, n, hy_skip[j, n], hy_conv[j],
                                dft_mats, mixed_hy if last else None, row0=row0, n_seq=n_seq, seq=seq, d=d,
                                out_rows=t if last else n_seq * seq, out_row0=row0 if last else 0,
                                out_dtype=BF16 if last else F32)
                mixed_hy = z
            mixer = (mixed_hy, hy_w_out_b, j, hy_b_out[j])
            mixed = mixed_hy if shared else mixed
        ffn = functools.partial(_ffn, x, norm2_g[layer], mods_all, ffn_w_gu_b, ffn_w_down_b, layer, rows,
                                mixer=mixer)
        if layer < depth - 1:
            x = ffn()
        else:
            y_prompt = ffn(final_g=final_g, tile0=0, n_tiles=rows.n_ctx_tiles)
            y_sample = ffn(final_g=final_g, tile0=rows.n_ctx_tiles, n_tiles=rows.n_tiles - rows.n_ctx_tiles)
    return (y_prompt.reshape(bc, lc, d), y_sample.reshape(bl, ll, d), new_state_delta)
```
